```python
import math
import jax, jax.numpy as jnp
from jax import lax
import numpy as np

D_MODEL = 1024
BATCH = 8
SEQ = 2048
DEPTH = 2

CHUNK = 64
N_EVEN = (DEPTH + 1) // 2
N_ODD = DEPTH // 2
EPS = 1e-6
NEG_INF = -1e30
FOX_HEADS = 8
FOX_HEAD_DIM = 64
FOX_WIDTH = FOX_HEADS * FOX_HEAD_DIM
Q_BLOCK = 2 * CHUNK
POOL_WINDOWS = (2, 4, 8, 16)
POOL_GROUPS = 4
POOL_GROUP_DIM = 128
POOL_WIDTH = POOL_GROUPS * POOL_GROUP_DIM
EVEN_IN_WIDTH = 3 * FOX_WIDTH + FOX_HEADS + POOL_WIDTH
EVEN_MIX_WIDTH = FOX_WIDTH + POOL_WIDTH
SSM_WIDTH = D_MODEL
SSM_GROUP = 16
SSM_GROUPS = SSM_WIDTH // SSM_GROUP
SSM_STATE = 64
DT_MIN = 1e-3
DT_MAX = 1e-1
D_FF = 2816
N_EXPERTS = 8
TOP_K = 2

kernel_name = "hybrid_fox_pool_s5_moe_trunk"


def rms_norm(x, g):
    xf = x.astype(jnp.float32)
    xf = xf * lax.rsqrt(jnp.mean(xf * xf, axis=-1, keepdims=True) + EPS)
    return (xf * g.astype(jnp.float32)).astype(x.dtype)


def swiglu(h, w1, w3, w2):
    return (jax.nn.silu(h @ w1) * (h @ w3)) @ w2


def forgetting_attention(q, k, v, log_f):
    b, s, h, dh = q.shape
    nb = s // Q_BLOCK
    scale = dh ** -0.5
    cum = jnp.cumsum(log_f.astype(jnp.float32), axis=1)
    cum_k = cum.transpose(0, 2, 1)
    pos = jnp.arange(s, dtype=jnp.int32)
    q_blocks = q.reshape(b, nb, Q_BLOCK, h, dh).transpose(1, 0, 2, 3, 4)
    cum_q = cum.reshape(b, nb, Q_BLOCK, h).transpose(1, 0, 3, 2)
    q_pos = pos.reshape(nb, Q_BLOCK)

    def one_block(args):
        qi, ci, pi = args
        logits = jnp.einsum('bqhd,bkhd->bhqk', qi, k,
                            preferred_element_type=jnp.float32) * scale
        logits = logits + ci[..., :, None] - cum_k[..., None, :]
        mask = pos[None, :] <= pi[:, None]
        logits = jnp.where(mask, logits, NEG_INF)
        p = jax.nn.softmax(logits, axis=-1)
        return jnp.einsum('bhqk,bkhd->bqhd', p.astype(v.dtype), v)

    out = lax.map(one_block, (q_blocks, cum_q, q_pos))
    return out.transpose(1, 0, 2, 3, 4).reshape(b, s, h * dh)


def multiscale_pool(p, w_pool, pool_scale):
    b, s, _ = p.shape
    pf = p.astype(jnp.float32).reshape(b, s, POOL_GROUPS, POOL_GROUP_DIM)
    cs = jnp.cumsum(pf, axis=1)
    count = jnp.arange(1, s + 1, dtype=jnp.float32)
    outs = []
    for gi, w in enumerate(POOL_WINDOWS):
        c = cs[:, :, gi]
        lag = jnp.pad(c[:, :s - w], ((0, 0), (w, 0), (0, 0)))
        mean = (c - lag) / jnp.minimum(count, float(w))[None, :, None]
        outs.append(mean - pf[:, :, gi])
    pooled = jnp.stack(outs, axis=2).astype(p.dtype)
    mixed = jnp.einsum('bsgc,gcd->bsgd', pooled, w_pool)
    return mixed.reshape(b, s, POOL_WIDTH) * pool_scale


def s5_layer(u, a_re, a_im, log_dt, b_re, b_im, c_re, c_im, d):
    f32 = jnp.float32
    bsz, s, _ = u.shape
    ug = u.astype(f32).reshape(bsz, s, SSM_GROUPS, SSM_GROUP)
    dt = jnp.exp(log_dt.astype(f32))[:, None]
    ar = a_re.astype(f32)
    ai = a_im.astype(f32)
    mag = jnp.exp(ar * dt)
    abar_re = mag * jnp.cos(ai * dt)
    abar_im = mag * jnp.sin(ai * dt)
    den = ar * ar + ai * ai
    nr = abar_re - 1.0
    ni = abar_im
    coef_re = (nr * ar + ni * ai) / den
    coef_im = (ni * ar - nr * ai) / den
    br = b_re.astype(f32)
    bi = b_im.astype(f32)
    bbar_re = coef_re[..., None] * br - coef_im[..., None] * bi
    bbar_im = coef_re[..., None] * bi + coef_im[..., None] * br
    bu_re = jnp.einsum('bsgh,gph->bsgp', ug, bbar_re)
    bu_im = jnp.einsum('bsgh,gph->bsgp', ug, bbar_im)
    a_re_s = jnp.broadcast_to(abar_re, (1, s) + abar_re.shape)
    a_im_s = jnp.broadcast_to(abar_im, (1, s) + abar_im.shape)

    def combine(left, right):
        ar_l, ai_l, br_l, bi_l = left
        ar_r, ai_r, br_r, bi_r = right
        return (ar_r * ar_l - ai_r * ai_l,
                ar_r * ai_l + ai_r * ar_l,
                ar_r * br_l - ai_r * bi_l + br_r,
                ar_r * bi_l + ai_r * br_l + bi_r)

    _, _, x_re, x_im = lax.associative_scan(
        combine, (a_re_s, a_im_s, bu_re, bu_im), axis=1)
    y = (jnp.einsum('bsgp,ghp->bsgh', x_re, c_re.astype(f32))
         - jnp.einsum('bsgp,ghp->bsgh', x_im, c_im.astype(f32)))
    y = y.reshape(bsz, s, SSM_WIDTH) + d.astype(f32) * u.astype(f32)
    return y.astype(u.dtype)


def moe_swiglu(h, router_w, router_b, w1, w3, w2):
    b, s, d = h.shape
    xf = h.reshape(b * s, d)
    logits = (xf @ router_w).astype(jnp.float32) + router_b.astype(jnp.float32)
    top_vals, top_idx = lax.top_k(logits, TOP_K)
    gates = jax.nn.softmax(top_vals, axis=-1)
    combine = jnp.sum(jax.nn.one_hot(top_idx, N_EXPERTS, dtype=jnp.float32)
                      * gates[..., None], axis=1)
    out = jnp.zeros((b * s, d), jnp.float32)
    for e in range(N_EXPERTS):
        ye = swiglu(xf, w1[e], w3[e], w2[e])
        out = out + combine[:, e:e + 1] * ye.astype(jnp.float32)
    return out.astype(h.dtype).reshape(b, s, d)


def setup_inputs(seed: int = 0) -> dict:
    key = jax.random.key(seed)
    ks = jax.random.split(key, 32)
    f32 = jnp.float32

    def nrm(k, shape, std):
        return jax.random.normal(k, shape, f32) * std

    def gain(k, shape):
        return 1.0 + 0.02 * jax.random.normal(k, shape, f32)

    n_idx = jnp.arange(SSM_STATE, dtype=f32)
    inp = {}
    inp["x"] = jax.random.normal(ks[0], (BATCH, SEQ, D_MODEL), f32)
    inp["even_mix_norm"] = gain(ks[1], (N_EVEN, D_MODEL))
    inp["even_w_in"] = nrm(ks[2], (N_EVEN, D_MODEL, EVEN_IN_WIDTH), D_MODEL ** -0.5)
    inp["even_b_forget"] = 2.0 + 0.5 * jax.random.normal(ks[3], (N_EVEN, FOX_HEADS), f32)
    inp["even_w_pool"] = nrm(ks[4], (N_EVEN, POOL_GROUPS, POOL_GROUP_DIM, POOL_GROUP_DIM), POOL_GROUP_DIM ** -0.5)
    inp["even_pool_scale"] = gain(ks[5], (N_EVEN, POOL_WIDTH))
    inp["even_w_out"] = nrm(ks[6], (N_EVEN, EVEN_MIX_WIDTH, D_MODEL), EVEN_MIX_WIDTH ** -0.5)
    inp["even_ffn_norm"] = gain(ks[7], (N_EVEN, D_MODEL))
    inp["even_ffn_w1"] = nrm(ks[8], (N_EVEN, D_MODEL, D_FF), D_MODEL ** -0.5)
    inp["even_ffn_w3"] = nrm(ks[9], (N_EVEN, D_MODEL, D_FF), D_MODEL ** -0.5)
    inp["even_ffn_w2"] = nrm(ks[10], (N_EVEN, D_FF, D_MODEL), D_FF ** -0.5)
    inp["odd_mix_norm"] = gain(ks[11], (N_ODD, D_MODEL))
    inp["odd_w_in"] = nrm(ks[12], (N_ODD, D_MODEL, SSM_WIDTH), D_MODEL ** -0.5)
    inp["ssm_a_re"] = -0.5 + 0.01 * jax.random.normal(ks[13], (N_ODD, SSM_GROUPS, SSM_STATE), f32)
    inp["ssm_a_im"] = math.pi * n_idx + 0.01 * jax.random.normal(ks[14], (N_ODD, SSM_GROUPS, SSM_STATE), f32)
    inp["ssm_log_dt"] = jax.random.uniform(ks[15], (N_ODD, SSM_GROUPS), f32,
                                           math.log(DT_MIN), math.log(DT_MAX))
    inp["ssm_b_re"] = nrm(ks[16], (N_ODD, SSM_GROUPS, SSM_STATE, SSM_GROUP), (2.0 * SSM_GROUP) ** -0.5)
    inp["ssm_b_im"] = nrm(ks[17], (N_ODD, SSM_GROUPS, SSM_STATE, SSM_GROUP), (2.0 * SSM_GROUP) ** -0.5)
    inp["ssm_c_re"] = nrm(ks[18], (N_ODD, SSM_GROUPS, SSM_GROUP, SSM_STATE), 0.5)
    inp["ssm_c_im"] = nrm(ks[19], (N_ODD, SSM_GROUPS, SSM_GROUP, SSM_STATE), 0.5)
    inp["ssm_d"] = nrm(ks[20], (N_ODD, SSM_WIDTH), 1.0)
    inp["odd_w_glu_a"] = nrm(ks[21], (N_ODD, SSM_WIDTH, D_MODEL), SSM_WIDTH ** -0.5)
    inp["odd_w_glu_b"] = nrm(ks[22], (N_ODD, SSM_WIDTH, D_MODEL), SSM_WIDTH ** -0.5)
    inp["odd_moe_norm"] = gain(ks[23], (N_ODD, D_MODEL))
    inp["router_w"] = nrm(ks[24], (N_ODD, D_MODEL, N_EXPERTS), D_MODEL ** -0.5)
    inp["router_b"] = nrm(ks[25], (N_ODD, N_EXPERTS), 0.01)
    inp["expert_w1"] = nrm(ks[26], (N_ODD, N_EXPERTS, D_MODEL, D_FF), D_MODEL ** -0.5)
    inp["expert_w3"] = nrm(ks[27], (N_ODD, N_EXPERTS, D_MODEL, D_FF), D_MODEL ** -0.5)
    inp["expert_w2"] = nrm(ks[28], (N_ODD, N_EXPERTS, D_FF, D_MODEL), D_FF ** -0.5)
    inp["final_norm"] = gain(ks[29], (D_MODEL,))
    return inp


def reference(x, even_mix_norm, even_w_in, even_b_forget, even_w_pool, even_pool_scale,
              even_w_out, even_ffn_norm, even_ffn_w1, even_ffn_w3, even_ffn_w2,
              odd_mix_norm, odd_w_in, ssm_a_re, ssm_a_im, ssm_log_dt, ssm_b_re, ssm_b_im,
              ssm_c_re, ssm_c_im, ssm_d, odd_w_glu_a, odd_w_glu_b, odd_moe_norm,
              router_w, router_b, expert_w1, expert_w3, expert_w2, final_norm):
    b, s, _ = x.shape
    splits = [FOX_WIDTH, 2 * FOX_WIDTH, 3 * FOX_WIDTH, 3 * FOX_WIDTH + FOX_HEADS]
    for layer in range(DEPTH):
        j = layer // 2
        if layer % 2 == 0:
            h = rms_norm(x, even_mix_norm[j])
            z = h @ even_w_in[j]
            q, k, v, fg, p_in = jnp.split(z, splits, axis=-1)
            log_f = jax.nn.log_sigmoid(fg.astype(jnp.float32)
                                       + even_b_forget[j].astype(jnp.float32))
            hd = (b, s, FOX_HEADS, FOX_HEAD_DIM)
            att = forgetting_attention(q.reshape(hd), k.reshape(hd), v.reshape(hd), log_f)
            pool = multiscale_pool(p_in, even_w_pool[j], even_pool_scale[j])
            mix = jnp.concatenate([att.astype(x.dtype), pool.astype(x.dtype)], axis=-1)
            x = x + mix @ even_w_out[j]
            h = rms_norm(x, even_ffn_norm[j])
            x = x + swiglu(h, even_ffn_w1[j], even_ffn_w3[j], even_ffn_w2[j])
        else:
            h = rms_norm(x, odd_mix_norm[j])
            u = h @ odd_w_in[j]
            y = s5_layer(u, ssm_a_re[j], ssm_a_im[j], ssm_log_dt[j], ssm_b_re[j],
                         ssm_b_im[j], ssm_c_re[j], ssm_c_im[j], ssm_d[j])
            g = jax.nn.gelu(y)
            x = x + (g @ odd_w_glu_a[j]) * jax.nn.sigmoid(g @ odd_w_glu_b[j])
            h = rms_norm(x, odd_moe_norm[j])
            x = x + moe_swiglu(h, router_w[j], router_b[j], expert_w1[j],
                               expert_w3[j], expert_w2[j])
    return rms_norm(x, final_norm)
```

```python
import functools
import math

import numpy as np
import jax
import jax.numpy as jnp
from jax import lax
from jax.experimental import pallas as pl
from jax.experimental.pallas import tpu as pltpu

F32 = jnp.float32
BF16 = jnp.bfloat16

EPS = 1e-6
NEG_INF = -1e30
LANES = 128
SUBLANES = 8
VMEM_LIMIT = 56 * 1024 * 1024

FOX_HEADS = 8
FOX_HEAD_DIM = 64
FOX_WIDTH = FOX_HEADS * FOX_HEAD_DIM
POOL_WINDOWS = (2, 4, 8, 16)
POOL_GROUP_DIM = 128
POOL_WIDTH = len(POOL_WINDOWS) * POOL_GROUP_DIM
POOL_HALO = 16
SSM_GROUP = 16
SSM_STATE = 64
SSM_SLAB_GROUPS = LANES // SSM_GROUP
SSM_SLAB_STATE = SSM_SLAB_GROUPS * SSM_STATE
N_EXPERTS = 8

BIAS_LANE = FOX_HEAD_DIM
ONES_LANE = FOX_HEAD_DIM + 3


def _params(*sem):
    return pltpu.CompilerParams(dimension_semantics=sem,
                                vmem_limit_bytes=VMEM_LIMIT)


def _rms(x, g):
    ms = jnp.mean(x * x, axis=-1, keepdims=True)
    return x * lax.rsqrt(ms + EPS) * g


def _sigmoid(x):
    return 1.0 / (1.0 + jnp.exp(-x))


def _lane_range_ones(lo, hi):
    lane = lax.broadcasted_iota(jnp.int32, (1, LANES), 1)
    return jnp.where((lane >= lo) & (lane < hi), 1.0, 0.0).astype(F32)


def _even_inproj_kernel(x_ref, g_ref, w_ref, bias_ref, pq_ref, pk_ref,
                        q_ref, k_ref, v_ref, p_ref, carry_ref, *, tiles_per_seq):
    i = pl.program_id(0)
    tm = x_ref.shape[0]
    qw = FOX_HEADS * LANES
    h = _rms(x_ref[...], g_ref[...]).astype(BF16)
    z = jnp.dot(h, w_ref[...], preferred_element_type=F32)
    v_ref[...] = z[:, 2 * qw:2 * qw + FOX_WIDTH].astype(BF16)
    p_ref[...] = z[:, 2 * qw + FOX_WIDTH:2 * qw + FOX_WIDTH + POOL_WIDTH]
    fg = z[:, 2 * qw + FOX_WIDTH + POOL_WIDTH:] + bias_ref[...]
    lf = jnp.minimum(fg, 0.0) - jnp.log1p(jnp.exp(-jnp.abs(fg)))
    row = lax.broadcasted_iota(jnp.int32, lf.shape, 0)
    c = lf
    sh = 1
    while sh < tm:
        c = c + jnp.where(row >= sh, pltpu.roll(c, sh, axis=0), 0.0)
        sh *= 2

    @pl.when(i % tiles_per_seq == 0)
    def _():
        carry_ref[...] = jnp.zeros_like(carry_ref)

    c = c + carry_ref[0:1, :]
    carry_ref[...] = jnp.broadcast_to(c[tm - 1:tm, :], carry_ref.shape)
    hi = c.astype(BF16)
    r1 = c - hi.astype(F32)
    mid = r1.astype(BF16)
    lo = (r1 - mid.astype(F32)).astype(BF16)
    cc = jnp.concatenate([hi, mid, lo], axis=1)
    ones_q = _lane_range_ones(ONES_LANE, ONES_LANE + 3)
    ones_k = _lane_range_ones(BIAS_LANE, BIAS_LANE + 3)
    for hh in range(FOX_HEADS):
        sl = slice(hh * LANES, (hh + 1) * LANES)
        qb = jnp.dot(cc, pq_ref[hh], preferred_element_type=F32)
        q_ref[:, sl] = (z[:, sl] + qb + ones_q).astype(BF16)
        kb = jnp.dot(cc, pk_ref[hh], preferred_element_type=F32)
        k_ref[:, sl] = (z[:, qw + hh * LANES:qw + (hh + 1) * LANES]
                        + kb + ones_k).astype(BF16)


def _even_inproj(x2d, gain, w_aug, bias, pq, pk, *, seq, tm):
    t, d = x2d.shape
    n = w_aug.shape[1]
    qw = FOX_HEADS * LANES
    kern = functools.partial(_even_inproj_kernel, tiles_per_seq=seq // tm)
    return pl.pallas_call(
        kern,
        grid=(t // tm,),
        in_specs=[
            pl.BlockSpec((tm, d), lambda i: (i, 0)),
            pl.BlockSpec((1, d), lambda i: (0, 0)),
            pl.BlockSpec((d, n), lambda i: (0, 0)),
            pl.BlockSpec((1, LANES), lambda i: (0, 0)),
            pl.BlockSpec(pq.shape, lambda i: (0, 0, 0)),
            pl.BlockSpec(pk.shape, lambda i: (0, 0, 0)),
        ],
        out_specs=[
            pl.BlockSpec((tm, qw), lambda i: (i, 0)),
            pl.BlockSpec((tm, qw), lambda i: (i, 0)),
            pl.BlockSpec((tm, FOX_WIDTH), lambda i: (i, 0)),
            pl.BlockSpec((tm, POOL_WIDTH), lambda i: (i, 0)),
        ],
        out_shape=[
            jax.ShapeDtypeStruct((t, qw), BF16),
            jax.ShapeDtypeStruct((t, qw), BF16),
            jax.ShapeDtypeStruct((t, FOX_WIDTH), BF16),
            jax.ShapeDtypeStruct((t, POOL_WIDTH), F32),
        ],
        scratch_shapes=[pltpu.VMEM((SUBLANES, LANES), F32)],
        compiler_params=_params("arbitrary"),
        name="even_inproj",
    )(x2d, gain, w_aug, bias, pq, pk)


def _attn_kernel(q_ref, k_ref, v_ref, o_ref, m_ref, l_ref, acc_ref):
    qi = pl.program_id(2)
    ki = pl.program_id(3)
    tq = q_ref.shape[0]
    tk = k_ref.shape[0]

    @pl.when(ki == 0)
    def _():
        m_ref[...] = jnp.full_like(m_ref, NEG_INF)
        l_ref[...] = jnp.zeros_like(l_ref)
        acc_ref[...] = jnp.zeros_like(acc_ref)

    def step(masked):
        v = v_ref[...]
        for hh in range(2):
            sl = slice(hh * LANES, (hh + 1) * LANES)
            s = lax.dot_general(q_ref[:, sl], k_ref[:, sl],
                                (((1,), (1,)), ((), ())),
                                preferred_element_type=F32)
            if masked:
                row = lax.broadcasted_iota(jnp.int32, (tq, tk), 0)
                col = lax.broadcasted_iota(jnp.int32, (tq, tk), 1)
                s = jnp.where(col <= row, s, NEG_INF)
            m_prev = m_ref[hh]
            m_new = jnp.maximum(m_prev, jnp.max(s, axis=1, keepdims=True))
            alpha = jnp.exp(m_prev - m_new)
            p = jnp.exp(s - m_new)
            l_ref[hh] = alpha * l_ref[hh] + jnp.sum(p, axis=1, keepdims=True)
            acc_ref[hh] = alpha * acc_ref[hh] + jnp.dot(
                p.astype(BF16), v, preferred_element_type=F32)
            m_ref[hh] = m_new

    @pl.when(ki < qi)
    def _():
        step(False)

    @pl.when(ki == qi)
    def _():
        step(True)
        lane = lax.broadcasted_iota(jnp.int32, (tq, LANES), 1)
        o = jnp.where(lane < FOX_HEAD_DIM,
                      acc_ref[0] / l_ref[0], acc_ref[1] / l_ref[1])
        o_ref[...] = o.astype(o_ref.dtype)


def _attention(q_aug, k_aug, v, *, batch, seq, tq):
    t = q_aug.shape[0]
    nq = seq // tq
    pairs = FOX_HEADS // 2

    def q_map(b, hp, qi, ki):
        return (b * nq + qi, hp)

    def kv_map(b, hp, qi, ki):
        return (b * nq + jnp.minimum(ki, qi), hp)

    return pl.pallas_call(
        _attn_kernel,
        grid=(batch, pairs, nq, nq),
        in_specs=[
            pl.BlockSpec((tq, 2 * LANES), q_map),
            pl.BlockSpec((tq, 2 * LANES), kv_map),
            pl.BlockSpec((tq, LANES), kv_map),
        ],
        out_specs=pl.BlockSpec((tq, LANES), q_map),
        out_shape=jax.ShapeDtypeStruct((t, FOX_WIDTH), BF16),
        scratch_shapes=[
            pltpu.VMEM((2, tq, 1), F32),
            pltpu.VMEM((2, tq, 1), F32),
            pltpu.VMEM((2, tq, LANES), F32),
        ],
        compiler_params=_params("parallel", "parallel", "parallel", "arbitrary"),
        name="fox_attention",
    )(q_aug, k_aug, v)


def _pool_outproj_kernel(x_ref, att_ref, p_ref, halo_ref, wpool_ref, scale_ref,
                         wo_ref, o_ref, *, tiles_per_seq):
    i = pl.program_id(0)
    tm = x_ref.shape[0]
    tile_in_seq = i % tiles_per_seq
    p = p_ref[...]
    halo = jnp.where(tile_in_seq == 0, 0.0, halo_ref[...])
    ext = jnp.concatenate([halo, p], axis=0)
    pos = tile_in_seq * tm + lax.broadcasted_iota(jnp.int32, (tm, 1), 0)
    count = (pos + 1).astype(F32)
    mixed = []
    for gi, w in enumerate(POOL_WINDOWS):
        sl = slice(gi * POOL_GROUP_DIM, (gi + 1) * POOL_GROUP_DIM)
        acc = ext[:, sl]
        sh = 1
        while sh < w:
            acc = acc + pltpu.roll(acc, sh, axis=0)
            sh *= 2
        mean = acc[POOL_HALO:, :] / jnp.minimum(count, float(w))
        pooled = (mean - p[:, sl]).astype(BF16)
        mixed.append(jnp.dot(pooled, wpool_ref[gi], preferred_element_type=F32))
    pool = (jnp.concatenate(mixed, axis=1) * scale_ref[...]).astype(BF16)
    y = jnp.dot(att_ref[...], wo_ref[0:FOX_WIDTH, :], preferred_element_type=F32)
    y = y + jnp.dot(pool, wo_ref[FOX_WIDTH:, :], preferred_element_type=F32)
    o_ref[...] = x_ref[...] + y


def _pool_outproj(x2d, att, p_in, w_pool, pool_scale, w_out, *, seq, tm):
    t, d = x2d.shape
    halo_blocks = tm // POOL_HALO
    kern = functools.partial(_pool_outproj_kernel, tiles_per_seq=seq // tm)
    return pl.pallas_call(
        kern,
        grid=(t // tm,),
        in_specs=[
            pl.BlockSpec((tm, d), lambda i: (i, 0)),
            pl.BlockSpec((tm, FOX_WIDTH), lambda i: (i, 0)),
            pl.BlockSpec((tm, POOL_WIDTH), lambda i: (i, 0)),
            pl.BlockSpec((POOL_HALO, POOL_WIDTH),
                         lambda i: (jnp.maximum(i * halo_blocks - 1, 0), 0)),
            pl.BlockSpec(w_pool.shape, lambda i: (0, 0, 0)),
            pl.BlockSpec((1, POOL_WIDTH), lambda i: (0, 0)),
            pl.BlockSpec(w_out.shape, lambda i: (0, 0)),
        ],
        out_specs=pl.BlockSpec((tm, d), lambda i: (i, 0)),
        out_shape=jax.ShapeDtypeStruct((t, d), F32),
        compiler_params=_params("parallel"),
        name="pool_outproj",
    )(x2d, att, p_in, p_in, w_pool, pool_scale, w_out)


def _ffn_kernel(*refs, gated, final_norm):
    x_ref, g_ref, w1_ref, w3_ref, w2_ref = refs[:5]
    rest = list(refs[5:])
    gate_ref = rest.pop(0) if gated else None
    fg_ref = rest.pop(0) if final_norm else None
    o_ref, h_ref, acc_ref = rest
    e = pl.program_id(1)
    j = pl.program_id(2)
    first = (e == 0) & (j == 0)
    last = (e == pl.num_programs(1) - 1) & (j == pl.num_programs(2) - 1)

    @pl.when(first)
    def _():
        x = x_ref[...]
        h_ref[...] = _rms(x, g_ref[...]).astype(BF16)
        acc_ref[...] = x

    h = h_ref[...]
    a = jnp.dot(h, w1_ref[...], preferred_element_type=F32)
    b = jnp.dot(h, w3_ref[...], preferred_element_type=F32)
    act = a * _sigmoid(a) * b
    if gated:
        act = act * gate_ref[...]
    acc_ref[...] += jnp.dot(act.astype(BF16), w2_ref[...],
                            preferred_element_type=F32)

    @pl.when(last)
    def _():
        y = acc_ref[...]
        if final_norm:
            y = _rms(y, fg_ref[...])
        o_ref[...] = y


def _ffn(x2d, gain, w1, w3, w2, gates=None, final_gain=None, *, tm, tf):
    t, d = x2d.shape
    n_e, _, f = w1.shape
    gated = gates is not None
    final_norm = final_gain is not None
    in_specs = [
        pl.BlockSpec((tm, d), lambda i, e, j: (i, 0)),
        pl.BlockSpec((1, d), lambda i, e, j: (0, 0)),
        pl.BlockSpec((None, d, tf), lambda i, e, j: (e, 0, j)),
        pl.BlockSpec((None, d, tf), lambda i, e, j: (e, 0, j)),
        pl.BlockSpec((None, tf, d), lambda i, e, j: (e, j, 0)),
    ]
    args = [x2d, gain, w1, w3, w2]
    if gated:
        in_specs.append(pl.BlockSpec((None, tm, 1), lambda i, e, j: (e, i, 0)))
        args.append(gates)
    if final_norm:
        in_specs.append(pl.BlockSpec((1, d), lambda i, e, j: (0, 0)))
        args.append(final_gain)
    kern = functools.partial(_ffn_kernel, gated=gated, final_norm=final_norm)
    return pl.pallas_call(
        kern,
        grid=(t // tm, n_e, f // tf),
        in_specs=in_specs,
        out_specs=pl.BlockSpec((tm, d), lambda i, e, j: (i, 0)),
        out_shape=jax.ShapeDtypeStruct((t, d), F32),
        scratch_shapes=[pltpu.VMEM((tm, d), BF16), pltpu.VMEM((tm, d), F32)],
        compiler_params=_params("parallel", "arbitrary", "arbitrary"),
        name="moe_ffn" if gated else "dense_ffn",
    )(*args)


def _gelu_tanh(x):
    c = math.sqrt(2.0 / math.pi)
    return 0.5 * x * (1.0 + jnp.tanh(c * (x + 0.044715 * (x * x * x))))


def _s5_kernel(x_ref, g_ref, win_ref, perm_ref, permt_ref, bblk_ref, cblk_ref,
               are_ref, aim_ref, d_ref, o_ref, xs_ref, st_ref, *, slabs_per_loop):
    nb, tc, d = x_ref.shape
    rows = nb * tc
    n_slabs = d // LANES
    sw = 2 * SSM_SLAB_STATE

    @pl.when(pl.program_id(0) == 0)
    def _():
        st_ref[...] = jnp.zeros_like(st_ref)

    x = x_ref[...].reshape(rows, d)
    h = _rms(x, g_ref[...]).astype(BF16)
    h_tb = jnp.dot(perm_ref[...], h, preferred_element_type=F32).astype(BF16)
    u = jnp.dot(h_tb, win_ref[...], preferred_element_type=F32)
    ub = u.astype(BF16)
    for s in range(n_slabs):
        xs_ref[:, s * sw:(s + 1) * sw] = jnp.dot(
            ub[:, s * LANES:(s + 1) * LANES], bblk_ref[s],
            preferred_element_type=F32)

    for s0 in range(0, n_slabs, slabs_per_loop):
        lo = s0 * sw
        width = slabs_per_loop * sw
        ar = are_ref[:, lo:lo + width]
        ai = aim_ref[:, lo:lo + width]

        def body(t, carry, lo=lo, width=width, ar=ar, ai=ai):
            xr, xi = carry
            r0 = pl.multiple_of(t * SUBLANES, SUBLANES)
            bu = xs_ref[pl.ds(r0, SUBLANES), lo:lo + width]
            new_r, new_i = [], []
            for k in range(slabs_per_loop):
                hs = SSM_SLAB_STATE
                a_r = ar[:, k * sw:k * sw + hs]
                a_i = ai[:, k * sw:k * sw + hs]
                x_r = xr[:, k * hs:(k + 1) * hs]
                x_i = xi[:, k * hs:(k + 1) * hs]
                new_r.append(a_r * x_r - a_i * x_i + bu[:, k * sw:k * sw + hs])
                new_i.append(a_r * x_i + a_i * x_r + bu[:, k * sw + hs:(k + 1) * sw])
            for k in range(slabs_per_loop):
                hs = SSM_SLAB_STATE
                xs_ref[pl.ds(r0, SUBLANES), lo + k * sw:lo + k * sw + hs] = new_r[k]
                xs_ref[pl.ds(r0, SUBLANES), lo + k * sw + hs:lo + (k + 1) * sw] = new_i[k]
            return (jnp.concatenate(new_r, axis=1), jnp.concatenate(new_i, axis=1))

        init_r = jnp.concatenate(
            [st_ref[:, lo + k * sw:lo + k * sw + SSM_SLAB_STATE]
             for k in range(slabs_per_loop)], axis=1)
        init_i = jnp.concatenate(
            [st_ref[:, lo + k * sw + SSM_SLAB_STATE:lo + (k + 1) * sw]
             for k in range(slabs_per_loop)], axis=1)
        lax.fori_loop(0, tc, body, (init_r, init_i))
    st_ref[...] = xs_ref[rows - SUBLANES:rows, :]

    ys = []
    for s in range(n_slabs):
        ys.append(jnp.dot(xs_ref[:, s * sw:(s + 1) * sw].astype(BF16), cblk_ref[s],
                          preferred_element_type=F32))
    y = jnp.concatenate(ys, axis=1) + d_ref[...] * u
    g = _gelu_tanh(y).astype(BF16)
    g_bt = jnp.dot(permt_ref[...], g, preferred_element_type=F32)
    o_ref[...] = g_bt.reshape(nb, tc, d).astype(o_ref.dtype)


def _s5_mixer(x3d, gain, w_in, bblk, cblk, a_re, a_im, d_skip, *, tc):
    nb, seq, d = x3d.shape
    rows = nb * tc
    n_slabs = d // LANES
    sw = 2 * SSM_SLAB_STATE
    r = np.arange(rows)
    perm = np.zeros((rows, rows), np.float32)
    perm[(r % tc) * nb + r // tc, r] = 1.0
    perm_j = jnp.asarray(perm, BF16)
    permt_j = jnp.asarray(perm.T, BF16)
    kern = functools.partial(_s5_kernel, slabs_per_loop=2)
    full2 = lambda a: pl.BlockSpec(a.shape, lambda i: (0, 0))
    full3 = lambda a: pl.BlockSpec(a.shape, lambda i: (0, 0, 0))
    return pl.pallas_call(
        kern,
        grid=(seq // tc,),
        in_specs=[
            pl.BlockSpec((nb, tc, d), lambda i: (0, i, 0)),
            full2(gain), full2(w_in), full2(perm_j), full2(permt_j),
            full3(bblk), full3(cblk), full2(a_re), full2(a_im), full2(d_skip),
        ],
        out_specs=pl.BlockSpec((nb, tc, d), lambda i: (0, i, 0)),
        out_shape=jax.ShapeDtypeStruct((nb, seq, d), BF16),
        scratch_shapes=[
            pltpu.VMEM((rows, n_slabs * sw), F32),
            pltpu.VMEM((SUBLANES, n_slabs * sw), F32),
        ],
        compiler_params=_params("arbitrary"),
        name="s5_mixer",
    )(x3d, gain, w_in, perm_j, permt_j, bblk, cblk, a_re, a_im, d_skip)


def _s5_coefficients(a_re, a_im, log_dt, b_re, b_im, c_re, c_im):
    dt = jnp.exp(log_dt.astype(F32))[:, None]
    ar = a_re.astype(F32)
    ai = a_im.astype(F32)
    mag = jnp.exp(ar * dt)
    abar_re = mag * jnp.cos(ai * dt)
    abar_im = mag * jnp.sin(ai * dt)
    den = ar * ar + ai * ai
    nr = abar_re - 1.0
    ni = abar_im
    coef_re = (nr * ar + ni * ai) / den
    coef_im = (ni * ar - nr * ai) / den
    br = b_re.astype(F32)
    bi = b_im.astype(F32)
    bbar_re = coef_re[..., None] * br - coef_im[..., None] * bi
    bbar_im = coef_re[..., None] * bi + coef_im[..., None] * br
    n_groups = ar.shape[0]
    n_slabs = n_groups // SSM_SLAB_GROUPS
    eye = jnp.eye(SSM_SLAB_GROUPS, dtype=F32)

    def in_block(bb):
        bb = bb.reshape(n_slabs, SSM_SLAB_GROUPS, SSM_STATE, SSM_GROUP)
        blk = jnp.einsum('sgph,gk->sghkp', bb, eye)
        return blk.reshape(n_slabs, LANES, SSM_SLAB_STATE)

    def out_block(cc):
        cc = cc.reshape(n_slabs, SSM_SLAB_GROUPS, SSM_GROUP, SSM_STATE)
        blk = jnp.einsum('sghp,gk->sgpkh', cc, eye)
        return blk.reshape(n_slabs, SSM_SLAB_STATE, LANES)

    bblk = jnp.concatenate([in_block(bbar_re), in_block(bbar_im)], axis=2)
    cblk = jnp.concatenate([out_block(c_re.astype(F32)),
                            -out_block(c_im.astype(F32))], axis=1)

    def lanes(a):
        a = a.reshape(n_slabs, 1, SSM_SLAB_STATE)
        a = jnp.concatenate([a, a], axis=2).reshape(1, -1)
        return jnp.broadcast_to(a, (SUBLANES, a.shape[1]))

    return (bblk.astype(BF16), cblk.astype(BF16), lanes(abar_re), lanes(abar_im))


def _glu_kernel(x_ref, g_ref, wa_ref, wb_ref, o_ref):
    g = g_ref[...]
    a = jnp.dot(g, wa_ref[...], preferred_element_type=F32)
    b = jnp.dot(g, wb_ref[...], preferred_element_type=F32)
    o_ref[...] = x_ref[...] + a * _sigmoid(b)


def _glu(x2d, g2d, wa, wb, *, tm):
    t, d = x2d.shape
    return pl.pallas_call(
        _glu_kernel,
        grid=(t // tm,),
        in_specs=[
            pl.BlockSpec((tm, d), lambda i: (i, 0)),
            pl.BlockSpec((tm, d), lambda i: (i, 0)),
            pl.BlockSpec(wa.shape, lambda i: (0, 0)),
            pl.BlockSpec(wb.shape, lambda i: (0, 0)),
        ],
        out_specs=pl.BlockSpec((tm, d), lambda i: (i, 0)),
        out_shape=jax.ShapeDtypeStruct((t, d), F32),
        compiler_params=_params("parallel"),
        name="glu_outproj",
    )(x2d, g2d, wa, wb)


def _router_kernel(x_ref, g_ref, w_ref, b_ref, o_ref):
    h = _rms(x_ref[...], g_ref[...])
    logits = jnp.dot(h, w_ref[...], preferred_element_type=F32,
                     precision=lax.Precision.HIGHEST) + b_ref[...]
    lane = lax.broadcasted_iota(jnp.int32, logits.shape, 1)
    logits = jnp.where(lane < N_EXPERTS, logits, -jnp.inf)
    m1 = jnp.max(logits, axis=1, keepdims=True)
    i1 = jnp.min(jnp.where(logits == m1, lane, LANES), axis=1, keepdims=True)
    rest = jnp.where(lane == i1, -jnp.inf, logits)
    m2 = jnp.max(rest, axis=1, keepdims=True)
    i2 = jnp.min(jnp.where(rest == m2, lane, LANES), axis=1, keepdims=True)
    e2 = jnp.exp(m2 - m1)
    g1 = 1.0 / (1.0 + e2)
    g2 = e2 / (1.0 + e2)
    o_ref[...] = jnp.where(lane == i1, g1, 0.0) + jnp.where(lane == i2, g2, 0.0)


def _router(x2d, gain, w_pad, b_pad, *, tm):
    t, d = x2d.shape
    return pl.pallas_call(
        _router_kernel,
        grid=(t // tm,),
        in_specs=[
            pl.BlockSpec((tm, d), lambda i: (i, 0)),
            pl.BlockSpec((1, d), lambda i: (0, 0)),
            pl.BlockSpec((d, LANES), lambda i: (0, 0)),
            pl.BlockSpec((1, LANES), lambda i: (0, 0)),
        ],
        out_specs=pl.BlockSpec((tm, LANES), lambda i: (i, 0)),
        out_shape=jax.ShapeDtypeStruct((t, LANES), F32),
        compiler_params=_params("parallel"),
        name="router",
    )(x2d, gain, w_pad, b_pad)


def _even_weights(w_in, b_forget):
    d = w_in.shape[0]
    fw = FOX_WIDTH

    def heads(w, scale):
        w = (w * scale).reshape(d, FOX_HEADS, FOX_HEAD_DIM)
        w = jnp.pad(w, ((0, 0), (0, 0), (0, LANES - FOX_HEAD_DIM)))
        return w.reshape(d, FOX_HEADS * LANES)

    wq = heads(w_in[:, :fw], FOX_HEAD_DIM ** -0.5)
    wk = heads(w_in[:, fw:2 * fw], 1.0)
    wv = w_in[:, 2 * fw:3 * fw]
    wf = jnp.pad(w_in[:, 3 * fw:3 * fw + FOX_HEADS],
                 ((0, 0), (0, LANES - FOX_HEADS)))
    wp = w_in[:, 3 * fw + FOX_HEADS:]
    w_aug = jnp.concatenate([wq, wk, wv, wp, wf], axis=1).astype(BF16)
    bias = jnp.pad(b_forget.astype(F32), (0, LANES - FOX_HEADS)).reshape(1, LANES)
    pq = np.zeros((FOX_HEADS, 3 * LANES, LANES), np.float32)
    pk = np.zeros((FOX_HEADS, 3 * LANES, LANES), np.float32)
    for hh in range(FOX_HEADS):
        for piece in range(3):
            pq[hh, piece * LANES + hh, BIAS_LANE + piece] = 1.0
            pk[hh, piece * LANES + hh, ONES_LANE + piece] = -1.0
    return w_aug, bias, jnp.asarray(pq, BF16), jnp.asarray(pk, BF16)


def kernel(x, even_mix_norm, even_w_in, even_b_forget, even_w_pool, even_pool_scale, even_w_out, even_ffn_norm, even_ffn_w1, even_ffn_w3, even_ffn_w2, odd_mix_norm, odd_w_in, ssm_a_re, ssm_a_im, ssm_log_dt, ssm_b_re, ssm_b_im, ssm_c_re, ssm_c_im, ssm_d, odd_w_glu_a, odd_w_glu_b, odd_moe_norm, router_w, router_b, expert_w1, expert_w3, expert_w2, final_norm):
    b, s, d = x.shape
    t = b * s
    assert b == SUBLANES, "the S5 recurrence keeps one batch row per sublane"
    x2d = x.reshape(t, d)
    row = lambda v: v.reshape(1, -1).astype(F32)

    w_aug, bias, pq, pk = _even_weights(even_w_in[0], even_b_forget[0])
    q_aug, k_aug, v, p_in = _even_inproj(
        x2d, row(even_mix_norm[0]), w_aug, bias, pq, pk, seq=s, tm=512)
    att = _attention(q_aug, k_aug, v, batch=b, seq=s, tq=512)
    x1 = _pool_outproj(x2d, att, p_in, even_w_pool[0].astype(BF16),
                       row(even_pool_scale[0]), even_w_out[0].astype(BF16),
                       seq=s, tm=512)
    x2 = _ffn(x1, row(even_ffn_norm[0]), even_ffn_w1.astype(BF16),
              even_ffn_w3.astype(BF16), even_ffn_w2.astype(BF16), tm=512, tf=1408)

    bblk, cblk, a_re, a_im = _s5_coefficients(
        ssm_a_re[0], ssm_a_im[0], ssm_log_dt[0], ssm_b_re[0], ssm_b_im[0],
        ssm_c_re[0], ssm_c_im[0])
    g = _s5_mixer(x2.reshape(b, s, d), row(odd_mix_norm[0]),
                  odd_w_in[0].astype(BF16), bblk, cblk, a_re, a_im,
                  row(ssm_d[0]), tc=32)
    x3 = _glu(x2, g.reshape(t, d), odd_w_glu_a[0].astype(BF16),
              odd_w_glu_b[0].astype(BF16), tm=512)
    rw = jnp.pad(router_w[0].astype(F32), ((0, 0), (0, LANES - N_EXPERTS)))
    rb = jnp.pad(router_b[0].astype(F32), (0, LANES - N_EXPERTS)).reshape(1, LANES)
    combine = _router(x3, row(odd_moe_norm[0]), rw, rb, tm=512)
    gates = combine[:, :N_EXPERTS].T.reshape(N_EXPERTS, t, 1)
    out = _ffn(x3, row(odd_moe_norm[0]), expert_w1[0].astype(BF16),
               expert_w3[0].astype(BF16), expert_w2[0].astype(BF16),
               gates=gates, final_gain=row(final_norm), tm=512, tf=1408)
    return out.reshape(b, s, d)
```

```python
import functools
import math

import numpy as np
import jax
import jax.numpy as jnp
from jax import lax
from jax.experimental import pallas as pl
from jax.experimental.pallas import tpu as pltpu

F32 = jnp.float32
BF16 = jnp.bfloat16

EPS = 1e-6
NEG_INF = -1e30
LANES = 128
SUBLANES = 8
VMEM_LIMIT = 56 * 1024 * 1024

FOX_HEADS = 8
FOX_HEAD_DIM = 64
FOX_WIDTH = FOX_HEADS * FOX_HEAD_DIM
POOL_WINDOWS = (2, 4, 8, 16)
POOL_GROUP_DIM = 128
POOL_WIDTH = len(POOL_WINDOWS) * POOL_GROUP_DIM
POOL_HALO = 16
SSM_GROUP = 16
SSM_STATE = 64
SSM_SLAB_GROUPS = LANES // SSM_GROUP
SSM_SLAB_STATE = SSM_SLAB_GROUPS * SSM_STATE
N_EXPERTS = 8

BIAS_LANE = FOX_HEAD_DIM
ONES_LANE = FOX_HEAD_DIM + 3
DENOM_LANE = FOX_HEAD_DIM


def _params(*sem):
    return pltpu.CompilerParams(dimension_semantics=sem,
                                vmem_limit_bytes=VMEM_LIMIT)


def _rms(x, g):
    ms = jnp.mean(x * x, axis=-1, keepdims=True)
    return x * lax.rsqrt(ms + EPS) * g


def _sigmoid(x):
    return 1.0 / (1.0 + jnp.exp(-x))


def _lane_range_ones(lo, hi):
    lane = lax.broadcasted_iota(jnp.int32, (1, LANES), 1)
    return jnp.where((lane >= lo) & (lane < hi), 1.0, 0.0).astype(F32)


def _even_inproj_kernel(x_ref, g_ref, w_ref, bias_ref, place_ref,
                        q_ref, k_ref, v_ref, p_ref, carry_ref, *, tiles_per_seq):
    i = pl.program_id(0)
    tm = x_ref.shape[0]
    fw = FOX_WIDTH
    qw = FOX_HEADS * LANES
    h = _rms(x_ref[...], g_ref[...]).astype(BF16)
    z = jnp.dot(h, w_ref[...], preferred_element_type=F32)
    p_ref[...] = z[:, 3 * fw:3 * fw + POOL_WIDTH]
    fg = z[:, 3 * fw + POOL_WIDTH:] + bias_ref[...]
    lf = jnp.minimum(fg, 0.0) - jnp.log1p(jnp.exp(-jnp.abs(fg)))
    row = lax.broadcasted_iota(jnp.int32, lf.shape, 0)
    c = lf
    sh = 1
    while sh < tm:
        c = c + jnp.where(row >= sh, pltpu.roll(c, sh, axis=0), 0.0)
        sh *= 2

    @pl.when(i % tiles_per_seq == 0)
    def _():
        carry_ref[...] = jnp.zeros_like(carry_ref)

    c = c + carry_ref[0:1, :]
    carry_ref[...] = jnp.broadcast_to(c[tm - 1:tm, :], carry_ref.shape)
    hi = c.astype(BF16).astype(F32)
    r1 = c - hi
    mid = r1.astype(BF16).astype(F32)
    lo = r1 - mid
    lane = lax.broadcasted_iota(jnp.int32, (tm, LANES), 1)
    packed = jnp.where(lane < FOX_HEADS, hi,
                       jnp.where(lane < 2 * FOX_HEADS,
                                 pltpu.roll(mid, FOX_HEADS, axis=1),
                                 pltpu.roll(lo, 2 * FOX_HEADS, axis=1)))
    placed = jnp.dot(packed.astype(BF16), place_ref[...],
                     preferred_element_type=F32)
    low = lane < FOX_HEAD_DIM
    ones_q = _lane_range_ones(ONES_LANE, ONES_LANE + 3)
    ones_k = _lane_range_ones(BIAS_LANE, BIAS_LANE + 3)
    ones_v = _lane_range_ones(DENOM_LANE, DENOM_LANE + 1)

    def head_lanes(base, hh):
        pair = z[:, base + (hh // 2) * LANES:base + (hh // 2 + 1) * LANES]
        return pltpu.roll(pair, FOX_HEAD_DIM, axis=1) if hh % 2 else pair

    for hh in range(FOX_HEADS):
        sl = slice(hh * LANES, (hh + 1) * LANES)
        q_ref[:, sl] = jnp.where(low, head_lanes(0, hh),
                                 placed[:, sl] + ones_q).astype(BF16)
        k_ref[:, sl] = jnp.where(
            low, head_lanes(fw, hh),
            placed[:, qw + hh * LANES:qw + (hh + 1) * LANES] + ones_k).astype(BF16)
        v_ref[:, sl] = jnp.where(low, head_lanes(2 * fw, hh), ones_v).astype(BF16)


def _even_inproj(x2d, gain, w_all, bias, place, *, seq, tm):
    t, d = x2d.shape
    n = w_all.shape[1]
    qw = FOX_HEADS * LANES
    kern = functools.partial(_even_inproj_kernel, tiles_per_seq=seq // tm)
    return pl.pallas_call(
        kern,
        grid=(t // tm,),
        in_specs=[
            pl.BlockSpec((tm, d), lambda i: (i, 0)),
            pl.BlockSpec((1, d), lambda i: (0, 0)),
            pl.BlockSpec((d, n), lambda i: (0, 0)),
            pl.BlockSpec((1, LANES), lambda i: (0, 0)),
            pl.BlockSpec(place.shape, lambda i: (0, 0)),
        ],
        out_specs=[
            pl.BlockSpec((tm, qw), lambda i: (i, 0)),
            pl.BlockSpec((tm, qw), lambda i: (i, 0)),
            pl.BlockSpec((tm, qw), lambda i: (i, 0)),
            pl.BlockSpec((tm, POOL_WIDTH), lambda i: (i, 0)),
        ],
        out_shape=[
            jax.ShapeDtypeStruct((t, qw), BF16),
            jax.ShapeDtypeStruct((t, qw), BF16),
            jax.ShapeDtypeStruct((t, qw), BF16),
            jax.ShapeDtypeStruct((t, POOL_WIDTH), F32),
        ],
        scratch_shapes=[pltpu.VMEM((SUBLANES, LANES), F32)],
        compiler_params=_params("arbitrary"),
        name="even_inproj",
    )(x2d, gain, w_all, bias, place)


def _dot_nt(a, b):
    return lax.dot_general(a, b, (((1,), (1,)), ((), ())),
                           preferred_element_type=F32)


def _attn_kernel(q_ref, k_ref, v_ref, o_ref, *, tq):
    seq = q_ref.shape[0]
    row = lax.broadcasted_iota(jnp.int32, (tq, tq), 0)
    col = lax.broadcasted_iota(jnp.int32, (tq, tq), 1)
    causal = col <= row
    for qi in range(seq // tq):
        r0 = qi * tq
        q = q_ref[r0:r0 + tq, :]
        s_diag = jnp.where(causal, _dot_nt(q, k_ref[r0:r0 + tq, :]), NEG_INF)
        m = jnp.max(s_diag, axis=1, keepdims=True)
        if qi > 0:
            s_past = _dot_nt(q, k_ref[0:r0, :])
            m = jnp.maximum(m, jnp.max(s_past, axis=1, keepdims=True))
        acc = jnp.dot(jnp.exp(s_diag - m).astype(BF16), v_ref[r0:r0 + tq, :],
                      preferred_element_type=F32)
        if qi > 0:
            acc = acc + jnp.dot(jnp.exp(s_past - m).astype(BF16), v_ref[0:r0, :],
                                preferred_element_type=F32)
        o_ref[r0:r0 + tq, :] = (
            acc / acc[:, DENOM_LANE:DENOM_LANE + 1]).astype(o_ref.dtype)


def _attention(q_aug, k_aug, v_aug, *, batch, seq, tq):
    t = q_aug.shape[0]
    spec = pl.BlockSpec((seq, LANES), lambda b, h: (b, h))
    return pl.pallas_call(
        functools.partial(_attn_kernel, tq=tq),
        grid=(batch, FOX_HEADS),
        in_specs=[spec, spec, spec],
        out_specs=spec,
        out_shape=jax.ShapeDtypeStruct((t, FOX_HEADS * LANES), BF16),
        compiler_params=_params("parallel", "parallel"),
        name="fox_attention",
    )(q_aug, k_aug, v_aug)


def _pool_outproj_kernel(x_ref, att_ref, p_ref, halo_ref, wpool_ref, scale_ref,
                         wo_att_ref, wo_pool_ref, o_ref, *, tiles_per_seq):
    i = pl.program_id(0)
    tm = x_ref.shape[0]
    tile_in_seq = i % tiles_per_seq
    p = p_ref[...]
    halo = jnp.where(tile_in_seq == 0, 0.0, halo_ref[...])
    ext = jnp.concatenate([halo, p], axis=0)
    pos = tile_in_seq * tm + lax.broadcasted_iota(jnp.int32, (tm, 1), 0)
    count = (pos + 1).astype(F32)
    mixed = []
    for gi, w in enumerate(POOL_WINDOWS):
        sl = slice(gi * POOL_GROUP_DIM, (gi + 1) * POOL_GROUP_DIM)
        acc = ext[:, sl]
        sh = 1
        while sh < w:
            acc = acc + pltpu.roll(acc, sh, axis=0)
            sh *= 2
        mean = acc[POOL_HALO:, :] / jnp.minimum(count, float(w))
        pooled = (mean - p[:, sl]).astype(BF16)
        mixed.append(jnp.dot(pooled, wpool_ref[gi], preferred_element_type=F32))
    pool = (jnp.concatenate(mixed, axis=1) * scale_ref[...]).astype(BF16)
    y = jnp.dot(att_ref[...], wo_att_ref[...], preferred_element_type=F32)
    y = y + jnp.dot(pool, wo_pool_ref[...], preferred_element_type=F32)
    o_ref[...] = x_ref[...] + y


def _pool_outproj(x2d, att, p_in, w_pool, pool_scale, w_out, *, seq, tm):
    t, d = x2d.shape
    wo_att = w_out[:FOX_WIDTH].reshape(FOX_HEADS, FOX_HEAD_DIM, d)
    wo_att = jnp.pad(wo_att, ((0, 0), (0, LANES - FOX_HEAD_DIM), (0, 0)))
    wo_att = wo_att.reshape(FOX_HEADS * LANES, d)
    wo_pool = w_out[FOX_WIDTH:]
    halo_blocks = tm // POOL_HALO
    kern = functools.partial(_pool_outproj_kernel, tiles_per_seq=seq // tm)
    return pl.pallas_call(
        kern,
        grid=(t // tm,),
        in_specs=[
            pl.BlockSpec((tm, d), lambda i: (i, 0)),
            pl.BlockSpec((tm, FOX_HEADS * LANES), lambda i: (i, 0)),
            pl.BlockSpec((tm, POOL_WIDTH), lambda i: (i, 0)),
            pl.BlockSpec((POOL_HALO, POOL_WIDTH),
                         lambda i: (jnp.maximum(i * halo_blocks - 1, 0), 0)),
            pl.BlockSpec(w_pool.shape, lambda i: (0, 0, 0)),
            pl.BlockSpec((1, POOL_WIDTH), lambda i: (0, 0)),
            pl.BlockSpec(wo_att.shape, lambda i: (0, 0)),
            pl.BlockSpec(wo_pool.shape, lambda i: (0, 0)),
        ],
        out_specs=pl.BlockSpec((tm, d), lambda i: (i, 0)),
        out_shape=jax.ShapeDtypeStruct((t, d), F32),
        compiler_params=_params("parallel"),
        name="pool_outproj",
    )(x2d, att, p_in, p_in, w_pool, pool_scale, wo_att, wo_pool)


def _ffn_kernel(*refs, gated, final_norm):
    x_ref, g_ref, w1_ref, w3_ref, w2_ref = refs[:5]
    rest = list(refs[5:])
    gate_ref = rest.pop(0) if gated else None
    fg_ref = rest.pop(0) if final_norm else None
    o_ref, h_ref, acc_ref = rest
    e = pl.program_id(1)
    j = pl.program_id(2)
    first = (e == 0) & (j == 0)
    last = (e == pl.num_programs(1) - 1) & (j == pl.num_programs(2) - 1)

    @pl.when(first)
    def _():
        x = x_ref[...]
        h_ref[...] = _rms(x, g_ref[...]).astype(BF16)
        acc_ref[...] = x

    h = h_ref[...]
    a = jnp.dot(h, w1_ref[...], preferred_element_type=F32)
    b = jnp.dot(h, w3_ref[...], preferred_element_type=F32)
    act = a * _sigmoid(a) * b
    if gated:
        act = act * gate_ref[...]
    acc_ref[...] += jnp.dot(act.astype(BF16), w2_ref[...],
                            preferred_element_type=F32)

    @pl.when(last)
    def _():
        y = acc_ref[...]
        if final_norm:
            y = _rms(y, fg_ref[...])
        o_ref[...] = y


def _ffn(x2d, gain, w1, w3, w2, gates=None, final_gain=None, *, tm, tf):
    t, d = x2d.shape
    n_e, _, f = w1.shape
    gated = gates is not None
    final_norm = final_gain is not None
    in_specs = [
        pl.BlockSpec((tm, d), lambda i, e, j: (i, 0)),
        pl.BlockSpec((1, d), lambda i, e, j: (0, 0)),
        pl.BlockSpec((None, d, tf), lambda i, e, j: (e, 0, j)),
        pl.BlockSpec((None, d, tf), lambda i, e, j: (e, 0, j)),
        pl.BlockSpec((None, tf, d), lambda i, e, j: (e, j, 0)),
    ]
    args = [x2d, gain, w1, w3, w2]
    if gated:
        in_specs.append(pl.BlockSpec((None, tm, 1), lambda i, e, j: (e, i, 0)))
        args.append(gates)
    if final_norm:
        in_specs.append(pl.BlockSpec((1, d), lambda i, e, j: (0, 0)))
        args.append(final_gain)
    kern = functools.partial(_ffn_kernel, gated=gated, final_norm=final_norm)
    return pl.pallas_call(
        kern,
        grid=(t // tm, n_e, f // tf),
        in_specs=in_specs,
        out_specs=pl.BlockSpec((tm, d), lambda i, e, j: (i, 0)),
        out_shape=jax.ShapeDtypeStruct((t, d), F32),
        scratch_shapes=[pltpu.VMEM((tm, d), BF16), pltpu.VMEM((tm, d), F32)],
        compiler_params=_params("parallel", "arbitrary", "arbitrary"),
        name="moe_ffn" if gated else "dense_ffn",
    )(*args)


def _gelu_tanh(x):
    c = math.sqrt(2.0 / math.pi)
    return 0.5 * x * (1.0 + jnp.tanh(c * (x + 0.044715 * (x * x * x))))


def _s5_kernel(x_ref, g_ref, win_ref, perm_ref, permt_ref, bblk_ref, cblk_ref,
               are_ref, aim_ref, d_ref, o_ref, xs_ref, st_ref, *, slabs_per_loop):
    nb, tc, d = x_ref.shape
    rows = nb * tc
    n_slabs = d // LANES
    sw = 2 * SSM_SLAB_STATE

    @pl.when(pl.program_id(0) == 0)
    def _():
        st_ref[...] = jnp.zeros_like(st_ref)

    x = x_ref[...].reshape(rows, d)
    h = _rms(x, g_ref[...]).astype(BF16)
    h_tb = jnp.dot(perm_ref[...], h, preferred_element_type=F32).astype(BF16)
    u = jnp.dot(h_tb, win_ref[...], preferred_element_type=F32)
    ub = u.astype(BF16)
    for s in range(n_slabs):
        xs_ref[:, s * sw:(s + 1) * sw] = jnp.dot(
            ub[:, s * LANES:(s + 1) * LANES], bblk_ref[s],
            preferred_element_type=F32)

    for s0 in range(0, n_slabs, slabs_per_loop):
        lo = s0 * sw
        width = slabs_per_loop * sw
        ar = are_ref[:, lo:lo + width]
        ai = aim_ref[:, lo:lo + width]

        def body(t, carry, lo=lo, width=width, ar=ar, ai=ai):
            xr, xi = carry
            r0 = pl.multiple_of(t * SUBLANES, SUBLANES)
            bu = xs_ref[pl.ds(r0, SUBLANES), lo:lo + width]
            new_r, new_i = [], []
            for k in range(slabs_per_loop):
                hs = SSM_SLAB_STATE
                a_r = ar[:, k * sw:k * sw + hs]
                a_i = ai[:, k * sw:k * sw + hs]
                x_r = xr[:, k * hs:(k + 1) * hs]
                x_i = xi[:, k * hs:(k + 1) * hs]
                new_r.append(a_r * x_r - a_i * x_i + bu[:, k * sw:k * sw + hs])
                new_i.append(a_r * x_i + a_i * x_r + bu[:, k * sw + hs:(k + 1) * sw])
            for k in range(slabs_per_loop):
                hs = SSM_SLAB_STATE
                xs_ref[pl.ds(r0, SUBLANES), lo + k * sw:lo + k * sw + hs] = new_r[k]
                xs_ref[pl.ds(r0, SUBLANES), lo + k * sw + hs:lo + (k + 1) * sw] = new_i[k]
            return (jnp.concatenate(new_r, axis=1), jnp.concatenate(new_i, axis=1))

        init_r = jnp.concatenate(
            [st_ref[:, lo + k * sw:lo + k * sw + SSM_SLAB_STATE]
             for k in range(slabs_per_loop)], axis=1)
        init_i = jnp.concatenate(
            [st_ref[:, lo + k * sw + SSM_SLAB_STATE:lo + (k + 1) * sw]
             for k in range(slabs_per_loop)], axis=1)
        lax.fori_loop(0, tc, body, (init_r, init_i))
    st_ref[...] = xs_ref[rows - SUBLANES:rows, :]

    ys = []
    for s in range(n_slabs):
        ys.append(jnp.dot(xs_ref[:, s * sw:(s + 1) * sw].astype(BF16), cblk_ref[s],
                          preferred_element_type=F32))
    y = jnp.concatenate(ys, axis=1) + d_ref[...] * u
    g = _gelu_tanh(y).astype(BF16)
    g_bt = jnp.dot(permt_ref[...], g, preferred_element_type=F32)
    o_ref[...] = g_bt.reshape(nb, tc, d).astype(o_ref.dtype)


def _s5_mixer(x3d, gain, w_in, bblk, cblk, a_re, a_im, d_skip, *, tc):
    nb, seq, d = x3d.shape
    rows = nb * tc
    n_slabs = d // LANES
    sw = 2 * SSM_SLAB_STATE
    r = np.arange(rows)
    perm = np.zeros((rows, rows), np.float32)
    perm[(r % tc) * nb + r // tc, r] = 1.0
    perm_j = jnp.asarray(perm, BF16)
    permt_j = jnp.asarray(perm.T, BF16)
    kern = functools.partial(_s5_kernel, slabs_per_loop=2)
    full2 = lambda a: pl.BlockSpec(a.shape, lambda i: (0, 0))
    full3 = lambda a: pl.BlockSpec(a.shape, lambda i: (0, 0, 0))
    return pl.pallas_call(
        kern,
        grid=(seq // tc,),
        in_specs=[
            pl.BlockSpec((nb, tc, d), lambda i: (0, i, 0)),
            full2(gain), full2(w_in), full2(perm_j), full2(permt_j),
            full3(bblk), full3(cblk), full2(a_re), full2(a_im), full2(d_skip),
        ],
        out_specs=pl.BlockSpec((nb, tc, d), lambda i: (0, i, 0)),
        out_shape=jax.ShapeDtypeStruct((nb, seq, d), BF16),
        scratch_shapes=[
            pltpu.VMEM((rows, n_slabs * sw), F32),
            pltpu.VMEM((SUBLANES, n_slabs * sw), F32),
        ],
        compiler_params=_params("arbitrary"),
        name="s5_mixer",
    )(x3d, gain, w_in, perm_j, permt_j, bblk, cblk, a_re, a_im, d_skip)


def _s5_coefficients(a_re, a_im, log_dt, b_re, b_im, c_re, c_im):
    dt = jnp.exp(log_dt.astype(F32))[:, None]
    ar = a_re.astype(F32)
    ai = a_im.astype(F32)
    mag = jnp.exp(ar * dt)
    abar_re = mag * jnp.cos(ai * dt)
    abar_im = mag * jnp.sin(ai * dt)
    den = ar * ar + ai * ai
    nr = abar_re - 1.0
    ni = abar_im
    coef_re = (nr * ar + ni * ai) / den
    coef_im = (ni * ar - nr * ai) / den
    br = b_re.astype(F32)
    bi = b_im.astype(F32)
    bbar_re = coef_re[..., None] * br - coef_im[..., None] * bi
    bbar_im = coef_re[..., None] * bi + coef_im[..., None] * br
    n_groups = ar.shape[0]
    n_slabs = n_groups // SSM_SLAB_GROUPS
    eye = jnp.eye(SSM_SLAB_GROUPS, dtype=F32)

    def in_block(bb):
        bb = bb.reshape(n_slabs, SSM_SLAB_GROUPS, SSM_STATE, SSM_GROUP)
        blk = jnp.einsum('sgph,gk->sghkp', bb, eye)
        return blk.reshape(n_slabs, LANES, SSM_SLAB_STATE)

    def out_block(cc):
        cc = cc.reshape(n_slabs, SSM_SLAB_GROUPS, SSM_GROUP, SSM_STATE)
        blk = jnp.einsum('sghp,gk->sgpkh', cc, eye)
        return blk.reshape(n_slabs, SSM_SLAB_STATE, LANES)

    bblk = jnp.concatenate([in_block(bbar_re), in_block(bbar_im)], axis=2)
    cblk = jnp.concatenate([out_block(c_re.astype(F32)),
                            -out_block(c_im.astype(F32))], axis=1)

    def lanes(a):
        a = a.reshape(n_slabs, 1, SSM_SLAB_STATE)
        a = jnp.concatenate([a, a], axis=2).reshape(1, -1)
        return jnp.broadcast_to(a, (SUBLANES, a.shape[1]))

    return (bblk.astype(BF16), cblk.astype(BF16), lanes(abar_re), lanes(abar_im))


def _glu_kernel(x_ref, g_ref, wa_ref, wb_ref, o_ref):
    g = g_ref[...]
    a = jnp.dot(g, wa_ref[...], preferred_element_type=F32)
    b = jnp.dot(g, wb_ref[...], preferred_element_type=F32)
    o_ref[...] = x_ref[...] + a * _sigmoid(b)


def _glu(x2d, g2d, wa, wb, *, tm):
    t, d = x2d.shape
    return pl.pallas_call(
        _glu_kernel,
        grid=(t // tm,),
        in_specs=[
            pl.BlockSpec((tm, d), lambda i: (i, 0)),
            pl.BlockSpec((tm, d), lambda i: (i, 0)),
            pl.BlockSpec(wa.shape, lambda i: (0, 0)),
            pl.BlockSpec(wb.shape, lambda i: (0, 0)),
        ],
        out_specs=pl.BlockSpec((tm, d), lambda i: (i, 0)),
        out_shape=jax.ShapeDtypeStruct((t, d), F32),
        compiler_params=_params("parallel"),
        name="glu_outproj",
    )(x2d, g2d, wa, wb)


def _router_kernel(x_ref, g_ref, w_ref, b_ref, o_ref):
    h = _rms(x_ref[...], g_ref[...])
    logits = jnp.dot(h, w_ref[...], preferred_element_type=F32,
                     precision=lax.Precision.HIGHEST) + b_ref[...]
    lane = lax.broadcasted_iota(jnp.int32, logits.shape, 1)
    logits = jnp.where(lane < N_EXPERTS, logits, -jnp.inf)
    m1 = jnp.max(logits, axis=1, keepdims=True)
    i1 = jnp.min(jnp.where(logits == m1, lane, LANES), axis=1, keepdims=True)
    rest = jnp.where(lane == i1, -jnp.inf, logits)
    m2 = jnp.max(rest, axis=1, keepdims=True)
    i2 = jnp.min(jnp.where(rest == m2, lane, LANES), axis=1, keepdims=True)
    e2 = jnp.exp(m2 - m1)
    g1 = 1.0 / (1.0 + e2)
    g2 = e2 / (1.0 + e2)
    o_ref[...] = jnp.where(lane == i1, g1, 0.0) + jnp.where(lane == i2, g2, 0.0)


def _router(x2d, gain, w_pad, b_pad, *, tm):
    t, d = x2d.shape
    return pl.pallas_call(
        _router_kernel,
        grid=(t // tm,),
        in_specs=[
            pl.BlockSpec((tm, d), lambda i: (i, 0)),
            pl.BlockSpec((1, d), lambda i: (0, 0)),
            pl.BlockSpec((d, LANES), lambda i: (0, 0)),
            pl.BlockSpec((1, LANES), lambda i: (0, 0)),
        ],
        out_specs=pl.BlockSpec((tm, LANES), lambda i: (i, 0)),
        out_shape=jax.ShapeDtypeStruct((t, LANES), F32),
        compiler_params=_params("parallel"),
        name="router",
    )(x2d, gain, w_pad, b_pad)


def _even_weights(w_in, b_forget):
    fw = FOX_WIDTH
    wq = w_in[:, :fw] * FOX_HEAD_DIM ** -0.5
    wf = jnp.pad(w_in[:, 3 * fw:3 * fw + FOX_HEADS],
                 ((0, 0), (0, LANES - FOX_HEADS)))
    wp = w_in[:, 3 * fw + FOX_HEADS:]
    w_all = jnp.concatenate([wq, w_in[:, fw:3 * fw], wp, wf], axis=1).astype(BF16)
    bias = jnp.pad(b_forget.astype(F32), (0, LANES - FOX_HEADS)).reshape(1, LANES)
    qw = FOX_HEADS * LANES
    place = np.zeros((LANES, 2 * qw), np.float32)
    for hh in range(FOX_HEADS):
        for piece in range(3):
            place[piece * FOX_HEADS + hh, hh * LANES + BIAS_LANE + piece] = 1.0
            place[piece * FOX_HEADS + hh, qw + hh * LANES + ONES_LANE + piece] = -1.0
    return w_all, bias, jnp.asarray(place, BF16)


def kernel(x, even_mix_norm, even_w_in, even_b_forget, even_w_pool, even_pool_scale, even_w_out, even_ffn_norm, even_ffn_w1, even_ffn_w3, even_ffn_w2, odd_mix_norm, odd_w_in, ssm_a_re, ssm_a_im, ssm_log_dt, ssm_b_re, ssm_b_im, ssm_c_re, ssm_c_im, ssm_d, odd_w_glu_a, odd_w_glu_b, odd_moe_norm, router_w, router_b, expert_w1, expert_w3, expert_w2, final_norm):
    b, s, d = x.shape
    t = b * s
    assert b == SUBLANES, "the S5 recurrence keeps one batch row per sublane"
    x2d = x.reshape(t, d)
    row = lambda v: v.reshape(1, -1).astype(F32)

    w_all, bias, place = _even_weights(even_w_in[0], even_b_forget[0])
    q_aug, k_aug, v, p_in = _even_inproj(
        x2d, row(even_mix_norm[0]), w_all, bias, place, seq=s, tm=512)
    att = _attention(q_aug, k_aug, v, batch=b, seq=s, tq=256)
    x1 = _pool_outproj(x2d, att, p_in, even_w_pool[0].astype(BF16),
                       row(even_pool_scale[0]), even_w_out[0].astype(BF16),
                       seq=s, tm=512)
    x2 = _ffn(x1, row(even_ffn_norm[0]), even_ffn_w1.astype(BF16),
              even_ffn_w3.astype(BF16), even_ffn_w2.astype(BF16), tm=512, tf=1408)

    bblk, cblk, a_re, a_im = _s5_coefficients(
        ssm_a_re[0], ssm_a_im[0], ssm_log_dt[0], ssm_b_re[0], ssm_b_im[0],
        ssm_c_re[0], ssm_c_im[0])
    g = _s5_mixer(x2.reshape(b, s, d), row(odd_mix_norm[0]),
                  odd_w_in[0].astype(BF16), bblk, cblk, a_re, a_im,
                  row(ssm_d[0]), tc=32)
    x3 = _glu(x2, g.reshape(t, d), odd_w_glu_a[0].astype(BF16),
              odd_w_glu_b[0].astype(BF16), tm=512)
    rw = jnp.pad(router_w[0].astype(F32), ((0, 0), (0, LANES - N_EXPERTS)))
    rb = jnp.pad(router_b[0].astype(F32), (0, LANES - N_EXPERTS)).reshape(1, LANES)
    combine = _router(x3, row(odd_moe_norm[0]), rw, rb, tm=512)
    gates = combine[:, :N_EXPERTS].T.reshape(N_EXPERTS, t, 1)
    out = _ffn(x3, row(odd_moe_norm[0]), expert_w1[0].astype(BF16),
               expert_w3[0].astype(BF16), expert_w2[0].astype(BF16),
               gates=gates, final_gain=row(final_norm), tm=512, tf=1408)
    return out.reshape(b, s, d)
```

```python
import functools
import math

import numpy as np
import jax
import jax.numpy as jnp
from jax import lax
from jax.experimental import pallas as pl
from jax.experimental.pallas import tpu as pltpu

F32 = jnp.float32
BF16 = jnp.bfloat16

EPS = 1e-6
NEG_INF = -1e30
LANES = 128
SUBLANES = 8
VMEM_LIMIT = 56 * 1024 * 1024

FOX_HEADS = 8
FOX_HEAD_DIM = 64
FOX_WIDTH = FOX_HEADS * FOX_HEAD_DIM
POOL_WINDOWS = (2, 4, 8, 16)
POOL_GROUP_DIM = 128
POOL_WIDTH = len(POOL_WINDOWS) * POOL_GROUP_DIM
POOL_HALO = 16
SSM_GROUP = 16
SSM_STATE = 64
SSM_SLAB_GROUPS = LANES // SSM_GROUP
SSM_SLAB_STATE = SSM_SLAB_GROUPS * SSM_STATE
N_EXPERTS = 8
TOP_K = 2

BIAS_LANE = FOX_HEAD_DIM
ONES_LANE = FOX_HEAD_DIM + 3
DENOM_LANE = FOX_HEAD_DIM


def _params(*sem):
    return pltpu.CompilerParams(dimension_semantics=sem,
                                vmem_limit_bytes=VMEM_LIMIT)


def _rms(x, g):
    ms = jnp.mean(x * x, axis=-1, keepdims=True)
    return x * lax.rsqrt(ms + EPS) * g


def _sigmoid(x):
    return 1.0 / (1.0 + jnp.exp(-x))


def _lane_range_ones(lo, hi):
    lane = lax.broadcasted_iota(jnp.int32, (1, LANES), 1)
    return jnp.where((lane >= lo) & (lane < hi), 1.0, 0.0).astype(F32)


def _even_inproj_kernel(x_ref, g_ref, w_ref, bias_ref, place_ref,
                        q_ref, k_ref, v_ref, p_ref, carry_ref, *, tiles_per_seq):
    i = pl.program_id(0)
    tm = x_ref.shape[0]
    fw = FOX_WIDTH
    qw = FOX_HEADS * LANES
    h = _rms(x_ref[...], g_ref[...]).astype(BF16)
    z = jnp.dot(h, w_ref[...], preferred_element_type=F32)
    p_ref[...] = z[:, 3 * fw:3 * fw + POOL_WIDTH]
    fg = z[:, 3 * fw + POOL_WIDTH:] + bias_ref[...]
    lf = jnp.minimum(fg, 0.0) - jnp.log1p(jnp.exp(-jnp.abs(fg)))
    row = lax.broadcasted_iota(jnp.int32, lf.shape, 0)
    c = lf
    sh = 1
    while sh < tm:
        c = c + jnp.where(row >= sh, pltpu.roll(c, sh, axis=0), 0.0)
        sh *= 2

    @pl.when(i % tiles_per_seq == 0)
    def _():
        carry_ref[...] = jnp.zeros_like(carry_ref)

    c = c + carry_ref[0:1, :]
    carry_ref[...] = jnp.broadcast_to(c[tm - 1:tm, :], carry_ref.shape)
    hi = c.astype(BF16).astype(F32)
    r1 = c - hi
    mid = r1.astype(BF16).astype(F32)
    lo = r1 - mid
    lane = lax.broadcasted_iota(jnp.int32, (tm, LANES), 1)
    packed = jnp.where(lane < FOX_HEADS, hi,
                       jnp.where(lane < 2 * FOX_HEADS,
                                 pltpu.roll(mid, FOX_HEADS, axis=1),
                                 pltpu.roll(lo, 2 * FOX_HEADS, axis=1)))
    placed = jnp.dot(packed.astype(BF16), place_ref[...],
                     preferred_element_type=F32)
    low = lane < FOX_HEAD_DIM
    ones_q = _lane_range_ones(ONES_LANE, ONES_LANE + 3)
    ones_k = _lane_range_ones(BIAS_LANE, BIAS_LANE + 3)
    ones_v = _lane_range_ones(DENOM_LANE, DENOM_LANE + 1)

    def head_lanes(base, hh):
        pair = z[:, base + (hh // 2) * LANES:base + (hh // 2 + 1) * LANES]
        return pltpu.roll(pair, FOX_HEAD_DIM, axis=1) if hh % 2 else pair

    for hh in range(FOX_HEADS):
        sl = slice(hh * LANES, (hh + 1) * LANES)
        q_ref[:, sl] = jnp.where(low, head_lanes(0, hh),
                                 placed[:, sl] + ones_q).astype(BF16)
        k_ref[:, sl] = jnp.where(
            low, head_lanes(fw, hh),
            placed[:, qw + hh * LANES:qw + (hh + 1) * LANES] + ones_k).astype(BF16)
        v_ref[:, sl] = jnp.where(low, head_lanes(2 * fw, hh), ones_v).astype(BF16)


def _even_inproj(x2d, gain, w_all, bias, place, *, seq, tm):
    t, d = x2d.shape
    n = w_all.shape[1]
    qw = FOX_HEADS * LANES
    kern = functools.partial(_even_inproj_kernel, tiles_per_seq=seq // tm)
    return pl.pallas_call(
        kern,
        grid=(t // tm,),
        in_specs=[
            pl.BlockSpec((tm, d), lambda i: (i, 0)),
            pl.BlockSpec((1, d), lambda i: (0, 0)),
            pl.BlockSpec((d, n), lambda i: (0, 0)),
            pl.BlockSpec((1, LANES), lambda i: (0, 0)),
            pl.BlockSpec(place.shape, lambda i: (0, 0)),
        ],
        out_specs=[
            pl.BlockSpec((tm, qw), lambda i: (i, 0)),
            pl.BlockSpec((tm, qw), lambda i: (i, 0)),
            pl.BlockSpec((tm, qw), lambda i: (i, 0)),
            pl.BlockSpec((tm, POOL_WIDTH), lambda i: (i, 0)),
        ],
        out_shape=[
            jax.ShapeDtypeStruct((t, qw), BF16),
            jax.ShapeDtypeStruct((t, qw), BF16),
            jax.ShapeDtypeStruct((t, qw), BF16),
            jax.ShapeDtypeStruct((t, POOL_WIDTH), F32),
        ],
        scratch_shapes=[pltpu.VMEM((SUBLANES, LANES), F32)],
        compiler_params=_params("arbitrary"),
        name="even_inproj",
    )(x2d, gain, w_all, bias, place)


def _dot_nt(a, b):
    return lax.dot_general(a, b, (((1,), (1,)), ((), ())),
                           preferred_element_type=F32)


def _attn_kernel(q_ref, k_ref, v_ref, o_ref, *, tq):
    seq = q_ref.shape[0]
    row = lax.broadcasted_iota(jnp.int32, (tq, tq), 0)
    col = lax.broadcasted_iota(jnp.int32, (tq, tq), 1)
    causal = col <= row
    for qi in range(seq // tq):
        r0 = qi * tq
        q = q_ref[r0:r0 + tq, :]
        s_diag = jnp.where(causal, _dot_nt(q, k_ref[r0:r0 + tq, :]), NEG_INF)
        m = jnp.max(s_diag, axis=1, keepdims=True)
        if qi > 0:
            s_past = _dot_nt(q, k_ref[0:r0, :])
            m = jnp.maximum(m, jnp.max(s_past, axis=1, keepdims=True))
        acc = jnp.dot(jnp.exp(s_diag - m).astype(BF16), v_ref[r0:r0 + tq, :],
                      preferred_element_type=F32)
        if qi > 0:
            acc = acc + jnp.dot(jnp.exp(s_past - m).astype(BF16), v_ref[0:r0, :],
                                preferred_element_type=F32)
        o_ref[r0:r0 + tq, :] = (
            acc / acc[:, DENOM_LANE:DENOM_LANE + 1]).astype(o_ref.dtype)


def _attention(q_aug, k_aug, v_aug, *, batch, seq, tq):
    t = q_aug.shape[0]
    spec = pl.BlockSpec((seq, LANES), lambda b, h: (b, h))
    return pl.pallas_call(
        functools.partial(_attn_kernel, tq=tq),
        grid=(batch, FOX_HEADS),
        in_specs=[spec, spec, spec],
        out_specs=spec,
        out_shape=jax.ShapeDtypeStruct((t, FOX_HEADS * LANES), BF16),
        compiler_params=_params("parallel", "parallel"),
        name="fox_attention",
    )(q_aug, k_aug, v_aug)


def _pool_outproj_kernel(x_ref, att_ref, p_ref, halo_ref, wpool_ref, scale_ref,
                         wo_att_ref, wo_pool_ref, o_ref, *, tiles_per_seq):
    i = pl.program_id(0)
    tm = x_ref.shape[0]
    tile_in_seq = i % tiles_per_seq
    p = p_ref[...]
    halo = jnp.where(tile_in_seq == 0, 0.0, halo_ref[...])
    ext = jnp.concatenate([halo, p], axis=0)
    pos = tile_in_seq * tm + lax.broadcasted_iota(jnp.int32, (tm, 1), 0)
    count = (pos + 1).astype(F32)
    mixed = []
    for gi, w in enumerate(POOL_WINDOWS):
        sl = slice(gi * POOL_GROUP_DIM, (gi + 1) * POOL_GROUP_DIM)
        acc = ext[:, sl]
        sh = 1
        while sh < w:
            acc = acc + pltpu.roll(acc, sh, axis=0)
            sh *= 2
        mean = acc[POOL_HALO:, :] / jnp.minimum(count, float(w))
        pooled = (mean - p[:, sl]).astype(BF16)
        mixed.append(jnp.dot(pooled, wpool_ref[gi], preferred_element_type=F32))
    pool = (jnp.concatenate(mixed, axis=1) * scale_ref[...]).astype(BF16)
    y = jnp.dot(att_ref[...], wo_att_ref[...], preferred_element_type=F32)
    y = y + jnp.dot(pool, wo_pool_ref[...], preferred_element_type=F32)
    o_ref[...] = x_ref[...] + y


def _pool_outproj(x2d, att, p_in, w_pool, pool_scale, w_out, *, seq, tm):
    t, d = x2d.shape
    wo_att = w_out[:FOX_WIDTH].reshape(FOX_HEADS, FOX_HEAD_DIM, d)
    wo_att = jnp.pad(wo_att, ((0, 0), (0, LANES - FOX_HEAD_DIM), (0, 0)))
    wo_att = wo_att.reshape(FOX_HEADS * LANES, d)
    wo_pool = w_out[FOX_WIDTH:]
    halo_blocks = tm // POOL_HALO
    kern = functools.partial(_pool_outproj_kernel, tiles_per_seq=seq // tm)
    return pl.pallas_call(
        kern,
        grid=(t // tm,),
        in_specs=[
            pl.BlockSpec((tm, d), lambda i: (i, 0)),
            pl.BlockSpec((tm, FOX_HEADS * LANES), lambda i: (i, 0)),
            pl.BlockSpec((tm, POOL_WIDTH), lambda i: (i, 0)),
            pl.BlockSpec((POOL_HALO, POOL_WIDTH),
                         lambda i: (jnp.maximum(i * halo_blocks - 1, 0), 0)),
            pl.BlockSpec(w_pool.shape, lambda i: (0, 0, 0)),
            pl.BlockSpec((1, POOL_WIDTH), lambda i: (0, 0)),
            pl.BlockSpec(wo_att.shape, lambda i: (0, 0)),
            pl.BlockSpec(wo_pool.shape, lambda i: (0, 0)),
        ],
        out_specs=pl.BlockSpec((tm, d), lambda i: (i, 0)),
        out_shape=jax.ShapeDtypeStruct((t, d), F32),
        compiler_params=_params("parallel"),
        name="pool_outproj",
    )(x2d, att, p_in, p_in, w_pool, pool_scale, wo_att, wo_pool)


def _swiglu_partial(h, w1, w3, w2):
    a = jnp.dot(h, w1, preferred_element_type=F32)
    b = jnp.dot(h, w3, preferred_element_type=F32)
    act = a * _sigmoid(a) * b
    return jnp.dot(act.astype(BF16), w2, preferred_element_type=F32)


def _ffn_kernel(x_ref, g_ref, w1_ref, w3_ref, w2_ref, o_ref, h_ref, acc_ref):
    j = pl.program_id(1)

    @pl.when(j == 0)
    def _():
        x = x_ref[...]
        h_ref[...] = _rms(x, g_ref[...]).astype(BF16)
        acc_ref[...] = x

    acc_ref[...] += _swiglu_partial(h_ref[...], w1_ref[...], w3_ref[...], w2_ref[...])

    @pl.when(j == pl.num_programs(1) - 1)
    def _():
        o_ref[...] = acc_ref[...]


def _ffn(x2d, gain, w1, w3, w2, *, tm, tf):
    t, d = x2d.shape
    f = w1.shape[1]
    return pl.pallas_call(
        _ffn_kernel,
        grid=(t // tm, f // tf),
        in_specs=[
            pl.BlockSpec((tm, d), lambda i, j: (i, 0)),
            pl.BlockSpec((1, d), lambda i, j: (0, 0)),
            pl.BlockSpec((d, tf), lambda i, j: (0, j)),
            pl.BlockSpec((d, tf), lambda i, j: (0, j)),
            pl.BlockSpec((tf, d), lambda i, j: (j, 0)),
        ],
        out_specs=pl.BlockSpec((tm, d), lambda i, j: (i, 0)),
        out_shape=jax.ShapeDtypeStruct((t, d), F32),
        scratch_shapes=[pltpu.VMEM((tm, d), BF16), pltpu.VMEM((tm, d), F32)],
        compiler_params=_params("parallel", "arbitrary"),
        name="dense_ffn",
    )(x2d, gain, w1, w3, w2)


def _gelu_tanh(x):
    c = math.sqrt(2.0 / math.pi)
    return 0.5 * x * (1.0 + jnp.tanh(c * (x + 0.044715 * (x * x * x))))


def _s5_kernel(x_ref, g_ref, win_ref, perm_ref, permt_ref, bblk_ref, cblk_ref,
               are_ref, aim_ref, d_ref, o_ref, xs_ref, st_ref, *, slabs_per_loop):
    nb, tc, d = x_ref.shape
    rows = nb * tc
    n_slabs = d // LANES
    sw = 2 * SSM_SLAB_STATE

    @pl.when(pl.program_id(0) == 0)
    def _():
        st_ref[...] = jnp.zeros_like(st_ref)

    x = x_ref[...].reshape(rows, d)
    h = _rms(x, g_ref[...]).astype(BF16)
    h_tb = jnp.dot(perm_ref[...], h, preferred_element_type=F32).astype(BF16)
    u = jnp.dot(h_tb, win_ref[...], preferred_element_type=F32)
    ub = u.astype(BF16)
    for s in range(n_slabs):
        xs_ref[:, s * sw:(s + 1) * sw] = jnp.dot(
            ub[:, s * LANES:(s + 1) * LANES], bblk_ref[s],
            preferred_element_type=F32)

    for s0 in range(0, n_slabs, slabs_per_loop):
        lo = s0 * sw
        width = slabs_per_loop * sw
        ar = are_ref[:, lo:lo + width]
        ai = aim_ref[:, lo:lo + width]

        def body(t, carry, lo=lo, width=width, ar=ar, ai=ai):
            xr, xi = carry
            r0 = pl.multiple_of(t * SUBLANES, SUBLANES)
            bu = xs_ref[pl.ds(r0, SUBLANES), lo:lo + width]
            new_r, new_i = [], []
            for k in range(slabs_per_loop):
                hs = SSM_SLAB_STATE
                a_r = ar[:, k * sw:k * sw + hs]
                a_i = ai[:, k * sw:k * sw + hs]
                x_r = xr[:, k * hs:(k + 1) * hs]
                x_i = xi[:, k * hs:(k + 1) * hs]
                new_r.append(a_r * x_r - a_i * x_i + bu[:, k * sw:k * sw + hs])
                new_i.append(a_r * x_i + a_i * x_r + bu[:, k * sw + hs:(k + 1) * sw])
            for k in range(slabs_per_loop):
                hs = SSM_SLAB_STATE
                xs_ref[pl.ds(r0, SUBLANES), lo + k * sw:lo + k * sw + hs] = new_r[k]
                xs_ref[pl.ds(r0, SUBLANES), lo + k * sw + hs:lo + (k + 1) * sw] = new_i[k]
            return (jnp.concatenate(new_r, axis=1), jnp.concatenate(new_i, axis=1))

        init_r = jnp.concatenate(
            [st_ref[:, lo + k * sw:lo + k * sw + SSM_SLAB_STATE]
             for k in range(slabs_per_loop)], axis=1)
        init_i = jnp.concatenate(
            [st_ref[:, lo + k * sw + SSM_SLAB_STATE:lo + (k + 1) * sw]
             for k in range(slabs_per_loop)], axis=1)
        lax.fori_loop(0, tc, body, (init_r, init_i))
    st_ref[...] = xs_ref[rows - SUBLANES:rows, :]

    ys = []
    for s in range(n_slabs):
        ys.append(jnp.dot(xs_ref[:, s * sw:(s + 1) * sw].astype(BF16), cblk_ref[s],
                          preferred_element_type=F32))
    y = jnp.concatenate(ys, axis=1) + d_ref[...] * u
    g = _gelu_tanh(y).astype(BF16)
    g_bt = jnp.dot(permt_ref[...], g, preferred_element_type=F32)
    o_ref[...] = g_bt.reshape(nb, tc, d).astype(o_ref.dtype)


def _s5_mixer(x3d, gain, w_in, bblk, cblk, a_re, a_im, d_skip, *, tc):
    nb, seq, d = x3d.shape
    rows = nb * tc
    n_slabs = d // LANES
    sw = 2 * SSM_SLAB_STATE
    r = np.arange(rows)
    perm = np.zeros((rows, rows), np.float32)
    perm[(r % tc) * nb + r // tc, r] = 1.0
    perm_j = jnp.asarray(perm, BF16)
    permt_j = jnp.asarray(perm.T, BF16)
    kern = functools.partial(_s5_kernel, slabs_per_loop=2)
    full2 = lambda a: pl.BlockSpec(a.shape, lambda i: (0, 0))
    full3 = lambda a: pl.BlockSpec(a.shape, lambda i: (0, 0, 0))
    return pl.pallas_call(
        kern,
        grid=(seq // tc,),
        in_specs=[
            pl.BlockSpec((nb, tc, d), lambda i: (0, i, 0)),
            full2(gain), full2(w_in), full2(perm_j), full2(permt_j),
            full3(bblk), full3(cblk), full2(a_re), full2(a_im), full2(d_skip),
        ],
        out_specs=pl.BlockSpec((nb, tc, d), lambda i: (0, i, 0)),
        out_shape=jax.ShapeDtypeStruct((nb, seq, d), BF16),
        scratch_shapes=[
            pltpu.VMEM((rows, n_slabs * sw), F32),
            pltpu.VMEM((SUBLANES, n_slabs * sw), F32),
        ],
        compiler_params=_params("arbitrary"),
        name="s5_mixer",
    )(x3d, gain, w_in, perm_j, permt_j, bblk, cblk, a_re, a_im, d_skip)


def _s5_coefficients(a_re, a_im, log_dt, b_re, b_im, c_re, c_im):
    dt = jnp.exp(log_dt.astype(F32))[:, None]
    ar = a_re.astype(F32)
    ai = a_im.astype(F32)
    mag = jnp.exp(ar * dt)
    abar_re = mag * jnp.cos(ai * dt)
    abar_im = mag * jnp.sin(ai * dt)
    den = ar * ar + ai * ai
    nr = abar_re - 1.0
    ni = abar_im
    coef_re = (nr * ar + ni * ai) / den
    coef_im = (ni * ar - nr * ai) / den
    br = b_re.astype(F32)
    bi = b_im.astype(F32)
    bbar_re = coef_re[..., None] * br - coef_im[..., None] * bi
    bbar_im = coef_re[..., None] * bi + coef_im[..., None] * br
    n_groups = ar.shape[0]
    n_slabs = n_groups // SSM_SLAB_GROUPS
    eye = jnp.eye(SSM_SLAB_GROUPS, dtype=F32)

    def in_block(bb):
        bb = bb.reshape(n_slabs, SSM_SLAB_GROUPS, SSM_STATE, SSM_GROUP)
        blk = jnp.einsum('sgph,gk->sghkp', bb, eye)
        return blk.reshape(n_slabs, LANES, SSM_SLAB_STATE)

    def out_block(cc):
        cc = cc.reshape(n_slabs, SSM_SLAB_GROUPS, SSM_GROUP, SSM_STATE)
        blk = jnp.einsum('sghp,gk->sgpkh', cc, eye)
        return blk.reshape(n_slabs, SSM_SLAB_STATE, LANES)

    bblk = jnp.concatenate([in_block(bbar_re), in_block(bbar_im)], axis=2)
    cblk = jnp.concatenate([out_block(c_re.astype(F32)),
                            -out_block(c_im.astype(F32))], axis=1)

    def lanes(a):
        a = a.reshape(n_slabs, 1, SSM_SLAB_STATE)
        a = jnp.concatenate([a, a], axis=2).reshape(1, -1)
        return jnp.broadcast_to(a, (SUBLANES, a.shape[1]))

    return (bblk.astype(BF16), cblk.astype(BF16), lanes(abar_re), lanes(abar_im))


def _glu_kernel(x_ref, g_ref, wa_ref, wb_ref, o_ref):
    g = g_ref[...]
    a = jnp.dot(g, wa_ref[...], preferred_element_type=F32)
    b = jnp.dot(g, wb_ref[...], preferred_element_type=F32)
    o_ref[...] = x_ref[...] + a * _sigmoid(b)


def _glu(x2d, g2d, wa, wb, *, tm):
    t, d = x2d.shape
    return pl.pallas_call(
        _glu_kernel,
        grid=(t // tm,),
        in_specs=[
            pl.BlockSpec((tm, d), lambda i: (i, 0)),
            pl.BlockSpec((tm, d), lambda i: (i, 0)),
            pl.BlockSpec(wa.shape, lambda i: (0, 0)),
            pl.BlockSpec(wb.shape, lambda i: (0, 0)),
        ],
        out_specs=pl.BlockSpec((tm, d), lambda i: (i, 0)),
        out_shape=jax.ShapeDtypeStruct((t, d), F32),
        compiler_params=_params("parallel"),
        name="glu_outproj",
    )(x2d, g2d, wa, wb)


MOE_CHUNK = 512
MOE_HALF = MOE_CHUNK // 2
MOE_TILE = 512
SEG_ALIGN = 16


def _router_kernel(x_ref, g_ref, w_ref, b_ref, h_ref, gate_ref, pos_ref,
                   post_ref, cnt_ref):
    tm = x_ref.shape[0]
    h = _rms(x_ref[...], g_ref[...])
    h_ref[...] = h.astype(BF16)
    logits = jnp.dot(h, w_ref[...], preferred_element_type=F32,
                     precision=lax.Precision.HIGHEST) + b_ref[...]
    lane = lax.broadcasted_iota(jnp.int32, logits.shape, 1)
    logits = jnp.where(lane < N_EXPERTS, logits, -jnp.inf)
    m1 = jnp.max(logits, axis=1, keepdims=True)
    i1 = jnp.min(jnp.where(logits == m1, lane, LANES), axis=1, keepdims=True)
    rest = jnp.where(lane == i1, -jnp.inf, logits)
    m2 = jnp.max(rest, axis=1, keepdims=True)
    i2 = jnp.min(jnp.where(rest == m2, lane, LANES), axis=1, keepdims=True)
    e2 = jnp.exp(m2 - m1)
    g1 = 1.0 / (1.0 + e2)
    g2 = e2 / (1.0 + e2)
    gate_ref[...] = jnp.where(lane == i1, g1, 0.0) + jnp.where(lane == i2, g2, 0.0)
    member = jnp.where((lane == i1) | (lane == i2), 1.0, 0.0)
    row = lax.broadcasted_iota(jnp.int32, member.shape, 0)
    c = member
    sh = 1
    while sh < tm:
        c = c + jnp.where(row >= sh, pltpu.roll(c, sh, axis=0), 0.0)
        sh *= 2
    pos = jnp.where(member > 0.0, c - member, -1.0)
    pos_ref[...] = pos
    post_ref[0] = pos.T[0:SUBLANES, :]
    cnt_ref[0] = jnp.broadcast_to(c[tm - 1:tm, :], (SUBLANES, LANES))


def _router(x2d, gain, w_pad, b_pad):
    t, d = x2d.shape
    tm = MOE_CHUNK
    n_chunks = t // tm
    return pl.pallas_call(
        _router_kernel,
        grid=(n_chunks,),
        in_specs=[
            pl.BlockSpec((tm, d), lambda i: (i, 0)),
            pl.BlockSpec((1, d), lambda i: (0, 0)),
            pl.BlockSpec((d, LANES), lambda i: (0, 0)),
            pl.BlockSpec((1, LANES), lambda i: (0, 0)),
        ],
        out_specs=[
            pl.BlockSpec((tm, d), lambda i: (i, 0)),
            pl.BlockSpec((tm, LANES), lambda i: (i, 0)),
            pl.BlockSpec((tm, LANES), lambda i: (i, 0)),
            pl.BlockSpec((1, SUBLANES, tm), lambda i: (i, 0, 0)),
            pl.BlockSpec((1, SUBLANES, LANES), lambda i: (i, 0, 0)),
        ],
        out_shape=[
            jax.ShapeDtypeStruct((t, d), BF16),
            jax.ShapeDtypeStruct((t, LANES), F32),
            jax.ShapeDtypeStruct((t, LANES), F32),
            jax.ShapeDtypeStruct((n_chunks, SUBLANES, tm), F32),
            jax.ShapeDtypeStruct((n_chunks, SUBLANES, LANES), F32),
        ],
        compiler_params=_params("parallel"),
        name="router",
    )(x2d, gain, w_pad, b_pad)


def _moe_layout(cnt, n_tokens):
    n_chunks = cnt.shape[0]
    seg = (cnt + SEG_ALIGN - 1) // SEG_ALIGN * SEG_ALIGN
    used = jnp.sum(seg, axis=0)
    padded = (used + MOE_HALF + MOE_TILE - 1) // MOE_TILE * MOE_TILE
    ends = jnp.cumsum(padded)
    start = ends - padded
    off = start[None, :] + jnp.cumsum(seg, axis=0) - seg
    max_rows = (TOP_K * n_tokens + (SEG_ALIGN - 1) * N_EXPERTS * n_chunks
                + N_EXPERTS * (MOE_HALF + MOE_TILE))
    n_tiles = -(-max_rows // MOE_TILE)
    tile_start = jnp.arange(n_tiles, dtype=jnp.int32) * MOE_TILE
    tile_e = jnp.sum((tile_start[:, None] >= ends[None, :]).astype(jnp.int32), axis=1)
    tile_e = jnp.minimum(tile_e, N_EXPERTS - 1)
    tile_rows = jnp.clip(used[tile_e] - (tile_start - start[tile_e]), 0, MOE_TILE)
    return (off.reshape(-1).astype(jnp.int32), cnt.reshape(-1).astype(jnp.int32),
            tile_e.astype(jnp.int32), tile_rows.astype(jnp.int32), n_tiles)


def _segment_copies(hbm_ref, buf_ref, sem_ref, off_ref, chunk, to_hbm):
    copies = []
    for e in range(N_EXPERTS):
        off = pl.multiple_of(off_ref[chunk * N_EXPERTS + e], SEG_ALIGN)
        for half in range(2):
            hbm = hbm_ref.at[pl.ds(off + half * MOE_HALF, MOE_HALF)]
            buf = buf_ref.at[e, half * MOE_HALF:(half + 1) * MOE_HALF]
            src, dst = (buf, hbm) if to_hbm else (hbm, buf)
            copies.append(pltpu.make_async_copy(src, dst, sem_ref.at[e, half]))
    return copies


def _gather_kernel(off_ref, cnt_ref, h_ref, post_ref, zeros_hbm, o_hbm, stage_ref,
                   sem_ref):
    del zeros_hbm
    c = pl.program_id(0)
    h = h_ref[...]
    rank = lax.broadcasted_iota(jnp.int32, (MOE_HALF, MOE_CHUNK), 0).astype(F32)
    copies = _segment_copies(o_hbm, stage_ref, sem_ref, off_ref, c, to_hbm=True)
    for e in range(N_EXPERTS):
        pos = post_ref[0, e:e + 1, :]
        long_seg = cnt_ref[c * N_EXPERTS + e] > MOE_HALF
        onehot = jnp.where(rank == pos, 1.0, 0.0).astype(BF16)
        stage_ref[e, 0:MOE_HALF] = jnp.dot(
            onehot, h, preferred_element_type=F32).astype(BF16)
        copies[2 * e].start()

        @pl.when(long_seg)
        def _(e=e, pos=pos):
            onehot_hi = jnp.where(rank + float(MOE_HALF) == pos, 1.0, 0.0).astype(BF16)
            stage_ref[e, MOE_HALF:MOE_CHUNK] = jnp.dot(
                onehot_hi, h, preferred_element_type=F32).astype(BF16)
            copies[2 * e + 1].start()

    for e in range(N_EXPERTS):
        copies[2 * e].wait()

        @pl.when(cnt_ref[c * N_EXPERTS + e] > MOE_HALF)
        def _(e=e):
            copies[2 * e + 1].wait()


def _gather(h2d, post, off, cnt, *, n_rows):
    t, d = h2d.shape
    n_chunks = t // MOE_CHUNK
    grid_spec = pltpu.PrefetchScalarGridSpec(
        num_scalar_prefetch=2,
        grid=(n_chunks,),
        in_specs=[
            pl.BlockSpec((MOE_CHUNK, d), lambda i, off, cnt: (i, 0)),
            pl.BlockSpec((1, SUBLANES, MOE_CHUNK), lambda i, off, cnt: (i, 0, 0)),
            pl.BlockSpec(memory_space=pl.ANY),
        ],
        out_specs=pl.BlockSpec(memory_space=pl.ANY),
        scratch_shapes=[
            pltpu.VMEM((N_EXPERTS, MOE_CHUNK, d), BF16),
            pltpu.SemaphoreType.DMA((N_EXPERTS, 2)),
        ],
    )
    return pl.pallas_call(
        _gather_kernel,
        grid_spec=grid_spec,
        out_shape=jax.ShapeDtypeStruct((n_rows, d), BF16),
        input_output_aliases={4: 0},
        compiler_params=_params("arbitrary"),
        name="moe_gather",
    )(off, cnt, h2d, post, jnp.zeros((n_rows, d), BF16))


def _expert_kernel(te_ref, tr_ref, x_ref, w1_ref, w3_ref, w2_ref, o_ref, acc_ref):
    i = pl.program_id(0)
    j = pl.program_id(1)
    rows = tr_ref[i]

    @pl.when(rows > 0)
    def _():
        row = lax.broadcasted_iota(jnp.int32, x_ref.shape, 0)
        x = jnp.where(row < rows, x_ref[...], jnp.zeros_like(x_ref))
        part = _swiglu_partial(x, w1_ref[...], w3_ref[...], w2_ref[...])

        @pl.when(j == 0)
        def _():
            acc_ref[...] = part

        @pl.when(j > 0)
        def _():
            acc_ref[...] += part

    @pl.when(j == pl.num_programs(1) - 1)
    def _():
        @pl.when(rows > 0)
        def _():
            o_ref[...] = acc_ref[...].astype(o_ref.dtype)

        @pl.when(rows == 0)
        def _():
            o_ref[...] = jnp.zeros_like(o_ref)


def _experts(x_sorted, w1, w3, w2, tile_e, tile_rows, *, tf):
    n_rows, d = x_sorted.shape
    f = w1.shape[2]
    nj = f // tf

    def jj(i, j, tr):
        return jnp.where(tr[i] > 0, j, nj - 1)

    grid_spec = pltpu.PrefetchScalarGridSpec(
        num_scalar_prefetch=2,
        grid=(n_rows // MOE_TILE, nj),
        in_specs=[
            pl.BlockSpec((MOE_TILE, d), lambda i, j, te, tr: (i, 0)),
            pl.BlockSpec((None, d, tf), lambda i, j, te, tr: (te[i], 0, jj(i, j, tr))),
            pl.BlockSpec((None, d, tf), lambda i, j, te, tr: (te[i], 0, jj(i, j, tr))),
            pl.BlockSpec((None, tf, d), lambda i, j, te, tr: (te[i], jj(i, j, tr), 0)),
        ],
        out_specs=pl.BlockSpec((MOE_TILE, d), lambda i, j, te, tr: (i, 0)),
        scratch_shapes=[pltpu.VMEM((MOE_TILE, d), F32)],
    )
    return pl.pallas_call(
        _expert_kernel,
        grid_spec=grid_spec,
        out_shape=jax.ShapeDtypeStruct((n_rows, d), BF16),
        compiler_params=_params("arbitrary", "arbitrary"),
        name="moe_experts",
    )(tile_e, tile_rows, x_sorted, w1, w3, w2)


def _combine_kernel(off_ref, cnt_ref, x_ref, gate_ref, pos_ref, fg_ref, y_hbm,
                    o_ref, ybuf_ref, part_ref, sem_ref):
    c = pl.program_id(0)
    n_chunks = pl.num_programs(0)
    slot = c % 2

    def fetch(chunk, sl, start):
        copies = _segment_copies(y_hbm, ybuf_ref.at[sl], sem_ref.at[sl], off_ref,
                                 chunk, to_hbm=False)
        for e in range(N_EXPERTS):
            first, second = copies[2 * e], copies[2 * e + 1]
            if start:
                first.start()
            else:
                first.wait()

            @pl.when(cnt_ref[chunk * N_EXPERTS + e] > MOE_HALF)
            def _(second=second):
                if start:
                    second.start()
                else:
                    second.wait()

    @pl.when(c == 0)
    def _():
        fetch(c, slot, True)

    @pl.when(c + 1 < n_chunks)
    def _():
        fetch(c + 1, 1 - slot, True)

    fetch(c, slot, False)
    lane = lax.broadcasted_iota(jnp.int32, (MOE_CHUNK, MOE_HALF), 1).astype(F32)
    acc = x_ref[...]
    for e in range(N_EXPERTS):
        pos = pos_ref[:, e:e + 1]
        onehot = jnp.where(lane == pos, 1.0, 0.0).astype(BF16)
        part_ref[...] = jnp.dot(onehot, ybuf_ref[slot, e, 0:MOE_HALF],
                                preferred_element_type=F32)

        @pl.when(cnt_ref[c * N_EXPERTS + e] > MOE_HALF)
        def _(e=e, pos=pos):
            onehot_hi = jnp.where(lane + float(MOE_HALF) == pos, 1.0, 0.0).astype(BF16)
            part_ref[...] += jnp.dot(onehot_hi, ybuf_ref[slot, e, MOE_HALF:MOE_CHUNK],
                                     preferred_element_type=F32)

        acc = acc + gate_ref[:, e:e + 1] * part_ref[...]
    o_ref[...] = _rms(acc, fg_ref[...])


def _combine(x2d, gate, pos, final_gain, y_sorted, off, cnt):
    t, d = x2d.shape
    n_chunks = t // MOE_CHUNK
    grid_spec = pltpu.PrefetchScalarGridSpec(
        num_scalar_prefetch=2,
        grid=(n_chunks,),
        in_specs=[
            pl.BlockSpec((MOE_CHUNK, d), lambda i, off, cnt: (i, 0)),
            pl.BlockSpec((MOE_CHUNK, LANES), lambda i, off, cnt: (i, 0)),
            pl.BlockSpec((MOE_CHUNK, LANES), lambda i, off, cnt: (i, 0)),
            pl.BlockSpec((1, d), lambda i, off, cnt: (0, 0)),
            pl.BlockSpec(memory_space=pl.ANY),
        ],
        out_specs=pl.BlockSpec((MOE_CHUNK, d), lambda i, off, cnt: (i, 0)),
        scratch_shapes=[
            pltpu.VMEM((2, N_EXPERTS, MOE_CHUNK, d), BF16),
            pltpu.VMEM((MOE_CHUNK, d), F32),
            pltpu.SemaphoreType.DMA((2, N_EXPERTS, 2)),
        ],
    )
    return pl.pallas_call(
        _combine_kernel,
        grid_spec=grid_spec,
        out_shape=jax.ShapeDtypeStruct((t, d), F32),
        compiler_params=_params("arbitrary"),
        name="moe_combine",
    )(off, cnt, x2d, gate, pos, final_gain, y_sorted)


def _moe(x2d, gain, router_w, router_b, w1, w3, w2, final_gain):
    t, d = x2d.shape
    rw = jnp.pad(router_w.astype(F32), ((0, 0), (0, LANES - N_EXPERTS)))
    rb = jnp.pad(router_b.astype(F32), (0, LANES - N_EXPERTS)).reshape(1, LANES)
    h, gate, pos, post, cnt = _router(x2d, gain, rw, rb)
    cnt = cnt[:, 0, :N_EXPERTS].astype(jnp.int32)
    off, cnt_flat, tile_e, tile_rows, n_tiles = _moe_layout(cnt, t)
    h_sorted = _gather(h, post, off, cnt_flat, n_rows=n_tiles * MOE_TILE)
    y_sorted = _experts(h_sorted, w1, w3, w2, tile_e, tile_rows, tf=1408)
    return _combine(x2d, gate, pos, final_gain, y_sorted, off, cnt_flat)


def _even_weights(w_in, b_forget):
    fw = FOX_WIDTH
    wq = w_in[:, :fw] * FOX_HEAD_DIM ** -0.5
    wf = jnp.pad(w_in[:, 3 * fw:3 * fw + FOX_HEADS],
                 ((0, 0), (0, LANES - FOX_HEADS)))
    wp = w_in[:, 3 * fw + FOX_HEADS:]
    w_all = jnp.concatenate([wq, w_in[:, fw:3 * fw], wp, wf], axis=1).astype(BF16)
    bias = jnp.pad(b_forget.astype(F32), (0, LANES - FOX_HEADS)).reshape(1, LANES)
    qw = FOX_HEADS * LANES
    place = np.zeros((LANES, 2 * qw), np.float32)
    for hh in range(FOX_HEADS):
        for piece in range(3):
            place[piece * FOX_HEADS + hh, hh * LANES + BIAS_LANE + piece] = 1.0
            place[piece * FOX_HEADS + hh, qw + hh * LANES + ONES_LANE + piece] = -1.0
    return w_all, bias, jnp.asarray(place, BF16)


def kernel(x, even_mix_norm, even_w_in, even_b_forget, even_w_pool, even_pool_scale, even_w_out, even_ffn_norm, even_ffn_w1, even_ffn_w3, even_ffn_w2, odd_mix_norm, odd_w_in, ssm_a_re, ssm_a_im, ssm_log_dt, ssm_b_re, ssm_b_im, ssm_c_re, ssm_c_im, ssm_d, odd_w_glu_a, odd_w_glu_b, odd_moe_norm, router_w, router_b, expert_w1, expert_w3, expert_w2, final_norm):
    b, s, d = x.shape
    t = b * s
    assert b == SUBLANES, "the S5 recurrence keeps one batch row per sublane"
    x2d = x.reshape(t, d)
    row = lambda v: v.reshape(1, -1).astype(F32)

    w_all, bias, place = _even_weights(even_w_in[0], even_b_forget[0])
    q_aug, k_aug, v, p_in = _even_inproj(
        x2d, row(even_mix_norm[0]), w_all, bias, place, seq=s, tm=512)
    att = _attention(q_aug, k_aug, v, batch=b, seq=s, tq=256)
    x1 = _pool_outproj(x2d, att, p_in, even_w_pool[0].astype(BF16),
                       row(even_pool_scale[0]), even_w_out[0].astype(BF16),
                       seq=s, tm=512)
    x2 = _ffn(x1, row(even_ffn_norm[0]), even_ffn_w1[0].astype(BF16),
              even_ffn_w3[0].astype(BF16), even_ffn_w2[0].astype(BF16), tm=512, tf=1408)

    bblk, cblk, a_re, a_im = _s5_coefficients(
        ssm_a_re[0], ssm_a_im[0], ssm_log_dt[0], ssm_b_re[0], ssm_b_im[0],
        ssm_c_re[0], ssm_c_im[0])
    g = _s5_mixer(x2.reshape(b, s, d), row(odd_mix_norm[0]),
                  odd_w_in[0].astype(BF16), bblk, cblk, a_re, a_im,
                  row(ssm_d[0]), tc=32)
    x3 = _glu(x2, g.reshape(t, d), odd_w_glu_a[0].astype(BF16),
              odd_w_glu_b[0].astype(BF16), tm=512)
    out = _moe(x3, row(odd_moe_norm[0]), router_w[0], router_b[0],
               expert_w1[0].astype(BF16), expert_w3[0].astype(BF16),
               expert_w2[0].astype(BF16), row(final_norm))
    return out.reshape(b, s, d)
```

```python
import functools
import math

import numpy as np
import jax
import jax.numpy as jnp
from jax import lax
from jax.experimental import pallas as pl
from jax.experimental.pallas import tpu as pltpu

F32 = jnp.float32
BF16 = jnp.bfloat16

EPS = 1e-6
NEG_INF = -1e30
LANES = 128
SUBLANES = 8
VMEM_LIMIT = 56 * 1024 * 1024

FOX_HEADS = 8
FOX_HEAD_DIM = 64
FOX_WIDTH = FOX_HEADS * FOX_HEAD_DIM
POOL_WINDOWS = (2, 4, 8, 16)
POOL_GROUP_DIM = 128
POOL_WIDTH = len(POOL_WINDOWS) * POOL_GROUP_DIM
POOL_HALO = 16
SSM_GROUP = 16
SSM_STATE = 64
SSM_SLAB_GROUPS = LANES // SSM_GROUP
SSM_SLAB_STATE = SSM_SLAB_GROUPS * SSM_STATE
N_EXPERTS = 8
TOP_K = 2

BIAS_LANE = FOX_HEAD_DIM
ONES_LANE = FOX_HEAD_DIM + 3
DENOM_LANE = FOX_HEAD_DIM


def _params(*sem):
    return pltpu.CompilerParams(dimension_semantics=sem,
                                vmem_limit_bytes=VMEM_LIMIT)


def _rms(x, g):
    ms = jnp.mean(x * x, axis=-1, keepdims=True)
    return x * lax.rsqrt(ms + EPS) * g


def _sigmoid(x):
    return 1.0 / (1.0 + jnp.exp(-x))


def _lane_range_ones(lo, hi):
    lane = lax.broadcasted_iota(jnp.int32, (1, LANES), 1)
    return jnp.where((lane >= lo) & (lane < hi), 1.0, 0.0).astype(F32)


def _even_inproj_kernel(x_ref, g_ref, w_ref, bias_ref, place_ref,
                        q_ref, k_ref, v_ref, p_ref, carry_ref, *, tiles_per_seq):
    i = pl.program_id(0)
    tm = x_ref.shape[0]
    fw = FOX_WIDTH
    qw = FOX_HEADS * LANES
    h = _rms(x_ref[...], g_ref[...]).astype(BF16)
    z = jnp.dot(h, w_ref[...], preferred_element_type=F32)
    p_ref[...] = z[:, 3 * fw:3 * fw + POOL_WIDTH]
    fg = z[:, 3 * fw + POOL_WIDTH:] + bias_ref[...]
    lf = jnp.minimum(fg, 0.0) - jnp.log1p(jnp.exp(-jnp.abs(fg)))
    row = lax.broadcasted_iota(jnp.int32, lf.shape, 0)
    c = lf
    sh = 1
    while sh < tm:
        c = c + jnp.where(row >= sh, pltpu.roll(c, sh, axis=0), 0.0)
        sh *= 2

    @pl.when(i % tiles_per_seq == 0)
    def _():
        carry_ref[...] = jnp.zeros_like(carry_ref)

    c = c + carry_ref[0:1, :]
    carry_ref[...] = jnp.broadcast_to(c[tm - 1:tm, :], carry_ref.shape)
    hi = c.astype(BF16).astype(F32)
    r1 = c - hi
    mid = r1.astype(BF16).astype(F32)
    lo = r1 - mid
    lane = lax.broadcasted_iota(jnp.int32, (tm, LANES), 1)
    packed = jnp.where(lane < FOX_HEADS, hi,
                       jnp.where(lane < 2 * FOX_HEADS,
                                 pltpu.roll(mid, FOX_HEADS, axis=1),
                                 pltpu.roll(lo, 2 * FOX_HEADS, axis=1)))
    placed = jnp.dot(packed.astype(BF16), place_ref[...],
                     preferred_element_type=F32)
    low = lane < FOX_HEAD_DIM
    ones_q = _lane_range_ones(ONES_LANE, ONES_LANE + 3)
    ones_k = _lane_range_ones(BIAS_LANE, BIAS_LANE + 3)
    ones_v = _lane_range_ones(DENOM_LANE, DENOM_LANE + 1)

    def head_lanes(base, hh):
        pair = z[:, base + (hh // 2) * LANES:base + (hh // 2 + 1) * LANES]
        return pltpu.roll(pair, FOX_HEAD_DIM, axis=1) if hh % 2 else pair

    for hh in range(FOX_HEADS):
        sl = slice(hh * LANES, (hh + 1) * LANES)
        q_ref[:, sl] = jnp.where(low, head_lanes(0, hh),
                                 placed[:, sl] + ones_q).astype(BF16)
        k_ref[:, sl] = jnp.where(
            low, head_lanes(fw, hh),
            placed[:, qw + hh * LANES:qw + (hh + 1) * LANES] + ones_k).astype(BF16)
        v_ref[:, sl] = jnp.where(low, head_lanes(2 * fw, hh), ones_v).astype(BF16)


def _even_inproj(x2d, gain, w_all, bias, place, *, seq, tm):
    t, d = x2d.shape
    n = w_all.shape[1]
    qw = FOX_HEADS * LANES
    kern = functools.partial(_even_inproj_kernel, tiles_per_seq=seq // tm)
    return pl.pallas_call(
        kern,
        grid=(t // tm,),
        in_specs=[
            pl.BlockSpec((tm, d), lambda i: (i, 0)),
            pl.BlockSpec((1, d), lambda i: (0, 0)),
            pl.BlockSpec((d, n), lambda i: (0, 0)),
            pl.BlockSpec((1, LANES), lambda i: (0, 0)),
            pl.BlockSpec(place.shape, lambda i: (0, 0)),
        ],
        out_specs=[
            pl.BlockSpec((tm, qw), lambda i: (i, 0)),
            pl.BlockSpec((tm, qw), lambda i: (i, 0)),
            pl.BlockSpec((tm, qw), lambda i: (i, 0)),
            pl.BlockSpec((tm, POOL_WIDTH), lambda i: (i, 0)),
        ],
        out_shape=[
            jax.ShapeDtypeStruct((t, qw), BF16),
            jax.ShapeDtypeStruct((t, qw), BF16),
            jax.ShapeDtypeStruct((t, qw), BF16),
            jax.ShapeDtypeStruct((t, POOL_WIDTH), F32),
        ],
        scratch_shapes=[pltpu.VMEM((SUBLANES, LANES), F32)],
        compiler_params=_params("arbitrary"),
        name="even_inproj",
    )(x2d, gain, w_all, bias, place)


def _dot_nt(a, b):
    return lax.dot_general(a, b, (((1,), (1,)), ((), ())),
                           preferred_element_type=F32)


def _attn_kernel(q_ref, k_ref, v_ref, o_ref, *, tq):
    seq = q_ref.shape[0]
    row = lax.broadcasted_iota(jnp.int32, (tq, tq), 0)
    col = lax.broadcasted_iota(jnp.int32, (tq, tq), 1)
    causal = col <= row
    for qi in range(seq // tq):
        r0 = qi * tq
        q = q_ref[r0:r0 + tq, :]
        s_diag = jnp.where(causal, _dot_nt(q, k_ref[r0:r0 + tq, :]), NEG_INF)
        m = jnp.max(s_diag, axis=1, keepdims=True)
        if qi > 0:
            s_past = _dot_nt(q, k_ref[0:r0, :])
            m = jnp.maximum(m, jnp.max(s_past, axis=1, keepdims=True))
        acc = jnp.dot(jnp.exp(s_diag - m).astype(BF16), v_ref[r0:r0 + tq, :],
                      preferred_element_type=F32)
        if qi > 0:
            acc = acc + jnp.dot(jnp.exp(s_past - m).astype(BF16), v_ref[0:r0, :],
                                preferred_element_type=F32)
        o_ref[r0:r0 + tq, :] = (
            acc / acc[:, DENOM_LANE:DENOM_LANE + 1]).astype(o_ref.dtype)


def _attention(q_aug, k_aug, v_aug, *, batch, seq, tq):
    t = q_aug.shape[0]
    spec = pl.BlockSpec((seq, LANES), lambda b, h: (b, h))
    return pl.pallas_call(
        functools.partial(_attn_kernel, tq=tq),
        grid=(batch, FOX_HEADS),
        in_specs=[spec, spec, spec],
        out_specs=spec,
        out_shape=jax.ShapeDtypeStruct((t, FOX_HEADS * LANES), BF16),
        compiler_params=_params("parallel", "parallel"),
        name="fox_attention",
    )(q_aug, k_aug, v_aug)


def _pool_outproj_kernel(x_ref, att_ref, p_ref, halo_ref, wpool_ref, scale_ref,
                         wo_att_ref, wo_pool_ref, o_ref, *, tiles_per_seq):
    i = pl.program_id(0)
    tm = x_ref.shape[0]
    tile_in_seq = i % tiles_per_seq
    p = p_ref[...]
    halo = jnp.where(tile_in_seq == 0, 0.0, halo_ref[...])
    ext = jnp.concatenate([halo, p], axis=0)
    pos = tile_in_seq * tm + lax.broadcasted_iota(jnp.int32, (tm, 1), 0)
    count = (pos + 1).astype(F32)
    mixed = []
    for gi, w in enumerate(POOL_WINDOWS):
        sl = slice(gi * POOL_GROUP_DIM, (gi + 1) * POOL_GROUP_DIM)
        acc = ext[:, sl]
        sh = 1
        while sh < w:
            acc = acc + pltpu.roll(acc, sh, axis=0)
            sh *= 2
        mean = acc[POOL_HALO:, :] / jnp.minimum(count, float(w))
        pooled = (mean - p[:, sl]).astype(BF16)
        mixed.append(jnp.dot(pooled, wpool_ref[gi], preferred_element_type=F32))
    pool = (jnp.concatenate(mixed, axis=1) * scale_ref[...]).astype(BF16)
    y = jnp.dot(att_ref[...], wo_att_ref[...], preferred_element_type=F32)
    y = y + jnp.dot(pool, wo_pool_ref[...], preferred_element_type=F32)
    o_ref[...] = x_ref[...] + y


def _pool_outproj(x2d, att, p_in, w_pool, pool_scale, w_out, *, seq, tm):
    t, d = x2d.shape
    wo_att = w_out[:FOX_WIDTH].reshape(FOX_HEADS, FOX_HEAD_DIM, d)
    wo_att = jnp.pad(wo_att, ((0, 0), (0, LANES - FOX_HEAD_DIM), (0, 0)))
    wo_att = wo_att.reshape(FOX_HEADS * LANES, d)
    wo_pool = w_out[FOX_WIDTH:]
    halo_blocks = tm // POOL_HALO
    kern = functools.partial(_pool_outproj_kernel, tiles_per_seq=seq // tm)
    return pl.pallas_call(
        kern,
        grid=(t // tm,),
        in_specs=[
            pl.BlockSpec((tm, d), lambda i: (i, 0)),
            pl.BlockSpec((tm, FOX_HEADS * LANES), lambda i: (i, 0)),
            pl.BlockSpec((tm, POOL_WIDTH), lambda i: (i, 0)),
            pl.BlockSpec((POOL_HALO, POOL_WIDTH),
                         lambda i: (jnp.maximum(i * halo_blocks - 1, 0), 0)),
            pl.BlockSpec(w_pool.shape, lambda i: (0, 0, 0)),
            pl.BlockSpec((1, POOL_WIDTH), lambda i: (0, 0)),
            pl.BlockSpec(wo_att.shape, lambda i: (0, 0)),
            pl.BlockSpec(wo_pool.shape, lambda i: (0, 0)),
        ],
        out_specs=pl.BlockSpec((tm, d), lambda i: (i, 0)),
        out_shape=jax.ShapeDtypeStruct((t, d), F32),
        compiler_params=_params("parallel"),
        name="pool_outproj",
    )(x2d, att, p_in, p_in, w_pool, pool_scale, wo_att, wo_pool)


def _swiglu_partial(h, w1, w3, w2):
    a = jnp.dot(h, w1, preferred_element_type=F32)
    b = jnp.dot(h, w3, preferred_element_type=F32)
    act = a * _sigmoid(a) * b
    return jnp.dot(act.astype(BF16), w2, preferred_element_type=F32)


MXU_TILE = 256


def _ff_splits(f):
    cut = -(-(f // MXU_TILE) // 2) * MXU_TILE
    return ((0, cut), (cut, f)) if 0 < cut < f else ((0, f),)


def _swiglu(h, w1_ref, w3_ref, w2_ref):
    y = None
    for lo, hi in _ff_splits(w1_ref.shape[1]):
        part = _swiglu_partial(h, w1_ref[:, lo:hi], w3_ref[:, lo:hi], w2_ref[lo:hi, :])
        y = part if y is None else y + part
    return y


def _ffn_kernel(x_ref, g_ref, w1_ref, w3_ref, w2_ref, o_ref):
    x = x_ref[...]
    h = _rms(x, g_ref[...]).astype(BF16)
    o_ref[...] = x + _swiglu(h, w1_ref, w3_ref, w2_ref)


def _ffn(x2d, gain, w1, w3, w2, *, tm):
    t, d = x2d.shape
    resident = lambda w: pl.BlockSpec(w.shape, lambda i: (0, 0),
                                      pipeline_mode=pl.Buffered(1))
    return pl.pallas_call(
        _ffn_kernel,
        grid=(t // tm,),
        in_specs=[
            pl.BlockSpec((tm, d), lambda i: (i, 0)),
            pl.BlockSpec((1, d), lambda i: (0, 0)),
            resident(w1), resident(w3), resident(w2),
        ],
        out_specs=pl.BlockSpec((tm, d), lambda i: (i, 0)),
        out_shape=jax.ShapeDtypeStruct((t, d), F32),
        compiler_params=_params("parallel"),
        name="dense_ffn",
    )(x2d, gain, w1, w3, w2)


def _gelu_tanh(x):
    c = math.sqrt(2.0 / math.pi)
    return 0.5 * x * (1.0 + jnp.tanh(c * (x + 0.044715 * (x * x * x))))


def _s5_kernel(x_ref, g_ref, win_ref, perm_ref, permt_ref, bblk_ref, cblk_ref,
               are_ref, aim_ref, d_ref, o_ref, xs_ref, st_ref):
    nb, tc, d = x_ref.shape
    rows = nb * tc
    n_slabs = d // LANES
    sw = 2 * SSM_SLAB_STATE

    @pl.when(pl.program_id(0) == 0)
    def _():
        st_ref[...] = jnp.zeros_like(st_ref)

    x = x_ref[...].reshape(rows, d)
    h = _rms(x, g_ref[...]).astype(BF16)
    h_tb = jnp.dot(perm_ref[...], h, preferred_element_type=F32).astype(BF16)
    u = jnp.dot(h_tb, win_ref[...], preferred_element_type=F32)
    ub = u.astype(BF16)
    hs = SSM_SLAB_STATE
    ys = []
    for s in range(n_slabs):
        lo = s * sw
        xs_ref[:, lo:lo + sw] = jnp.dot(
            ub[:, s * LANES:(s + 1) * LANES], bblk_ref[s],
            preferred_element_type=F32)
        a_r = are_ref[:, lo:lo + hs]
        a_i = aim_ref[:, lo:lo + hs]
        x_r = st_ref[:, lo:lo + hs]
        x_i = st_ref[:, lo + hs:lo + sw]
        for t in range(tc):
            r0 = t * SUBLANES
            new_r = a_r * x_r - a_i * x_i + xs_ref[r0:r0 + SUBLANES, lo:lo + hs]
            new_i = a_r * x_i + a_i * x_r + xs_ref[r0:r0 + SUBLANES, lo + hs:lo + sw]
            xs_ref[r0:r0 + SUBLANES, lo:lo + hs] = new_r
            xs_ref[r0:r0 + SUBLANES, lo + hs:lo + sw] = new_i
            x_r, x_i = new_r, new_i
        st_ref[:, lo:lo + hs] = x_r
        st_ref[:, lo + hs:lo + sw] = x_i
        ys.append(jnp.dot(xs_ref[:, lo:lo + sw].astype(BF16), cblk_ref[s],
                          preferred_element_type=F32))
    y = jnp.concatenate(ys, axis=1) + d_ref[...] * u
    g = _gelu_tanh(y).astype(BF16)
    g_bt = jnp.dot(permt_ref[...], g, preferred_element_type=F32)
    o_ref[...] = g_bt.reshape(nb, tc, d).astype(o_ref.dtype)


def _s5_mixer(x3d, gain, w_in, bblk, cblk, a_re, a_im, d_skip, *, tc):
    nb, seq, d = x3d.shape
    rows = nb * tc
    n_slabs = d // LANES
    sw = 2 * SSM_SLAB_STATE
    r = np.arange(rows)
    perm = np.zeros((rows, rows), np.float32)
    perm[(r % tc) * nb + r // tc, r] = 1.0
    perm_j = jnp.asarray(perm, BF16)
    permt_j = jnp.asarray(perm.T, BF16)
    kern = _s5_kernel
    full2 = lambda a: pl.BlockSpec(a.shape, lambda i: (0, 0))
    full3 = lambda a: pl.BlockSpec(a.shape, lambda i: (0, 0, 0))
    return pl.pallas_call(
        kern,
        grid=(seq // tc,),
        in_specs=[
            pl.BlockSpec((nb, tc, d), lambda i: (0, i, 0)),
            full2(gain), full2(w_in), full2(perm_j), full2(permt_j),
            full3(bblk), full3(cblk), full2(a_re), full2(a_im), full2(d_skip),
        ],
        out_specs=pl.BlockSpec((nb, tc, d), lambda i: (0, i, 0)),
        out_shape=jax.ShapeDtypeStruct((nb, seq, d), BF16),
        scratch_shapes=[
            pltpu.VMEM((rows, n_slabs * sw), F32),
            pltpu.VMEM((SUBLANES, n_slabs * sw), F32),
        ],
        compiler_params=_params("arbitrary"),
        name="s5_mixer",
    )(x3d, gain, w_in, perm_j, permt_j, bblk, cblk, a_re, a_im, d_skip)


def _s5_coefficients(a_re, a_im, log_dt, b_re, b_im, c_re, c_im):
    dt = jnp.exp(log_dt.astype(F32))[:, None]
    ar = a_re.astype(F32)
    ai = a_im.astype(F32)
    mag = jnp.exp(ar * dt)
    abar_re = mag * jnp.cos(ai * dt)
    abar_im = mag * jnp.sin(ai * dt)
    den = ar * ar + ai * ai
    nr = abar_re - 1.0
    ni = abar_im
    coef_re = (nr * ar + ni * ai) / den
    coef_im = (ni * ar - nr * ai) / den
    br = b_re.astype(F32)
    bi = b_im.astype(F32)
    bbar_re = coef_re[..., None] * br - coef_im[..., None] * bi
    bbar_im = coef_re[..., None] * bi + coef_im[..., None] * br
    n_groups = ar.shape[0]
    n_slabs = n_groups // SSM_SLAB_GROUPS
    eye = jnp.eye(SSM_SLAB_GROUPS, dtype=F32)

    def in_block(bb):
        bb = bb.reshape(n_slabs, SSM_SLAB_GROUPS, SSM_STATE, SSM_GROUP)
        blk = jnp.einsum('sgph,gk->sghkp', bb, eye)
        return blk.reshape(n_slabs, LANES, SSM_SLAB_STATE)

    def out_block(cc):
        cc = cc.reshape(n_slabs, SSM_SLAB_GROUPS, SSM_GROUP, SSM_STATE)
        blk = jnp.einsum('sghp,gk->sgpkh', cc, eye)
        return blk.reshape(n_slabs, SSM_SLAB_STATE, LANES)

    bblk = jnp.concatenate([in_block(bbar_re), in_block(bbar_im)], axis=2)
    cblk = jnp.concatenate([out_block(c_re.astype(F32)),
                            -out_block(c_im.astype(F32))], axis=1)

    def lanes(a):
        a = a.reshape(n_slabs, 1, SSM_SLAB_STATE)
        a = jnp.concatenate([a, a], axis=2).reshape(1, -1)
        return jnp.broadcast_to(a, (SUBLANES, a.shape[1]))

    return (bblk.astype(BF16), cblk.astype(BF16), lanes(abar_re), lanes(abar_im))


def _glu_kernel(x_ref, g_ref, wa_ref, wb_ref, o_ref):
    g = g_ref[...]
    a = jnp.dot(g, wa_ref[...], preferred_element_type=F32)
    b = jnp.dot(g, wb_ref[...], preferred_element_type=F32)
    o_ref[...] = x_ref[...] + a * _sigmoid(b)


def _glu(x2d, g2d, wa, wb, *, tm):
    t, d = x2d.shape
    return pl.pallas_call(
        _glu_kernel,
        grid=(t // tm,),
        in_specs=[
            pl.BlockSpec((tm, d), lambda i: (i, 0)),
            pl.BlockSpec((tm, d), lambda i: (i, 0)),
            pl.BlockSpec(wa.shape, lambda i: (0, 0)),
            pl.BlockSpec(wb.shape, lambda i: (0, 0)),
        ],
        out_specs=pl.BlockSpec((tm, d), lambda i: (i, 0)),
        out_shape=jax.ShapeDtypeStruct((t, d), F32),
        compiler_params=_params("parallel"),
        name="glu_outproj",
    )(x2d, g2d, wa, wb)


MOE_CHUNK = 512
MOE_HALF = MOE_CHUNK // 2
MOE_TILE = 512
SEG_ALIGN = 16


def _router_kernel(x_ref, g_ref, w_ref, b_ref, h_ref, gate_ref, pos_ref,
                   post_ref, cnt_ref):
    tm = x_ref.shape[0]
    h = _rms(x_ref[...], g_ref[...])
    h_hi = h.astype(BF16)
    h_ref[...] = h_hi
    h_lo = (h - h_hi.astype(F32)).astype(BF16)
    p_hi = jnp.dot(h_hi, w_ref[...], preferred_element_type=F32)
    p_lo = jnp.dot(h_lo, w_ref[...], preferred_element_type=F32)
    logits = p_hi + pltpu.roll(p_hi, LANES - N_EXPERTS, axis=1) + p_lo + b_ref[...]
    lane = lax.broadcasted_iota(jnp.int32, logits.shape, 1)
    logits = jnp.where(lane < N_EXPERTS, logits, -jnp.inf)
    m1 = jnp.max(logits, axis=1, keepdims=True)
    i1 = jnp.min(jnp.where(logits == m1, lane, LANES), axis=1, keepdims=True)
    rest = jnp.where(lane == i1, -jnp.inf, logits)
    m2 = jnp.max(rest, axis=1, keepdims=True)
    i2 = jnp.min(jnp.where(rest == m2, lane, LANES), axis=1, keepdims=True)
    e2 = jnp.exp(m2 - m1)
    g1 = 1.0 / (1.0 + e2)
    g2 = e2 / (1.0 + e2)
    gate_ref[...] = jnp.where(lane == i1, g1, 0.0) + jnp.where(lane == i2, g2, 0.0)
    member = jnp.where((lane == i1) | (lane == i2), 1.0, 0.0)
    row = lax.broadcasted_iota(jnp.int32, member.shape, 0)
    c = member
    sh = 1
    while sh < tm:
        c = c + jnp.where(row >= sh, pltpu.roll(c, sh, axis=0), 0.0)
        sh *= 2
    pos = jnp.where(member > 0.0, c - member, -1.0)
    pos_ref[...] = pos
    post_ref[0] = pos.T[0:SUBLANES, :]
    cnt_ref[0] = jnp.broadcast_to(c[tm - 1:tm, :], (SUBLANES, LANES))


def _router(x2d, gain, w_pad, b_pad):
    t, d = x2d.shape
    tm = MOE_CHUNK
    n_chunks = t // tm
    return pl.pallas_call(
        _router_kernel,
        grid=(n_chunks,),
        in_specs=[
            pl.BlockSpec((tm, d), lambda i: (i, 0)),
            pl.BlockSpec((1, d), lambda i: (0, 0)),
            pl.BlockSpec((d, LANES), lambda i: (0, 0)),
            pl.BlockSpec((1, LANES), lambda i: (0, 0)),
        ],
        out_specs=[
            pl.BlockSpec((tm, d), lambda i: (i, 0)),
            pl.BlockSpec((tm, LANES), lambda i: (i, 0)),
            pl.BlockSpec((tm, LANES), lambda i: (i, 0)),
            pl.BlockSpec((1, SUBLANES, tm), lambda i: (i, 0, 0)),
            pl.BlockSpec((1, SUBLANES, LANES), lambda i: (i, 0, 0)),
        ],
        out_shape=[
            jax.ShapeDtypeStruct((t, d), BF16),
            jax.ShapeDtypeStruct((t, LANES), F32),
            jax.ShapeDtypeStruct((t, LANES), F32),
            jax.ShapeDtypeStruct((n_chunks, SUBLANES, tm), F32),
            jax.ShapeDtypeStruct((n_chunks, SUBLANES, LANES), F32),
        ],
        compiler_params=_params("parallel"),
        name="router",
    )(x2d, gain, w_pad, b_pad)


def _moe_layout(cnt, n_tokens):
    n_chunks = cnt.shape[0]
    seg = (cnt + SEG_ALIGN - 1) // SEG_ALIGN * SEG_ALIGN
    used = jnp.sum(seg, axis=0)
    padded = (used + MOE_HALF + MOE_TILE - 1) // MOE_TILE * MOE_TILE
    ends = jnp.cumsum(padded)
    start = ends - padded
    off = start[None, :] + jnp.cumsum(seg, axis=0) - seg
    max_rows = (TOP_K * n_tokens + (SEG_ALIGN - 1) * N_EXPERTS * n_chunks
                + N_EXPERTS * (MOE_HALF + MOE_TILE))
    n_tiles = -(-max_rows // MOE_TILE)
    tile_start = jnp.arange(n_tiles, dtype=jnp.int32) * MOE_TILE
    tile_e = jnp.sum((tile_start[:, None] >= ends[None, :]).astype(jnp.int32), axis=1)
    tile_e = jnp.minimum(tile_e, N_EXPERTS - 1)
    tile_rows = jnp.clip(used[tile_e] - (tile_start - start[tile_e]), 0, MOE_TILE)
    return (off.reshape(-1).astype(jnp.int32), cnt.reshape(-1).astype(jnp.int32),
            tile_e.astype(jnp.int32), tile_rows.astype(jnp.int32), n_tiles)


def _segment_copies(hbm_ref, buf_ref, sem_ref, off_ref, chunk, to_hbm, slot=()):
    copies = []
    for e in range(N_EXPERTS):
        off = pl.multiple_of(off_ref[chunk * N_EXPERTS + e], SEG_ALIGN)
        for half in range(2):
            hbm = hbm_ref.at[pl.ds(off + half * MOE_HALF, MOE_HALF)]
            buf = buf_ref.at[(*slot, e, pl.ds(half * MOE_HALF, MOE_HALF))]
            src, dst = (buf, hbm) if to_hbm else (hbm, buf)
            copies.append(pltpu.make_async_copy(src, dst, sem_ref.at[(*slot, e, half)]))
    return copies


def _gather_kernel(off_ref, cnt_ref, h_ref, post_ref, zeros_hbm, o_hbm, stage_ref,
                   sem_ref):
    del zeros_hbm
    c = pl.program_id(0)
    h = h_ref[...]
    rank = lax.broadcasted_iota(jnp.int32, (MOE_HALF, MOE_CHUNK), 0).astype(F32)
    copies = _segment_copies(o_hbm, stage_ref, sem_ref, off_ref, c, to_hbm=True)
    for e in range(N_EXPERTS):
        pos = post_ref[0, e:e + 1, :]
        long_seg = cnt_ref[c * N_EXPERTS + e] > MOE_HALF
        onehot = jnp.where(rank == pos, 1.0, 0.0).astype(BF16)
        stage_ref[e, 0:MOE_HALF] = jnp.dot(
            onehot, h, preferred_element_type=F32).astype(BF16)
        copies[2 * e].start()

        @pl.when(long_seg)
        def _(e=e, pos=pos):
            onehot_hi = jnp.where(rank + float(MOE_HALF) == pos, 1.0, 0.0).astype(BF16)
            stage_ref[e, MOE_HALF:MOE_CHUNK] = jnp.dot(
                onehot_hi, h, preferred_element_type=F32).astype(BF16)
            copies[2 * e + 1].start()

    for e in range(N_EXPERTS):
        copies[2 * e].wait()

        @pl.when(cnt_ref[c * N_EXPERTS + e] > MOE_HALF)
        def _(e=e):
            copies[2 * e + 1].wait()


def _gather(h2d, post, off, cnt, *, n_rows):
    t, d = h2d.shape
    n_chunks = t // MOE_CHUNK
    grid_spec = pltpu.PrefetchScalarGridSpec(
        num_scalar_prefetch=2,
        grid=(n_chunks,),
        in_specs=[
            pl.BlockSpec((MOE_CHUNK, d), lambda i, off, cnt: (i, 0)),
            pl.BlockSpec((1, SUBLANES, MOE_CHUNK), lambda i, off, cnt: (i, 0, 0)),
            pl.BlockSpec(memory_space=pl.ANY),
        ],
        out_specs=pl.BlockSpec(memory_space=pl.ANY),
        scratch_shapes=[
            pltpu.VMEM((N_EXPERTS, MOE_CHUNK, d), BF16),
            pltpu.SemaphoreType.DMA((N_EXPERTS, 2)),
        ],
    )
    return pl.pallas_call(
        _gather_kernel,
        grid_spec=grid_spec,
        out_shape=jax.ShapeDtypeStruct((n_rows, d), BF16),
        input_output_aliases={4: 0},
        compiler_params=_params("arbitrary"),
        name="moe_gather",
    )(off, cnt, h2d, post, jnp.zeros((n_rows, d), BF16))


def _expert_kernel(te_ref, tr_ref, x_ref, w1_ref, w3_ref, w2_ref, o_ref):
    rows = tr_ref[pl.program_id(0)]

    @pl.when(rows > 0)
    def _():
        row = lax.broadcasted_iota(jnp.int32, x_ref.shape, 0)
        x = jnp.where(row < rows, x_ref[...], jnp.zeros_like(x_ref))
        o_ref[...] = _swiglu(x, w1_ref, w3_ref, w2_ref).astype(o_ref.dtype)

    @pl.when(rows == 0)
    def _():
        o_ref[...] = jnp.zeros_like(o_ref)


def _experts(x_sorted, w1, w3, w2, tile_e, tile_rows):
    n_rows, d = x_sorted.shape
    f = w1.shape[2]
    grid_spec = pltpu.PrefetchScalarGridSpec(
        num_scalar_prefetch=2,
        grid=(n_rows // MOE_TILE,),
        in_specs=[
            pl.BlockSpec((MOE_TILE, d), lambda i, te, tr: (i, 0)),
            pl.BlockSpec((None, d, f), lambda i, te, tr: (te[i], 0, 0)),
            pl.BlockSpec((None, d, f), lambda i, te, tr: (te[i], 0, 0)),
            pl.BlockSpec((None, f, d), lambda i, te, tr: (te[i], 0, 0)),
        ],
        out_specs=pl.BlockSpec((MOE_TILE, d), lambda i, te, tr: (i, 0)),
    )
    return pl.pallas_call(
        _expert_kernel,
        grid_spec=grid_spec,
        out_shape=jax.ShapeDtypeStruct((n_rows, d), BF16),
        compiler_params=_params("arbitrary"),
        name="moe_experts",
    )(tile_e, tile_rows, x_sorted, w1, w3, w2)


def _combine_kernel(off_ref, cnt_ref, x_ref, gate_ref, pos_ref, fg_ref, y_hbm,
                    o_ref, ybuf_ref, part_ref, sem_ref):
    c = pl.program_id(0)
    n_chunks = pl.num_programs(0)
    slot = c % 2

    def fetch(chunk, sl, start):
        copies = _segment_copies(y_hbm, ybuf_ref, sem_ref, off_ref, chunk,
                                 to_hbm=False, slot=(sl,))
        for e in range(N_EXPERTS):
            first, second = copies[2 * e], copies[2 * e + 1]
            if start:
                first.start()
            else:
                first.wait()

            @pl.when(cnt_ref[chunk * N_EXPERTS + e] > MOE_HALF)
            def _(second=second):
                if start:
                    second.start()
                else:
                    second.wait()

    @pl.when(c == 0)
    def _():
        fetch(c, slot, True)

    @pl.when(c + 1 < n_chunks)
    def _():
        fetch(c + 1, 1 - slot, True)

    fetch(c, slot, False)
    lane = lax.broadcasted_iota(jnp.int32, (MOE_CHUNK, MOE_HALF), 1).astype(F32)
    acc = x_ref[...]
    for e in range(N_EXPERTS):
        pos = pos_ref[:, e:e + 1]
        onehot = jnp.where(lane == pos, 1.0, 0.0).astype(BF16)
        part_ref[...] = jnp.dot(onehot, ybuf_ref[slot, e, 0:MOE_HALF],
                                preferred_element_type=F32)

        @pl.when(cnt_ref[c * N_EXPERTS + e] > MOE_HALF)
        def _(e=e, pos=pos):
            onehot_hi = jnp.where(lane + float(MOE_HALF) == pos, 1.0, 0.0).astype(BF16)
            part_ref[...] += jnp.dot(onehot_hi, ybuf_ref[slot, e, MOE_HALF:MOE_CHUNK],
                                     preferred_element_type=F32)

        acc = acc + gate_ref[:, e:e + 1] * part_ref[...]
    o_ref[...] = _rms(acc, fg_ref[...])


def _combine(x2d, gate, pos, final_gain, y_sorted, off, cnt):
    t, d = x2d.shape
    n_chunks = t // MOE_CHUNK
    grid_spec = pltpu.PrefetchScalarGridSpec(
        num_scalar_prefetch=2,
        grid=(n_chunks,),
        in_specs=[
            pl.BlockSpec((MOE_CHUNK, d), lambda i, off, cnt: (i, 0)),
            pl.BlockSpec((MOE_CHUNK, LANES), lambda i, off, cnt: (i, 0)),
            pl.BlockSpec((MOE_CHUNK, LANES), lambda i, off, cnt: (i, 0)),
            pl.BlockSpec((1, d), lambda i, off, cnt: (0, 0)),
            pl.BlockSpec(memory_space=pl.ANY),
        ],
        out_specs=pl.BlockSpec((MOE_CHUNK, d), lambda i, off, cnt: (i, 0)),
        scratch_shapes=[
            pltpu.VMEM((2, N_EXPERTS, MOE_CHUNK, d), BF16),
            pltpu.VMEM((MOE_CHUNK, d), F32),
            pltpu.SemaphoreType.DMA((2, N_EXPERTS, 2)),
        ],
    )
    return pl.pallas_call(
        _combine_kernel,
        grid_spec=grid_spec,
        out_shape=jax.ShapeDtypeStruct((t, d), F32),
        compiler_params=_params("arbitrary"),
        name="moe_combine",
    )(off, cnt, x2d, gate, pos, final_gain, y_sorted)


def _moe(x2d, gain, router_w, router_b, w1, w3, w2, final_gain):
    t, d = x2d.shape
    rw_hi = router_w.astype(BF16)
    rw_lo = (router_w.astype(F32) - rw_hi.astype(F32)).astype(BF16)
    rw = jnp.pad(jnp.concatenate([rw_hi, rw_lo], axis=1),
                 ((0, 0), (0, LANES - 2 * N_EXPERTS)))
    rb = jnp.pad(router_b.astype(F32), (0, LANES - N_EXPERTS)).reshape(1, LANES)
    h, gate, pos, post, cnt = _router(x2d, gain, rw, rb)
    cnt = cnt[:, 0, :N_EXPERTS].astype(jnp.int32)
    off, cnt_flat, tile_e, tile_rows, n_tiles = _moe_layout(cnt, t)
    h_sorted = _gather(h, post, off, cnt_flat, n_rows=n_tiles * MOE_TILE)
    y_sorted = _experts(h_sorted, w1, w3, w2, tile_e, tile_rows)
    return _combine(x2d, gate, pos, final_gain, y_sorted, off, cnt_flat)


def _even_weights(w_in, b_forget):
    fw = FOX_WIDTH
    wq = w_in[:, :fw] * FOX_HEAD_DIM ** -0.5
    wf = jnp.pad(w_in[:, 3 * fw:3 * fw + FOX_HEADS],
                 ((0, 0), (0, LANES - FOX_HEADS)))
    wp = w_in[:, 3 * fw + FOX_HEADS:]
    w_all = jnp.concatenate([wq, w_in[:, fw:3 * fw], wp, wf], axis=1).astype(BF16)
    bias = jnp.pad(b_forget.astype(F32), (0, LANES - FOX_HEADS)).reshape(1, LANES)
    qw = FOX_HEADS * LANES
    place = np.zeros((LANES, 2 * qw), np.float32)
    for hh in range(FOX_HEADS):
        for piece in range(3):
            place[piece * FOX_HEADS + hh, hh * LANES + BIAS_LANE + piece] = 1.0
            place[piece * FOX_HEADS + hh, qw + hh * LANES + ONES_LANE + piece] = -1.0
    return w_all, bias, jnp.asarray(place, BF16)


def kernel(x, even_mix_norm, even_w_in, even_b_forget, even_w_pool, even_pool_scale, even_w_out, even_ffn_norm, even_ffn_w1, even_ffn_w3, even_ffn_w2, odd_mix_norm, odd_w_in, ssm_a_re, ssm_a_im, ssm_log_dt, ssm_b_re, ssm_b_im, ssm_c_re, ssm_c_im, ssm_d, odd_w_glu_a, odd_w_glu_b, odd_moe_norm, router_w, router_b, expert_w1, expert_w3, expert_w2, final_norm):
    b, s, d = x.shape
    t = b * s
    assert b == SUBLANES, "the S5 recurrence keeps one batch row per sublane"
    x2d = x.reshape(t, d)
    row = lambda v: v.reshape(1, -1).astype(F32)

    w_all, bias, place = _even_weights(even_w_in[0], even_b_forget[0])
    q_aug, k_aug, v, p_in = _even_inproj(
        x2d, row(even_mix_norm[0]), w_all, bias, place, seq=s, tm=512)
    att = _attention(q_aug, k_aug, v, batch=b, seq=s, tq=256)
    x1 = _pool_outproj(x2d, att, p_in, even_w_pool[0].astype(BF16),
                       row(even_pool_scale[0]), even_w_out[0].astype(BF16),
                       seq=s, tm=512)
    x2 = _ffn(x1, row(even_ffn_norm[0]), even_ffn_w1[0].astype(BF16),
              even_ffn_w3[0].astype(BF16), even_ffn_w2[0].astype(BF16), tm=1024)

    bblk, cblk, a_re, a_im = _s5_coefficients(
        ssm_a_re[0], ssm_a_im[0], ssm_log_dt[0], ssm_b_re[0], ssm_b_im[0],
        ssm_c_re[0], ssm_c_im[0])
    g = _s5_mixer(x2.reshape(b, s, d), row(odd_mix_norm[0]),
                  odd_w_in[0].astype(BF16), bblk, cblk, a_re, a_im,
                  row(ssm_d[0]), tc=32)
    x3 = _glu(x2, g.reshape(t, d), odd_w_glu_a[0].astype(BF16),
              odd_w_glu_b[0].astype(BF16), tm=512)
    out = _moe(x3, row(odd_moe_norm[0]), router_w[0], router_b[0],
               expert_w1[0].astype(BF16), expert_w3[0].astype(BF16),
               expert_w2[0].astype(BF16), row(final_norm))
    return out.reshape(b, s, d)
```

```python
import functools
import math

import numpy as np
import jax
import jax.numpy as jnp
from jax import lax
from jax.experimental import pallas as pl
from jax.experimental.pallas import tpu as pltpu

F32 = jnp.float32
BF16 = jnp.bfloat16

EPS = 1e-6
NEG_INF = -1e30
LANES = 128
SUBLANES = 8
VMEM_LIMIT = 56 * 1024 * 1024

FOX_HEADS = 8
FOX_HEAD_DIM = 64
FOX_WIDTH = FOX_HEADS * FOX_HEAD_DIM
POOL_WINDOWS = (2, 4, 8, 16)
POOL_GROUP_DIM = 128
POOL_WIDTH = len(POOL_WINDOWS) * POOL_GROUP_DIM
POOL_HALO = 16
SSM_GROUP = 16
SSM_STATE = 64
SSM_SLAB_GROUPS = LANES // SSM_GROUP
SSM_SLAB_STATE = SSM_SLAB_GROUPS * SSM_STATE
N_EXPERTS = 8
TOP_K = 2

BIAS_LANE = FOX_HEAD_DIM
ONES_LANE = FOX_HEAD_DIM + 3
DENOM_LANE = FOX_HEAD_DIM


def _params(*sem):
    return pltpu.CompilerParams(dimension_semantics=sem,
                                vmem_limit_bytes=VMEM_LIMIT)


def _rms(x, g):
    ms = jnp.mean(x * x, axis=-1, keepdims=True)
    return x * lax.rsqrt(ms + EPS) * g


def _sigmoid(x):
    return 1.0 / (1.0 + jnp.exp(-x))


def _lane_range_ones(lo, hi):
    lane = lax.broadcasted_iota(jnp.int32, (1, LANES), 1)
    return jnp.where((lane >= lo) & (lane < hi), 1.0, 0.0).astype(F32)


def _even_inproj_kernel(x_ref, g_ref, w_ref, bias_ref, place_ref,
                        q_ref, k_ref, v_ref, p_ref, carry_ref, *, tiles_per_seq):
    i = pl.program_id(0)
    tm = x_ref.shape[0]
    fw = FOX_WIDTH
    qw = FOX_HEADS * LANES
    h = _rms(x_ref[...], g_ref[...]).astype(BF16)
    z = jnp.dot(h, w_ref[...], preferred_element_type=F32)
    p_ref[...] = z[:, 3 * fw:3 * fw + POOL_WIDTH]
    fg = z[:, 3 * fw + POOL_WIDTH:] + bias_ref[...]
    lf = jnp.minimum(fg, 0.0) - jnp.log1p(jnp.exp(-jnp.abs(fg)))
    row = lax.broadcasted_iota(jnp.int32, lf.shape, 0)
    c = lf
    sh = 1
    while sh < tm:
        c = c + jnp.where(row >= sh, pltpu.roll(c, sh, axis=0), 0.0)
        sh *= 2

    @pl.when(i % tiles_per_seq == 0)
    def _():
        carry_ref[...] = jnp.zeros_like(carry_ref)

    c = c + carry_ref[0:1, :]
    carry_ref[...] = jnp.broadcast_to(c[tm - 1:tm, :], carry_ref.shape)
    hi = c.astype(BF16).astype(F32)
    r1 = c - hi
    mid = r1.astype(BF16).astype(F32)
    lo = r1 - mid
    lane = lax.broadcasted_iota(jnp.int32, (tm, LANES), 1)
    packed = jnp.where(lane < FOX_HEADS, hi,
                       jnp.where(lane < 2 * FOX_HEADS,
                                 pltpu.roll(mid, FOX_HEADS, axis=1),
                                 pltpu.roll(lo, 2 * FOX_HEADS, axis=1)))
    placed = jnp.dot(packed.astype(BF16), place_ref[...],
                     preferred_element_type=F32)
    low = lane < FOX_HEAD_DIM
    ones_q = _lane_range_ones(ONES_LANE, ONES_LANE + 3)
    ones_k = _lane_range_ones(BIAS_LANE, BIAS_LANE + 3)
    ones_v = _lane_range_ones(DENOM_LANE, DENOM_LANE + 1)

    def head_lanes(base, hh):
        pair = z[:, base + (hh // 2) * LANES:base + (hh // 2 + 1) * LANES]
        return pltpu.roll(pair, FOX_HEAD_DIM, axis=1) if hh % 2 else pair

    for hh in range(FOX_HEADS):
        sl = slice(hh * LANES, (hh + 1) * LANES)
        q_ref[:, sl] = jnp.where(low, head_lanes(0, hh),
                                 placed[:, sl] + ones_q).astype(BF16)
        k_ref[:, sl] = jnp.where(
            low, head_lanes(fw, hh),
            placed[:, qw + hh * LANES:qw + (hh + 1) * LANES] + ones_k).astype(BF16)
        v_ref[:, sl] = jnp.where(low, head_lanes(2 * fw, hh), ones_v).astype(BF16)


def _even_inproj(x2d, gain, w_all, bias, place, *, seq, tm):
    t, d = x2d.shape
    n = w_all.shape[1]
    qw = FOX_HEADS * LANES
    kern = functools.partial(_even_inproj_kernel, tiles_per_seq=seq // tm)
    return pl.pallas_call(
        kern,
        grid=(t // tm,),
        in_specs=[
            pl.BlockSpec((tm, d), lambda i: (i, 0)),
            pl.BlockSpec((1, d), lambda i: (0, 0)),
            pl.BlockSpec((d, n), lambda i: (0, 0)),
            pl.BlockSpec((1, LANES), lambda i: (0, 0)),
            pl.BlockSpec(place.shape, lambda i: (0, 0)),
        ],
        out_specs=[
            pl.BlockSpec((tm, qw), lambda i: (i, 0)),
            pl.BlockSpec((tm, qw), lambda i: (i, 0)),
            pl.BlockSpec((tm, qw), lambda i: (i, 0)),
            pl.BlockSpec((tm, POOL_WIDTH), lambda i: (i, 0)),
        ],
        out_shape=[
            jax.ShapeDtypeStruct((t, qw), BF16),
            jax.ShapeDtypeStruct((t, qw), BF16),
            jax.ShapeDtypeStruct((t, qw), BF16),
            jax.ShapeDtypeStruct((t, POOL_WIDTH), F32),
        ],
        scratch_shapes=[pltpu.VMEM((SUBLANES, LANES), F32)],
        compiler_params=_params("arbitrary"),
        name="even_inproj",
    )(x2d, gain, w_all, bias, place)


def _dot_nt(a, b):
    return lax.dot_general(a, b, (((1,), (1,)), ((), ())),
                           preferred_element_type=F32)


def _attn_kernel(q_ref, k_ref, v_ref, o_ref, *, tq):
    seq = q_ref.shape[0]
    row = lax.broadcasted_iota(jnp.int32, (tq, tq), 0)
    col = lax.broadcasted_iota(jnp.int32, (tq, tq), 1)
    causal = col <= row
    for qi in range(seq // tq):
        r0 = qi * tq
        q = q_ref[r0:r0 + tq, :]
        s_diag = jnp.where(causal, _dot_nt(q, k_ref[r0:r0 + tq, :]), NEG_INF)
        m = jnp.max(s_diag, axis=1, keepdims=True)
        if qi > 0:
            s_past = _dot_nt(q, k_ref[0:r0, :])
            m = jnp.maximum(m, jnp.max(s_past, axis=1, keepdims=True))
        acc = jnp.dot(jnp.exp(s_diag - m).astype(BF16), v_ref[r0:r0 + tq, :],
                      preferred_element_type=F32)
        if qi > 0:
            acc = acc + jnp.dot(jnp.exp(s_past - m).astype(BF16), v_ref[0:r0, :],
                                preferred_element_type=F32)
        o_ref[r0:r0 + tq, :] = (
            acc / acc[:, DENOM_LANE:DENOM_LANE + 1]).astype(o_ref.dtype)


def _attention(q_aug, k_aug, v_aug, *, batch, seq, tq):
    t = q_aug.shape[0]
    spec = pl.BlockSpec((seq, LANES), lambda b, h: (b, h))
    return pl.pallas_call(
        functools.partial(_attn_kernel, tq=tq),
        grid=(batch, FOX_HEADS),
        in_specs=[spec, spec, spec],
        out_specs=spec,
        out_shape=jax.ShapeDtypeStruct((t, FOX_HEADS * LANES), BF16),
        compiler_params=_params("parallel", "parallel"),
        name="fox_attention",
    )(q_aug, k_aug, v_aug)


def _pool_outproj_kernel(x_ref, att_ref, p_ref, halo_ref, wpool_ref, scale_ref,
                         wo_att_ref, wo_pool_ref, o_ref, *, tiles_per_seq):
    i = pl.program_id(0)
    tm = x_ref.shape[0]
    tile_in_seq = i % tiles_per_seq
    p = p_ref[...]
    halo = jnp.where(tile_in_seq == 0, 0.0, halo_ref[...])
    ext = jnp.concatenate([halo, p], axis=0)
    pos = tile_in_seq * tm + lax.broadcasted_iota(jnp.int32, (tm, 1), 0)
    count = (pos + 1).astype(F32)
    mixed = []
    for gi, w in enumerate(POOL_WINDOWS):
        sl = slice(gi * POOL_GROUP_DIM, (gi + 1) * POOL_GROUP_DIM)
        acc = ext[:, sl]
        sh = 1
        while sh < w:
            acc = acc + pltpu.roll(acc, sh, axis=0)
            sh *= 2
        mean = acc[POOL_HALO:, :] / jnp.minimum(count, float(w))
        pooled = (mean - p[:, sl]).astype(BF16)
        mixed.append(jnp.dot(pooled, wpool_ref[gi], preferred_element_type=F32))
    pool = (jnp.concatenate(mixed, axis=1) * scale_ref[...]).astype(BF16)
    y = jnp.dot(att_ref[...], wo_att_ref[...], preferred_element_type=F32)
    y = y + jnp.dot(pool, wo_pool_ref[...], preferred_element_type=F32)
    o_ref[...] = x_ref[...] + y


def _pool_outproj(x2d, att, p_in, w_pool, pool_scale, w_out, *, seq, tm):
    t, d = x2d.shape
    wo_att = w_out[:FOX_WIDTH].reshape(FOX_HEADS, FOX_HEAD_DIM, d)
    wo_att = jnp.pad(wo_att, ((0, 0), (0, LANES - FOX_HEAD_DIM), (0, 0)))
    wo_att = wo_att.reshape(FOX_HEADS * LANES, d)
    wo_pool = w_out[FOX_WIDTH:]
    halo_blocks = tm // POOL_HALO
    kern = functools.partial(_pool_outproj_kernel, tiles_per_seq=seq // tm)
    return pl.pallas_call(
        kern,
        grid=(t // tm,),
        in_specs=[
            pl.BlockSpec((tm, d), lambda i: (i, 0)),
            pl.BlockSpec((tm, FOX_HEADS * LANES), lambda i: (i, 0)),
            pl.BlockSpec((tm, POOL_WIDTH), lambda i: (i, 0)),
            pl.BlockSpec((POOL_HALO, POOL_WIDTH),
                         lambda i: (jnp.maximum(i * halo_blocks - 1, 0), 0)),
            pl.BlockSpec(w_pool.shape, lambda i: (0, 0, 0)),
            pl.BlockSpec((1, POOL_WIDTH), lambda i: (0, 0)),
            pl.BlockSpec(wo_att.shape, lambda i: (0, 0)),
            pl.BlockSpec(wo_pool.shape, lambda i: (0, 0)),
        ],
        out_specs=pl.BlockSpec((tm, d), lambda i: (i, 0)),
        out_shape=jax.ShapeDtypeStruct((t, d), F32),
        compiler_params=_params("parallel"),
        name="pool_outproj",
    )(x2d, att, p_in, p_in, w_pool, pool_scale, wo_att, wo_pool)


def _swiglu_partial(h, w1, w3, w2):
    a = jnp.dot(h, w1, preferred_element_type=F32)
    b = jnp.dot(h, w3, preferred_element_type=F32)
    act = a * _sigmoid(a) * b
    return jnp.dot(act.astype(BF16), w2, preferred_element_type=F32)


MXU_TILE = 256


def _ff_splits(f):
    cut = -(-(f // MXU_TILE) // 2) * MXU_TILE
    return ((0, cut), (cut, f)) if 0 < cut < f else ((0, f),)


def _swiglu(h, w1_ref, w3_ref, w2_ref):
    y = None
    for lo, hi in _ff_splits(w1_ref.shape[1]):
        part = _swiglu_partial(h, w1_ref[:, lo:hi], w3_ref[:, lo:hi], w2_ref[lo:hi, :])
        y = part if y is None else y + part
    return y


def _ffn_kernel(x_ref, g_ref, w1_ref, w3_ref, w2_ref, o_ref):
    x = x_ref[...]
    h = _rms(x, g_ref[...]).astype(BF16)
    o_ref[...] = x + _swiglu(h, w1_ref, w3_ref, w2_ref)


def _ffn(x2d, gain, w1, w3, w2, *, tm):
    t, d = x2d.shape
    resident = lambda w: pl.BlockSpec(w.shape, lambda i: (0, 0),
                                      pipeline_mode=pl.Buffered(1))
    return pl.pallas_call(
        _ffn_kernel,
        grid=(t // tm,),
        in_specs=[
            pl.BlockSpec((tm, d), lambda i: (i, 0)),
            pl.BlockSpec((1, d), lambda i: (0, 0)),
            resident(w1), resident(w3), resident(w2),
        ],
        out_specs=pl.BlockSpec((tm, d), lambda i: (i, 0)),
        out_shape=jax.ShapeDtypeStruct((t, d), F32),
        compiler_params=_params("parallel"),
        name="dense_ffn",
    )(x2d, gain, w1, w3, w2)


def _gelu_tanh(x):
    c = math.sqrt(2.0 / math.pi)
    return 0.5 * x * (1.0 + jnp.tanh(c * (x + 0.044715 * (x * x * x))))


def _s5_kernel(x_ref, g_ref, win_ref, perm_ref, permt_ref, bblk_ref, cblk_ref,
               are_ref, aim_ref, d_ref, o_ref, xs_ref, st_ref):
    nb, tc, d = x_ref.shape
    rows = nb * tc
    n_slabs = d // LANES
    sw = 2 * SSM_SLAB_STATE

    @pl.when(pl.program_id(0) == 0)
    def _():
        st_ref[...] = jnp.zeros_like(st_ref)

    x = x_ref[...].reshape(rows, d)
    h = _rms(x, g_ref[...]).astype(BF16)
    h_tb = jnp.dot(perm_ref[...], h, preferred_element_type=F32).astype(BF16)
    u = jnp.dot(h_tb, win_ref[...], preferred_element_type=F32)
    ub = u.astype(BF16)
    hs = SSM_SLAB_STATE
    ys = []
    for s in range(n_slabs):
        lo = s * sw
        xs_ref[:, lo:lo + sw] = jnp.dot(
            ub[:, s * LANES:(s + 1) * LANES], bblk_ref[s],
            preferred_element_type=F32)
        a_r = are_ref[:, lo:lo + hs]
        a_i = aim_ref[:, lo:lo + hs]
        x_r = st_ref[:, lo:lo + hs]
        x_i = st_ref[:, lo + hs:lo + sw]
        for t in range(tc):
            r0 = t * SUBLANES
            new_r = a_r * x_r - a_i * x_i + xs_ref[r0:r0 + SUBLANES, lo:lo + hs]
            new_i = a_r * x_i + a_i * x_r + xs_ref[r0:r0 + SUBLANES, lo + hs:lo + sw]
            xs_ref[r0:r0 + SUBLANES, lo:lo + hs] = new_r
            xs_ref[r0:r0 + SUBLANES, lo + hs:lo + sw] = new_i
            x_r, x_i = new_r, new_i
        st_ref[:, lo:lo + hs] = x_r
        st_ref[:, lo + hs:lo + sw] = x_i
        ys.append(jnp.dot(xs_ref[:, lo:lo + sw].astype(BF16), cblk_ref[s],
                          preferred_element_type=F32))
    y = jnp.concatenate(ys, axis=1) + d_ref[...] * u
    g = _gelu_tanh(y).astype(BF16)
    g_bt = jnp.dot(permt_ref[...], g, preferred_element_type=F32)
    o_ref[...] = g_bt.reshape(nb, tc, d).astype(o_ref.dtype)


def _s5_mixer(x3d, gain, w_in, bblk, cblk, a_re, a_im, d_skip, *, tc):
    nb, seq, d = x3d.shape
    rows = nb * tc
    n_slabs = d // LANES
    sw = 2 * SSM_SLAB_STATE
    r = np.arange(rows)
    perm = np.zeros((rows, rows), np.float32)
    perm[(r % tc) * nb + r // tc, r] = 1.0
    perm_j = jnp.asarray(perm, BF16)
    permt_j = jnp.asarray(perm.T, BF16)
    kern = _s5_kernel
    full2 = lambda a: pl.BlockSpec(a.shape, lambda i: (0, 0))
    full3 = lambda a: pl.BlockSpec(a.shape, lambda i: (0, 0, 0))
    return pl.pallas_call(
        kern,
        grid=(seq // tc,),
        in_specs=[
            pl.BlockSpec((nb, tc, d), lambda i: (0, i, 0)),
            full2(gain), full2(w_in), full2(perm_j), full2(permt_j),
            full3(bblk), full3(cblk), full2(a_re), full2(a_im), full2(d_skip),
        ],
        out_specs=pl.BlockSpec((nb, tc, d), lambda i: (0, i, 0)),
        out_shape=jax.ShapeDtypeStruct((nb, seq, d), BF16),
        scratch_shapes=[
            pltpu.VMEM((rows, n_slabs * sw), F32),
            pltpu.VMEM((SUBLANES, n_slabs * sw), F32),
        ],
        compiler_params=_params("arbitrary"),
        name="s5_mixer",
    )(x3d, gain, w_in, perm_j, permt_j, bblk, cblk, a_re, a_im, d_skip)


def _s5_coefficients(a_re, a_im, log_dt, b_re, b_im, c_re, c_im):
    dt = jnp.exp(log_dt.astype(F32))[:, None]
    ar = a_re.astype(F32)
    ai = a_im.astype(F32)
    mag = jnp.exp(ar * dt)
    abar_re = mag * jnp.cos(ai * dt)
    abar_im = mag * jnp.sin(ai * dt)
    den = ar * ar + ai * ai
    nr = abar_re - 1.0
    ni = abar_im
    coef_re = (nr * ar + ni * ai) / den
    coef_im = (ni * ar - nr * ai) / den
    br = b_re.astype(F32)
    bi = b_im.astype(F32)
    bbar_re = coef_re[..., None] * br - coef_im[..., None] * bi
    bbar_im = coef_re[..., None] * bi + coef_im[..., None] * br
    n_groups = ar.shape[0]
    n_slabs = n_groups // SSM_SLAB_GROUPS
    eye = jnp.eye(SSM_SLAB_GROUPS, dtype=F32)

    def in_block(bb):
        bb = bb.reshape(n_slabs, SSM_SLAB_GROUPS, SSM_STATE, SSM_GROUP)
        blk = jnp.einsum('sgph,gk->sghkp', bb, eye)
        return blk.reshape(n_slabs, LANES, SSM_SLAB_STATE)

    def out_block(cc):
        cc = cc.reshape(n_slabs, SSM_SLAB_GROUPS, SSM_GROUP, SSM_STATE)
        blk = jnp.einsum('sghp,gk->sgpkh', cc, eye)
        return blk.reshape(n_slabs, SSM_SLAB_STATE, LANES)

    bblk = jnp.concatenate([in_block(bbar_re), in_block(bbar_im)], axis=2)
    cblk = jnp.concatenate([out_block(c_re.astype(F32)),
                            -out_block(c_im.astype(F32))], axis=1)

    def lanes(a):
        a = a.reshape(n_slabs, 1, SSM_SLAB_STATE)
        a = jnp.concatenate([a, a], axis=2).reshape(1, -1)
        return jnp.broadcast_to(a, (SUBLANES, a.shape[1]))

    return (bblk.astype(BF16), cblk.astype(BF16), lanes(abar_re), lanes(abar_im))


def _glu_kernel(x_ref, g_ref, wa_ref, wb_ref, o_ref):
    g = g_ref[...]
    a = jnp.dot(g, wa_ref[...], preferred_element_type=F32)
    b = jnp.dot(g, wb_ref[...], preferred_element_type=F32)
    o_ref[...] = x_ref[...] + a * _sigmoid(b)


def _glu(x2d, g2d, wa, wb, *, tm):
    t, d = x2d.shape
    return pl.pallas_call(
        _glu_kernel,
        grid=(t // tm,),
        in_specs=[
            pl.BlockSpec((tm, d), lambda i: (i, 0)),
            pl.BlockSpec((tm, d), lambda i: (i, 0)),
            pl.BlockSpec(wa.shape, lambda i: (0, 0)),
            pl.BlockSpec(wb.shape, lambda i: (0, 0)),
        ],
        out_specs=pl.BlockSpec((tm, d), lambda i: (i, 0)),
        out_shape=jax.ShapeDtypeStruct((t, d), F32),
        compiler_params=_params("parallel"),
        name="glu_outproj",
    )(x2d, g2d, wa, wb)


MOE_CHUNK = 512
MOE_FIRST = 192
MOE_REST = MOE_CHUNK - MOE_FIRST
MOE_PIECES = ((0, MOE_FIRST), (MOE_FIRST, MOE_REST))
MOE_TILE = 512
SEG_ALIGN = 16


def _router_kernel(x_ref, g_ref, w_ref, b_ref, h_ref, pos_ref, post_ref, cnt_ref):
    tm, d = x_ref.shape
    h = _rms(x_ref[...], g_ref[...])
    h_hi = h.astype(BF16)
    h_ref[:, 0:d] = h_hi
    h_lo = (h - h_hi.astype(F32)).astype(BF16)
    p_hi = jnp.dot(h_hi, w_ref[...], preferred_element_type=F32)
    p_lo = jnp.dot(h_lo, w_ref[...], preferred_element_type=F32)
    logits = p_hi + pltpu.roll(p_hi, LANES - N_EXPERTS, axis=1) + p_lo + b_ref[...]
    lane = lax.broadcasted_iota(jnp.int32, logits.shape, 1)
    logits = jnp.where(lane < N_EXPERTS, logits, -jnp.inf)
    m1 = jnp.max(logits, axis=1, keepdims=True)
    i1 = jnp.min(jnp.where(logits == m1, lane, LANES), axis=1, keepdims=True)
    rest = jnp.where(lane == i1, -jnp.inf, logits)
    m2 = jnp.max(rest, axis=1, keepdims=True)
    i2 = jnp.min(jnp.where(rest == m2, lane, LANES), axis=1, keepdims=True)
    e2 = jnp.exp(m2 - m1)
    g1 = 1.0 / (1.0 + e2)
    g2 = e2 / (1.0 + e2)
    gate = jnp.where(lane == i1, g1, 0.0) + jnp.where(lane == i2, g2, 0.0)
    g_hi = gate.astype(BF16).astype(F32)
    g_r = gate - g_hi
    g_mid = g_r.astype(BF16).astype(F32)
    packed = jnp.where(lane < N_EXPERTS, g_hi,
                       jnp.where(lane < 2 * N_EXPERTS,
                                 pltpu.roll(g_mid, N_EXPERTS, axis=1),
                                 pltpu.roll(g_r - g_mid, 2 * N_EXPERTS, axis=1)))
    h_ref[:, d:d + LANES] = packed.astype(BF16)
    member = jnp.where((lane == i1) | (lane == i2), 1.0, 0.0)
    row = lax.broadcasted_iota(jnp.int32, member.shape, 0)
    c = member
    sh = 1
    while sh < tm:
        c = c + jnp.where(row >= sh, pltpu.roll(c, sh, axis=0), 0.0)
        sh *= 2
    pos = jnp.where(member > 0.0, c - member, -1.0)
    pos_ref[...] = pos
    post_ref[0] = pos.T[0:SUBLANES, :]
    cnt_ref[0] = jnp.broadcast_to(c[tm - 1:tm, :], (SUBLANES, LANES))


def _router(x2d, gain, w_pad, b_pad):
    t, d = x2d.shape
    tm = MOE_CHUNK
    n_chunks = t // tm
    return pl.pallas_call(
        _router_kernel,
        grid=(n_chunks,),
        in_specs=[
            pl.BlockSpec((tm, d), lambda i: (i, 0)),
            pl.BlockSpec((1, d), lambda i: (0, 0)),
            pl.BlockSpec((d, LANES), lambda i: (0, 0)),
            pl.BlockSpec((1, LANES), lambda i: (0, 0)),
        ],
        out_specs=[
            pl.BlockSpec((tm, d + LANES), lambda i: (i, 0)),
            pl.BlockSpec((tm, LANES), lambda i: (i, 0)),
            pl.BlockSpec((1, SUBLANES, tm), lambda i: (i, 0, 0)),
            pl.BlockSpec((1, SUBLANES, LANES), lambda i: (i, 0, 0)),
        ],
        out_shape=[
            jax.ShapeDtypeStruct((t, d + LANES), BF16),
            jax.ShapeDtypeStruct((t, LANES), F32),
            jax.ShapeDtypeStruct((n_chunks, SUBLANES, tm), F32),
            jax.ShapeDtypeStruct((n_chunks, SUBLANES, LANES), F32),
        ],
        compiler_params=_params("parallel"),
        name="router",
    )(x2d, gain, w_pad, b_pad)


def _moe_layout(cnt, n_tokens):
    n_chunks = cnt.shape[0]
    seg = (cnt + SEG_ALIGN - 1) // SEG_ALIGN * SEG_ALIGN
    used = jnp.sum(seg, axis=0)
    spare = max(MOE_FIRST, MOE_REST)
    padded = (used + spare + MOE_TILE - 1) // MOE_TILE * MOE_TILE
    ends = jnp.cumsum(padded)
    start = ends - padded
    off = start[None, :] + jnp.cumsum(seg, axis=0) - seg
    max_rows = (TOP_K * n_tokens + (SEG_ALIGN - 1) * N_EXPERTS * n_chunks
                + N_EXPERTS * (spare + MOE_TILE))
    n_tiles = -(-max_rows // MOE_TILE)
    tile_start = jnp.arange(n_tiles, dtype=jnp.int32) * MOE_TILE
    tile_e = jnp.sum((tile_start[:, None] >= ends[None, :]).astype(jnp.int32), axis=1)
    tile_e = jnp.minimum(tile_e, N_EXPERTS - 1)
    tile_rows = jnp.clip(used[tile_e] - (tile_start - start[tile_e]), 0, MOE_TILE)
    return (off.reshape(-1).astype(jnp.int32), cnt.reshape(-1).astype(jnp.int32),
            tile_e.astype(jnp.int32), tile_rows.astype(jnp.int32), n_tiles)


def _segment_copies(hbm_ref, buf_ref, sem_ref, off_ref, chunk, to_hbm, slot=()):
    copies = []
    for e in range(N_EXPERTS):
        off = pl.multiple_of(off_ref[chunk * N_EXPERTS + e], SEG_ALIGN)
        for piece, (first, rows) in enumerate(MOE_PIECES):
            hbm = hbm_ref.at[pl.ds(off + first, rows)]
            buf = buf_ref.at[(*slot, e, pl.ds(first, rows))]
            src, dst = (buf, hbm) if to_hbm else (hbm, buf)
            copies.append(pltpu.make_async_copy(src, dst, sem_ref.at[(*slot, e, piece)]))
    return copies


def _onehot(index, target):
    return jnp.where(index == target, 1.0, 0.0).astype(BF16)


def _gather_kernel(off_ref, cnt_ref, h_ref, post_ref, zeros_hbm, o_hbm, stage_ref,
                   sem_ref):
    del zeros_hbm
    c = pl.program_id(0)
    h = h_ref[...]
    copies = _segment_copies(o_hbm, stage_ref, sem_ref, off_ref, c, to_hbm=True)

    def move_piece(e, piece):
        first, rows = MOE_PIECES[piece]
        rank = first + lax.broadcasted_iota(jnp.int32, (rows, MOE_CHUNK), 0)
        onehot = _onehot(rank.astype(F32), post_ref[0, e:e + 1, :])
        stage_ref[e, first:first + rows] = jnp.dot(
            onehot, h, preferred_element_type=F32).astype(BF16)
        copies[2 * e + piece].start()

    for e in range(N_EXPERTS):
        move_piece(e, 0)

        @pl.when(cnt_ref[c * N_EXPERTS + e] > MOE_FIRST)
        def _(e=e):
            move_piece(e, 1)

    for e in range(N_EXPERTS):
        copies[2 * e].wait()

        @pl.when(cnt_ref[c * N_EXPERTS + e] > MOE_FIRST)
        def _(e=e):
            copies[2 * e + 1].wait()


def _gather(h2d, post, off, cnt, *, n_rows):
    t, d = h2d.shape
    n_chunks = t // MOE_CHUNK
    grid_spec = pltpu.PrefetchScalarGridSpec(
        num_scalar_prefetch=2,
        grid=(n_chunks,),
        in_specs=[
            pl.BlockSpec((MOE_CHUNK, d), lambda i, off, cnt: (i, 0)),
            pl.BlockSpec((1, SUBLANES, MOE_CHUNK), lambda i, off, cnt: (i, 0, 0)),
            pl.BlockSpec(memory_space=pl.ANY),
        ],
        out_specs=pl.BlockSpec(memory_space=pl.ANY),
        scratch_shapes=[
            pltpu.VMEM((N_EXPERTS, MOE_CHUNK, d), BF16),
            pltpu.SemaphoreType.DMA((N_EXPERTS, 2)),
        ],
    )
    return pl.pallas_call(
        _gather_kernel,
        grid_spec=grid_spec,
        out_shape=jax.ShapeDtypeStruct((n_rows, d), BF16),
        input_output_aliases={4: 0},
        compiler_params=_params("arbitrary"),
        name="moe_gather",
    )(off, cnt, h2d, post, jnp.zeros((n_rows, d), BF16))


def _expert_kernel(te_ref, tr_ref, x_ref, w1_ref, w3_ref, w2_ref, o_ref):
    i = pl.program_id(0)
    rows = tr_ref[i]
    d = o_ref.shape[1]

    def run(n):
        row = lax.broadcasted_iota(jnp.int32, (n, x_ref.shape[1]), 0)
        x = jnp.where(row < rows, x_ref[0:n, :], jnp.zeros((), x_ref.dtype))
        lane = lax.broadcasted_iota(jnp.int32, (n, LANES), 1)
        mine = (lane < 3 * N_EXPERTS) & (jnp.bitwise_and(lane, N_EXPERTS - 1) == te_ref[i])
        gate = jnp.sum(jnp.where(mine, x[:, d:d + LANES].astype(F32), 0.0),
                       axis=1, keepdims=True)
        y = _swiglu(x[:, 0:d], w1_ref, w3_ref, w2_ref)
        o_ref[0:n, :] = (gate * y).astype(o_ref.dtype)

    half = MOE_TILE // 2

    @pl.when(rows > half)
    def _():
        run(MOE_TILE)

    @pl.when((rows > 0) & (rows <= half))
    def _():
        run(half)
        o_ref[half:MOE_TILE, :] = jnp.zeros((MOE_TILE - half, d), o_ref.dtype)

    @pl.when(rows == 0)
    def _():
        o_ref[...] = jnp.zeros_like(o_ref)


def _experts(x_sorted, w1, w3, w2, tile_e, tile_rows):
    assert N_EXPERTS & (N_EXPERTS - 1) == 0
    n_rows = x_sorted.shape[0]
    d, f = w1.shape[1:]
    grid_spec = pltpu.PrefetchScalarGridSpec(
        num_scalar_prefetch=2,
        grid=(n_rows // MOE_TILE,),
        in_specs=[
            pl.BlockSpec((MOE_TILE, d + LANES), lambda i, te, tr: (i, 0)),
            pl.BlockSpec((None, d, f), lambda i, te, tr: (te[i], 0, 0)),
            pl.BlockSpec((None, d, f), lambda i, te, tr: (te[i], 0, 0)),
            pl.BlockSpec((None, f, d), lambda i, te, tr: (te[i], 0, 0)),
        ],
        out_specs=pl.BlockSpec((MOE_TILE, d), lambda i, te, tr: (i, 0)),
    )
    return pl.pallas_call(
        _expert_kernel,
        grid_spec=grid_spec,
        out_shape=jax.ShapeDtypeStruct((n_rows, d), BF16),
        compiler_params=_params("arbitrary"),
        name="moe_experts",
    )(tile_e, tile_rows, x_sorted, w1, w3, w2)


def _combine_kernel(off_ref, cnt_ref, x_ref, pos_ref, fg_ref, y_hbm,
                    o_ref, ybuf_ref, acc_ref, sem_ref):
    c = pl.program_id(0)
    n_chunks = pl.num_programs(0)
    slot = c % 2

    def fetch(chunk, sl, start):
        copies = _segment_copies(y_hbm, ybuf_ref, sem_ref, off_ref, chunk,
                                 to_hbm=False, slot=(sl,))
        for e in range(N_EXPERTS):
            first, second = copies[2 * e], copies[2 * e + 1]
            if start:
                first.start()
            else:
                first.wait()

            @pl.when(cnt_ref[chunk * N_EXPERTS + e] > MOE_FIRST)
            def _(second=second):
                if start:
                    second.start()
                else:
                    second.wait()

    @pl.when(c == 0)
    def _():
        fetch(c, slot, True)

    @pl.when(c + 1 < n_chunks)
    def _():
        fetch(c + 1, 1 - slot, True)

    fetch(c, slot, False)

    def piece_rows(e, piece):
        first, rows = MOE_PIECES[piece]
        rank = first + lax.broadcasted_iota(jnp.int32, (MOE_CHUNK, rows), 1)
        onehot = _onehot(rank.astype(F32), pos_ref[:, e:e + 1])
        return jnp.dot(onehot, ybuf_ref[slot, e, first:first + rows],
                       preferred_element_type=F32)

    acc = x_ref[...]
    for e in range(N_EXPERTS):
        acc = acc + piece_rows(e, 0)
    acc_ref[...] = acc
    for e in range(N_EXPERTS):
        @pl.when(cnt_ref[c * N_EXPERTS + e] > MOE_FIRST)
        def _(e=e):
            acc_ref[...] += piece_rows(e, 1)

    o_ref[...] = _rms(acc_ref[...], fg_ref[...])


def _combine(x2d, pos, final_gain, y_sorted, off, cnt):
    t, d = x2d.shape
    n_chunks = t // MOE_CHUNK
    grid_spec = pltpu.PrefetchScalarGridSpec(
        num_scalar_prefetch=2,
        grid=(n_chunks,),
        in_specs=[
            pl.BlockSpec((MOE_CHUNK, d), lambda i, off, cnt: (i, 0)),
            pl.BlockSpec((MOE_CHUNK, LANES), lambda i, off, cnt: (i, 0)),
            pl.BlockSpec((1, d), lambda i, off, cnt: (0, 0)),
            pl.BlockSpec(memory_space=pl.ANY),
        ],
        out_specs=pl.BlockSpec((MOE_CHUNK, d), lambda i, off, cnt: (i, 0)),
        scratch_shapes=[
            pltpu.VMEM((2, N_EXPERTS, MOE_CHUNK, d), BF16),
            pltpu.VMEM((MOE_CHUNK, d), F32),
            pltpu.SemaphoreType.DMA((2, N_EXPERTS, 2)),
        ],
    )
    return pl.pallas_call(
        _combine_kernel,
        grid_spec=grid_spec,
        out_shape=jax.ShapeDtypeStruct((t, d), F32),
        compiler_params=_params("arbitrary"),
        name="moe_combine",
    )(off, cnt, x2d, pos, final_gain, y_sorted)


def _moe(x2d, gain, router_w, router_b, w1, w3, w2, final_gain):
    t, d = x2d.shape
    rw_hi = router_w.astype(BF16)
    rw_lo = (router_w.astype(F32) - rw_hi.astype(F32)).astype(BF16)
    rw = jnp.pad(jnp.concatenate([rw_hi, rw_lo], axis=1),
                 ((0, 0), (0, LANES - 2 * N_EXPERTS)))
    rb = jnp.pad(router_b.astype(F32), (0, LANES - N_EXPERTS)).reshape(1, LANES)
    h, pos, post, cnt = _router(x2d, gain, rw, rb)
    cnt = cnt[:, 0, :N_EXPERTS].astype(jnp.int32)
    off, cnt_flat, tile_e, tile_rows, n_tiles = _moe_layout(cnt, t)
    h_sorted = _gather(h, post, off, cnt_flat, n_rows=n_tiles * MOE_TILE)
    y_sorted = _experts(h_sorted, w1, w3, w2, tile_e, tile_rows)
    return _combine(x2d, pos, final_gain, y_sorted, off, cnt_flat)


def _even_weights(w_in, b_forget):
    fw = FOX_WIDTH
    wq = w_in[:, :fw] * FOX_HEAD_DIM ** -0.5
    wf = jnp.pad(w_in[:, 3 * fw:3 * fw + FOX_HEADS],
                 ((0, 0), (0, LANES - FOX_HEADS)))
    wp = w_in[:, 3 * fw + FOX_HEADS:]
    w_all = jnp.concatenate([wq, w_in[:, fw:3 * fw], wp, wf], axis=1).astype(BF16)
    bias = jnp.pad(b_forget.astype(F32), (0, LANES - FOX_HEADS)).reshape(1, LANES)
    qw = FOX_HEADS * LANES
    place = np.zeros((LANES, 2 * qw), np.float32)
    for hh in range(FOX_HEADS):
        for piece in range(3):
            place[piece * FOX_HEADS + hh, hh * LANES + BIAS_LANE + piece] = 1.0
            place[piece * FOX_HEADS + hh, qw + hh * LANES + ONES_LANE + piece] = -1.0
    return w_all, bias, jnp.asarray(place, BF16)


def kernel(x, even_mix_norm, even_w_in, even_b_forget, even_w_pool, even_pool_scale, even_w_out, even_ffn_norm, even_ffn_w1, even_ffn_w3, even_ffn_w2, odd_mix_norm, odd_w_in, ssm_a_re, ssm_a_im, ssm_log_dt, ssm_b_re, ssm_b_im, ssm_c_re, ssm_c_im, ssm_d, odd_w_glu_a, odd_w_glu_b, odd_moe_norm, router_w, router_b, expert_w1, expert_w3, expert_w2, final_norm):
    b, s, d = x.shape
    t = b * s
    assert b == SUBLANES, "the S5 recurrence keeps one batch row per sublane"
    x2d = x.reshape(t, d)
    row = lambda v: v.reshape(1, -1).astype(F32)

    w_all, bias, place = _even_weights(even_w_in[0], even_b_forget[0])
    q_aug, k_aug, v, p_in = _even_inproj(
        x2d, row(even_mix_norm[0]), w_all, bias, place, seq=s, tm=512)
    att = _attention(q_aug, k_aug, v, batch=b, seq=s, tq=256)
    x1 = _pool_outproj(x2d, att, p_in, even_w_pool[0].astype(BF16),
                       row(even_pool_scale[0]), even_w_out[0].astype(BF16),
                       seq=s, tm=512)
    x2 = _ffn(x1, row(even_ffn_norm[0]), even_ffn_w1[0].astype(BF16),
              even_ffn_w3[0].astype(BF16), even_ffn_w2[0].astype(BF16), tm=1024)

    bblk, cblk, a_re, a_im = _s5_coefficients(
        ssm_a_re[0], ssm_a_im[0], ssm_log_dt[0], ssm_b_re[0], ssm_b_im[0],
        ssm_c_re[0], ssm_c_im[0])
    g = _s5_mixer(x2.reshape(b, s, d), row(odd_mix_norm[0]),
                  odd_w_in[0].astype(BF16), bblk, cblk, a_re, a_im,
                  row(ssm_d[0]), tc=32)
    x3 = _glu(x2, g.reshape(t, d), odd_w_glu_a[0].astype(BF16),
              odd_w_glu_b[0].astype(BF16), tm=512)
    out = _moe(x3, row(odd_moe_norm[0]), router_w[0], router_b[0],
               expert_w1[0].astype(BF16), expert_w3[0].astype(BF16),
               expert_w2[0].astype(BF16), row(final_norm))
    return out.reshape(b, s, d)
```

```python
import functools
import math

import numpy as np
import jax
import jax.numpy as jnp
from jax import lax
from jax.experimental import pallas as pl
from jax.experimental.pallas import tpu as pltpu

F32 = jnp.float32
BF16 = jnp.bfloat16

EPS = 1e-6
NEG_INF = -1e30
LANES = 128
SUBLANES = 8
SEG_ALIGN = 16
VMEM_LIMIT = 56 * 1024 * 1024

FOX_HEADS = 8
FOX_HEAD_DIM = 64
FOX_WIDTH = FOX_HEADS * FOX_HEAD_DIM
POOL_WINDOWS = (2, 4, 8, 16)
POOL_GROUP_DIM = 128
POOL_WIDTH = len(POOL_WINDOWS) * POOL_GROUP_DIM
POOL_HALO = 16
SSM_GROUP = 16
SSM_STATE = 64
SSM_SLAB_GROUPS = LANES // SSM_GROUP
SSM_SLAB_STATE = SSM_SLAB_GROUPS * SSM_STATE
S5_STEP = 32
N_EXPERTS = 8
TOP_K = 2

BIAS_LANE = FOX_HEAD_DIM
ONES_LANE = FOX_HEAD_DIM + 3
DENOM_LANE = FOX_HEAD_DIM


def _params(*sem):
    return pltpu.CompilerParams(dimension_semantics=sem,
                                vmem_limit_bytes=VMEM_LIMIT)


def _rms(x, g):
    ms = jnp.mean(x * x, axis=-1, keepdims=True)
    return x * lax.rsqrt(ms + EPS) * g


def _sigmoid(x):
    return 1.0 / (1.0 + jnp.exp(-x))


def _lane_range_ones(lo, hi):
    lane = lax.broadcasted_iota(jnp.int32, (1, LANES), 1)
    return jnp.where((lane >= lo) & (lane < hi), 1.0, 0.0).astype(F32)


def _even_inproj_kernel(x_ref, g_ref, w_ref, bias_ref, place_ref,
                        q_ref, k_ref, v_ref, p_ref, carry_ref, *, tiles_per_seq):
    i = pl.program_id(0)
    tm = x_ref.shape[0]
    fw = FOX_WIDTH
    qw = FOX_HEADS * LANES
    h = _rms(x_ref[...], g_ref[...]).astype(BF16)
    z = jnp.dot(h, w_ref[...], preferred_element_type=F32)
    p_ref[...] = z[:, 3 * fw:3 * fw + POOL_WIDTH]
    fg = z[:, 3 * fw + POOL_WIDTH:] + bias_ref[...]
    lf = jnp.minimum(fg, 0.0) - jnp.log1p(jnp.exp(-jnp.abs(fg)))
    row = lax.broadcasted_iota(jnp.int32, lf.shape, 0)
    c = lf
    sh = 1
    while sh < tm:
        c = c + jnp.where(row >= sh, pltpu.roll(c, sh, axis=0), 0.0)
        sh *= 2

    @pl.when(i % tiles_per_seq == 0)
    def _():
        carry_ref[...] = jnp.zeros_like(carry_ref)

    c = c + carry_ref[0:1, :]
    carry_ref[...] = jnp.broadcast_to(c[tm - 1:tm, :], carry_ref.shape)
    hi = c.astype(BF16).astype(F32)
    r1 = c - hi
    mid = r1.astype(BF16).astype(F32)
    lo = r1 - mid
    lane = lax.broadcasted_iota(jnp.int32, (tm, LANES), 1)
    packed = jnp.where(lane < FOX_HEADS, hi,
                       jnp.where(lane < 2 * FOX_HEADS,
                                 pltpu.roll(mid, FOX_HEADS, axis=1),
                                 pltpu.roll(lo, 2 * FOX_HEADS, axis=1)))
    placed = jnp.dot(packed.astype(BF16), place_ref[...],
                     preferred_element_type=F32)
    low = lane < FOX_HEAD_DIM
    ones_q = _lane_range_ones(ONES_LANE, ONES_LANE + 3)
    ones_k = _lane_range_ones(BIAS_LANE, BIAS_LANE + 3)
    ones_v = _lane_range_ones(DENOM_LANE, DENOM_LANE + 1)

    def head_lanes(base, hh):
        pair = z[:, base + (hh // 2) * LANES:base + (hh // 2 + 1) * LANES]
        return pltpu.roll(pair, FOX_HEAD_DIM, axis=1) if hh % 2 else pair

    for hh in range(FOX_HEADS):
        sl = slice(hh * LANES, (hh + 1) * LANES)
        q_ref[:, sl] = jnp.where(low, head_lanes(0, hh),
                                 placed[:, sl] + ones_q).astype(BF16)
        k_ref[:, sl] = jnp.where(
            low, head_lanes(fw, hh),
            placed[:, qw + hh * LANES:qw + (hh + 1) * LANES] + ones_k).astype(BF16)
        v_ref[:, sl] = jnp.where(low, head_lanes(2 * fw, hh), ones_v).astype(BF16)


def _even_inproj(x2d, gain, w_all, bias, place, *, seq, tm):
    t, d = x2d.shape
    n = w_all.shape[1]
    qw = FOX_HEADS * LANES
    kern = functools.partial(_even_inproj_kernel, tiles_per_seq=seq // tm)
    return pl.pallas_call(
        kern,
        grid=(t // tm,),
        in_specs=[
            pl.BlockSpec((tm, d), lambda i: (i, 0)),
            pl.BlockSpec((1, d), lambda i: (0, 0)),
            pl.BlockSpec((d, n), lambda i: (0, 0)),
            pl.BlockSpec((1, LANES), lambda i: (0, 0)),
            pl.BlockSpec(place.shape, lambda i: (0, 0)),
        ],
        out_specs=[
            pl.BlockSpec((tm, qw), lambda i: (i, 0)),
            pl.BlockSpec((tm, qw), lambda i: (i, 0)),
            pl.BlockSpec((tm, qw), lambda i: (i, 0)),
            pl.BlockSpec((tm, POOL_WIDTH), lambda i: (i, 0)),
        ],
        out_shape=[
            jax.ShapeDtypeStruct((t, qw), BF16),
            jax.ShapeDtypeStruct((t, qw), BF16),
            jax.ShapeDtypeStruct((t, qw), BF16),
            jax.ShapeDtypeStruct((t, POOL_WIDTH), F32),
        ],
        scratch_shapes=[pltpu.VMEM((SUBLANES, LANES), F32)],
        compiler_params=_params("arbitrary"),
        name="even_inproj",
    )(x2d, gain, w_all, bias, place)


def _dot_nt(a, b):
    return lax.dot_general(a, b, (((1,), (1,)), ((), ())),
                           preferred_element_type=F32)


def _attn_kernel(q_ref, k_ref, v_ref, *rest, tq, n_cast):
    cast_in, o_ref, cast_out = rest[:n_cast], rest[n_cast], rest[n_cast + 1:]
    seq = q_ref.shape[0]
    row = lax.broadcasted_iota(jnp.int32, (tq, tq), 0)
    col = lax.broadcasted_iota(jnp.int32, (tq, tq), 1)
    causal = col <= row
    for qi in range(seq // tq):
        r0 = qi * tq
        q = q_ref[r0:r0 + tq, :]
        s_diag = jnp.where(causal, _dot_nt(q, k_ref[r0:r0 + tq, :]), NEG_INF)
        m = jnp.max(s_diag, axis=1, keepdims=True)
        if qi > 0:
            s_past = _dot_nt(q, k_ref[0:r0, :])
            m = jnp.maximum(m, jnp.max(s_past, axis=1, keepdims=True))
        acc = jnp.dot(jnp.exp(s_diag - m).astype(BF16), v_ref[r0:r0 + tq, :],
                      preferred_element_type=F32)
        if qi > 0:
            acc = acc + jnp.dot(jnp.exp(s_past - m).astype(BF16), v_ref[0:r0, :],
                                preferred_element_type=F32)
        o_ref[r0:r0 + tq, :] = (
            acc / acc[:, DENOM_LANE:DENOM_LANE + 1]).astype(o_ref.dtype)
    for src_ref, dst_ref in zip(cast_in, cast_out):
        dst_ref[...] = src_ref[...].astype(dst_ref.dtype)


def _attention(q_aug, k_aug, v_aug, cast_weights, *, batch, seq, tq):
    t = q_aug.shape[0]
    steps = batch * FOX_HEADS
    spec = pl.BlockSpec((seq, LANES), lambda b, h: (b, h))
    cast_specs = []
    for w in cast_weights:
        rows = w.shape[0] // steps
        assert rows * steps == w.shape[0] and rows % SEG_ALIGN == 0
        cast_specs.append(pl.BlockSpec((rows, w.shape[1]),
                                       lambda b, h: (b * FOX_HEADS + h, 0)))
    outs = pl.pallas_call(
        functools.partial(_attn_kernel, tq=tq, n_cast=len(cast_weights)),
        grid=(batch, FOX_HEADS),
        in_specs=[spec, spec, spec, *cast_specs],
        out_specs=[spec, *cast_specs],
        out_shape=[jax.ShapeDtypeStruct((t, FOX_HEADS * LANES), BF16)]
        + [jax.ShapeDtypeStruct(w.shape, BF16) for w in cast_weights],
        compiler_params=_params("parallel", "parallel"),
        name="fox_attention",
    )(q_aug, k_aug, v_aug, *cast_weights)
    return outs[0], outs[1:]


def _pool_outproj_kernel(x_ref, att_ref, p_ref, halo_ref, wpool_ref, scale_ref,
                         wo_att_ref, wo_pool_ref, o_ref, *, tiles_per_seq):
    i = pl.program_id(0)
    tm = x_ref.shape[0]
    tile_in_seq = i % tiles_per_seq
    p = p_ref[...]
    halo = jnp.where(tile_in_seq == 0, 0.0, halo_ref[...])
    ext = jnp.concatenate([halo, p], axis=0)
    pos = tile_in_seq * tm + lax.broadcasted_iota(jnp.int32, (tm, 1), 0)
    count = (pos + 1).astype(F32)
    mixed = []
    for gi, w in enumerate(POOL_WINDOWS):
        sl = slice(gi * POOL_GROUP_DIM, (gi + 1) * POOL_GROUP_DIM)
        acc = ext[:, sl]
        sh = 1
        while sh < w:
            acc = acc + pltpu.roll(acc, sh, axis=0)
            sh *= 2
        mean = acc[POOL_HALO:, :] / jnp.minimum(count, float(w))
        pooled = (mean - p[:, sl]).astype(BF16)
        mixed.append(jnp.dot(pooled, wpool_ref[gi], preferred_element_type=F32))
    pool = (jnp.concatenate(mixed, axis=1) * scale_ref[...]).astype(BF16)
    y = jnp.dot(att_ref[...], wo_att_ref[...], preferred_element_type=F32)
    y = y + jnp.dot(pool, wo_pool_ref[...], preferred_element_type=F32)
    o_ref[...] = x_ref[...] + y


def _pool_outproj(x2d, att, p_in, w_pool, pool_scale, w_out, *, seq, tm):
    t, d = x2d.shape
    wo_att = w_out[:FOX_WIDTH].reshape(FOX_HEADS, FOX_HEAD_DIM, d)
    wo_att = jnp.pad(wo_att, ((0, 0), (0, LANES - FOX_HEAD_DIM), (0, 0)))
    wo_att = wo_att.reshape(FOX_HEADS * LANES, d)
    wo_pool = w_out[FOX_WIDTH:]
    halo_blocks = tm // POOL_HALO
    kern = functools.partial(_pool_outproj_kernel, tiles_per_seq=seq // tm)
    return pl.pallas_call(
        kern,
        grid=(t // tm,),
        in_specs=[
            pl.BlockSpec((tm, d), lambda i: (i, 0)),
            pl.BlockSpec((tm, FOX_HEADS * LANES), lambda i: (i, 0)),
            pl.BlockSpec((tm, POOL_WIDTH), lambda i: (i, 0)),
            pl.BlockSpec((POOL_HALO, POOL_WIDTH),
                         lambda i: (jnp.maximum(i * halo_blocks - 1, 0), 0)),
            pl.BlockSpec(w_pool.shape, lambda i: (0, 0, 0)),
            pl.BlockSpec((1, POOL_WIDTH), lambda i: (0, 0)),
            pl.BlockSpec(wo_att.shape, lambda i: (0, 0)),
            pl.BlockSpec(wo_pool.shape, lambda i: (0, 0)),
        ],
        out_specs=pl.BlockSpec((tm, d), lambda i: (i, 0)),
        out_shape=jax.ShapeDtypeStruct((t, d), F32),
        compiler_params=_params("parallel"),
        name="pool_outproj",
    )(x2d, att, p_in, p_in, w_pool, pool_scale, wo_att, wo_pool)


def _swiglu_partial(h, w1, w3, w2):
    a = jnp.dot(h, w1, preferred_element_type=F32)
    b = jnp.dot(h, w3, preferred_element_type=F32)
    act = a * _sigmoid(a) * b
    return jnp.dot(act.astype(BF16), w2, preferred_element_type=F32)


MXU_TILE = 256


def _ff_splits(f):
    cut = -(-(f // MXU_TILE) // 2) * MXU_TILE
    return ((0, cut), (cut, f)) if 0 < cut < f else ((0, f),)


def _swiglu(h, w1_ref, w3_ref, w2_ref):
    y = None
    for lo, hi in _ff_splits(w1_ref.shape[1]):
        part = _swiglu_partial(h, w1_ref[:, lo:hi], w3_ref[:, lo:hi], w2_ref[lo:hi, :])
        y = part if y is None else y + part
    return y


def _ffn_kernel(x_ref, g_ref, w1_ref, w3_ref, w2_ref, o_ref):
    x = x_ref[...]
    h = _rms(x, g_ref[...]).astype(BF16)
    o_ref[...] = x + _swiglu(h, w1_ref, w3_ref, w2_ref)


def _ffn(x2d, gain, w1, w3, w2, *, tm):
    t, d = x2d.shape
    resident = lambda w: pl.BlockSpec(w.shape, lambda i: (0, 0),
                                      pipeline_mode=pl.Buffered(1))
    return pl.pallas_call(
        _ffn_kernel,
        grid=(t // tm,),
        in_specs=[
            pl.BlockSpec((tm, d), lambda i: (i, 0)),
            pl.BlockSpec((1, d), lambda i: (0, 0)),
            resident(w1), resident(w3), resident(w2),
        ],
        out_specs=pl.BlockSpec((tm, d), lambda i: (i, 0)),
        out_shape=jax.ShapeDtypeStruct((t, d), F32),
        compiler_params=_params("parallel"),
        name="dense_ffn",
    )(x2d, gain, w1, w3, w2)


def _gelu_tanh(x):
    c = math.sqrt(2.0 / math.pi)
    return 0.5 * x * (1.0 + jnp.tanh(c * (x + 0.044715 * (x * x * x))))


def _s5_kernel(x_ref, g_ref, win_ref, perm_ref, permt_ref, bblk_ref, cblk_ref,
               are_ref, aim_ref, d_ref, o_ref, zero_ref, xs_ref, st_ref):
    nb, tc, d = x_ref.shape
    rows = nb * tc
    n_slabs = d // LANES
    sw = 2 * SSM_SLAB_STATE

    @pl.when(pl.program_id(0) == 0)
    def _():
        st_ref[...] = jnp.zeros_like(st_ref)

    x = x_ref[...].reshape(rows, d)
    h = _rms(x, g_ref[...]).astype(BF16)
    h_tb = jnp.dot(perm_ref[...], h, preferred_element_type=F32).astype(BF16)
    u = jnp.dot(h_tb, win_ref[...], preferred_element_type=F32)
    ub = u.astype(BF16)
    hs = SSM_SLAB_STATE
    ys = []
    for s in range(n_slabs):
        lo = s * sw
        xs_ref[:, lo:lo + sw] = jnp.dot(
            ub[:, s * LANES:(s + 1) * LANES], bblk_ref[s],
            preferred_element_type=F32)
        a_r = are_ref[:, lo:lo + hs]
        a_i = aim_ref[:, lo:lo + hs]
        x_r = st_ref[:, lo:lo + hs]
        x_i = st_ref[:, lo + hs:lo + sw]
        for t in range(tc):
            r0 = t * SUBLANES
            new_r = a_r * x_r - a_i * x_i + xs_ref[r0:r0 + SUBLANES, lo:lo + hs]
            new_i = a_r * x_i + a_i * x_r + xs_ref[r0:r0 + SUBLANES, lo + hs:lo + sw]
            xs_ref[r0:r0 + SUBLANES, lo:lo + hs] = new_r
            xs_ref[r0:r0 + SUBLANES, lo + hs:lo + sw] = new_i
            x_r, x_i = new_r, new_i
        st_ref[:, lo:lo + hs] = x_r
        st_ref[:, lo + hs:lo + sw] = x_i
        ys.append(jnp.dot(xs_ref[:, lo:lo + sw].astype(BF16), cblk_ref[s],
                          preferred_element_type=F32))
    y = jnp.concatenate(ys, axis=1) + d_ref[...] * u
    g = _gelu_tanh(y).astype(BF16)
    g_bt = jnp.dot(permt_ref[...], g, preferred_element_type=F32)
    o_ref[...] = g_bt.reshape(nb, tc, d).astype(o_ref.dtype)
    zero_ref[...] = jnp.zeros_like(zero_ref)


def _s5_mixer(x3d, gain, w_in, bblk, cblk, a_re, a_im, d_skip, *, tc, zero_shape):
    nb, seq, d = x3d.shape
    steps = seq // tc
    zero_rows = zero_shape[0] // steps
    assert zero_rows * steps == zero_shape[0] and zero_rows % SEG_ALIGN == 0
    rows = nb * tc
    n_slabs = d // LANES
    sw = 2 * SSM_SLAB_STATE
    r = np.arange(rows)
    perm = np.zeros((rows, rows), np.float32)
    perm[(r % tc) * nb + r // tc, r] = 1.0
    perm_j = jnp.asarray(perm, BF16)
    permt_j = jnp.asarray(perm.T, BF16)
    kern = _s5_kernel
    full2 = lambda a: pl.BlockSpec(a.shape, lambda i: (0, 0))
    full3 = lambda a: pl.BlockSpec(a.shape, lambda i: (0, 0, 0))
    return pl.pallas_call(
        kern,
        grid=(seq // tc,),
        in_specs=[
            pl.BlockSpec((nb, tc, d), lambda i: (0, i, 0)),
            full2(gain), full2(w_in), full2(perm_j), full2(permt_j),
            full3(bblk), full3(cblk), full2(a_re), full2(a_im), full2(d_skip),
        ],
        out_specs=[pl.BlockSpec((nb, tc, d), lambda i: (0, i, 0)),
                   pl.BlockSpec((zero_rows, zero_shape[1]), lambda i: (i, 0))],
        out_shape=[jax.ShapeDtypeStruct((nb, seq, d), BF16),
                   jax.ShapeDtypeStruct(zero_shape, BF16)],
        scratch_shapes=[
            pltpu.VMEM((rows, n_slabs * sw), F32),
            pltpu.VMEM((SUBLANES, n_slabs * sw), F32),
        ],
        compiler_params=_params("arbitrary"),
        name="s5_mixer",
    )(x3d, gain, w_in, perm_j, permt_j, bblk, cblk, a_re, a_im, d_skip)


def _s5_coefficients(a_re, a_im, log_dt, b_re, b_im, c_re, c_im):
    dt = jnp.exp(log_dt.astype(F32))[:, None]
    ar = a_re.astype(F32)
    ai = a_im.astype(F32)
    mag = jnp.exp(ar * dt)
    abar_re = mag * jnp.cos(ai * dt)
    abar_im = mag * jnp.sin(ai * dt)
    den = ar * ar + ai * ai
    nr = abar_re - 1.0
    ni = abar_im
    coef_re = (nr * ar + ni * ai) / den
    coef_im = (ni * ar - nr * ai) / den
    br = b_re.astype(F32)
    bi = b_im.astype(F32)
    bbar_re = coef_re[..., None] * br - coef_im[..., None] * bi
    bbar_im = coef_re[..., None] * bi + coef_im[..., None] * br
    n_groups = ar.shape[0]
    n_slabs = n_groups // SSM_SLAB_GROUPS
    eye = jnp.eye(SSM_SLAB_GROUPS, dtype=F32)

    def in_block(bb):
        bb = bb.reshape(n_slabs, SSM_SLAB_GROUPS, SSM_STATE, SSM_GROUP)
        blk = jnp.einsum('sgph,gk->sghkp', bb, eye)
        return blk.reshape(n_slabs, LANES, SSM_SLAB_STATE)

    def out_block(cc):
        cc = cc.reshape(n_slabs, SSM_SLAB_GROUPS, SSM_GROUP, SSM_STATE)
        blk = jnp.einsum('sghp,gk->sgpkh', cc, eye)
        return blk.reshape(n_slabs, SSM_SLAB_STATE, LANES)

    bblk = jnp.concatenate([in_block(bbar_re), in_block(bbar_im)], axis=2)
    cblk = jnp.concatenate([out_block(c_re.astype(F32)),
                            -out_block(c_im.astype(F32))], axis=1)

    def lanes(a):
        a = a.reshape(n_slabs, 1, SSM_SLAB_STATE)
        a = jnp.concatenate([a, a], axis=2).reshape(1, -1)
        return jnp.broadcast_to(a, (SUBLANES, a.shape[1]))

    return (bblk.astype(BF16), cblk.astype(BF16), lanes(abar_re), lanes(abar_im))


def _glu_kernel(x_ref, g_ref, wa_ref, wb_ref, o_ref):
    g = g_ref[...]
    a = jnp.dot(g, wa_ref[...], preferred_element_type=F32)
    b = jnp.dot(g, wb_ref[...], preferred_element_type=F32)
    o_ref[...] = x_ref[...] + a * _sigmoid(b)


def _glu(x2d, g2d, wa, wb, *, tm):
    t, d = x2d.shape
    return pl.pallas_call(
        _glu_kernel,
        grid=(t // tm,),
        in_specs=[
            pl.BlockSpec((tm, d), lambda i: (i, 0)),
            pl.BlockSpec((tm, d), lambda i: (i, 0)),
            pl.BlockSpec(wa.shape, lambda i: (0, 0)),
            pl.BlockSpec(wb.shape, lambda i: (0, 0)),
        ],
        out_specs=pl.BlockSpec((tm, d), lambda i: (i, 0)),
        out_shape=jax.ShapeDtypeStruct((t, d), F32),
        compiler_params=_params("parallel"),
        name="glu_outproj",
    )(x2d, g2d, wa, wb)


MOE_CHUNK = 512
MOE_FIRST = 192
MOE_REST = MOE_CHUNK - MOE_FIRST
MOE_PIECES = ((0, MOE_FIRST), (MOE_FIRST, MOE_REST))
MOE_TILE = 512


def _router_kernel(x_ref, g_ref, w_ref, b_ref, h_ref, pos_ref, post_ref, cnt_ref):
    tm, d = x_ref.shape
    h = _rms(x_ref[...], g_ref[...])
    h_hi = h.astype(BF16)
    h_ref[:, 0:d] = h_hi
    h_lo = (h - h_hi.astype(F32)).astype(BF16)
    p_hi = jnp.dot(h_hi, w_ref[...], preferred_element_type=F32)
    p_lo = jnp.dot(h_lo, w_ref[...], preferred_element_type=F32)
    logits = p_hi + pltpu.roll(p_hi, LANES - N_EXPERTS, axis=1) + p_lo + b_ref[...]
    lane = lax.broadcasted_iota(jnp.int32, logits.shape, 1)
    logits = jnp.where(lane < N_EXPERTS, logits, -jnp.inf)
    m1 = jnp.max(logits, axis=1, keepdims=True)
    i1 = jnp.min(jnp.where(logits == m1, lane, LANES), axis=1, keepdims=True)
    rest = jnp.where(lane == i1, -jnp.inf, logits)
    m2 = jnp.max(rest, axis=1, keepdims=True)
    i2 = jnp.min(jnp.where(rest == m2, lane, LANES), axis=1, keepdims=True)
    e2 = jnp.exp(m2 - m1)
    g1 = 1.0 / (1.0 + e2)
    g2 = e2 / (1.0 + e2)
    gate = jnp.where(lane == i1, g1, 0.0) + jnp.where(lane == i2, g2, 0.0)
    g_hi = gate.astype(BF16).astype(F32)
    g_r = gate - g_hi
    g_mid = g_r.astype(BF16).astype(F32)
    packed = jnp.where(lane < N_EXPERTS, g_hi,
                       jnp.where(lane < 2 * N_EXPERTS,
                                 pltpu.roll(g_mid, N_EXPERTS, axis=1),
                                 pltpu.roll(g_r - g_mid, 2 * N_EXPERTS, axis=1)))
    h_ref[:, d:d + LANES] = packed.astype(BF16)
    member = jnp.where((lane == i1) | (lane == i2), 1.0, 0.0)
    row = lax.broadcasted_iota(jnp.int32, member.shape, 0)
    c = member
    sh = 1
    while sh < tm:
        c = c + jnp.where(row >= sh, pltpu.roll(c, sh, axis=0), 0.0)
        sh *= 2
    pos = jnp.where(member > 0.0, c - member, -1.0)
    pos_ref[...] = pos
    post_ref[0] = pos.T[0:SUBLANES, :]
    cnt_ref[0] = jnp.broadcast_to(c[tm - 1:tm, :], (SUBLANES, LANES))


def _router(x2d, gain, w_pad, b_pad):
    t, d = x2d.shape
    tm = MOE_CHUNK
    n_chunks = t // tm
    return pl.pallas_call(
        _router_kernel,
        grid=(n_chunks,),
        in_specs=[
            pl.BlockSpec((tm, d), lambda i: (i, 0)),
            pl.BlockSpec((1, d), lambda i: (0, 0)),
            pl.BlockSpec((d, LANES), lambda i: (0, 0)),
            pl.BlockSpec((1, LANES), lambda i: (0, 0)),
        ],
        out_specs=[
            pl.BlockSpec((tm, d + LANES), lambda i: (i, 0)),
            pl.BlockSpec((tm, LANES), lambda i: (i, 0)),
            pl.BlockSpec((1, SUBLANES, tm), lambda i: (i, 0, 0)),
            pl.BlockSpec((1, SUBLANES, LANES), lambda i: (i, 0, 0)),
        ],
        out_shape=[
            jax.ShapeDtypeStruct((t, d + LANES), BF16),
            jax.ShapeDtypeStruct((t, LANES), F32),
            jax.ShapeDtypeStruct((n_chunks, SUBLANES, tm), F32),
            jax.ShapeDtypeStruct((n_chunks, SUBLANES, LANES), F32),
        ],
        compiler_params=_params("parallel"),
        name="router",
    )(x2d, gain, w_pad, b_pad)


MOE_SPARE = max(MOE_FIRST, MOE_REST)


def _moe_tiles(n_tokens, row_multiple):
    n_chunks = n_tokens // MOE_CHUNK
    max_rows = (TOP_K * n_tokens + (SEG_ALIGN - 1) * N_EXPERTS * n_chunks
                + N_EXPERTS * (MOE_SPARE + MOE_TILE))
    unit = math.lcm(MOE_TILE, row_multiple)
    return -(-max_rows // unit) * (unit // MOE_TILE)


def _moe_layout(cnt, n_tiles):
    seg = (cnt + SEG_ALIGN - 1) // SEG_ALIGN * SEG_ALIGN
    used = jnp.sum(seg, axis=0)
    padded = (used + MOE_SPARE + MOE_TILE - 1) // MOE_TILE * MOE_TILE
    ends = jnp.cumsum(padded)
    start = ends - padded
    off = start[None, :] + jnp.cumsum(seg, axis=0) - seg
    tile_start = jnp.arange(n_tiles, dtype=jnp.int32) * MOE_TILE
    tile_e = jnp.sum((tile_start[:, None] >= ends[None, :]).astype(jnp.int32), axis=1)
    tile_e = jnp.minimum(tile_e, N_EXPERTS - 1)
    tile_rows = jnp.clip(used[tile_e] - (tile_start - start[tile_e]), 0, MOE_TILE)
    return (off.reshape(-1).astype(jnp.int32), cnt.reshape(-1).astype(jnp.int32),
            tile_e.astype(jnp.int32), tile_rows.astype(jnp.int32))


def _segment_copies(hbm_ref, buf_ref, sem_ref, off_ref, chunk, to_hbm, slot=()):
    copies = []
    for e in range(N_EXPERTS):
        off = pl.multiple_of(off_ref[chunk * N_EXPERTS + e], SEG_ALIGN)
        for piece, (first, rows) in enumerate(MOE_PIECES):
            hbm = hbm_ref.at[pl.ds(off + first, rows)]
            buf = buf_ref.at[(*slot, e, pl.ds(first, rows))]
            src, dst = (buf, hbm) if to_hbm else (hbm, buf)
            copies.append(pltpu.make_async_copy(src, dst, sem_ref.at[(*slot, e, piece)]))
    return copies


def _onehot(index, target):
    return jnp.where(index == target, 1.0, 0.0).astype(BF16)


def _gather_kernel(off_ref, cnt_ref, h_ref, post_ref, zeros_hbm, o_hbm, stage_ref,
                   sem_ref):
    del zeros_hbm
    c = pl.program_id(0)
    h = h_ref[...]
    copies = _segment_copies(o_hbm, stage_ref, sem_ref, off_ref, c, to_hbm=True)

    def move_piece(e, piece):
        first, rows = MOE_PIECES[piece]
        rank = first + lax.broadcasted_iota(jnp.int32, (rows, MOE_CHUNK), 0)
        onehot = _onehot(rank.astype(F32), post_ref[0, e:e + 1, :])
        stage_ref[e, first:first + rows] = jnp.dot(
            onehot, h, preferred_element_type=F32).astype(BF16)
        copies[2 * e + piece].start()

    for e in range(N_EXPERTS):
        move_piece(e, 0)

        @pl.when(cnt_ref[c * N_EXPERTS + e] > MOE_FIRST)
        def _(e=e):
            move_piece(e, 1)

    for e in range(N_EXPERTS):
        copies[2 * e].wait()

        @pl.when(cnt_ref[c * N_EXPERTS + e] > MOE_FIRST)
        def _(e=e):
            copies[2 * e + 1].wait()


def _gather(h2d, post, off, cnt, zero_buf):
    t, d = h2d.shape
    n_chunks = t // MOE_CHUNK
    grid_spec = pltpu.PrefetchScalarGridSpec(
        num_scalar_prefetch=2,
        grid=(n_chunks,),
        in_specs=[
            pl.BlockSpec((MOE_CHUNK, d), lambda i, off, cnt: (i, 0)),
            pl.BlockSpec((1, SUBLANES, MOE_CHUNK), lambda i, off, cnt: (i, 0, 0)),
            pl.BlockSpec(memory_space=pl.ANY),
        ],
        out_specs=pl.BlockSpec(memory_space=pl.ANY),
        scratch_shapes=[
            pltpu.VMEM((N_EXPERTS, MOE_CHUNK, d), BF16),
            pltpu.SemaphoreType.DMA((N_EXPERTS, 2)),
        ],
    )
    return pl.pallas_call(
        _gather_kernel,
        grid_spec=grid_spec,
        out_shape=jax.ShapeDtypeStruct(zero_buf.shape, BF16),
        input_output_aliases={4: 0},
        compiler_params=_params("arbitrary"),
        name="moe_gather",
    )(off, cnt, h2d, post, zero_buf)


def _expert_kernel(te_ref, tr_ref, x_ref, w1_ref, w3_ref, w2_ref, o_ref):
    i = pl.program_id(0)
    rows = tr_ref[i]
    d = o_ref.shape[1]

    def run(n):
        row = lax.broadcasted_iota(jnp.int32, (n, x_ref.shape[1]), 0)
        x = jnp.where(row < rows, x_ref[0:n, :], jnp.zeros((), x_ref.dtype))
        lane = lax.broadcasted_iota(jnp.int32, (n, LANES), 1)
        mine = (lane < 3 * N_EXPERTS) & (jnp.bitwise_and(lane, N_EXPERTS - 1) == te_ref[i])
        gate = jnp.sum(jnp.where(mine, x[:, d:d + LANES].astype(F32), 0.0),
                       axis=1, keepdims=True)
        y = _swiglu(x[:, 0:d], w1_ref, w3_ref, w2_ref)
        o_ref[0:n, :] = (gate * y).astype(o_ref.dtype)

    half = MOE_TILE // 2

    @pl.when(rows > half)
    def _():
        run(MOE_TILE)

    @pl.when((rows > 0) & (rows <= half))
    def _():
        run(half)
        o_ref[half:MOE_TILE, :] = jnp.zeros((MOE_TILE - half, d), o_ref.dtype)

    @pl.when(rows == 0)
    def _():
        o_ref[...] = jnp.zeros_like(o_ref)


def _experts(x_sorted, w1, w3, w2, tile_e, tile_rows):
    assert N_EXPERTS & (N_EXPERTS - 1) == 0
    n_rows = x_sorted.shape[0]
    d, f = w1.shape[1:]
    grid_spec = pltpu.PrefetchScalarGridSpec(
        num_scalar_prefetch=2,
        grid=(n_rows // MOE_TILE,),
        in_specs=[
            pl.BlockSpec((MOE_TILE, d + LANES), lambda i, te, tr: (i, 0)),
            pl.BlockSpec((None, d, f), lambda i, te, tr: (te[i], 0, 0)),
            pl.BlockSpec((None, d, f), lambda i, te, tr: (te[i], 0, 0)),
            pl.BlockSpec((None, f, d), lambda i, te, tr: (te[i], 0, 0)),
        ],
        out_specs=pl.BlockSpec((MOE_TILE, d), lambda i, te, tr: (i, 0)),
    )
    return pl.pallas_call(
        _expert_kernel,
        grid_spec=grid_spec,
        out_shape=jax.ShapeDtypeStruct((n_rows, d), BF16),
        compiler_params=_params("arbitrary"),
        name="moe_experts",
    )(tile_e, tile_rows, x_sorted, w1, w3, w2)


def _combine_kernel(off_ref, cnt_ref, x_ref, pos_ref, fg_ref, y_hbm,
                    o_ref, ybuf_ref, acc_ref, sem_ref):
    c = pl.program_id(0)
    n_chunks = pl.num_programs(0)
    slot = c % 2

    def fetch(chunk, sl, start):
        copies = _segment_copies(y_hbm, ybuf_ref, sem_ref, off_ref, chunk,
                                 to_hbm=False, slot=(sl,))
        for e in range(N_EXPERTS):
            first, second = copies[2 * e], copies[2 * e + 1]
            if start:
                first.start()
            else:
                first.wait()

            @pl.when(cnt_ref[chunk * N_EXPERTS + e] > MOE_FIRST)
            def _(second=second):
                if start:
                    second.start()
                else:
                    second.wait()

    @pl.when(c == 0)
    def _():
        fetch(c, slot, True)

    @pl.when(c + 1 < n_chunks)
    def _():
        fetch(c + 1, 1 - slot, True)

    fetch(c, slot, False)

    def piece_rows(e, piece):
        first, rows = MOE_PIECES[piece]
        rank = first + lax.broadcasted_iota(jnp.int32, (MOE_CHUNK, rows), 1)
        onehot = _onehot(rank.astype(F32), pos_ref[:, e:e + 1])
        return jnp.dot(onehot, ybuf_ref[slot, e, first:first + rows],
                       preferred_element_type=F32)

    acc = x_ref[...]
    for e in range(N_EXPERTS):
        acc = acc + piece_rows(e, 0)
    acc_ref[...] = acc
    for e in range(N_EXPERTS):
        @pl.when(cnt_ref[c * N_EXPERTS + e] > MOE_FIRST)
        def _(e=e):
            acc_ref[...] += piece_rows(e, 1)

    o_ref[...] = _rms(acc_ref[...], fg_ref[...])


def _combine(x2d, pos, final_gain, y_sorted, off, cnt):
    t, d = x2d.shape
    n_chunks = t // MOE_CHUNK
    grid_spec = pltpu.PrefetchScalarGridSpec(
        num_scalar_prefetch=2,
        grid=(n_chunks,),
        in_specs=[
            pl.BlockSpec((MOE_CHUNK, d), lambda i, off, cnt: (i, 0)),
            pl.BlockSpec((MOE_CHUNK, LANES), lambda i, off, cnt: (i, 0)),
            pl.BlockSpec((1, d), lambda i, off, cnt: (0, 0)),
            pl.BlockSpec(memory_space=pl.ANY),
        ],
        out_specs=pl.BlockSpec((MOE_CHUNK, d), lambda i, off, cnt: (i, 0)),
        scratch_shapes=[
            pltpu.VMEM((2, N_EXPERTS, MOE_CHUNK, d), BF16),
            pltpu.VMEM((MOE_CHUNK, d), F32),
            pltpu.SemaphoreType.DMA((2, N_EXPERTS, 2)),
        ],
    )
    return pl.pallas_call(
        _combine_kernel,
        grid_spec=grid_spec,
        out_shape=jax.ShapeDtypeStruct((t, d), F32),
        compiler_params=_params("arbitrary"),
        name="moe_combine",
    )(off, cnt, x2d, pos, final_gain, y_sorted)


def _moe(x2d, gain, router_w, router_b, w1, w3, w2, final_gain, zero_buf):
    t, d = x2d.shape
    rw_hi = router_w.astype(BF16)
    rw_lo = (router_w.astype(F32) - rw_hi.astype(F32)).astype(BF16)
    rw = jnp.pad(jnp.concatenate([rw_hi, rw_lo], axis=1),
                 ((0, 0), (0, LANES - 2 * N_EXPERTS)))
    rb = jnp.pad(router_b.astype(F32), (0, LANES - N_EXPERTS)).reshape(1, LANES)
    h, pos, post, cnt = _router(x2d, gain, rw, rb)
    cnt = cnt[:, 0, :N_EXPERTS].astype(jnp.int32)
    off, cnt_flat, tile_e, tile_rows = _moe_layout(cnt, zero_buf.shape[0] // MOE_TILE)
    h_sorted = _gather(h, post, off, cnt_flat, zero_buf)
    y_sorted = _experts(h_sorted, w1, w3, w2, tile_e, tile_rows)
    return _combine(x2d, pos, final_gain, y_sorted, off, cnt_flat)


def _even_weights(w_in, b_forget):
    fw = FOX_WIDTH
    wq = w_in[:, :fw] * FOX_HEAD_DIM ** -0.5
    wf = jnp.pad(w_in[:, 3 * fw:3 * fw + FOX_HEADS],
                 ((0, 0), (0, LANES - FOX_HEADS)))
    wp = w_in[:, 3 * fw + FOX_HEADS:]
    w_all = jnp.concatenate([wq, w_in[:, fw:3 * fw], wp, wf], axis=1).astype(BF16)
    bias = jnp.pad(b_forget.astype(F32), (0, LANES - FOX_HEADS)).reshape(1, LANES)
    qw = FOX_HEADS * LANES
    place = np.zeros((LANES, 2 * qw), np.float32)
    for hh in range(FOX_HEADS):
        for piece in range(3):
            place[piece * FOX_HEADS + hh, hh * LANES + BIAS_LANE + piece] = 1.0
            place[piece * FOX_HEADS + hh, qw + hh * LANES + ONES_LANE + piece] = -1.0
    return w_all, bias, jnp.asarray(place, BF16)


def kernel(x, even_mix_norm, even_w_in, even_b_forget, even_w_pool, even_pool_scale, even_w_out, even_ffn_norm, even_ffn_w1, even_ffn_w3, even_ffn_w2, odd_mix_norm, odd_w_in, ssm_a_re, ssm_a_im, ssm_log_dt, ssm_b_re, ssm_b_im, ssm_c_re, ssm_c_im, ssm_d, odd_w_glu_a, odd_w_glu_b, odd_moe_norm, router_w, router_b, expert_w1, expert_w3, expert_w2, final_norm):
    b, s, d = x.shape
    t = b * s
    assert b == SUBLANES, "the S5 recurrence keeps one batch row per sublane"
    x2d = x.reshape(t, d)
    row = lambda v: v.reshape(1, -1).astype(F32)

    w_all, bias, place = _even_weights(even_w_in[0], even_b_forget[0])
    q_aug, k_aug, v, p_in = _even_inproj(
        x2d, row(even_mix_norm[0]), w_all, bias, place, seq=s, tm=512)
    n_e, _, f = expert_w1[0].shape
    att, (ew1, ew3, ew2) = _attention(
        q_aug, k_aug, v,
        [expert_w1[0].reshape(n_e * d, f), expert_w3[0].reshape(n_e * d, f),
         expert_w2[0].reshape(n_e * f, d)],
        batch=b, seq=s, tq=256)
    x1 = _pool_outproj(x2d, att, p_in, even_w_pool[0].astype(BF16),
                       row(even_pool_scale[0]), even_w_out[0].astype(BF16),
                       seq=s, tm=512)
    x2 = _ffn(x1, row(even_ffn_norm[0]), even_ffn_w1[0].astype(BF16),
              even_ffn_w3[0].astype(BF16), even_ffn_w2[0].astype(BF16), tm=1024)

    bblk, cblk, a_re, a_im = _s5_coefficients(
        ssm_a_re[0], ssm_a_im[0], ssm_log_dt[0], ssm_b_re[0], ssm_b_im[0],
        ssm_c_re[0], ssm_c_im[0])
    moe_rows = _moe_tiles(t, (s // S5_STEP) * SEG_ALIGN) * MOE_TILE
    g, zero_buf = _s5_mixer(x2.reshape(b, s, d), row(odd_mix_norm[0]),
                            odd_w_in[0].astype(BF16), bblk, cblk, a_re, a_im,
                            row(ssm_d[0]), tc=S5_STEP,
                            zero_shape=(moe_rows, d + LANES))
    x3 = _glu(x2, g.reshape(t, d), odd_w_glu_a[0].astype(BF16),
              odd_w_glu_b[0].astype(BF16), tm=512)
    out = _moe(x3, row(odd_moe_norm[0]), router_w[0], router_b[0],
               ew1.reshape(n_e, d, f), ew3.reshape(n_e, d, f),
               ew2.reshape(n_e, f, d), row(final_norm), zero_buf)
    return out.reshape(b, s, d)
```

```python
import functools
import math

import numpy as np
import jax
import jax.numpy as jnp
from jax import lax
from jax.experimental import pallas as pl
from jax.experimental.pallas import tpu as pltpu

F32 = jnp.float32
BF16 = jnp.bfloat16

EPS = 1e-6
NEG_INF = -1e30
LANES = 128
SUBLANES = 8
SEG_ALIGN = 16
VMEM_LIMIT = 56 * 1024 * 1024

FOX_HEADS = 8
FOX_HEAD_DIM = 64
FOX_WIDTH = FOX_HEADS * FOX_HEAD_DIM
POOL_WINDOWS = (2, 4, 8, 16)
POOL_GROUP_DIM = 128
POOL_WIDTH = len(POOL_WINDOWS) * POOL_GROUP_DIM
POOL_HALO = 16
SSM_GROUP = 16
SSM_STATE = 64
SSM_SLAB_GROUPS = LANES // SSM_GROUP
SSM_SLAB_STATE = SSM_SLAB_GROUPS * SSM_STATE
S5_STEP = 32
N_EXPERTS = 8
TOP_K = 2

BIAS_LANE = FOX_HEAD_DIM
ONES_LANE = FOX_HEAD_DIM + 3
DENOM_LANE = FOX_HEAD_DIM


def _params(*sem):
    return pltpu.CompilerParams(dimension_semantics=sem,
                                vmem_limit_bytes=VMEM_LIMIT)


def _rms(x, g):
    ms = jnp.mean(x * x, axis=-1, keepdims=True)
    return x * lax.rsqrt(ms + EPS) * g


def _sigmoid(x):
    return 1.0 / (1.0 + jnp.exp(-x))


def _lane_range_ones(lo, hi):
    lane = lax.broadcasted_iota(jnp.int32, (1, LANES), 1)
    return jnp.where((lane >= lo) & (lane < hi), 1.0, 0.0).astype(F32)


def _cast_specs(weights, steps, index_map):
    specs = []
    for w in weights:
        rows = w.shape[0] // steps
        assert rows * steps == w.shape[0] and rows % SEG_ALIGN == 0
        specs.append(pl.BlockSpec((rows, w.shape[1]), index_map))
    return specs


def _cast_slabs(src_refs, dst_refs):
    for src_ref, dst_ref in zip(src_refs, dst_refs):
        dst_ref[...] = src_ref[...].astype(dst_ref.dtype)


def _even_inproj_kernel(x_ref, g_ref, w_ref, bias_ref, place_ref, *rest,
                        tiles_per_seq, n_cast):
    cast_in, rest = rest[:n_cast], rest[n_cast:]
    q_ref, k_ref, v_ref, p_ref = rest[:4]
    cast_out, carry_ref = rest[4:4 + n_cast], rest[4 + n_cast]
    _cast_slabs(cast_in, cast_out)
    i = pl.program_id(0)
    tm = x_ref.shape[0]
    fw = FOX_WIDTH
    qw = FOX_HEADS * LANES
    h = _rms(x_ref[...], g_ref[...]).astype(BF16)
    z = jnp.dot(h, w_ref[...], preferred_element_type=F32)
    p_ref[...] = z[:, 3 * fw:3 * fw + POOL_WIDTH]
    fg = z[:, 3 * fw + POOL_WIDTH:] + bias_ref[...]
    lf = jnp.minimum(fg, 0.0) - jnp.log1p(jnp.exp(-jnp.abs(fg)))
    row = lax.broadcasted_iota(jnp.int32, lf.shape, 0)
    c = lf
    sh = 1
    while sh < tm:
        c = c + jnp.where(row >= sh, pltpu.roll(c, sh, axis=0), 0.0)
        sh *= 2

    @pl.when(i % tiles_per_seq == 0)
    def _():
        carry_ref[...] = jnp.zeros_like(carry_ref)

    c = c + carry_ref[0:1, :]
    carry_ref[...] = jnp.broadcast_to(c[tm - 1:tm, :], carry_ref.shape)
    hi = c.astype(BF16).astype(F32)
    r1 = c - hi
    mid = r1.astype(BF16).astype(F32)
    lo = r1 - mid
    lane = lax.broadcasted_iota(jnp.int32, (tm, LANES), 1)
    packed = jnp.where(lane < FOX_HEADS, hi,
                       jnp.where(lane < 2 * FOX_HEADS,
                                 pltpu.roll(mid, FOX_HEADS, axis=1),
                                 pltpu.roll(lo, 2 * FOX_HEADS, axis=1)))
    placed = jnp.dot(packed.astype(BF16), place_ref[...],
                     preferred_element_type=F32)
    low = lane < FOX_HEAD_DIM
    ones_q = _lane_range_ones(ONES_LANE, ONES_LANE + 3)
    ones_k = _lane_range_ones(BIAS_LANE, BIAS_LANE + 3)
    ones_v = _lane_range_ones(DENOM_LANE, DENOM_LANE + 1)

    def head_lanes(base, hh):
        pair = z[:, base + (hh // 2) * LANES:base + (hh // 2 + 1) * LANES]
        return pltpu.roll(pair, FOX_HEAD_DIM, axis=1) if hh % 2 else pair

    for hh in range(FOX_HEADS):
        sl = slice(hh * LANES, (hh + 1) * LANES)
        q_ref[:, sl] = jnp.where(low, head_lanes(0, hh),
                                 placed[:, sl] + ones_q).astype(BF16)
        k_ref[:, sl] = jnp.where(
            low, head_lanes(fw, hh),
            placed[:, qw + hh * LANES:qw + (hh + 1) * LANES] + ones_k).astype(BF16)
        v_ref[:, sl] = jnp.where(low, head_lanes(2 * fw, hh), ones_v).astype(BF16)


def _even_inproj(x2d, gain, w_all, bias, place, cast_weights, *, seq, tm):
    t, d = x2d.shape
    n = w_all.shape[1]
    qw = FOX_HEADS * LANES
    cast_specs = _cast_specs(cast_weights, t // tm, lambda i: (i, 0))
    kern = functools.partial(_even_inproj_kernel, tiles_per_seq=seq // tm,
                             n_cast=len(cast_weights))
    outs = pl.pallas_call(
        kern,
        grid=(t // tm,),
        in_specs=[
            pl.BlockSpec((tm, d), lambda i: (i, 0)),
            pl.BlockSpec((1, d), lambda i: (0, 0)),
            pl.BlockSpec((d, n), lambda i: (0, 0)),
            pl.BlockSpec((1, LANES), lambda i: (0, 0)),
            pl.BlockSpec(place.shape, lambda i: (0, 0)),
            *cast_specs,
        ],
        out_specs=[
            pl.BlockSpec((tm, qw), lambda i: (i, 0)),
            pl.BlockSpec((tm, qw), lambda i: (i, 0)),
            pl.BlockSpec((tm, qw), lambda i: (i, 0)),
            pl.BlockSpec((tm, POOL_WIDTH), lambda i: (i, 0)),
            *cast_specs,
        ],
        out_shape=[
            jax.ShapeDtypeStruct((t, qw), BF16),
            jax.ShapeDtypeStruct((t, qw), BF16),
            jax.ShapeDtypeStruct((t, qw), BF16),
            jax.ShapeDtypeStruct((t, POOL_WIDTH), F32),
        ] + [jax.ShapeDtypeStruct(w.shape, BF16) for w in cast_weights],
        scratch_shapes=[pltpu.VMEM((SUBLANES, LANES), F32)],
        compiler_params=_params("arbitrary"),
        name="even_inproj",
    )(x2d, gain, w_all, bias, place, *cast_weights)
    return outs[:4], outs[4:]


def _dot_nt(a, b):
    return lax.dot_general(a, b, (((1,), (1,)), ((), ())),
                           preferred_element_type=F32)


def _attn_kernel(q_ref, k_ref, v_ref, *rest, tq, n_cast):
    cast_in, o_ref, cast_out = rest[:n_cast], rest[n_cast], rest[n_cast + 1:]
    seq = q_ref.shape[0]
    row = lax.broadcasted_iota(jnp.int32, (tq, tq), 0)
    col = lax.broadcasted_iota(jnp.int32, (tq, tq), 1)
    causal = col <= row
    for qi in range(seq // tq):
        r0 = qi * tq
        q = q_ref[r0:r0 + tq, :]
        s_diag = jnp.where(causal, _dot_nt(q, k_ref[r0:r0 + tq, :]), NEG_INF)
        m = jnp.max(s_diag, axis=1, keepdims=True)
        if qi > 0:
            s_past = _dot_nt(q, k_ref[0:r0, :])
            m = jnp.maximum(m, jnp.max(s_past, axis=1, keepdims=True))
        acc = jnp.dot(jnp.exp(s_diag - m).astype(BF16), v_ref[r0:r0 + tq, :],
                      preferred_element_type=F32)
        if qi > 0:
            acc = acc + jnp.dot(jnp.exp(s_past - m).astype(BF16), v_ref[0:r0, :],
                                preferred_element_type=F32)
        o_ref[r0:r0 + tq, :] = (
            acc / acc[:, DENOM_LANE:DENOM_LANE + 1]).astype(o_ref.dtype)
    _cast_slabs(cast_in, cast_out)


def _attention(q_aug, k_aug, v_aug, cast_weights, *, batch, seq, tq):
    t = q_aug.shape[0]
    steps = batch * FOX_HEADS
    spec = pl.BlockSpec((seq, LANES), lambda b, h: (b, h))
    cast_specs = _cast_specs(cast_weights, steps, lambda b, h: (b * FOX_HEADS + h, 0))
    outs = pl.pallas_call(
        functools.partial(_attn_kernel, tq=tq, n_cast=len(cast_weights)),
        grid=(batch, FOX_HEADS),
        in_specs=[spec, spec, spec, *cast_specs],
        out_specs=[spec, *cast_specs],
        out_shape=[jax.ShapeDtypeStruct((t, FOX_HEADS * LANES), BF16)]
        + [jax.ShapeDtypeStruct(w.shape, BF16) for w in cast_weights],
        compiler_params=_params("parallel", "parallel"),
        name="fox_attention",
    )(q_aug, k_aug, v_aug, *cast_weights)
    return outs[0], outs[1:]


MXU_TILE = 256


def _swiglu_partial(h, w1, w3, w2):
    a = jnp.dot(h, w1, preferred_element_type=F32)
    b = jnp.dot(h, w3, preferred_element_type=F32)
    act = a * _sigmoid(a) * b
    return jnp.dot(act.astype(BF16), w2, preferred_element_type=F32)


def _ff_splits(f):
    cut = -(-(f // MXU_TILE) // 2) * MXU_TILE
    return ((0, cut), (cut, f)) if 0 < cut < f else ((0, f),)


def _swiglu(h, w1_ref, w3_ref, w2_ref):
    y = None
    for lo, hi in _ff_splits(w1_ref.shape[1]):
        part = _swiglu_partial(h, w1_ref[:, lo:hi], w3_ref[:, lo:hi], w2_ref[lo:hi, :])
        y = part if y is None else y + part
    return y


def _even_tail_kernel(x_ref, att_ref, p_ref, halo_ref, wpool_ref, scale_ref,
                      wo_att_ref, wo_pool_ref, g_ref, w1_ref, w3_ref, w2_ref, o_ref,
                      *, tiles_per_seq):
    i = pl.program_id(0)
    tm = x_ref.shape[0]
    tile_in_seq = i % tiles_per_seq
    p = p_ref[...]
    halo = jnp.where(tile_in_seq == 0, 0.0, halo_ref[...])
    ext = jnp.concatenate([halo, p], axis=0)
    pos = tile_in_seq * tm + lax.broadcasted_iota(jnp.int32, (tm, 1), 0)
    count = (pos + 1).astype(F32)
    mixed = []
    for gi, w in enumerate(POOL_WINDOWS):
        sl = slice(gi * POOL_GROUP_DIM, (gi + 1) * POOL_GROUP_DIM)
        acc = ext[:, sl]
        sh = 1
        while sh < w:
            acc = acc + pltpu.roll(acc, sh, axis=0)
            sh *= 2
        mean = acc[POOL_HALO:, :] / jnp.minimum(count, float(w))
        pooled = (mean - p[:, sl]).astype(BF16)
        mixed.append(jnp.dot(pooled, wpool_ref[gi], preferred_element_type=F32))
    pool = (jnp.concatenate(mixed, axis=1) * scale_ref[...]).astype(BF16)
    y = jnp.dot(att_ref[...], wo_att_ref[...], preferred_element_type=F32)
    y = y + jnp.dot(pool, wo_pool_ref[...], preferred_element_type=F32)
    x1 = x_ref[...] + y
    h = _rms(x1, g_ref[...]).astype(BF16)
    o_ref[...] = x1 + _swiglu(h, w1_ref, w3_ref, w2_ref)


def _even_tail(x2d, att, p_in, w_pool, pool_scale, w_out, ffn_gain, w1, w3, w2,
               *, seq, tm):
    t, d = x2d.shape
    resident = lambda w: pl.BlockSpec(w.shape, lambda i: (0,) * w.ndim,
                                      pipeline_mode=pl.Buffered(1))
    wo_att = w_out[:FOX_WIDTH].reshape(FOX_HEADS, FOX_HEAD_DIM, d)
    wo_att = jnp.pad(wo_att, ((0, 0), (0, LANES - FOX_HEAD_DIM), (0, 0)))
    wo_att = wo_att.reshape(FOX_HEADS * LANES, d)
    wo_pool = w_out[FOX_WIDTH:]
    halo_blocks = tm // POOL_HALO
    kern = functools.partial(_even_tail_kernel, tiles_per_seq=seq // tm)
    return pl.pallas_call(
        kern,
        grid=(t // tm,),
        in_specs=[
            pl.BlockSpec((tm, d), lambda i: (i, 0)),
            pl.BlockSpec((tm, FOX_HEADS * LANES), lambda i: (i, 0)),
            pl.BlockSpec((tm, POOL_WIDTH), lambda i: (i, 0)),
            pl.BlockSpec((POOL_HALO, POOL_WIDTH),
                         lambda i: (jnp.maximum(i * halo_blocks - 1, 0), 0)),
            resident(w_pool),
            pl.BlockSpec((1, POOL_WIDTH), lambda i: (0, 0)),
            resident(wo_att), resident(wo_pool),
            pl.BlockSpec((1, d), lambda i: (0, 0)),
            resident(w1), resident(w3), resident(w2),
        ],
        out_specs=pl.BlockSpec((tm, d), lambda i: (i, 0)),
        out_shape=jax.ShapeDtypeStruct((t, d), F32),
        compiler_params=_params("parallel"),
        name="even_tail",
    )(x2d, att, p_in, p_in, w_pool, pool_scale, wo_att, wo_pool, ffn_gain, w1, w3, w2)


def _gelu_tanh(x):
    c = math.sqrt(2.0 / math.pi)
    return 0.5 * x * (1.0 + jnp.tanh(c * (x + 0.044715 * (x * x * x))))


def _s5_kernel(x_ref, g_ref, win_ref, perm_ref, permt_ref, bblk_ref, cblk_ref,
               are_ref, aim_ref, d_ref, o_ref, zero_ref, xs_ref, st_ref):
    nb, tc, d = x_ref.shape
    rows = nb * tc
    n_slabs = d // LANES
    sw = 2 * SSM_SLAB_STATE

    @pl.when(pl.program_id(0) == 0)
    def _():
        st_ref[...] = jnp.zeros_like(st_ref)

    x = x_ref[...].reshape(rows, d)
    h = _rms(x, g_ref[...]).astype(BF16)
    h_tb = jnp.dot(perm_ref[...], h, preferred_element_type=F32).astype(BF16)
    u = jnp.dot(h_tb, win_ref[...], preferred_element_type=F32)
    ub = u.astype(BF16)
    hs = SSM_SLAB_STATE
    ys = []
    for s in range(n_slabs):
        lo = s * sw
        xs_ref[:, lo:lo + sw] = jnp.dot(
            ub[:, s * LANES:(s + 1) * LANES], bblk_ref[s],
            preferred_element_type=F32)
        a_r = are_ref[:, lo:lo + hs]
        a_i = aim_ref[:, lo:lo + hs]
        x_r = st_ref[:, lo:lo + hs]
        x_i = st_ref[:, lo + hs:lo + sw]
        for t in range(tc):
            r0 = t * SUBLANES
            new_r = a_r * x_r - a_i * x_i + xs_ref[r0:r0 + SUBLANES, lo:lo + hs]
            new_i = a_r * x_i + a_i * x_r + xs_ref[r0:r0 + SUBLANES, lo + hs:lo + sw]
            xs_ref[r0:r0 + SUBLANES, lo:lo + hs] = new_r
            xs_ref[r0:r0 + SUBLANES, lo + hs:lo + sw] = new_i
            x_r, x_i = new_r, new_i
        st_ref[:, lo:lo + hs] = x_r
        st_ref[:, lo + hs:lo + sw] = x_i
        ys.append(jnp.dot(xs_ref[:, lo:lo + sw].astype(BF16), cblk_ref[s],
                          preferred_element_type=F32))
    y = jnp.concatenate(ys, axis=1) + d_ref[...] * u
    g = _gelu_tanh(y).astype(BF16)
    g_bt = jnp.dot(permt_ref[...], g, preferred_element_type=F32)
    o_ref[...] = g_bt.reshape(nb, tc, d).astype(o_ref.dtype)
    zero_ref[...] = jnp.zeros_like(zero_ref)


def _s5_mixer(x3d, gain, w_in, bblk, cblk, a_re, a_im, d_skip, *, tc, zero_shape):
    nb, seq, d = x3d.shape
    steps = seq // tc
    zero_rows = zero_shape[0] // steps
    assert zero_rows * steps == zero_shape[0] and zero_rows % SEG_ALIGN == 0
    rows = nb * tc
    n_slabs = d // LANES
    sw = 2 * SSM_SLAB_STATE
    r = np.arange(rows)
    perm = np.zeros((rows, rows), np.float32)
    perm[(r % tc) * nb + r // tc, r] = 1.0
    perm_j = jnp.asarray(perm, BF16)
    permt_j = jnp.asarray(perm.T, BF16)
    kern = _s5_kernel
    full2 = lambda a: pl.BlockSpec(a.shape, lambda i: (0, 0))
    full3 = lambda a: pl.BlockSpec(a.shape, lambda i: (0, 0, 0))
    return pl.pallas_call(
        kern,
        grid=(seq // tc,),
        in_specs=[
            pl.BlockSpec((nb, tc, d), lambda i: (0, i, 0)),
            full2(gain), full2(w_in), full2(perm_j), full2(permt_j),
            full3(bblk), full3(cblk), full2(a_re), full2(a_im), full2(d_skip),
        ],
        out_specs=[pl.BlockSpec((nb, tc, d), lambda i: (0, i, 0)),
                   pl.BlockSpec((zero_rows, zero_shape[1]), lambda i: (i, 0))],
        out_shape=[jax.ShapeDtypeStruct((nb, seq, d), BF16),
                   jax.ShapeDtypeStruct(zero_shape, BF16)],
        scratch_shapes=[
            pltpu.VMEM((rows, n_slabs * sw), F32),
            pltpu.VMEM((SUBLANES, n_slabs * sw), F32),
        ],
        compiler_params=_params("arbitrary"),
        name="s5_mixer",
    )(x3d, gain, w_in, perm_j, permt_j, bblk, cblk, a_re, a_im, d_skip)


def _s5_coefficients(a_re, a_im, log_dt, b_re, b_im, c_re, c_im):
    dt = jnp.exp(log_dt.astype(F32))[:, None]
    ar = a_re.astype(F32)
    ai = a_im.astype(F32)
    mag = jnp.exp(ar * dt)
    abar_re = mag * jnp.cos(ai * dt)
    abar_im = mag * jnp.sin(ai * dt)
    den = ar * ar + ai * ai
    nr = abar_re - 1.0
    ni = abar_im
    coef_re = (nr * ar + ni * ai) / den
    coef_im = (ni * ar - nr * ai) / den
    br = b_re.astype(F32)
    bi = b_im.astype(F32)
    bbar_re = coef_re[..., None] * br - coef_im[..., None] * bi
    bbar_im = coef_re[..., None] * bi + coef_im[..., None] * br
    n_groups = ar.shape[0]
    n_slabs = n_groups // SSM_SLAB_GROUPS
    eye = jnp.eye(SSM_SLAB_GROUPS, dtype=F32)

    def in_block(bb):
        bb = bb.reshape(n_slabs, SSM_SLAB_GROUPS, SSM_STATE, SSM_GROUP)
        blk = jnp.einsum('sgph,gk->sghkp', bb, eye)
        return blk.reshape(n_slabs, LANES, SSM_SLAB_STATE)

    def out_block(cc):
        cc = cc.reshape(n_slabs, SSM_SLAB_GROUPS, SSM_GROUP, SSM_STATE)
        blk = jnp.einsum('sghp,gk->sgpkh', cc, eye)
        return blk.reshape(n_slabs, SSM_SLAB_STATE, LANES)

    bblk = jnp.concatenate([in_block(bbar_re), in_block(bbar_im)], axis=2)
    cblk = jnp.concatenate([out_block(c_re.astype(F32)),
                            -out_block(c_im.astype(F32))], axis=1)

    def lanes(a):
        a = a.reshape(n_slabs, 1, SSM_SLAB_STATE)
        a = jnp.concatenate([a, a], axis=2).reshape(1, -1)
        return jnp.broadcast_to(a, (SUBLANES, a.shape[1]))

    return (bblk.astype(BF16), cblk.astype(BF16), lanes(abar_re), lanes(abar_im))


MOE_CHUNK = 512
MOE_FIRST = 192
MOE_REST = MOE_CHUNK - MOE_FIRST
MOE_PIECES = ((0, MOE_FIRST), (MOE_FIRST, MOE_REST))
MOE_TILE = 512


def _glu_router_kernel(x_ref, act_ref, wa_ref, wb_ref, g_ref, w_ref, b_ref,
                       x3_ref, h_ref, pos_ref, post_ref, cnt_ref):
    tm, d = x_ref.shape
    act = act_ref[...]
    a = jnp.dot(act, wa_ref[...], preferred_element_type=F32)
    b = jnp.dot(act, wb_ref[...], preferred_element_type=F32)
    x3 = x_ref[...] + a * _sigmoid(b)
    x3_ref[...] = x3
    h = _rms(x3, g_ref[...])
    h_hi = h.astype(BF16)
    h_ref[:, 0:d] = h_hi
    h_lo = (h - h_hi.astype(F32)).astype(BF16)
    p_hi = jnp.dot(h_hi, w_ref[...], preferred_element_type=F32)
    p_lo = jnp.dot(h_lo, w_ref[...], preferred_element_type=F32)
    logits = p_hi + pltpu.roll(p_hi, LANES - N_EXPERTS, axis=1) + p_lo + b_ref[...]
    lane = lax.broadcasted_iota(jnp.int32, logits.shape, 1)
    logits = jnp.where(lane < N_EXPERTS, logits, -jnp.inf)
    m1 = jnp.max(logits, axis=1, keepdims=True)
    i1 = jnp.min(jnp.where(logits == m1, lane, LANES), axis=1, keepdims=True)
    rest = jnp.where(lane == i1, -jnp.inf, logits)
    m2 = jnp.max(rest, axis=1, keepdims=True)
    i2 = jnp.min(jnp.where(rest == m2, lane, LANES), axis=1, keepdims=True)
    e2 = jnp.exp(m2 - m1)
    g1 = 1.0 / (1.0 + e2)
    g2 = e2 / (1.0 + e2)
    gate = jnp.where(lane == i1, g1, 0.0) + jnp.where(lane == i2, g2, 0.0)
    g_hi = gate.astype(BF16).astype(F32)
    g_r = gate - g_hi
    g_mid = g_r.astype(BF16).astype(F32)
    packed = jnp.where(lane < N_EXPERTS, g_hi,
                       jnp.where(lane < 2 * N_EXPERTS,
                                 pltpu.roll(g_mid, N_EXPERTS, axis=1),
                                 pltpu.roll(g_r - g_mid, 2 * N_EXPERTS, axis=1)))
    h_ref[:, d:d + LANES] = packed.astype(BF16)
    member = jnp.where((lane == i1) | (lane == i2), 1.0, 0.0)
    row = lax.broadcasted_iota(jnp.int32, member.shape, 0)
    c = member
    sh = 1
    while sh < tm:
        c = c + jnp.where(row >= sh, pltpu.roll(c, sh, axis=0), 0.0)
        sh *= 2
    pos = jnp.where(member > 0.0, c - member, -1.0)
    pos_ref[...] = pos
    post_ref[0] = pos.T[0:SUBLANES, :]
    cnt_ref[0] = jnp.broadcast_to(c[tm - 1:tm, :], (SUBLANES, LANES))


def _glu_router(x2d, act2d, wa, wb, gain, w_pad, b_pad):
    t, d = x2d.shape
    tm = MOE_CHUNK
    n_chunks = t // tm
    return pl.pallas_call(
        _glu_router_kernel,
        grid=(n_chunks,),
        in_specs=[
            pl.BlockSpec((tm, d), lambda i: (i, 0)),
            pl.BlockSpec((tm, d), lambda i: (i, 0)),
            pl.BlockSpec(wa.shape, lambda i: (0, 0)),
            pl.BlockSpec(wb.shape, lambda i: (0, 0)),
            pl.BlockSpec((1, d), lambda i: (0, 0)),
            pl.BlockSpec((d, LANES), lambda i: (0, 0)),
            pl.BlockSpec((1, LANES), lambda i: (0, 0)),
        ],
        out_specs=[
            pl.BlockSpec((tm, d), lambda i: (i, 0)),
            pl.BlockSpec((tm, d + LANES), lambda i: (i, 0)),
            pl.BlockSpec((tm, LANES), lambda i: (i, 0)),
            pl.BlockSpec((1, SUBLANES, tm), lambda i: (i, 0, 0)),
            pl.BlockSpec((1, SUBLANES, LANES), lambda i: (i, 0, 0)),
        ],
        out_shape=[
            jax.ShapeDtypeStruct((t, d), F32),
            jax.ShapeDtypeStruct((t, d + LANES), BF16),
            jax.ShapeDtypeStruct((t, LANES), F32),
            jax.ShapeDtypeStruct((n_chunks, SUBLANES, tm), F32),
            jax.ShapeDtypeStruct((n_chunks, SUBLANES, LANES), F32),
        ],
        compiler_params=_params("parallel"),
        name="glu_router",
    )(x2d, act2d, wa, wb, gain, w_pad, b_pad)


MOE_SPARE = max(MOE_FIRST, MOE_REST)


def _moe_tiles(n_tokens, row_multiple):
    n_chunks = n_tokens // MOE_CHUNK
    max_rows = (TOP_K * n_tokens + (SEG_ALIGN - 1) * N_EXPERTS * n_chunks
                + N_EXPERTS * (MOE_SPARE + MOE_TILE))
    unit = math.lcm(MOE_TILE, row_multiple)
    return -(-max_rows // unit) * (unit // MOE_TILE)


def _moe_layout(cnt, n_tiles):
    seg = (cnt + SEG_ALIGN - 1) // SEG_ALIGN * SEG_ALIGN
    used = jnp.sum(seg, axis=0)
    padded = (used + MOE_SPARE + MOE_TILE - 1) // MOE_TILE * MOE_TILE
    ends = jnp.cumsum(padded)
    start = ends - padded
    off = start[None, :] + jnp.cumsum(seg, axis=0) - seg
    tile_start = jnp.arange(n_tiles, dtype=jnp.int32) * MOE_TILE
    tile_e = jnp.sum((tile_start[:, None] >= ends[None, :]).astype(jnp.int32), axis=1)
    tile_e = jnp.minimum(tile_e, N_EXPERTS - 1)
    tile_rows = jnp.clip(used[tile_e] - (tile_start - start[tile_e]), 0, MOE_TILE)
    return (off.reshape(-1).astype(jnp.int32), cnt.reshape(-1).astype(jnp.int32),
            tile_e.astype(jnp.int32), tile_rows.astype(jnp.int32))


def _segment_copies(hbm_ref, buf_ref, sem_ref, off_ref, chunk, to_hbm, slot=()):
    copies = []
    for e in range(N_EXPERTS):
        off = pl.multiple_of(off_ref[chunk * N_EXPERTS + e], SEG_ALIGN)
        for piece, (first, rows) in enumerate(MOE_PIECES):
            hbm = hbm_ref.at[pl.ds(off + first, rows)]
            buf = buf_ref.at[(*slot, e, pl.ds(first, rows))]
            src, dst = (buf, hbm) if to_hbm else (hbm, buf)
            copies.append(pltpu.make_async_copy(src, dst, sem_ref.at[(*slot, e, piece)]))
    return copies


def _onehot(index, target):
    return jnp.where(index == target, 1.0, 0.0).astype(BF16)


def _gather_kernel(off_ref, cnt_ref, h_ref, post_ref, zeros_hbm, o_hbm, stage_ref,
                   sem_ref):
    del zeros_hbm
    c = pl.program_id(0)
    h = h_ref[...]
    copies = _segment_copies(o_hbm, stage_ref, sem_ref, off_ref, c, to_hbm=True)

    def move_piece(e, piece):
        first, rows = MOE_PIECES[piece]
        rank = first + lax.broadcasted_iota(jnp.int32, (rows, MOE_CHUNK), 0)
        onehot = _onehot(rank.astype(F32), post_ref[0, e:e + 1, :])
        stage_ref[e, first:first + rows] = jnp.dot(
            onehot, h, preferred_element_type=F32).astype(BF16)
        copies[2 * e + piece].start()

    for e in range(N_EXPERTS):
        move_piece(e, 0)

        @pl.when(cnt_ref[c * N_EXPERTS + e] > MOE_FIRST)
        def _(e=e):
            move_piece(e, 1)

    for e in range(N_EXPERTS):
        copies[2 * e].wait()

        @pl.when(cnt_ref[c * N_EXPERTS + e] > MOE_FIRST)
        def _(e=e):
            copies[2 * e + 1].wait()


def _gather(h2d, post, off, cnt, zero_buf):
    t, d = h2d.shape
    n_chunks = t // MOE_CHUNK
    grid_spec = pltpu.PrefetchScalarGridSpec(
        num_scalar_prefetch=2,
        grid=(n_chunks,),
        in_specs=[
            pl.BlockSpec((MOE_CHUNK, d), lambda i, off, cnt: (i, 0)),
            pl.BlockSpec((1, SUBLANES, MOE_CHUNK), lambda i, off, cnt: (i, 0, 0)),
            pl.BlockSpec(memory_space=pl.ANY),
        ],
        out_specs=pl.BlockSpec(memory_space=pl.ANY),
        scratch_shapes=[
            pltpu.VMEM((N_EXPERTS, MOE_CHUNK, d), BF16),
            pltpu.SemaphoreType.DMA((N_EXPERTS, 2)),
        ],
    )
    return pl.pallas_call(
        _gather_kernel,
        grid_spec=grid_spec,
        out_shape=jax.ShapeDtypeStruct(zero_buf.shape, BF16),
        input_output_aliases={4: 0},
        compiler_params=_params("arbitrary"),
        name="moe_gather",
    )(off, cnt, h2d, post, zero_buf)


def _expert_kernel(te_ref, tr_ref, x_ref, w1_ref, w3_ref, w2_ref, o_ref):
    i = pl.program_id(0)
    rows = tr_ref[i]
    d = o_ref.shape[1]

    def run(n):
        row = lax.broadcasted_iota(jnp.int32, (n, x_ref.shape[1]), 0)
        x = jnp.where(row < rows, x_ref[0:n, :], jnp.zeros((), x_ref.dtype))
        lane = lax.broadcasted_iota(jnp.int32, (n, LANES), 1)
        mine = (lane < 3 * N_EXPERTS) & (jnp.bitwise_and(lane, N_EXPERTS - 1) == te_ref[i])
        gate = jnp.sum(jnp.where(mine, x[:, d:d + LANES].astype(F32), 0.0),
                       axis=1, keepdims=True)
        y = _swiglu(x[:, 0:d], w1_ref, w3_ref, w2_ref)
        o_ref[0:n, :] = (gate * y).astype(o_ref.dtype)

    half = MOE_TILE // 2

    @pl.when(rows > half)
    def _():
        run(MOE_TILE)

    @pl.when((rows > 0) & (rows <= half))
    def _():
        run(half)
        o_ref[half:MOE_TILE, :] = jnp.zeros((MOE_TILE - half, d), o_ref.dtype)

    @pl.when(rows == 0)
    def _():
        o_ref[...] = jnp.zeros_like(o_ref)


def _experts(x_sorted, w1, w3, w2, tile_e, tile_rows):
    assert N_EXPERTS & (N_EXPERTS - 1) == 0
    n_rows = x_sorted.shape[0]
    d, f = w1.shape[1:]
    grid_spec = pltpu.PrefetchScalarGridSpec(
        num_scalar_prefetch=2,
        grid=(n_rows // MOE_TILE,),
        in_specs=[
            pl.BlockSpec((MOE_TILE, d + LANES), lambda i, te, tr: (i, 0)),
            pl.BlockSpec((None, d, f), lambda i, te, tr: (te[i], 0, 0)),
            pl.BlockSpec((None, d, f), lambda i, te, tr: (te[i], 0, 0)),
            pl.BlockSpec((None, f, d), lambda i, te, tr: (te[i], 0, 0)),
        ],
        out_specs=pl.BlockSpec((MOE_TILE, d), lambda i, te, tr: (i, 0)),
    )
    return pl.pallas_call(
        _expert_kernel,
        grid_spec=grid_spec,
        out_shape=jax.ShapeDtypeStruct((n_rows, d), BF16),
        compiler_params=_params("arbitrary"),
        name="moe_experts",
    )(tile_e, tile_rows, x_sorted, w1, w3, w2)


def _combine_kernel(off_ref, cnt_ref, x_ref, pos_ref, fg_ref, y_hbm,
                    o_ref, ybuf_ref, acc_ref, sem_ref):
    c = pl.program_id(0)
    n_chunks = pl.num_programs(0)
    slot = c % 2

    def fetch(chunk, sl, start):
        copies = _segment_copies(y_hbm, ybuf_ref, sem_ref, off_ref, chunk,
                                 to_hbm=False, slot=(sl,))
        for e in range(N_EXPERTS):
            first, second = copies[2 * e], copies[2 * e + 1]
            if start:
                first.start()
            else:
                first.wait()

            @pl.when(cnt_ref[chunk * N_EXPERTS + e] > MOE_FIRST)
            def _(second=second):
                if start:
                    second.start()
                else:
                    second.wait()

    @pl.when(c == 0)
    def _():
        fetch(c, slot, True)

    @pl.when(c + 1 < n_chunks)
    def _():
        fetch(c + 1, 1 - slot, True)

    fetch(c, slot, False)

    def piece_rows(e, piece):
        first, rows = MOE_PIECES[piece]
        rank = first + lax.broadcasted_iota(jnp.int32, (MOE_CHUNK, rows), 1)
        onehot = _onehot(rank.astype(F32), pos_ref[:, e:e + 1])
        return jnp.dot(onehot, ybuf_ref[slot, e, first:first + rows],
                       preferred_element_type=F32)

    acc = x_ref[...]
    for e in range(N_EXPERTS):
        acc = acc + piece_rows(e, 0)
    acc_ref[...] = acc
    for e in range(N_EXPERTS):
        @pl.when(cnt_ref[c * N_EXPERTS + e] > MOE_FIRST)
        def _(e=e):
            acc_ref[...] += piece_rows(e, 1)

    o_ref[...] = _rms(acc_ref[...], fg_ref[...])


def _combine(x2d, pos, final_gain, y_sorted, off, cnt):
    t, d = x2d.shape
    n_chunks = t // MOE_CHUNK
    grid_spec = pltpu.PrefetchScalarGridSpec(
        num_scalar_prefetch=2,
        grid=(n_chunks,),
        in_specs=[
            pl.BlockSpec((MOE_CHUNK, d), lambda i, off, cnt: (i, 0)),
            pl.BlockSpec((MOE_CHUNK, LANES), lambda i, off, cnt: (i, 0)),
            pl.BlockSpec((1, d), lambda i, off, cnt: (0, 0)),
            pl.BlockSpec(memory_space=pl.ANY),
        ],
        out_specs=pl.BlockSpec((MOE_CHUNK, d), lambda i, off, cnt: (i, 0)),
        scratch_shapes=[
            pltpu.VMEM((2, N_EXPERTS, MOE_CHUNK, d), BF16),
            pltpu.VMEM((MOE_CHUNK, d), F32),
            pltpu.SemaphoreType.DMA((2, N_EXPERTS, 2)),
        ],
    )
    return pl.pallas_call(
        _combine_kernel,
        grid_spec=grid_spec,
        out_shape=jax.ShapeDtypeStruct((t, d), F32),
        compiler_params=_params("arbitrary"),
        name="moe_combine",
    )(off, cnt, x2d, pos, final_gain, y_sorted)


def _glu_moe(x2d, act2d, wa, wb, gain, router_w, router_b, w1, w3, w2, final_gain,
             zero_buf):
    t, d = x2d.shape
    rw_hi = router_w.astype(BF16)
    rw_lo = (router_w.astype(F32) - rw_hi.astype(F32)).astype(BF16)
    rw = jnp.pad(jnp.concatenate([rw_hi, rw_lo], axis=1),
                 ((0, 0), (0, LANES - 2 * N_EXPERTS)))
    rb = jnp.pad(router_b.astype(F32), (0, LANES - N_EXPERTS)).reshape(1, LANES)
    x3, h, pos, post, cnt = _glu_router(x2d, act2d, wa, wb, gain, rw, rb)
    cnt = cnt[:, 0, :N_EXPERTS].astype(jnp.int32)
    off, cnt_flat, tile_e, tile_rows = _moe_layout(cnt, zero_buf.shape[0] // MOE_TILE)
    h_sorted = _gather(h, post, off, cnt_flat, zero_buf)
    y_sorted = _experts(h_sorted, w1, w3, w2, tile_e, tile_rows)
    return _combine(x3, pos, final_gain, y_sorted, off, cnt_flat)


def _even_weights(w_in, b_forget):
    fw = FOX_WIDTH
    wq = w_in[:, :fw] * FOX_HEAD_DIM ** -0.5
    wf = jnp.pad(w_in[:, 3 * fw:3 * fw + FOX_HEADS],
                 ((0, 0), (0, LANES - FOX_HEADS)))
    wp = w_in[:, 3 * fw + FOX_HEADS:]
    w_all = jnp.concatenate([wq, w_in[:, fw:3 * fw], wp, wf], axis=1).astype(BF16)
    bias = jnp.pad(b_forget.astype(F32), (0, LANES - FOX_HEADS)).reshape(1, LANES)
    qw = FOX_HEADS * LANES
    place = np.zeros((LANES, 2 * qw), np.float32)
    for hh in range(FOX_HEADS):
        for piece in range(3):
            place[piece * FOX_HEADS + hh, hh * LANES + BIAS_LANE + piece] = 1.0
            place[piece * FOX_HEADS + hh, qw + hh * LANES + ONES_LANE + piece] = -1.0
    return w_all, bias, jnp.asarray(place, BF16)


def kernel(x, even_mix_norm, even_w_in, even_b_forget, even_w_pool, even_pool_scale, even_w_out, even_ffn_norm, even_ffn_w1, even_ffn_w3, even_ffn_w2, odd_mix_norm, odd_w_in, ssm_a_re, ssm_a_im, ssm_log_dt, ssm_b_re, ssm_b_im, ssm_c_re, ssm_c_im, ssm_d, odd_w_glu_a, odd_w_glu_b, odd_moe_norm, router_w, router_b, expert_w1, expert_w3, expert_w2, final_norm):
    b, s, d = x.shape
    t = b * s
    assert b == SUBLANES, "the S5 recurrence keeps one batch row per sublane"
    x2d = x.reshape(t, d)
    row = lambda v: v.reshape(1, -1).astype(F32)

    w_all, bias, place = _even_weights(even_w_in[0], even_b_forget[0])
    f_dense = even_ffn_w1.shape[2]
    (q_aug, k_aug, v, p_in), casted = _even_inproj(
        x2d, row(even_mix_norm[0]), w_all, bias, place,
        [even_ffn_w1[0], even_ffn_w3[0], even_ffn_w2[0].reshape(d, f_dense),
         even_w_out[0], even_w_pool[0].reshape(-1, POOL_GROUP_DIM),
         odd_w_in[0], odd_w_glu_a[0], odd_w_glu_b[0]],
        seq=s, tm=512)
    ffn_w1, ffn_w3, ffn_w2, w_out, w_pool, s5_w_in, glu_a, glu_b = casted
    n_e, _, f = expert_w1[0].shape
    att, (ew1, ew3, ew2) = _attention(
        q_aug, k_aug, v,
        [expert_w1[0].reshape(n_e * d, f), expert_w3[0].reshape(n_e * d, f),
         expert_w2[0].reshape(n_e * f, d)],
        batch=b, seq=s, tq=256)
    x2 = _even_tail(x2d, att, p_in, w_pool.reshape(even_w_pool[0].shape),
                    row(even_pool_scale[0]), w_out, row(even_ffn_norm[0]),
                    ffn_w1, ffn_w3, ffn_w2.reshape(f_dense, d), seq=s, tm=512)

    bblk, cblk, a_re, a_im = _s5_coefficients(
        ssm_a_re[0], ssm_a_im[0], ssm_log_dt[0], ssm_b_re[0], ssm_b_im[0],
        ssm_c_re[0], ssm_c_im[0])
    moe_rows = _moe_tiles(t, (s // S5_STEP) * SEG_ALIGN) * MOE_TILE
    g, zero_buf = _s5_mixer(x2.reshape(b, s, d), row(odd_mix_norm[0]),
                            s5_w_in, bblk, cblk, a_re, a_im,
                            row(ssm_d[0]), tc=S5_STEP,
                            zero_shape=(moe_rows, d + LANES))
    out = _glu_moe(x2, g.reshape(t, d), glu_a, glu_b, row(odd_moe_norm[0]),
                   router_w[0], router_b[0], ew1.reshape(n_e, d, f),
                   ew3.reshape(n_e, d, f), ew2.reshape(n_e, f, d),
                   row(final_norm), zero_buf)
    return out.reshape(b, s, d)
```

```python
import functools
import math

import numpy as np
import jax
import jax.numpy as jnp
from jax import lax
from jax.experimental import pallas as pl
from jax.experimental.pallas import tpu as pltpu

F32 = jnp.float32
BF16 = jnp.bfloat16

EPS = 1e-6
NEG_INF = -1e30
LANES = 128
SUBLANES = 8
SEG_ALIGN = 16
VMEM_LIMIT = 56 * 1024 * 1024

FOX_HEADS = 8
FOX_HEAD_DIM = 64
FOX_WIDTH = FOX_HEADS * FOX_HEAD_DIM
POOL_WINDOWS = (2, 4, 8, 16)
POOL_GROUP_DIM = 128
POOL_WIDTH = len(POOL_WINDOWS) * POOL_GROUP_DIM
POOL_HALO = 16
SSM_GROUP = 16
SSM_STATE = 64
SSM_SLAB_GROUPS = LANES // SSM_GROUP
SSM_SLAB_STATE = SSM_SLAB_GROUPS * SSM_STATE
S5_STEP = 32
N_EXPERTS = 8
TOP_K = 2

BIAS_LANE = FOX_HEAD_DIM
ONES_LANE = FOX_HEAD_DIM + 3
DENOM_LANE = FOX_HEAD_DIM


def _params(*sem):
    return pltpu.CompilerParams(dimension_semantics=sem,
                                vmem_limit_bytes=VMEM_LIMIT)


def _rms(x, g):
    ms = jnp.mean(x * x, axis=-1, keepdims=True)
    return x * lax.rsqrt(ms + EPS) * g


def _sigmoid(x):
    return 1.0 / (1.0 + jnp.exp(-x))


def _lane_range_ones(lo, hi):
    lane = lax.broadcasted_iota(jnp.int32, (1, LANES), 1)
    return jnp.where((lane >= lo) & (lane < hi), 1.0, 0.0).astype(F32)


def _cast_specs(weights, steps, index_map):
    specs = []
    for w in weights:
        rows = w.shape[0] // steps
        assert rows * steps == w.shape[0] and rows % SEG_ALIGN == 0
        specs.append(pl.BlockSpec((rows, w.shape[1]), index_map))
    return specs


def _cast_slabs(src_refs, dst_refs):
    for src_ref, dst_ref in zip(src_refs, dst_refs):
        dst_ref[...] = src_ref[...].astype(dst_ref.dtype)


def _even_inproj_kernel(x_ref, g_ref, w_ref, bias_ref, place_ref, *rest,
                        tiles_per_seq, n_cast):
    cast_in, rest = rest[:n_cast], rest[n_cast:]
    q_ref, k_ref, v_ref, p_ref = rest[:4]
    cast_out, carry_ref = rest[4:4 + n_cast], rest[4 + n_cast]
    _cast_slabs(cast_in, cast_out)
    i = pl.program_id(0)
    tm = x_ref.shape[0]
    fw = FOX_WIDTH
    qw = FOX_HEADS * LANES
    h = _rms(x_ref[...], g_ref[...]).astype(BF16)
    z = jnp.dot(h, w_ref[...], preferred_element_type=F32)
    p_ref[...] = z[:, 3 * fw:3 * fw + POOL_WIDTH]
    fg = z[:, 3 * fw + POOL_WIDTH:] + bias_ref[...]
    lf = jnp.minimum(fg, 0.0) - jnp.log1p(jnp.exp(-jnp.abs(fg)))
    row = lax.broadcasted_iota(jnp.int32, lf.shape, 0)
    c = lf
    sh = 1
    while sh < tm:
        c = c + jnp.where(row >= sh, pltpu.roll(c, sh, axis=0), 0.0)
        sh *= 2

    @pl.when(i % tiles_per_seq == 0)
    def _():
        carry_ref[...] = jnp.zeros_like(carry_ref)

    c = c + carry_ref[0:1, :]
    carry_ref[...] = jnp.broadcast_to(c[tm - 1:tm, :], carry_ref.shape)
    hi = c.astype(BF16).astype(F32)
    r1 = c - hi
    mid = r1.astype(BF16).astype(F32)
    lo = r1 - mid
    lane = lax.broadcasted_iota(jnp.int32, (tm, LANES), 1)
    packed = jnp.where(lane < FOX_HEADS, hi,
                       jnp.where(lane < 2 * FOX_HEADS,
                                 pltpu.roll(mid, FOX_HEADS, axis=1),
                                 pltpu.roll(lo, 2 * FOX_HEADS, axis=1)))
    placed = jnp.dot(packed.astype(BF16), place_ref[...],
                     preferred_element_type=F32)
    low = lane < FOX_HEAD_DIM
    ones_q = _lane_range_ones(ONES_LANE, ONES_LANE + 3)
    ones_k = _lane_range_ones(BIAS_LANE, BIAS_LANE + 3)
    ones_v = _lane_range_ones(DENOM_LANE, DENOM_LANE + 1)

    def head_lanes(base, hh):
        pair = z[:, base + (hh // 2) * LANES:base + (hh // 2 + 1) * LANES]
        return pltpu.roll(pair, FOX_HEAD_DIM, axis=1) if hh % 2 else pair

    for hh in range(FOX_HEADS):
        sl = slice(hh * LANES, (hh + 1) * LANES)
        q_ref[:, sl] = jnp.where(low, head_lanes(0, hh),
                                 placed[:, sl] + ones_q).astype(BF16)
        k_ref[:, sl] = jnp.where(
            low, head_lanes(fw, hh),
            placed[:, qw + hh * LANES:qw + (hh + 1) * LANES] + ones_k).astype(BF16)
        v_ref[:, sl] = jnp.where(low, head_lanes(2 * fw, hh), ones_v).astype(BF16)


def _even_inproj(x2d, gain, w_all, bias, place, cast_weights, *, seq, tm):
    t, d = x2d.shape
    n = w_all.shape[1]
    qw = FOX_HEADS * LANES
    cast_specs = _cast_specs(cast_weights, t // tm, lambda i: (i, 0))
    kern = functools.partial(_even_inproj_kernel, tiles_per_seq=seq // tm,
                             n_cast=len(cast_weights))
    outs = pl.pallas_call(
        kern,
        grid=(t // tm,),
        in_specs=[
            pl.BlockSpec((tm, d), lambda i: (i, 0)),
            pl.BlockSpec((1, d), lambda i: (0, 0)),
            pl.BlockSpec((d, n), lambda i: (0, 0)),
            pl.BlockSpec((1, LANES), lambda i: (0, 0)),
            pl.BlockSpec(place.shape, lambda i: (0, 0)),
            *cast_specs,
        ],
        out_specs=[
            pl.BlockSpec((tm, qw), lambda i: (i, 0)),
            pl.BlockSpec((tm, qw), lambda i: (i, 0)),
            pl.BlockSpec((tm, qw), lambda i: (i, 0)),
            pl.BlockSpec((tm, POOL_WIDTH), lambda i: (i, 0)),
            *cast_specs,
        ],
        out_shape=[
            jax.ShapeDtypeStruct((t, qw), BF16),
            jax.ShapeDtypeStruct((t, qw), BF16),
            jax.ShapeDtypeStruct((t, qw), BF16),
            jax.ShapeDtypeStruct((t, POOL_WIDTH), F32),
        ] + [jax.ShapeDtypeStruct(w.shape, BF16) for w in cast_weights],
        scratch_shapes=[pltpu.VMEM((SUBLANES, LANES), F32)],
        compiler_params=_params("arbitrary"),
        name="even_inproj",
    )(x2d, gain, w_all, bias, place, *cast_weights)
    return outs[:4], outs[4:]


def _dot_nt(a, b):
    return lax.dot_general(a, b, (((1,), (1,)), ((), ())),
                           preferred_element_type=F32)


def _attn_kernel(q_ref, k_ref, v_ref, *rest, tq, n_cast):
    cast_in, o_ref, cast_out = rest[:n_cast], rest[n_cast], rest[n_cast + 1:]
    seq = q_ref.shape[0]
    row = lax.broadcasted_iota(jnp.int32, (tq, tq), 0)
    col = lax.broadcasted_iota(jnp.int32, (tq, tq), 1)
    causal = col <= row
    n_tiles = seq // tq

    def scores(qi):
        r0 = qi * tq
        q = q_ref[r0:r0 + tq, :]
        s_diag = jnp.where(causal, _dot_nt(q, k_ref[r0:r0 + tq, :]), NEG_INF)
        s_past = _dot_nt(q, k_ref[0:r0, :]) if qi > 0 else None
        return s_diag, s_past

    nxt = scores(0)
    for qi in range(n_tiles):
        r0 = qi * tq
        (s_diag, s_past), nxt = nxt, None
        if qi + 1 < n_tiles:
            nxt = scores(qi + 1)
        m = jnp.max(s_diag, axis=1, keepdims=True)
        if qi > 0:
            m = jnp.maximum(m, jnp.max(s_past, axis=1, keepdims=True))
        acc = jnp.dot(jnp.exp(s_diag - m).astype(BF16), v_ref[r0:r0 + tq, :],
                      preferred_element_type=F32)
        if qi > 0:
            acc = acc + jnp.dot(jnp.exp(s_past - m).astype(BF16), v_ref[0:r0, :],
                                preferred_element_type=F32)
        o_ref[r0:r0 + tq, :] = (
            acc / acc[:, DENOM_LANE:DENOM_LANE + 1]).astype(o_ref.dtype)
    _cast_slabs(cast_in, cast_out)


def _attention(q_aug, k_aug, v_aug, cast_weights, *, batch, seq, tq):
    t = q_aug.shape[0]
    steps = batch * FOX_HEADS
    spec = pl.BlockSpec((seq, LANES), lambda b, h: (b, h))
    cast_specs = _cast_specs(cast_weights, steps, lambda b, h: (b * FOX_HEADS + h, 0))
    outs = pl.pallas_call(
        functools.partial(_attn_kernel, tq=tq, n_cast=len(cast_weights)),
        grid=(batch, FOX_HEADS),
        in_specs=[spec, spec, spec, *cast_specs],
        out_specs=[spec, *cast_specs],
        out_shape=[jax.ShapeDtypeStruct((t, FOX_HEADS * LANES), BF16)]
        + [jax.ShapeDtypeStruct(w.shape, BF16) for w in cast_weights],
        compiler_params=_params("parallel", "parallel"),
        name="fox_attention",
    )(q_aug, k_aug, v_aug, *cast_weights)
    return outs[0], outs[1:]


MXU_TILE = 256


def _swiglu_partial(h, w1, w3, w2):
    a = jnp.dot(h, w1, preferred_element_type=F32)
    b = jnp.dot(h, w3, preferred_element_type=F32)
    act = a * _sigmoid(a) * b
    return jnp.dot(act.astype(BF16), w2, preferred_element_type=F32)


def _ff_splits(f):
    cut = -(-(f // MXU_TILE) // 2) * MXU_TILE
    return ((0, cut), (cut, f)) if 0 < cut < f else ((0, f),)


def _swiglu(h, w1_ref, w3_ref, w2_ref):
    y = None
    for lo, hi in _ff_splits(w1_ref.shape[1]):
        part = _swiglu_partial(h, w1_ref[:, lo:hi], w3_ref[:, lo:hi], w2_ref[lo:hi, :])
        y = part if y is None else y + part
    return y


def _even_tail_kernel(x_ref, att_ref, p_ref, halo_ref, wpool_ref, scale_ref,
                      wo_att_ref, wo_pool_ref, g_ref, w1_ref, w3_ref, w2_ref, o_ref,
                      *, tiles_per_seq):
    i = pl.program_id(0)
    tm = x_ref.shape[0]
    tile_in_seq = i % tiles_per_seq
    p = p_ref[...]
    halo = jnp.where(tile_in_seq == 0, 0.0, halo_ref[...])
    ext = jnp.concatenate([halo, p], axis=0)
    pos = tile_in_seq * tm + lax.broadcasted_iota(jnp.int32, (tm, 1), 0)
    count = (pos + 1).astype(F32)
    mixed = []
    for gi, w in enumerate(POOL_WINDOWS):
        sl = slice(gi * POOL_GROUP_DIM, (gi + 1) * POOL_GROUP_DIM)
        acc = ext[:, sl]
        sh = 1
        while sh < w:
            acc = acc + pltpu.roll(acc, sh, axis=0)
            sh *= 2
        mean = acc[POOL_HALO:, :] / jnp.minimum(count, float(w))
        pooled = (mean - p[:, sl]).astype(BF16)
        mixed.append(jnp.dot(pooled, wpool_ref[gi], preferred_element_type=F32))
    pool = (jnp.concatenate(mixed, axis=1) * scale_ref[...]).astype(BF16)
    y = jnp.dot(att_ref[...], wo_att_ref[...], preferred_element_type=F32)
    y = y + jnp.dot(pool, wo_pool_ref[...], preferred_element_type=F32)
    x1 = x_ref[...] + y
    h = _rms(x1, g_ref[...]).astype(BF16)
    o_ref[...] = x1 + _swiglu(h, w1_ref, w3_ref, w2_ref)


def _even_tail(x2d, att, p_in, w_pool, pool_scale, w_out, ffn_gain, w1, w3, w2,
               *, seq, tm):
    t, d = x2d.shape
    resident = lambda w: pl.BlockSpec(w.shape, lambda i: (0,) * w.ndim,
                                      pipeline_mode=pl.Buffered(1))
    wo_att = w_out[:FOX_WIDTH].reshape(FOX_HEADS, FOX_HEAD_DIM, d)
    wo_att = jnp.pad(wo_att, ((0, 0), (0, LANES - FOX_HEAD_DIM), (0, 0)))
    wo_att = wo_att.reshape(FOX_HEADS * LANES, d)
    wo_pool = w_out[FOX_WIDTH:]
    halo_blocks = tm // POOL_HALO
    kern = functools.partial(_even_tail_kernel, tiles_per_seq=seq // tm)
    return pl.pallas_call(
        kern,
        grid=(t // tm,),
        in_specs=[
            pl.BlockSpec((tm, d), lambda i: (i, 0)),
            pl.BlockSpec((tm, FOX_HEADS * LANES), lambda i: (i, 0)),
            pl.BlockSpec((tm, POOL_WIDTH), lambda i: (i, 0)),
            pl.BlockSpec((POOL_HALO, POOL_WIDTH),
                         lambda i: (jnp.maximum(i * halo_blocks - 1, 0), 0)),
            resident(w_pool),
            pl.BlockSpec((1, POOL_WIDTH), lambda i: (0, 0)),
            resident(wo_att), resident(wo_pool),
            pl.BlockSpec((1, d), lambda i: (0, 0)),
            resident(w1), resident(w3), resident(w2),
        ],
        out_specs=pl.BlockSpec((tm, d), lambda i: (i, 0)),
        out_shape=jax.ShapeDtypeStruct((t, d), F32),
        compiler_params=_params("parallel"),
        name="even_tail",
    )(x2d, att, p_in, p_in, w_pool, pool_scale, wo_att, wo_pool, ffn_gain, w1, w3, w2)


def _gelu_tanh(x):
    c = math.sqrt(2.0 / math.pi)
    return 0.5 * x * (1.0 + jnp.tanh(c * (x + 0.044715 * (x * x * x))))


def _s5_kernel(x_ref, g_ref, win_ref, perm_ref, permt_ref, bblk_ref, cblk_ref,
               are_ref, aim_ref, d_ref, o_ref, zero_ref, xs_ref, st_ref, u_ref, act_ref):
    nb, tc, d = x_ref.shape
    rows = nb * tc
    n_slabs = d // LANES
    sw = 2 * SSM_SLAB_STATE

    @pl.when(pl.program_id(0) == 0)
    def _():
        st_ref[...] = jnp.zeros_like(st_ref)

    x = x_ref[...].reshape(rows, d)
    h = _rms(x, g_ref[...]).astype(BF16)
    h_tb = jnp.dot(perm_ref[...], h, preferred_element_type=F32).astype(BF16)
    u_ref[...] = jnp.dot(h_tb, win_ref[...], preferred_element_type=F32)
    hs = SSM_SLAB_STATE

    def input_matmul(s):
        xs_ref[:, s * sw:(s + 1) * sw] = jnp.dot(
            u_ref[:, s * LANES:(s + 1) * LANES].astype(BF16), bblk_ref[s],
            preferred_element_type=F32)

    def output_matmul(s):
        sl = slice(s * LANES, (s + 1) * LANES)
        y = jnp.dot(xs_ref[:, s * sw:(s + 1) * sw].astype(BF16), cblk_ref[s],
                    preferred_element_type=F32) + d_ref[:, sl] * u_ref[:, sl]
        act_ref[:, sl] = _gelu_tanh(y).astype(BF16)

    input_matmul(0)
    for s in range(n_slabs):
        lo = s * sw
        if s + 1 < n_slabs:
            input_matmul(s + 1)
        a_r = are_ref[:, lo:lo + hs]
        a_i = aim_ref[:, lo:lo + hs]
        x_r = st_ref[:, lo:lo + hs]
        x_i = st_ref[:, lo + hs:lo + sw]
        for t in range(tc):
            r0 = t * SUBLANES
            new_r = a_r * x_r - a_i * x_i + xs_ref[r0:r0 + SUBLANES, lo:lo + hs]
            new_i = a_r * x_i + a_i * x_r + xs_ref[r0:r0 + SUBLANES, lo + hs:lo + sw]
            xs_ref[r0:r0 + SUBLANES, lo:lo + hs] = new_r
            xs_ref[r0:r0 + SUBLANES, lo + hs:lo + sw] = new_i
            x_r, x_i = new_r, new_i
        st_ref[:, lo:lo + hs] = x_r
        st_ref[:, lo + hs:lo + sw] = x_i
        if s >= 1:
            output_matmul(s - 1)
    output_matmul(n_slabs - 1)
    g_bt = jnp.dot(permt_ref[...], act_ref[...], preferred_element_type=F32)
    o_ref[...] = g_bt.reshape(nb, tc, d).astype(o_ref.dtype)
    zero_ref[...] = jnp.zeros_like(zero_ref)


def _s5_mixer(x3d, gain, w_in, bblk, cblk, a_re, a_im, d_skip, *, tc, zero_shape):
    nb, seq, d = x3d.shape
    steps = seq // tc
    zero_rows = zero_shape[0] // steps
    assert zero_rows * steps == zero_shape[0] and zero_rows % SEG_ALIGN == 0
    rows = nb * tc
    n_slabs = d // LANES
    sw = 2 * SSM_SLAB_STATE
    r = np.arange(rows)
    perm = np.zeros((rows, rows), np.float32)
    perm[(r % tc) * nb + r // tc, r] = 1.0
    perm_j = jnp.asarray(perm, BF16)
    permt_j = jnp.asarray(perm.T, BF16)
    kern = _s5_kernel
    full2 = lambda a: pl.BlockSpec(a.shape, lambda i: (0, 0))
    full3 = lambda a: pl.BlockSpec(a.shape, lambda i: (0, 0, 0))
    return pl.pallas_call(
        kern,
        grid=(seq // tc,),
        in_specs=[
            pl.BlockSpec((nb, tc, d), lambda i: (0, i, 0)),
            full2(gain), full2(w_in), full2(perm_j), full2(permt_j),
            full3(bblk), full3(cblk), full2(a_re), full2(a_im), full2(d_skip),
        ],
        out_specs=[pl.BlockSpec((nb, tc, d), lambda i: (0, i, 0)),
                   pl.BlockSpec((zero_rows, zero_shape[1]), lambda i: (i, 0))],
        out_shape=[jax.ShapeDtypeStruct((nb, seq, d), BF16),
                   jax.ShapeDtypeStruct(zero_shape, BF16)],
        scratch_shapes=[
            pltpu.VMEM((rows, n_slabs * sw), F32),
            pltpu.VMEM((SUBLANES, n_slabs * sw), F32),
            pltpu.VMEM((rows, d), F32),
            pltpu.VMEM((rows, d), BF16),
        ],
        compiler_params=_params("arbitrary"),
        name="s5_mixer",
    )(x3d, gain, w_in, perm_j, permt_j, bblk, cblk, a_re, a_im, d_skip)


def _s5_coefficients(a_re, a_im, log_dt, b_re, b_im, c_re, c_im):
    dt = jnp.exp(log_dt.astype(F32))[:, None]
    ar = a_re.astype(F32)
    ai = a_im.astype(F32)
    mag = jnp.exp(ar * dt)
    abar_re = mag * jnp.cos(ai * dt)
    abar_im = mag * jnp.sin(ai * dt)
    den = ar * ar + ai * ai
    nr = abar_re - 1.0
    ni = abar_im
    coef_re = (nr * ar + ni * ai) / den
    coef_im = (ni * ar - nr * ai) / den
    br = b_re.astype(F32)
    bi = b_im.astype(F32)
    bbar_re = coef_re[..., None] * br - coef_im[..., None] * bi
    bbar_im = coef_re[..., None] * bi + coef_im[..., None] * br
    n_groups = ar.shape[0]
    n_slabs = n_groups // SSM_SLAB_GROUPS
    eye = jnp.eye(SSM_SLAB_GROUPS, dtype=F32)

    def in_block(bb):
        bb = bb.reshape(n_slabs, SSM_SLAB_GROUPS, SSM_STATE, SSM_GROUP)
        blk = jnp.einsum('sgph,gk->sghkp', bb, eye)
        return blk.reshape(n_slabs, LANES, SSM_SLAB_STATE)

    def out_block(cc):
        cc = cc.reshape(n_slabs, SSM_SLAB_GROUPS, SSM_GROUP, SSM_STATE)
        blk = jnp.einsum('sghp,gk->sgpkh', cc, eye)
        return blk.reshape(n_slabs, SSM_SLAB_STATE, LANES)

    bblk = jnp.concatenate([in_block(bbar_re), in_block(bbar_im)], axis=2)
    cblk = jnp.concatenate([out_block(c_re.astype(F32)),
                            -out_block(c_im.astype(F32))], axis=1)

    def lanes(a):
        a = a.reshape(n_slabs, 1, SSM_SLAB_STATE)
        a = jnp.concatenate([a, a], axis=2).reshape(1, -1)
        return jnp.broadcast_to(a, (SUBLANES, a.shape[1]))

    return (bblk.astype(BF16), cblk.astype(BF16), lanes(abar_re), lanes(abar_im))


MOE_CHUNK = 512
MOE_FIRST = 192
MOE_REST = MOE_CHUNK - MOE_FIRST
MOE_PIECES = ((0, MOE_FIRST), (MOE_FIRST, MOE_REST))
MOE_TILE = 512


def _glu_router_kernel(x_ref, act_ref, wa_ref, wb_ref, g_ref, w_ref, b_ref,
                       x3_ref, h_ref, pos_ref, post_ref, cnt_ref):
    tm, d = x_ref.shape
    half = tm // 2

    def glu(r0):
        act = act_ref[r0:r0 + half, :]
        a = jnp.dot(act, wa_ref[...], preferred_element_type=F32)
        b = jnp.dot(act, wb_ref[...], preferred_element_type=F32)
        x3 = x_ref[r0:r0 + half, :] + a * _sigmoid(b)
        x3_ref[r0:r0 + half, :] = x3
        return x3

    def route(r0, x3):
        h = _rms(x3, g_ref[...])
        h_hi = h.astype(BF16)
        h_ref[r0:r0 + half, 0:d] = h_hi
        h_lo = (h - h_hi.astype(F32)).astype(BF16)
        p_hi = jnp.dot(h_hi, w_ref[...], preferred_element_type=F32)
        p_lo = jnp.dot(h_lo, w_ref[...], preferred_element_type=F32)
        logits = p_hi + pltpu.roll(p_hi, LANES - N_EXPERTS, axis=1) + p_lo + b_ref[...]
        lane = lax.broadcasted_iota(jnp.int32, logits.shape, 1)
        logits = jnp.where(lane < N_EXPERTS, logits, -jnp.inf)
        m1 = jnp.max(logits, axis=1, keepdims=True)
        i1 = jnp.min(jnp.where(logits == m1, lane, LANES), axis=1, keepdims=True)
        rest = jnp.where(lane == i1, -jnp.inf, logits)
        m2 = jnp.max(rest, axis=1, keepdims=True)
        i2 = jnp.min(jnp.where(rest == m2, lane, LANES), axis=1, keepdims=True)
        e2 = jnp.exp(m2 - m1)
        g1 = 1.0 / (1.0 + e2)
        g2 = e2 / (1.0 + e2)
        gate = jnp.where(lane == i1, g1, 0.0) + jnp.where(lane == i2, g2, 0.0)
        g_hi = gate.astype(BF16).astype(F32)
        g_r = gate - g_hi
        g_mid = g_r.astype(BF16).astype(F32)
        packed = jnp.where(lane < N_EXPERTS, g_hi,
                           jnp.where(lane < 2 * N_EXPERTS,
                                     pltpu.roll(g_mid, N_EXPERTS, axis=1),
                                     pltpu.roll(g_r - g_mid, 2 * N_EXPERTS, axis=1)))
        h_ref[r0:r0 + half, d:d + LANES] = packed.astype(BF16)
        return jnp.where((lane == i1) | (lane == i2), 1.0, 0.0)

    x3_a = glu(0)
    x3_b = glu(half)
    member = jnp.concatenate([route(0, x3_a), route(half, x3_b)], axis=0)
    row = lax.broadcasted_iota(jnp.int32, member.shape, 0)
    c = member
    sh = 1
    while sh < tm:
        c = c + jnp.where(row >= sh, pltpu.roll(c, sh, axis=0), 0.0)
        sh *= 2
    pos = jnp.where(member > 0.0, c - member, -1.0)
    pos_ref[...] = pos
    post_ref[0] = pos.T[0:SUBLANES, :]
    cnt_ref[0] = jnp.broadcast_to(c[tm - 1:tm, :], (SUBLANES, LANES))


def _glu_router(x2d, act2d, wa, wb, gain, w_pad, b_pad):
    t, d = x2d.shape
    tm = MOE_CHUNK
    n_chunks = t // tm
    return pl.pallas_call(
        _glu_router_kernel,
        grid=(n_chunks,),
        in_specs=[
            pl.BlockSpec((tm, d), lambda i: (i, 0)),
            pl.BlockSpec((tm, d), lambda i: (i, 0)),
            pl.BlockSpec(wa.shape, lambda i: (0, 0)),
            pl.BlockSpec(wb.shape, lambda i: (0, 0)),
            pl.BlockSpec((1, d), lambda i: (0, 0)),
            pl.BlockSpec((d, LANES), lambda i: (0, 0)),
            pl.BlockSpec((1, LANES), lambda i: (0, 0)),
        ],
        out_specs=[
            pl.BlockSpec((tm, d), lambda i: (i, 0)),
            pl.BlockSpec((tm, d + LANES), lambda i: (i, 0)),
            pl.BlockSpec((tm, LANES), lambda i: (i, 0)),
            pl.BlockSpec((1, SUBLANES, tm), lambda i: (i, 0, 0)),
            pl.BlockSpec((1, SUBLANES, LANES), lambda i: (i, 0, 0)),
        ],
        out_shape=[
            jax.ShapeDtypeStruct((t, d), F32),
            jax.ShapeDtypeStruct((t, d + LANES), BF16),
            jax.ShapeDtypeStruct((t, LANES), F32),
            jax.ShapeDtypeStruct((n_chunks, SUBLANES, tm), F32),
            jax.ShapeDtypeStruct((n_chunks, SUBLANES, LANES), F32),
        ],
        compiler_params=_params("parallel"),
        name="glu_router",
    )(x2d, act2d, wa, wb, gain, w_pad, b_pad)


MOE_SPARE = max(MOE_FIRST, MOE_REST)


def _moe_tiles(n_tokens, row_multiple):
    n_chunks = n_tokens // MOE_CHUNK
    max_rows = (TOP_K * n_tokens + (SEG_ALIGN - 1) * N_EXPERTS * n_chunks
                + N_EXPERTS * (MOE_SPARE + MOE_TILE))
    unit = math.lcm(MOE_TILE, row_multiple)
    return -(-max_rows // unit) * (unit // MOE_TILE)


def _moe_layout(cnt, n_tiles):
    seg = (cnt + SEG_ALIGN - 1) // SEG_ALIGN * SEG_ALIGN
    used = jnp.sum(seg, axis=0)
    padded = (used + MOE_SPARE + MOE_TILE - 1) // MOE_TILE * MOE_TILE
    ends = jnp.cumsum(padded)
    start = ends - padded
    off = start[None, :] + jnp.cumsum(seg, axis=0) - seg
    tile_start = jnp.arange(n_tiles, dtype=jnp.int32) * MOE_TILE
    tile_e = jnp.sum((tile_start[:, None] >= ends[None, :]).astype(jnp.int32), axis=1)
    tile_e = jnp.minimum(tile_e, N_EXPERTS - 1)
    tile_rows = jnp.clip(used[tile_e] - (tile_start - start[tile_e]), 0, MOE_TILE)
    return (off.reshape(-1).astype(jnp.int32), cnt.reshape(-1).astype(jnp.int32),
            tile_e.astype(jnp.int32), tile_rows.astype(jnp.int32))


def _segment_copies(hbm_ref, buf_ref, sem_ref, off_ref, chunk, to_hbm, slot=()):
    copies = []
    for e in range(N_EXPERTS):
        off = pl.multiple_of(off_ref[chunk * N_EXPERTS + e], SEG_ALIGN)
        for piece, (first, rows) in enumerate(MOE_PIECES):
            hbm = hbm_ref.at[pl.ds(off + first, rows)]
            buf = buf_ref.at[(*slot, e, pl.ds(first, rows))]
            src, dst = (buf, hbm) if to_hbm else (hbm, buf)
            copies.append(pltpu.make_async_copy(src, dst, sem_ref.at[(*slot, e, piece)]))
    return copies


def _onehot(index, target):
    return jnp.where(index == target, 1.0, 0.0).astype(BF16)


def _gather_kernel(off_ref, cnt_ref, h_ref, post_ref, zeros_hbm, o_hbm, stage_ref,
                   sem_ref):
    del zeros_hbm
    c = pl.program_id(0)
    h = h_ref[...]
    copies = _segment_copies(o_hbm, stage_ref, sem_ref, off_ref, c, to_hbm=True)

    def move_piece(e, piece):
        first, rows = MOE_PIECES[piece]
        rank = first + lax.broadcasted_iota(jnp.int32, (rows, MOE_CHUNK), 0)
        onehot = _onehot(rank.astype(F32), post_ref[0, e:e + 1, :])
        stage_ref[e, first:first + rows] = jnp.dot(
            onehot, h, preferred_element_type=F32).astype(BF16)
        copies[2 * e + piece].start()

    for e in range(N_EXPERTS):
        move_piece(e, 0)
    for e in range(N_EXPERTS):
        @pl.when(cnt_ref[c * N_EXPERTS + e] > MOE_FIRST)
        def _(e=e):
            move_piece(e, 1)

    for e in range(N_EXPERTS):
        copies[2 * e].wait()

        @pl.when(cnt_ref[c * N_EXPERTS + e] > MOE_FIRST)
        def _(e=e):
            copies[2 * e + 1].wait()


def _gather(h2d, post, off, cnt, zero_buf):
    t, d = h2d.shape
    n_chunks = t // MOE_CHUNK
    grid_spec = pltpu.PrefetchScalarGridSpec(
        num_scalar_prefetch=2,
        grid=(n_chunks,),
        in_specs=[
            pl.BlockSpec((MOE_CHUNK, d), lambda i, off, cnt: (i, 0)),
            pl.BlockSpec((1, SUBLANES, MOE_CHUNK), lambda i, off, cnt: (i, 0, 0)),
            pl.BlockSpec(memory_space=pl.ANY),
        ],
        out_specs=pl.BlockSpec(memory_space=pl.ANY),
        scratch_shapes=[
            pltpu.VMEM((N_EXPERTS, MOE_CHUNK, d), BF16),
            pltpu.SemaphoreType.DMA((N_EXPERTS, 2)),
        ],
    )
    return pl.pallas_call(
        _gather_kernel,
        grid_spec=grid_spec,
        out_shape=jax.ShapeDtypeStruct(zero_buf.shape, BF16),
        input_output_aliases={4: 0},
        compiler_params=_params("arbitrary"),
        name="moe_gather",
    )(off, cnt, h2d, post, zero_buf)


def _expert_kernel(te_ref, tr_ref, x_ref, w1_ref, w3_ref, w2_ref, o_ref):
    i = pl.program_id(0)
    rows = tr_ref[i]
    d = o_ref.shape[1]

    def run(n):
        row = lax.broadcasted_iota(jnp.int32, (n, x_ref.shape[1]), 0)
        x = jnp.where(row < rows, x_ref[0:n, :], jnp.zeros((), x_ref.dtype))
        lane = lax.broadcasted_iota(jnp.int32, (n, LANES), 1)
        mine = (lane < 3 * N_EXPERTS) & (jnp.bitwise_and(lane, N_EXPERTS - 1) == te_ref[i])
        gate = jnp.sum(jnp.where(mine, x[:, d:d + LANES].astype(F32), 0.0),
                       axis=1, keepdims=True)
        y = _swiglu(x[:, 0:d], w1_ref, w3_ref, w2_ref)
        o_ref[0:n, :] = (gate * y).astype(o_ref.dtype)

    half = MOE_TILE // 2

    @pl.when(rows > half)
    def _():
        run(MOE_TILE)

    @pl.when((rows > 0) & (rows <= half))
    def _():
        run(half)
        o_ref[half:MOE_TILE, :] = jnp.zeros((MOE_TILE - half, d), o_ref.dtype)

    @pl.when(rows == 0)
    def _():
        o_ref[...] = jnp.zeros_like(o_ref)


def _experts(x_sorted, w1, w3, w2, tile_e, tile_rows):
    assert N_EXPERTS & (N_EXPERTS - 1) == 0
    n_rows = x_sorted.shape[0]
    d, f = w1.shape[1:]
    grid_spec = pltpu.PrefetchScalarGridSpec(
        num_scalar_prefetch=2,
        grid=(n_rows // MOE_TILE,),
        in_specs=[
            pl.BlockSpec((MOE_TILE, d + LANES), lambda i, te, tr: (i, 0)),
            pl.BlockSpec((None, d, f), lambda i, te, tr: (te[i], 0, 0)),
            pl.BlockSpec((None, d, f), lambda i, te, tr: (te[i], 0, 0)),
            pl.BlockSpec((None, f, d), lambda i, te, tr: (te[i], 0, 0)),
        ],
        out_specs=pl.BlockSpec((MOE_TILE, d), lambda i, te, tr: (i, 0)),
    )
    return pl.pallas_call(
        _expert_kernel,
        grid_spec=grid_spec,
        out_shape=jax.ShapeDtypeStruct((n_rows, d), BF16),
        compiler_params=_params("arbitrary"),
        name="moe_experts",
    )(tile_e, tile_rows, x_sorted, w1, w3, w2)


def _combine_kernel(off_ref, cnt_ref, x_ref, pos_ref, fg_ref, y_hbm,
                    o_ref, ybuf_ref, acc_ref, sem_ref):
    c = pl.program_id(0)
    n_chunks = pl.num_programs(0)
    slot = c % 2

    def fetch(chunk, sl, start):
        copies = _segment_copies(y_hbm, ybuf_ref, sem_ref, off_ref, chunk,
                                 to_hbm=False, slot=(sl,))
        for e in range(N_EXPERTS):
            first, second = copies[2 * e], copies[2 * e + 1]
            if start:
                first.start()
            else:
                first.wait()

            @pl.when(cnt_ref[chunk * N_EXPERTS + e] > MOE_FIRST)
            def _(second=second):
                if start:
                    second.start()
                else:
                    second.wait()

    @pl.when(c == 0)
    def _():
        fetch(c, slot, True)

    @pl.when(c + 1 < n_chunks)
    def _():
        fetch(c + 1, 1 - slot, True)

    fetch(c, slot, False)

    def piece_rows(e, piece):
        first, rows = MOE_PIECES[piece]
        rank = first + lax.broadcasted_iota(jnp.int32, (MOE_CHUNK, rows), 1)
        onehot = _onehot(rank.astype(F32), pos_ref[:, e:e + 1])
        return jnp.dot(onehot, ybuf_ref[slot, e, first:first + rows],
                       preferred_element_type=F32)

    acc = x_ref[...]
    for e in range(N_EXPERTS):
        acc = acc + piece_rows(e, 0)
    acc_ref[...] = acc
    for e in range(N_EXPERTS):
        @pl.when(cnt_ref[c * N_EXPERTS + e] > MOE_FIRST)
        def _(e=e):
            acc_ref[...] += piece_rows(e, 1)

    o_ref[...] = _rms(acc_ref[...], fg_ref[...])


def _combine(x2d, pos, final_gain, y_sorted, off, cnt):
    t, d = x2d.shape
    n_chunks = t // MOE_CHUNK
    grid_spec = pltpu.PrefetchScalarGridSpec(
        num_scalar_prefetch=2,
        grid=(n_chunks,),
        in_specs=[
            pl.BlockSpec((MOE_CHUNK, d), lambda i, off, cnt: (i, 0)),
            pl.BlockSpec((MOE_CHUNK, LANES), lambda i, off, cnt: (i, 0)),
            pl.BlockSpec((1, d), lambda i, off, cnt: (0, 0)),
            pl.BlockSpec(memory_space=pl.ANY),
        ],
        out_specs=pl.BlockSpec((MOE_CHUNK, d), lambda i, off, cnt: (i, 0)),
        scratch_shapes=[
            pltpu.VMEM((2, N_EXPERTS, MOE_CHUNK, d), BF16),
            pltpu.VMEM((MOE_CHUNK, d), F32),
            pltpu.SemaphoreType.DMA((2, N_EXPERTS, 2)),
        ],
    )
    return pl.pallas_call(
        _combine_kernel,
        grid_spec=grid_spec,
        out_shape=jax.ShapeDtypeStruct((t, d), F32),
        compiler_params=_params("arbitrary"),
        name="moe_combine",
    )(off, cnt, x2d, pos, final_gain, y_sorted)


def _glu_moe(x2d, act2d, wa, wb, gain, router_w, router_b, w1, w3, w2, final_gain,
             zero_buf):
    t, d = x2d.shape
    rw_hi = router_w.astype(BF16)
    rw_lo = (router_w.astype(F32) - rw_hi.astype(F32)).astype(BF16)
    rw = jnp.pad(jnp.concatenate([rw_hi, rw_lo], axis=1),
                 ((0, 0), (0, LANES - 2 * N_EXPERTS)))
    rb = jnp.pad(router_b.astype(F32), (0, LANES - N_EXPERTS)).reshape(1, LANES)
    x3, h, pos, post, cnt = _glu_router(x2d, act2d, wa, wb, gain, rw, rb)
    cnt = cnt[:, 0, :N_EXPERTS].astype(jnp.int32)
    off, cnt_flat, tile_e, tile_rows = _moe_layout(cnt, zero_buf.shape[0] // MOE_TILE)
    h_sorted = _gather(h, post, off, cnt_flat, zero_buf)
    y_sorted = _experts(h_sorted, w1, w3, w2, tile_e, tile_rows)
    return _combine(x3, pos, final_gain, y_sorted, off, cnt_flat)


def _even_weights(w_in, b_forget):
    fw = FOX_WIDTH
    wq = w_in[:, :fw] * FOX_HEAD_DIM ** -0.5
    wf = jnp.pad(w_in[:, 3 * fw:3 * fw + FOX_HEADS],
                 ((0, 0), (0, LANES - FOX_HEADS)))
    wp = w_in[:, 3 * fw + FOX_HEADS:]
    w_all = jnp.concatenate([wq, w_in[:, fw:3 * fw], wp, wf], axis=1).astype(BF16)
    bias = jnp.pad(b_forget.astype(F32), (0, LANES - FOX_HEADS)).reshape(1, LANES)
    qw = FOX_HEADS * LANES
    place = np.zeros((LANES, 2 * qw), np.float32)
    for hh in range(FOX_HEADS):
        for piece in range(3):
            place[piece * FOX_HEADS + hh, hh * LANES + BIAS_LANE + piece] = 1.0
            place[piece * FOX_HEADS + hh, qw + hh * LANES + ONES_LANE + piece] = -1.0
    return w_all, bias, jnp.asarray(place, BF16)


def kernel(x, even_mix_norm, even_w_in, even_b_forget, even_w_pool, even_pool_scale, even_w_out, even_ffn_norm, even_ffn_w1, even_ffn_w3, even_ffn_w2, odd_mix_norm, odd_w_in, ssm_a_re, ssm_a_im, ssm_log_dt, ssm_b_re, ssm_b_im, ssm_c_re, ssm_c_im, ssm_d, odd_w_glu_a, odd_w_glu_b, odd_moe_norm, router_w, router_b, expert_w1, expert_w3, expert_w2, final_norm):
    b, s, d = x.shape
    t = b * s
    assert b == SUBLANES, "the S5 recurrence keeps one batch row per sublane"
    x2d = x.reshape(t, d)
    row = lambda v: v.reshape(1, -1).astype(F32)

    w_all, bias, place = _even_weights(even_w_in[0], even_b_forget[0])
    (q_aug, k_aug, v, p_in), casted = _even_inproj(
        x2d, row(even_mix_norm[0]), w_all, bias, place,
        [even_ffn_w1[0], even_ffn_w3[0], even_w_out[0],
         even_w_pool[0].reshape(-1, POOL_GROUP_DIM),
         odd_w_in[0], odd_w_glu_a[0], odd_w_glu_b[0]],
        seq=s, tm=512)
    ffn_w1, ffn_w3, w_out, w_pool, s5_w_in, glu_a, glu_b = casted
    n_e, _, f = expert_w1[0].shape
    att, (ew1, ew3, ew2) = _attention(
        q_aug, k_aug, v,
        [expert_w1[0].reshape(n_e * d, f), expert_w3[0].reshape(n_e * d, f),
         expert_w2[0].reshape(n_e * f, d)],
        batch=b, seq=s, tq=256)
    x2 = _even_tail(x2d, att, p_in, w_pool.reshape(even_w_pool[0].shape),
                    row(even_pool_scale[0]), w_out, row(even_ffn_norm[0]),
                    ffn_w1, ffn_w3, even_ffn_w2[0].astype(BF16), seq=s, tm=512)

    bblk, cblk, a_re, a_im = _s5_coefficients(
        ssm_a_re[0], ssm_a_im[0], ssm_log_dt[0], ssm_b_re[0], ssm_b_im[0],
        ssm_c_re[0], ssm_c_im[0])
    moe_rows = _moe_tiles(t, (s // S5_STEP) * SEG_ALIGN) * MOE_TILE
    g, zero_buf = _s5_mixer(x2.reshape(b, s, d), row(odd_mix_norm[0]),
                            s5_w_in, bblk, cblk, a_re, a_im,
                            row(ssm_d[0]), tc=S5_STEP,
                            zero_shape=(moe_rows, d + LANES))
    out = _glu_moe(x2, g.reshape(t, d), glu_a, glu_b, row(odd_moe_norm[0]),
                   router_w[0], router_b[0], ew1.reshape(n_e, d, f),
                   ew3.reshape(n_e, d, f), ew2.reshape(n_e, f, d),
                   row(final_norm), zero_buf)
    return out.reshape(b, s, d)
```

```python
import functools
import math

import numpy as np
import jax
import jax.numpy as jnp
from jax import lax
from jax.experimental import pallas as pl
from jax.experimental.pallas import tpu as pltpu

F32 = jnp.float32
BF16 = jnp.bfloat16

EPS = 1e-6
NEG_INF = -1e30
LANES = 128
SUBLANES = 8
SEG_ALIGN = 16
VMEM_LIMIT = 56 * 1024 * 1024

FOX_HEADS = 8
FOX_HEAD_DIM = 64
FOX_WIDTH = FOX_HEADS * FOX_HEAD_DIM
POOL_WINDOWS = (2, 4, 8, 16)
POOL_GROUP_DIM = 128
POOL_WIDTH = len(POOL_WINDOWS) * POOL_GROUP_DIM
POOL_HALO = 16
SSM_GROUP = 16
SSM_STATE = 64
SSM_SLAB_GROUPS = LANES // SSM_GROUP
SSM_SLAB_STATE = SSM_SLAB_GROUPS * SSM_STATE
S5_STEP = 32
N_EXPERTS = 8
TOP_K = 2

BIAS_LANE = FOX_HEAD_DIM
ONES_LANE = FOX_HEAD_DIM + 3
DENOM_LANE = FOX_HEAD_DIM


def _params(*sem):
    return pltpu.CompilerParams(dimension_semantics=sem,
                                vmem_limit_bytes=VMEM_LIMIT)


def _rms(x, g):
    ms = jnp.mean(x * x, axis=-1, keepdims=True)
    return x * lax.rsqrt(ms + EPS) * g


def _sigmoid(x):
    return 1.0 / (1.0 + jnp.exp(-x))


def _lane_range_ones(lo, hi):
    lane = lax.broadcasted_iota(jnp.int32, (1, LANES), 1)
    return jnp.where((lane >= lo) & (lane < hi), 1.0, 0.0).astype(F32)


def _cast_specs(weights, steps, index_map):
    specs = []
    for w in weights:
        rows = w.shape[0] // steps
        assert rows * steps == w.shape[0] and rows % SEG_ALIGN == 0
        specs.append(pl.BlockSpec((rows, w.shape[1]), index_map))
    return specs


def _cast_slabs(src_refs, dst_refs):
    for src_ref, dst_ref in zip(src_refs, dst_refs):
        dst_ref[...] = src_ref[...].astype(dst_ref.dtype)


def _even_inproj_kernel(x_ref, g_ref, w_ref, bias_ref, place_ref, *rest,
                        tiles_per_seq, n_cast):
    cast_in, rest = rest[:n_cast], rest[n_cast:]
    q_ref, k_ref, v_ref, p_ref = rest[:4]
    cast_out, carry_ref = rest[4:4 + n_cast], rest[4 + n_cast]
    _cast_slabs(cast_in, cast_out)
    i = pl.program_id(0)
    tm = x_ref.shape[0]
    fw = FOX_WIDTH
    qw = FOX_HEADS * LANES
    h = _rms(x_ref[...], g_ref[...]).astype(BF16)
    z = jnp.dot(h, w_ref[...], preferred_element_type=F32)
    p_ref[...] = z[:, 3 * fw + FOX_HEADS:3 * fw + FOX_HEADS + POOL_WIDTH]
    fg = z[:, 3 * fw:3 * fw + LANES] + bias_ref[...]
    lf = jnp.minimum(fg, 0.0) - jnp.log1p(jnp.exp(-jnp.abs(fg)))
    row = lax.broadcasted_iota(jnp.int32, lf.shape, 0)
    c = lf
    sh = 1
    while sh < tm:
        c = c + jnp.where(row >= sh, pltpu.roll(c, sh, axis=0), 0.0)
        sh *= 2

    @pl.when(i % tiles_per_seq == 0)
    def _():
        carry_ref[...] = jnp.zeros_like(carry_ref)

    c = c + carry_ref[0:1, :]
    carry_ref[...] = jnp.broadcast_to(c[tm - 1:tm, :], carry_ref.shape)
    hi = c.astype(BF16).astype(F32)
    r1 = c - hi
    mid = r1.astype(BF16).astype(F32)
    lo = r1 - mid
    lane = lax.broadcasted_iota(jnp.int32, (tm, LANES), 1)
    packed = jnp.where(lane < FOX_HEADS, hi,
                       jnp.where(lane < 2 * FOX_HEADS,
                                 pltpu.roll(mid, FOX_HEADS, axis=1),
                                 pltpu.roll(lo, 2 * FOX_HEADS, axis=1)))
    placed = jnp.dot(packed.astype(BF16), place_ref[...],
                     preferred_element_type=F32)
    low = lane < FOX_HEAD_DIM
    ones_q = _lane_range_ones(ONES_LANE, ONES_LANE + 3)
    ones_k = _lane_range_ones(BIAS_LANE, BIAS_LANE + 3)
    ones_v = _lane_range_ones(DENOM_LANE, DENOM_LANE + 1)

    def head_lanes(base, hh):
        pair = z[:, base + (hh // 2) * LANES:base + (hh // 2 + 1) * LANES]
        return pltpu.roll(pair, FOX_HEAD_DIM, axis=1) if hh % 2 else pair

    q_scale = FOX_HEAD_DIM ** -0.5
    for hh in range(FOX_HEADS):
        sl = slice(hh * LANES, (hh + 1) * LANES)
        q_ref[:, sl] = jnp.where(low, head_lanes(0, hh) * q_scale,
                                 placed[:, sl] + ones_q).astype(BF16)
        k_ref[:, sl] = jnp.where(
            low, head_lanes(fw, hh),
            placed[:, qw + hh * LANES:qw + (hh + 1) * LANES] + ones_k).astype(BF16)
        v_ref[:, sl] = jnp.where(low, head_lanes(2 * fw, hh), ones_v).astype(BF16)


def _even_inproj(x2d, gain, w_all, bias, place, cast_weights, *, seq, tm):
    t, d = x2d.shape
    n = w_all.shape[1]
    qw = FOX_HEADS * LANES
    cast_specs = _cast_specs(cast_weights, t // tm, lambda i: (i, 0))
    kern = functools.partial(_even_inproj_kernel, tiles_per_seq=seq // tm,
                             n_cast=len(cast_weights))
    outs = pl.pallas_call(
        kern,
        grid=(t // tm,),
        in_specs=[
            pl.BlockSpec((tm, d), lambda i: (i, 0)),
            pl.BlockSpec((1, d), lambda i: (0, 0)),
            pl.BlockSpec((d, n), lambda i: (0, 0)),
            pl.BlockSpec((1, LANES), lambda i: (0, 0)),
            pl.BlockSpec(place.shape, lambda i: (0, 0)),
            *cast_specs,
        ],
        out_specs=[
            pl.BlockSpec((tm, qw), lambda i: (i, 0)),
            pl.BlockSpec((tm, qw), lambda i: (i, 0)),
            pl.BlockSpec((tm, qw), lambda i: (i, 0)),
            pl.BlockSpec((tm, POOL_WIDTH), lambda i: (i, 0)),
            *cast_specs,
        ],
        out_shape=[
            jax.ShapeDtypeStruct((t, qw), BF16),
            jax.ShapeDtypeStruct((t, qw), BF16),
            jax.ShapeDtypeStruct((t, qw), BF16),
            jax.ShapeDtypeStruct((t, POOL_WIDTH), F32),
        ] + [jax.ShapeDtypeStruct(w.shape, BF16) for w in cast_weights],
        scratch_shapes=[pltpu.VMEM((SUBLANES, LANES), F32)],
        compiler_params=_params("arbitrary"),
        name="even_inproj",
    )(x2d, gain, w_all, bias, place, *cast_weights)
    return outs[:4], outs[4:]


def _dot_nt(a, b):
    return lax.dot_general(a, b, (((1,), (1,)), ((), ())),
                           preferred_element_type=F32)


def _attn_kernel(q_ref, k_ref, v_ref, *rest, tq, n_cast):
    cast_in, o_ref, cast_out = rest[:n_cast], rest[n_cast], rest[n_cast + 1:]
    seq = q_ref.shape[0]
    row = lax.broadcasted_iota(jnp.int32, (tq, tq), 0)
    col = lax.broadcasted_iota(jnp.int32, (tq, tq), 1)
    causal = col <= row
    n_tiles = seq // tq

    def scores(qi):
        r0 = qi * tq
        q = q_ref[r0:r0 + tq, :]
        s_diag = jnp.where(causal, _dot_nt(q, k_ref[r0:r0 + tq, :]), NEG_INF)
        s_past = _dot_nt(q, k_ref[0:r0, :]) if qi > 0 else None
        return s_diag, s_past

    ahead = 2
    queue = [scores(qi) for qi in range(min(ahead, n_tiles))]
    for qi in range(n_tiles):
        r0 = qi * tq
        s_diag, s_past = queue.pop(0)
        if qi + ahead < n_tiles:
            queue.append(scores(qi + ahead))
        m = jnp.max(s_diag, axis=1, keepdims=True)
        if qi > 0:
            m = jnp.maximum(m, jnp.max(s_past, axis=1, keepdims=True))
        acc = jnp.dot(jnp.exp(s_diag - m).astype(BF16), v_ref[r0:r0 + tq, :],
                      preferred_element_type=F32)
        if qi > 0:
            acc = acc + jnp.dot(jnp.exp(s_past - m).astype(BF16), v_ref[0:r0, :],
                                preferred_element_type=F32)
        o_ref[r0:r0 + tq, :] = (
            acc / acc[:, DENOM_LANE:DENOM_LANE + 1]).astype(o_ref.dtype)
    _cast_slabs(cast_in, cast_out)


def _attention(q_aug, k_aug, v_aug, cast_weights, *, batch, seq, tq):
    t = q_aug.shape[0]
    steps = batch * FOX_HEADS
    spec = pl.BlockSpec((seq, LANES), lambda b, h: (b, h))
    cast_specs = _cast_specs(cast_weights, steps, lambda b, h: (b * FOX_HEADS + h, 0))
    outs = pl.pallas_call(
        functools.partial(_attn_kernel, tq=tq, n_cast=len(cast_weights)),
        grid=(batch, FOX_HEADS),
        in_specs=[spec, spec, spec, *cast_specs],
        out_specs=[spec, *cast_specs],
        out_shape=[jax.ShapeDtypeStruct((t, FOX_HEADS * LANES), BF16)]
        + [jax.ShapeDtypeStruct(w.shape, BF16) for w in cast_weights],
        compiler_params=_params("parallel", "parallel"),
        name="fox_attention",
    )(q_aug, k_aug, v_aug, *cast_weights)
    return outs[0], outs[1:]


MXU_TILE = 256


def _swiglu_partial(h, w1, w3, w2):
    a = jnp.dot(h, w1, preferred_element_type=F32)
    b = jnp.dot(h, w3, preferred_element_type=F32)
    act = a * _sigmoid(a) * b
    return jnp.dot(act.astype(BF16), w2, preferred_element_type=F32)


def _ff_splits(f):
    cut = -(-(f // MXU_TILE) // 2) * MXU_TILE
    return ((0, cut), (cut, f)) if 0 < cut < f else ((0, f),)


def _swiglu(h, w1_ref, w3_ref, w2_ref):
    y = None
    for lo, hi in _ff_splits(w1_ref.shape[1]):
        part = _swiglu_partial(h, w1_ref[:, lo:hi], w3_ref[:, lo:hi], w2_ref[lo:hi, :])
        y = part if y is None else y + part
    return y


def _even_tail_kernel(x_ref, att_ref, p_ref, halo_ref, wpool_ref, scale_ref,
                      wo_att_ref, wo_pool_ref, g_ref, w1_ref, w3_ref, w2_ref, o_ref,
                      *, tiles_per_seq):
    i = pl.program_id(0)
    tm = x_ref.shape[0]
    tile_in_seq = i % tiles_per_seq
    p = p_ref[...]
    halo = jnp.where(tile_in_seq == 0, 0.0, halo_ref[...])
    ext = jnp.concatenate([halo, p], axis=0)
    pos = tile_in_seq * tm + lax.broadcasted_iota(jnp.int32, (tm, 1), 0)
    count = (pos + 1).astype(F32)
    mixed = []
    for gi, w in enumerate(POOL_WINDOWS):
        sl = slice(gi * POOL_GROUP_DIM, (gi + 1) * POOL_GROUP_DIM)
        acc = ext[:, sl]
        sh = 1
        while sh < w:
            acc = acc + pltpu.roll(acc, sh, axis=0)
            sh *= 2
        mean = acc[POOL_HALO:, :] / jnp.minimum(count, float(w))
        pooled = (mean - p[:, sl]).astype(BF16)
        mixed.append(jnp.dot(pooled, wpool_ref[gi], preferred_element_type=F32))
    pool = (jnp.concatenate(mixed, axis=1) * scale_ref[...]).astype(BF16)
    y = jnp.dot(att_ref[...], wo_att_ref[...], preferred_element_type=F32)
    y = y + jnp.dot(pool, wo_pool_ref[...], preferred_element_type=F32)
    x1 = x_ref[...] + y
    h = _rms(x1, g_ref[...]).astype(BF16)
    o_ref[...] = x1 + _swiglu(h, w1_ref, w3_ref, w2_ref)


def _even_tail(x2d, att, p_in, w_pool, pool_scale, w_out, ffn_gain, w1, w3, w2,
               *, seq, tm):
    t, d = x2d.shape
    resident = lambda w: pl.BlockSpec(w.shape, lambda i: (0,) * w.ndim,
                                      pipeline_mode=pl.Buffered(1))
    wo_att = w_out[:FOX_WIDTH].reshape(FOX_HEADS, FOX_HEAD_DIM, d)
    wo_att = jnp.pad(wo_att, ((0, 0), (0, LANES - FOX_HEAD_DIM), (0, 0)))
    wo_att = wo_att.reshape(FOX_HEADS * LANES, d)
    wo_pool = w_out[FOX_WIDTH:]
    halo_blocks = tm // POOL_HALO
    kern = functools.partial(_even_tail_kernel, tiles_per_seq=seq // tm)
    return pl.pallas_call(
        kern,
        grid=(t // tm,),
        in_specs=[
            pl.BlockSpec((tm, d), lambda i: (i, 0)),
            pl.BlockSpec((tm, FOX_HEADS * LANES), lambda i: (i, 0)),
            pl.BlockSpec((tm, POOL_WIDTH), lambda i: (i, 0)),
            pl.BlockSpec((POOL_HALO, POOL_WIDTH),
                         lambda i: (jnp.maximum(i * halo_blocks - 1, 0), 0)),
            resident(w_pool),
            pl.BlockSpec((1, POOL_WIDTH), lambda i: (0, 0)),
            resident(wo_att), resident(wo_pool),
            pl.BlockSpec((1, d), lambda i: (0, 0)),
            resident(w1), resident(w3), resident(w2),
        ],
        out_specs=pl.BlockSpec((tm, d), lambda i: (i, 0)),
        out_shape=jax.ShapeDtypeStruct((t, d), F32),
        compiler_params=_params("parallel"),
        name="even_tail",
    )(x2d, att, p_in, p_in, w_pool, pool_scale, wo_att, wo_pool, ffn_gain, w1, w3, w2)


def _gelu_tanh(x):
    c = math.sqrt(2.0 / math.pi)
    return 0.5 * x * (1.0 + jnp.tanh(c * (x + 0.044715 * (x * x * x))))


def _s5_kernel(x_ref, g_ref, win_ref, perm_ref, permt_ref, bblk_ref, cblk_ref,
               are_ref, aim_ref, d_ref, o_ref, zero_ref, xs_ref, st_ref, u_ref, act_ref):
    nb, tc, d = x_ref.shape
    rows = nb * tc
    n_slabs = d // LANES
    sw = 2 * SSM_SLAB_STATE

    @pl.when(pl.program_id(0) == 0)
    def _():
        st_ref[...] = jnp.zeros_like(st_ref)

    x = x_ref[...].reshape(rows, d)
    h = _rms(x, g_ref[...]).astype(BF16)
    h_tb = jnp.dot(perm_ref[...], h, preferred_element_type=F32).astype(BF16)
    u_ref[...] = jnp.dot(h_tb, win_ref[...], preferred_element_type=F32)
    hs = SSM_SLAB_STATE

    def input_matmul(s):
        xs_ref[:, s * sw:(s + 1) * sw] = jnp.dot(
            u_ref[:, s * LANES:(s + 1) * LANES].astype(BF16), bblk_ref[s],
            preferred_element_type=F32)

    def output_matmul(s):
        sl = slice(s * LANES, (s + 1) * LANES)
        y = jnp.dot(xs_ref[:, s * sw:(s + 1) * sw].astype(BF16), cblk_ref[s],
                    preferred_element_type=F32) + d_ref[:, sl] * u_ref[:, sl]
        act_ref[:, sl] = _gelu_tanh(y).astype(BF16)

    ahead = 2
    for s in range(min(ahead, n_slabs)):
        input_matmul(s)
    for s in range(n_slabs):
        lo = s * sw
        if s + ahead < n_slabs:
            input_matmul(s + ahead)
        a_r = are_ref[:, lo:lo + hs]
        a_i = aim_ref[:, lo:lo + hs]
        x_r = st_ref[:, lo:lo + hs]
        x_i = st_ref[:, lo + hs:lo + sw]
        for t in range(tc):
            r0 = t * SUBLANES
            new_r = a_r * x_r - a_i * x_i + xs_ref[r0:r0 + SUBLANES, lo:lo + hs]
            new_i = a_r * x_i + a_i * x_r + xs_ref[r0:r0 + SUBLANES, lo + hs:lo + sw]
            xs_ref[r0:r0 + SUBLANES, lo:lo + hs] = new_r
            xs_ref[r0:r0 + SUBLANES, lo + hs:lo + sw] = new_i
            x_r, x_i = new_r, new_i
        st_ref[:, lo:lo + hs] = x_r
        st_ref[:, lo + hs:lo + sw] = x_i
        if s >= 1:
            output_matmul(s - 1)
    output_matmul(n_slabs - 1)
    g_bt = jnp.dot(permt_ref[...], act_ref[...], preferred_element_type=F32)
    o_ref[...] = g_bt.reshape(nb, tc, d).astype(o_ref.dtype)
    zero_ref[...] = jnp.zeros_like(zero_ref)


def _s5_mixer(x3d, gain, w_in, bblk, cblk, a_re, a_im, d_skip, *, tc, zero_shape):
    nb, seq, d = x3d.shape
    steps = seq // tc
    zero_rows = zero_shape[0] // steps
    assert zero_rows * steps == zero_shape[0] and zero_rows % SEG_ALIGN == 0
    rows = nb * tc
    n_slabs = d // LANES
    sw = 2 * SSM_SLAB_STATE
    r = np.arange(rows)
    perm = np.zeros((rows, rows), np.float32)
    perm[(r % tc) * nb + r // tc, r] = 1.0
    perm_j = jnp.asarray(perm, BF16)
    permt_j = jnp.asarray(perm.T, BF16)
    kern = _s5_kernel
    full2 = lambda a: pl.BlockSpec(a.shape, lambda i: (0, 0))
    full3 = lambda a: pl.BlockSpec(a.shape, lambda i: (0, 0, 0))
    return pl.pallas_call(
        kern,
        grid=(seq // tc,),
        in_specs=[
            pl.BlockSpec((nb, tc, d), lambda i: (0, i, 0)),
            full2(gain), full2(w_in), full2(perm_j), full2(permt_j),
            full3(bblk), full3(cblk), full2(a_re), full2(a_im), full2(d_skip),
        ],
        out_specs=[pl.BlockSpec((nb, tc, d), lambda i: (0, i, 0)),
                   pl.BlockSpec((zero_rows, zero_shape[1]), lambda i: (i, 0))],
        out_shape=[jax.ShapeDtypeStruct((nb, seq, d), BF16),
                   jax.ShapeDtypeStruct(zero_shape, BF16)],
        scratch_shapes=[
            pltpu.VMEM((rows, n_slabs * sw), F32),
            pltpu.VMEM((SUBLANES, n_slabs * sw), F32),
            pltpu.VMEM((rows, d), F32),
            pltpu.VMEM((rows, d), BF16),
        ],
        compiler_params=_params("arbitrary"),
        name="s5_mixer",
    )(x3d, gain, w_in, perm_j, permt_j, bblk, cblk, a_re, a_im, d_skip)


def _s5_coefficients(a_re, a_im, log_dt, b_re, b_im, c_re, c_im):
    dt = jnp.exp(log_dt.astype(F32))[:, None]
    ar = a_re.astype(F32)
    ai = a_im.astype(F32)
    mag = jnp.exp(ar * dt)
    abar_re = mag * jnp.cos(ai * dt)
    abar_im = mag * jnp.sin(ai * dt)
    den = ar * ar + ai * ai
    nr = abar_re - 1.0
    ni = abar_im
    coef_re = (nr * ar + ni * ai) / den
    coef_im = (ni * ar - nr * ai) / den
    br = b_re.astype(F32)
    bi = b_im.astype(F32)
    bbar_re = coef_re[..., None] * br - coef_im[..., None] * bi
    bbar_im = coef_re[..., None] * bi + coef_im[..., None] * br
    n_groups = ar.shape[0]
    n_slabs = n_groups // SSM_SLAB_GROUPS
    eye = jnp.eye(SSM_SLAB_GROUPS, dtype=F32)

    def in_block(bb):
        bb = bb.reshape(n_slabs, SSM_SLAB_GROUPS, SSM_STATE, SSM_GROUP)
        blk = jnp.einsum('sgph,gk->sghkp', bb, eye)
        return blk.reshape(n_slabs, LANES, SSM_SLAB_STATE)

    def out_block(cc):
        cc = cc.reshape(n_slabs, SSM_SLAB_GROUPS, SSM_GROUP, SSM_STATE)
        blk = jnp.einsum('sghp,gk->sgpkh', cc, eye)
        return blk.reshape(n_slabs, SSM_SLAB_STATE, LANES)

    bblk = jnp.concatenate([in_block(bbar_re), in_block(bbar_im)], axis=2)
    cblk = jnp.concatenate([out_block(c_re.astype(F32)),
                            -out_block(c_im.astype(F32))], axis=1)

    def lanes(a):
        a = a.reshape(n_slabs, 1, SSM_SLAB_STATE)
        a = jnp.concatenate([a, a], axis=2).reshape(1, -1)
        return jnp.broadcast_to(a, (SUBLANES, a.shape[1]))

    return (bblk.astype(BF16), cblk.astype(BF16), lanes(abar_re), lanes(abar_im))


MOE_CHUNK = 512
MOE_FIRST = 192
MOE_REST = MOE_CHUNK - MOE_FIRST
MOE_PIECES = ((0, MOE_FIRST), (MOE_FIRST, MOE_REST))
MOE_TILE = 512


def _glu_router_kernel(x_ref, act_ref, wa_ref, wb_ref, g_ref, w_ref, b_ref,
                       x3_ref, h_ref, pos_ref, post_ref, cnt_ref):
    tm, d = x_ref.shape
    half = tm // 2

    def glu(r0):
        act = act_ref[r0:r0 + half, :]
        a = jnp.dot(act, wa_ref[...], preferred_element_type=F32)
        b = jnp.dot(act, wb_ref[...], preferred_element_type=F32)
        x3 = x_ref[r0:r0 + half, :] + a * _sigmoid(b)
        x3_ref[r0:r0 + half, :] = x3
        return x3

    def route(r0, x3):
        h = _rms(x3, g_ref[...])
        h_hi = h.astype(BF16)
        h_ref[r0:r0 + half, 0:d] = h_hi
        h_lo = (h - h_hi.astype(F32)).astype(BF16)
        p_hi = jnp.dot(h_hi, w_ref[...], preferred_element_type=F32)
        p_lo = jnp.dot(h_lo, w_ref[...], preferred_element_type=F32)
        logits = p_hi + pltpu.roll(p_hi, LANES - N_EXPERTS, axis=1) + p_lo + b_ref[...]
        lane = lax.broadcasted_iota(jnp.int32, logits.shape, 1)
        logits = jnp.where(lane < N_EXPERTS, logits, -jnp.inf)
        m1 = jnp.max(logits, axis=1, keepdims=True)
        i1 = jnp.min(jnp.where(logits == m1, lane, LANES), axis=1, keepdims=True)
        rest = jnp.where(lane == i1, -jnp.inf, logits)
        m2 = jnp.max(rest, axis=1, keepdims=True)
        i2 = jnp.min(jnp.where(rest == m2, lane, LANES), axis=1, keepdims=True)
        e2 = jnp.exp(m2 - m1)
        g1 = 1.0 / (1.0 + e2)
        g2 = e2 / (1.0 + e2)
        gate = jnp.where(lane == i1, g1, 0.0) + jnp.where(lane == i2, g2, 0.0)
        g_hi = gate.astype(BF16).astype(F32)
        g_r = gate - g_hi
        g_mid = g_r.astype(BF16).astype(F32)
        packed = jnp.where(lane < N_EXPERTS, g_hi,
                           jnp.where(lane < 2 * N_EXPERTS,
                                     pltpu.roll(g_mid, N_EXPERTS, axis=1),
                                     pltpu.roll(g_r - g_mid, 2 * N_EXPERTS, axis=1)))
        h_ref[r0:r0 + half, d:d + LANES] = packed.astype(BF16)
        return jnp.where((lane == i1) | (lane == i2), 1.0, 0.0)

    x3_a = glu(0)
    x3_b = glu(half)
    member = jnp.concatenate([route(0, x3_a), route(half, x3_b)], axis=0)
    row = lax.broadcasted_iota(jnp.int32, member.shape, 0)
    c = member
    sh = 1
    while sh < tm:
        c = c + jnp.where(row >= sh, pltpu.roll(c, sh, axis=0), 0.0)
        sh *= 2
    pos = jnp.where(member > 0.0, c - member, -1.0)
    pos_ref[...] = pos
    post_ref[0] = pos.T[0:SUBLANES, :]
    cnt_ref[0] = jnp.broadcast_to(c[tm - 1:tm, :], (SUBLANES, LANES))


def _glu_router(x2d, act2d, wa, wb, gain, w_pad, b_pad):
    t, d = x2d.shape
    tm = MOE_CHUNK
    n_chunks = t // tm
    return pl.pallas_call(
        _glu_router_kernel,
        grid=(n_chunks,),
        in_specs=[
            pl.BlockSpec((tm, d), lambda i: (i, 0)),
            pl.BlockSpec((tm, d), lambda i: (i, 0)),
            pl.BlockSpec(wa.shape, lambda i: (0, 0)),
            pl.BlockSpec(wb.shape, lambda i: (0, 0)),
            pl.BlockSpec((1, d), lambda i: (0, 0)),
            pl.BlockSpec((d, LANES), lambda i: (0, 0)),
            pl.BlockSpec((1, LANES), lambda i: (0, 0)),
        ],
        out_specs=[
            pl.BlockSpec((tm, d), lambda i: (i, 0)),
            pl.BlockSpec((tm, d + LANES), lambda i: (i, 0)),
            pl.BlockSpec((tm, LANES), lambda i: (i, 0)),
            pl.BlockSpec((1, SUBLANES, tm), lambda i: (i, 0, 0)),
            pl.BlockSpec((1, SUBLANES, LANES), lambda i: (i, 0, 0)),
        ],
        out_shape=[
            jax.ShapeDtypeStruct((t, d), F32),
            jax.ShapeDtypeStruct((t, d + LANES), BF16),
            jax.ShapeDtypeStruct((t, LANES), F32),
            jax.ShapeDtypeStruct((n_chunks, SUBLANES, tm), F32),
            jax.ShapeDtypeStruct((n_chunks, SUBLANES, LANES), F32),
        ],
        compiler_params=_params("parallel"),
        name="glu_router",
    )(x2d, act2d, wa, wb, gain, w_pad, b_pad)


MOE_SPARE = max(MOE_FIRST, MOE_REST)


def _moe_tiles(n_tokens, row_multiple):
    n_chunks = n_tokens // MOE_CHUNK
    max_rows = (TOP_K * n_tokens + (SEG_ALIGN - 1) * N_EXPERTS * n_chunks
                + N_EXPERTS * (MOE_SPARE + MOE_TILE))
    unit = math.lcm(MOE_TILE, row_multiple)
    return -(-max_rows // unit) * (unit // MOE_TILE)


def _moe_layout(cnt, n_tiles):
    seg = (cnt + SEG_ALIGN - 1) // SEG_ALIGN * SEG_ALIGN
    used = jnp.sum(seg, axis=0)
    padded = (used + MOE_SPARE + MOE_TILE - 1) // MOE_TILE * MOE_TILE
    ends = jnp.cumsum(padded)
    start = ends - padded
    off = start[None, :] + jnp.cumsum(seg, axis=0) - seg
    tile_start = jnp.arange(n_tiles, dtype=jnp.int32) * MOE_TILE
    tile_e = jnp.sum((tile_start[:, None] >= ends[None, :]).astype(jnp.int32), axis=1)
    tile_e = jnp.minimum(tile_e, N_EXPERTS - 1)
    tile_rows = jnp.clip(used[tile_e] - (tile_start - start[tile_e]), 0, MOE_TILE)
    return (off.reshape(-1).astype(jnp.int32), cnt.reshape(-1).astype(jnp.int32),
            tile_e.astype(jnp.int32), tile_rows.astype(jnp.int32))


def _segment_copies(hbm_ref, buf_ref, sem_ref, off_ref, chunk, to_hbm, slot=()):
    copies = []
    for e in range(N_EXPERTS):
        off = pl.multiple_of(off_ref[chunk * N_EXPERTS + e], SEG_ALIGN)
        for piece, (first, rows) in enumerate(MOE_PIECES):
            hbm = hbm_ref.at[pl.ds(off + first, rows)]
            buf = buf_ref.at[(*slot, e, pl.ds(first, rows))]
            src, dst = (buf, hbm) if to_hbm else (hbm, buf)
            copies.append(pltpu.make_async_copy(src, dst, sem_ref.at[(*slot, e, piece)]))
    return copies


def _onehot(index, target):
    return jnp.where(index == target, 1.0, 0.0).astype(BF16)


def _gather_kernel(off_ref, cnt_ref, h_ref, post_ref, zeros_hbm, o_hbm, stage_ref,
                   sem_ref):
    del zeros_hbm
    c = pl.program_id(0)
    h = h_ref[...]
    copies = _segment_copies(o_hbm, stage_ref, sem_ref, off_ref, c, to_hbm=True)

    def piece_onehot(e, piece):
        first, rows = MOE_PIECES[piece]
        rank = first + lax.broadcasted_iota(jnp.int32, (rows, MOE_CHUNK), 0)
        return _onehot(rank.astype(F32), post_ref[0, e:e + 1, :])

    def move_piece(e, piece, onehot=None):
        first, rows = MOE_PIECES[piece]
        onehot = piece_onehot(e, piece) if onehot is None else onehot
        stage_ref[e, first:first + rows] = jnp.dot(
            onehot, h, preferred_element_type=F32).astype(BF16)
        copies[2 * e + piece].start()

    onehot = piece_onehot(0, 0)
    for e in range(N_EXPERTS):
        nxt = piece_onehot(e + 1, 0) if e + 1 < N_EXPERTS else None
        move_piece(e, 0, onehot)
        onehot = nxt
    for e in range(N_EXPERTS):
        @pl.when(cnt_ref[c * N_EXPERTS + e] > MOE_FIRST)
        def _(e=e):
            move_piece(e, 1)

    for e in range(N_EXPERTS):
        copies[2 * e].wait()

        @pl.when(cnt_ref[c * N_EXPERTS + e] > MOE_FIRST)
        def _(e=e):
            copies[2 * e + 1].wait()


def _gather(h2d, post, off, cnt, zero_buf):
    t, d = h2d.shape
    n_chunks = t // MOE_CHUNK
    grid_spec = pltpu.PrefetchScalarGridSpec(
        num_scalar_prefetch=2,
        grid=(n_chunks,),
        in_specs=[
            pl.BlockSpec((MOE_CHUNK, d), lambda i, off, cnt: (i, 0)),
            pl.BlockSpec((1, SUBLANES, MOE_CHUNK), lambda i, off, cnt: (i, 0, 0)),
            pl.BlockSpec(memory_space=pl.ANY),
        ],
        out_specs=pl.BlockSpec(memory_space=pl.ANY),
        scratch_shapes=[
            pltpu.VMEM((N_EXPERTS, MOE_CHUNK, d), BF16),
            pltpu.SemaphoreType.DMA((N_EXPERTS, 2)),
        ],
    )
    return pl.pallas_call(
        _gather_kernel,
        grid_spec=grid_spec,
        out_shape=jax.ShapeDtypeStruct(zero_buf.shape, BF16),
        input_output_aliases={4: 0},
        compiler_params=_params("arbitrary"),
        name="moe_gather",
    )(off, cnt, h2d, post, zero_buf)


def _expert_kernel(te_ref, tr_ref, x_ref, w1_ref, w3_ref, w2_ref, o_ref):
    i = pl.program_id(0)
    rows = tr_ref[i]
    d = o_ref.shape[1]

    def run(n):
        row = lax.broadcasted_iota(jnp.int32, (n, x_ref.shape[1]), 0)
        x = jnp.where(row < rows, x_ref[0:n, :], jnp.zeros((), x_ref.dtype))
        lane = lax.broadcasted_iota(jnp.int32, (n, LANES), 1)
        mine = (lane < 3 * N_EXPERTS) & (jnp.bitwise_and(lane, N_EXPERTS - 1) == te_ref[i])
        gate = jnp.sum(jnp.where(mine, x[:, d:d + LANES].astype(F32), 0.0),
                       axis=1, keepdims=True)
        y = _swiglu(x[:, 0:d], w1_ref, w3_ref, w2_ref)
        o_ref[0:n, :] = (gate * y).astype(o_ref.dtype)

    half = MOE_TILE // 2

    @pl.when(rows > half)
    def _():
        run(MOE_TILE)

    @pl.when((rows > 0) & (rows <= half))
    def _():
        run(half)
        o_ref[half:MOE_TILE, :] = jnp.zeros((MOE_TILE - half, d), o_ref.dtype)

    @pl.when(rows == 0)
    def _():
        o_ref[...] = jnp.zeros_like(o_ref)


def _experts(x_sorted, w1, w3, w2, tile_e, tile_rows):
    assert N_EXPERTS & (N_EXPERTS - 1) == 0
    n_rows = x_sorted.shape[0]
    f = w1.shape[1]
    d = w2.shape[1]
    grid_spec = pltpu.PrefetchScalarGridSpec(
        num_scalar_prefetch=2,
        grid=(n_rows // MOE_TILE,),
        in_specs=[
            pl.BlockSpec((MOE_TILE, d + LANES), lambda i, te, tr: (i, 0)),
            pl.BlockSpec((d, f), lambda i, te, tr: (te[i], 0)),
            pl.BlockSpec((d, f), lambda i, te, tr: (te[i], 0)),
            pl.BlockSpec((f, d), lambda i, te, tr: (te[i], 0)),
        ],
        out_specs=pl.BlockSpec((MOE_TILE, d), lambda i, te, tr: (i, 0)),
    )
    return pl.pallas_call(
        _expert_kernel,
        grid_spec=grid_spec,
        out_shape=jax.ShapeDtypeStruct((n_rows, d), BF16),
        compiler_params=_params("arbitrary"),
        name="moe_experts",
    )(tile_e, tile_rows, x_sorted, w1, w3, w2)


def _combine_kernel(off_ref, cnt_ref, x_ref, pos_ref, fg_ref, y_hbm,
                    o_ref, ybuf_ref, acc_ref, sem_ref):
    c = pl.program_id(0)
    n_chunks = pl.num_programs(0)
    slot = c % 2

    def fetch(chunk, sl, start):
        copies = _segment_copies(y_hbm, ybuf_ref, sem_ref, off_ref, chunk,
                                 to_hbm=False, slot=(sl,))
        for e in range(N_EXPERTS):
            first, second = copies[2 * e], copies[2 * e + 1]
            if start:
                first.start()
            else:
                first.wait()

            @pl.when(cnt_ref[chunk * N_EXPERTS + e] > MOE_FIRST)
            def _(second=second):
                if start:
                    second.start()
                else:
                    second.wait()

    @pl.when(c == 0)
    def _():
        fetch(c, slot, True)

    @pl.when(c + 1 < n_chunks)
    def _():
        fetch(c + 1, 1 - slot, True)

    fetch(c, slot, False)

    def piece_onehot(e, piece):
        first, rows = MOE_PIECES[piece]
        rank = first + lax.broadcasted_iota(jnp.int32, (MOE_CHUNK, rows), 1)
        return _onehot(rank.astype(F32), pos_ref[:, e:e + 1])

    def piece_rows(e, piece, onehot=None):
        first, rows = MOE_PIECES[piece]
        onehot = piece_onehot(e, piece) if onehot is None else onehot
        return jnp.dot(onehot, ybuf_ref[slot, e, first:first + rows],
                       preferred_element_type=F32)

    acc = x_ref[...]
    onehot = piece_onehot(0, 0)
    for e in range(N_EXPERTS):
        nxt = piece_onehot(e + 1, 0) if e + 1 < N_EXPERTS else None
        acc = acc + piece_rows(e, 0, onehot)
        onehot = nxt
    acc_ref[...] = acc
    for e in range(N_EXPERTS):
        @pl.when(cnt_ref[c * N_EXPERTS + e] > MOE_FIRST)
        def _(e=e):
            acc_ref[...] += piece_rows(e, 1)

    o_ref[...] = _rms(acc_ref[...], fg_ref[...])


def _combine(x2d, pos, final_gain, y_sorted, off, cnt):
    t, d = x2d.shape
    n_chunks = t // MOE_CHUNK
    grid_spec = pltpu.PrefetchScalarGridSpec(
        num_scalar_prefetch=2,
        grid=(n_chunks,),
        in_specs=[
            pl.BlockSpec((MOE_CHUNK, d), lambda i, off, cnt: (i, 0)),
            pl.BlockSpec((MOE_CHUNK, LANES), lambda i, off, cnt: (i, 0)),
            pl.BlockSpec((1, d), lambda i, off, cnt: (0, 0)),
            pl.BlockSpec(memory_space=pl.ANY),
        ],
        out_specs=pl.BlockSpec((MOE_CHUNK, d), lambda i, off, cnt: (i, 0)),
        scratch_shapes=[
            pltpu.VMEM((2, N_EXPERTS, MOE_CHUNK, d), BF16),
            pltpu.VMEM((MOE_CHUNK, d), F32),
            pltpu.SemaphoreType.DMA((2, N_EXPERTS, 2)),
        ],
    )
    return pl.pallas_call(
        _combine_kernel,
        grid_spec=grid_spec,
        out_shape=jax.ShapeDtypeStruct((t, d), F32),
        compiler_params=_params("arbitrary"),
        name="moe_combine",
    )(off, cnt, x2d, pos, final_gain, y_sorted)


def _glu_moe(x2d, act2d, wa, wb, gain, router_w, router_b, w1, w3, w2, final_gain,
             zero_buf):
    t, d = x2d.shape
    rw_hi = router_w.astype(BF16)
    rw_lo = (router_w.astype(F32) - rw_hi.astype(F32)).astype(BF16)
    rw = jnp.pad(jnp.concatenate([rw_hi, rw_lo], axis=1),
                 ((0, 0), (0, LANES - 2 * N_EXPERTS)))
    rb = jnp.pad(router_b.astype(F32), (0, LANES - N_EXPERTS)).reshape(1, LANES)
    x3, h, pos, post, cnt = _glu_router(x2d, act2d, wa, wb, gain, rw, rb)
    cnt = cnt[:, 0, :N_EXPERTS].astype(jnp.int32)
    off, cnt_flat, tile_e, tile_rows = _moe_layout(cnt, zero_buf.shape[0] // MOE_TILE)
    h_sorted = _gather(h, post, off, cnt_flat, zero_buf)
    y_sorted = _experts(h_sorted, w1, w3, w2, tile_e, tile_rows)
    return _combine(x3, pos, final_gain, y_sorted, off, cnt_flat)


def _even_weights(w_in, b_forget):
    n = w_in.shape[1]
    w_all = jnp.pad(w_in.astype(BF16), ((0, 0), (0, -n % LANES)))
    bias = jnp.pad(b_forget.astype(F32), (0, LANES - FOX_HEADS)).reshape(1, LANES)
    qw = FOX_HEADS * LANES
    place = np.zeros((LANES, 2 * qw), np.float32)
    for hh in range(FOX_HEADS):
        for piece in range(3):
            place[piece * FOX_HEADS + hh, hh * LANES + BIAS_LANE + piece] = 1.0
            place[piece * FOX_HEADS + hh, qw + hh * LANES + ONES_LANE + piece] = -1.0
    return w_all, bias, jnp.asarray(place, BF16)


def kernel(x, even_mix_norm, even_w_in, even_b_forget, even_w_pool, even_pool_scale, even_w_out, even_ffn_norm, even_ffn_w1, even_ffn_w3, even_ffn_w2, odd_mix_norm, odd_w_in, ssm_a_re, ssm_a_im, ssm_log_dt, ssm_b_re, ssm_b_im, ssm_c_re, ssm_c_im, ssm_d, odd_w_glu_a, odd_w_glu_b, odd_moe_norm, router_w, router_b, expert_w1, expert_w3, expert_w2, final_norm):
    b, s, d = x.shape
    t = b * s
    assert b == SUBLANES, "the S5 recurrence keeps one batch row per sublane"
    x2d = x.reshape(t, d)
    row = lambda v: v.reshape(1, -1).astype(F32)

    w_all, bias, place = _even_weights(even_w_in[0], even_b_forget[0])
    (q_aug, k_aug, v, p_in), casted = _even_inproj(
        x2d, row(even_mix_norm[0]), w_all, bias, place,
        [even_ffn_w1[0], even_ffn_w3[0], even_w_out[0],
         even_w_pool[0].reshape(-1, POOL_GROUP_DIM),
         odd_w_in[0], odd_w_glu_a[0], odd_w_glu_b[0]],
        seq=s, tm=512)
    ffn_w1, ffn_w3, w_out, w_pool, s5_w_in, glu_a, glu_b = casted
    n_e, _, f = expert_w1[0].shape
    att, (ew1, ew3, ew2) = _attention(
        q_aug, k_aug, v,
        [expert_w1[0].reshape(n_e * d, f), expert_w3[0].reshape(n_e * d, f),
         expert_w2[0].reshape(n_e * f, d)],
        batch=b, seq=s, tq=256)
    x2 = _even_tail(x2d, att, p_in, w_pool.reshape(even_w_pool[0].shape),
                    row(even_pool_scale[0]), w_out, row(even_ffn_norm[0]),
                    ffn_w1, ffn_w3, even_ffn_w2[0].astype(BF16), seq=s, tm=512)

    bblk, cblk, a_re, a_im = _s5_coefficients(
        ssm_a_re[0], ssm_a_im[0], ssm_log_dt[0], ssm_b_re[0], ssm_b_im[0],
        ssm_c_re[0], ssm_c_im[0])
    moe_rows = _moe_tiles(t, (s // S5_STEP) * SEG_ALIGN) * MOE_TILE
    g, zero_buf = _s5_mixer(x2.reshape(b, s, d), row(odd_mix_norm[0]),
                            s5_w_in, bblk, cblk, a_re, a_im,
                            row(ssm_d[0]), tc=S5_STEP,
                            zero_shape=(moe_rows, d + LANES))
    out = _glu_moe(x2, g.reshape(t, d), glu_a, glu_b, row(odd_moe_norm[0]),
                   router_w[0], router_b[0], ew1, ew3, ew2,
                   row(final_norm), zero_buf)
    return out.reshape(b, s, d)
```

```python
import functools
import math

import numpy as np
import jax
import jax.numpy as jnp
from jax import lax
from jax.experimental import pallas as pl
from jax.experimental.pallas import tpu as pltpu

F32 = jnp.float32
BF16 = jnp.bfloat16

EPS = 1e-6
NEG_INF = -1e30
LANES = 128
SUBLANES = 8
SEG_ALIGN = 16
VMEM_LIMIT = 56 * 1024 * 1024

FOX_HEADS = 8
FOX_HEAD_DIM = 64
FOX_WIDTH = FOX_HEADS * FOX_HEAD_DIM
POOL_WINDOWS = (2, 4, 8, 16)
POOL_GROUP_DIM = 128
POOL_WIDTH = len(POOL_WINDOWS) * POOL_GROUP_DIM
POOL_HALO = 16
SSM_GROUP = 16
SSM_STATE = 64
SSM_SLAB_GROUPS = LANES // SSM_GROUP
SSM_SLAB_STATE = SSM_SLAB_GROUPS * SSM_STATE
S5_STEP = 32
N_EXPERTS = 8
TOP_K = 2

BIAS_LANE = FOX_HEAD_DIM
ONES_LANE = FOX_HEAD_DIM + 3
DENOM_LANE = FOX_HEAD_DIM


def _params(*sem):
    return pltpu.CompilerParams(dimension_semantics=sem,
                                vmem_limit_bytes=VMEM_LIMIT)


def _rms(x, g):
    ms = jnp.mean(x * x, axis=-1, keepdims=True)
    return x * lax.rsqrt(ms + EPS) * g


def _sigmoid(x):
    return 1.0 / (1.0 + jnp.exp(-x))


def _lane_range_ones(lo, hi):
    lane = lax.broadcasted_iota(jnp.int32, (1, LANES), 1)
    return jnp.where((lane >= lo) & (lane < hi), 1.0, 0.0).astype(F32)


def _cast_specs(weights, steps, step_index):
    specs = []
    for w in weights:
        rows = -(-w.shape[0] // (steps * SEG_ALIGN)) * SEG_ALIGN
        last = -(-w.shape[0] // rows) - 1
        specs.append(pl.BlockSpec(
            (rows, w.shape[1]),
            lambda *idx, last=last: (jnp.minimum(step_index(*idx), last), 0)))
    return specs


def _cast_slabs(src_refs, dst_refs):
    for src_ref, dst_ref in zip(src_refs, dst_refs):
        dst_ref[...] = src_ref[...].astype(dst_ref.dtype)


def _even_inproj_kernel(x_ref, g_ref, w_ref, bias_ref, place_ref, *rest,
                        tiles_per_seq, n_cast):
    cast_in, rest = rest[:n_cast], rest[n_cast:]
    q_ref, k_ref, v_ref, p_ref = rest[:4]
    cast_out, carry_ref = rest[4:4 + n_cast], rest[4 + n_cast]
    _cast_slabs(cast_in, cast_out)
    i = pl.program_id(0)
    tm = x_ref.shape[0]
    parts = 2
    rows = tm // parts
    fw = FOX_WIDTH
    qw = FOX_HEADS * LANES

    @pl.when(i % tiles_per_seq == 0)
    def _():
        carry_ref[...] = jnp.zeros_like(carry_ref)

    def project(r0):
        h = _rms(x_ref[r0:r0 + rows, :], g_ref[...]).astype(BF16)
        return jnp.dot(h, w_ref[...], preferred_element_type=F32)

    def finish(r0, z, carry):
        out = slice(r0, r0 + rows)
        p_ref[out, :] = z[:, 3 * fw + FOX_HEADS:3 * fw + FOX_HEADS + POOL_WIDTH]
        fg = z[:, 3 * fw:3 * fw + LANES] + bias_ref[...]
        lf = jnp.minimum(fg, 0.0) - jnp.log1p(jnp.exp(-jnp.abs(fg)))
        row = lax.broadcasted_iota(jnp.int32, lf.shape, 0)
        c = lf
        sh = 1
        while sh < rows:
            c = c + jnp.where(row >= sh, pltpu.roll(c, sh, axis=0), 0.0)
            sh *= 2
        c = c + carry
        hi = c.astype(BF16).astype(F32)
        r1 = c - hi
        mid = r1.astype(BF16).astype(F32)
        lo = r1 - mid
        lane = lax.broadcasted_iota(jnp.int32, (rows, LANES), 1)
        packed = jnp.where(lane < FOX_HEADS, hi,
                           jnp.where(lane < 2 * FOX_HEADS,
                                     pltpu.roll(mid, FOX_HEADS, axis=1),
                                     pltpu.roll(lo, 2 * FOX_HEADS, axis=1)))
        placed = jnp.dot(packed.astype(BF16), place_ref[...],
                         preferred_element_type=F32)
        low = lane < FOX_HEAD_DIM
        ones_q = _lane_range_ones(ONES_LANE, ONES_LANE + 3)
        ones_k = _lane_range_ones(BIAS_LANE, BIAS_LANE + 3)
        ones_v = _lane_range_ones(DENOM_LANE, DENOM_LANE + 1)

        def head_lanes(base, hh):
            pair = z[:, base + (hh // 2) * LANES:base + (hh // 2 + 1) * LANES]
            return pltpu.roll(pair, FOX_HEAD_DIM, axis=1) if hh % 2 else pair

        q_scale = FOX_HEAD_DIM ** -0.5
        for hh in range(FOX_HEADS):
            sl = slice(hh * LANES, (hh + 1) * LANES)
            q_ref[out, sl] = jnp.where(low, head_lanes(0, hh) * q_scale,
                                       placed[:, sl] + ones_q).astype(BF16)
            k_ref[out, sl] = jnp.where(
                low, head_lanes(fw, hh),
                placed[:, qw + hh * LANES:qw + (hh + 1) * LANES] + ones_k).astype(BF16)
            v_ref[out, sl] = jnp.where(low, head_lanes(2 * fw, hh), ones_v).astype(BF16)
        return c[rows - 1:rows, :]

    carry = carry_ref[0:1, :]
    z = project(0)
    for part in range(parts):
        nxt = project((part + 1) * rows) if part + 1 < parts else None
        carry = finish(part * rows, z, carry)
        z = nxt
    carry_ref[...] = jnp.broadcast_to(carry, carry_ref.shape)


def _even_inproj(x2d, gain, w_all, bias, place, cast_weights, *, seq, tm):
    t, d = x2d.shape
    n = w_all.shape[1]
    qw = FOX_HEADS * LANES
    cast_specs = _cast_specs(cast_weights, t // tm, lambda i: i)
    kern = functools.partial(_even_inproj_kernel, tiles_per_seq=seq // tm,
                             n_cast=len(cast_weights))
    outs = pl.pallas_call(
        kern,
        grid=(t // tm,),
        in_specs=[
            pl.BlockSpec((tm, d), lambda i: (i, 0)),
            pl.BlockSpec((1, d), lambda i: (0, 0)),
            pl.BlockSpec((d, n), lambda i: (0, 0)),
            pl.BlockSpec((1, LANES), lambda i: (0, 0)),
            pl.BlockSpec(place.shape, lambda i: (0, 0)),
            *cast_specs,
        ],
        out_specs=[
            pl.BlockSpec((tm, qw), lambda i: (i, 0)),
            pl.BlockSpec((tm, qw), lambda i: (i, 0)),
            pl.BlockSpec((tm, qw), lambda i: (i, 0)),
            pl.BlockSpec((tm, POOL_WIDTH), lambda i: (i, 0)),
            *cast_specs,
        ],
        out_shape=[
            jax.ShapeDtypeStruct((t, qw), BF16),
            jax.ShapeDtypeStruct((t, qw), BF16),
            jax.ShapeDtypeStruct((t, qw), BF16),
            jax.ShapeDtypeStruct((t, POOL_WIDTH), F32),
        ] + [jax.ShapeDtypeStruct(w.shape, BF16) for w in cast_weights],
        scratch_shapes=[pltpu.VMEM((SUBLANES, LANES), F32)],
        compiler_params=_params("arbitrary"),
        name="even_inproj",
    )(x2d, gain, w_all, bias, place, *cast_weights)
    return outs[:4], outs[4:]


def _dot_nt(a, b):
    return lax.dot_general(a, b, (((1,), (1,)), ((), ())),
                           preferred_element_type=F32)


def _attn_kernel(q_ref, k_ref, v_ref, *rest, tq, n_cast):
    cast_in, o_ref, cast_out = rest[:n_cast], rest[n_cast], rest[n_cast + 1:]
    seq = q_ref.shape[0]
    row = lax.broadcasted_iota(jnp.int32, (tq, tq), 0)
    col = lax.broadcasted_iota(jnp.int32, (tq, tq), 1)
    causal = col <= row
    n_tiles = seq // tq

    def scores(qi):
        r0 = qi * tq
        q = q_ref[r0:r0 + tq, :]
        s_diag = jnp.where(causal, _dot_nt(q, k_ref[r0:r0 + tq, :]), NEG_INF)
        s_past = _dot_nt(q, k_ref[0:r0, :]) if qi > 0 else None
        return s_diag, s_past

    ahead = 2
    queue = [scores(qi) for qi in range(min(ahead, n_tiles))]
    for qi in range(n_tiles):
        r0 = qi * tq
        s_diag, s_past = queue.pop(0)
        if qi + ahead < n_tiles:
            queue.append(scores(qi + ahead))
        m = jnp.max(s_diag, axis=1, keepdims=True)
        if qi > 0:
            m = jnp.maximum(m, jnp.max(s_past, axis=1, keepdims=True))
        acc = jnp.dot(jnp.exp(s_diag - m).astype(BF16), v_ref[r0:r0 + tq, :],
                      preferred_element_type=F32)
        if qi > 0:
            acc = acc + jnp.dot(jnp.exp(s_past - m).astype(BF16), v_ref[0:r0, :],
                                preferred_element_type=F32)
        o_ref[r0:r0 + tq, :] = (
            acc / acc[:, DENOM_LANE:DENOM_LANE + 1]).astype(o_ref.dtype)
    _cast_slabs(cast_in, cast_out)


def _attention(q_aug, k_aug, v_aug, cast_weights, *, batch, seq, tq):
    t = q_aug.shape[0]
    steps = batch * FOX_HEADS
    spec = pl.BlockSpec((seq, LANES), lambda b, h: (b, h))
    cast_specs = _cast_specs(cast_weights, steps, lambda b, h: b * FOX_HEADS + h)
    outs = pl.pallas_call(
        functools.partial(_attn_kernel, tq=tq, n_cast=len(cast_weights)),
        grid=(batch, FOX_HEADS),
        in_specs=[spec, spec, spec, *cast_specs],
        out_specs=[spec, *cast_specs],
        out_shape=[jax.ShapeDtypeStruct((t, FOX_HEADS * LANES), BF16)]
        + [jax.ShapeDtypeStruct(w.shape, BF16) for w in cast_weights],
        compiler_params=_params("parallel", "parallel"),
        name="fox_attention",
    )(q_aug, k_aug, v_aug, *cast_weights)
    return outs[0], outs[1:]


MXU_TILE = 256


def _swiglu_partial(h, w1, w3, w2):
    a = jnp.dot(h, w1, preferred_element_type=F32)
    b = jnp.dot(h, w3, preferred_element_type=F32)
    act = a * _sigmoid(a) * b
    return jnp.dot(act.astype(BF16), w2, preferred_element_type=F32)


def _ff_splits(f):
    cut = -(-(f // MXU_TILE) // 2) * MXU_TILE
    return ((0, cut), (cut, f)) if 0 < cut < f else ((0, f),)


def _swiglu(h, w1_ref, w3_ref, w2_ref):
    y = None
    for lo, hi in _ff_splits(w1_ref.shape[1]):
        part = _swiglu_partial(h, w1_ref[:, lo:hi], w3_ref[:, lo:hi], w2_ref[lo:hi, :])
        y = part if y is None else y + part
    return y


def _even_tail_kernel(x_ref, att_ref, p_ref, halo_ref, wpool_ref, scale_ref,
                      wo_att_ref, wo_pool_ref, g_ref, w1_ref, w3_ref, w2_ref, o_ref,
                      *, tiles_per_seq):
    i = pl.program_id(0)
    tm = x_ref.shape[0]
    tile_in_seq = i % tiles_per_seq
    p = p_ref[...]
    halo = jnp.where(tile_in_seq == 0, 0.0, halo_ref[...])
    ext = jnp.concatenate([halo, p], axis=0)
    pos = tile_in_seq * tm + lax.broadcasted_iota(jnp.int32, (tm, 1), 0)
    count = (pos + 1).astype(F32)
    mixed = []
    for gi, w in enumerate(POOL_WINDOWS):
        sl = slice(gi * POOL_GROUP_DIM, (gi + 1) * POOL_GROUP_DIM)
        acc = ext[:, sl]
        sh = 1
        while sh < w:
            acc = acc + pltpu.roll(acc, sh, axis=0)
            sh *= 2
        mean = acc[POOL_HALO:, :] / jnp.minimum(count, float(w))
        pooled = (mean - p[:, sl]).astype(BF16)
        mixed.append(jnp.dot(pooled, wpool_ref[gi], preferred_element_type=F32))
    pool = (jnp.concatenate(mixed, axis=1) * scale_ref[...]).astype(BF16)
    y = jnp.dot(att_ref[...], wo_att_ref[...], preferred_element_type=F32)
    y = y + jnp.dot(pool, wo_pool_ref[...], preferred_element_type=F32)
    x1 = x_ref[...] + y
    h = _rms(x1, g_ref[...]).astype(BF16)
    o_ref[...] = x1 + _swiglu(h, w1_ref, w3_ref, w2_ref)


def _even_tail(x2d, att, p_in, w_pool, pool_scale, w_out, ffn_gain, w1, w3, w2,
               *, seq, tm):
    t, d = x2d.shape
    resident = lambda w: pl.BlockSpec(w.shape, lambda i: (0,) * w.ndim,
                                      pipeline_mode=pl.Buffered(1))
    wo_att = w_out[:FOX_WIDTH].reshape(FOX_HEADS, FOX_HEAD_DIM, d)
    wo_att = jnp.pad(wo_att, ((0, 0), (0, LANES - FOX_HEAD_DIM), (0, 0)))
    wo_att = wo_att.reshape(FOX_HEADS * LANES, d)
    wo_pool = w_out[FOX_WIDTH:]
    halo_blocks = tm // POOL_HALO
    kern = functools.partial(_even_tail_kernel, tiles_per_seq=seq // tm)
    return pl.pallas_call(
        kern,
        grid=(t // tm,),
        in_specs=[
            pl.BlockSpec((tm, d), lambda i: (i, 0)),
            pl.BlockSpec((tm, FOX_HEADS * LANES), lambda i: (i, 0)),
            pl.BlockSpec((tm, POOL_WIDTH), lambda i: (i, 0)),
            pl.BlockSpec((POOL_HALO, POOL_WIDTH),
                         lambda i: (jnp.maximum(i * halo_blocks - 1, 0), 0)),
            resident(w_pool),
            pl.BlockSpec((1, POOL_WIDTH), lambda i: (0, 0)),
            resident(wo_att), resident(wo_pool),
            pl.BlockSpec((1, d), lambda i: (0, 0)),
            resident(w1), resident(w3), resident(w2),
        ],
        out_specs=pl.BlockSpec((tm, d), lambda i: (i, 0)),
        out_shape=jax.ShapeDtypeStruct((t, d), F32),
        compiler_params=_params("parallel"),
        name="even_tail",
    )(x2d, att, p_in, p_in, w_pool, pool_scale, wo_att, wo_pool, ffn_gain, w1, w3, w2)


def _gelu_tanh(x):
    c = math.sqrt(2.0 / math.pi)
    return 0.5 * x * (1.0 + jnp.tanh(c * (x + 0.044715 * (x * x * x))))


def _s5_kernel(x_ref, g_ref, win_ref, perm_ref, permt_ref, bblk_ref, cblk_ref,
               are_ref, aim_ref, d_ref, o_ref, zero_ref, xs_ref, st_ref, u_ref, act_ref):
    nb, tc, d = x_ref.shape
    rows = nb * tc
    n_slabs = d // LANES
    sw = 2 * SSM_SLAB_STATE

    @pl.when(pl.program_id(0) == 0)
    def _():
        st_ref[...] = jnp.zeros_like(st_ref)

    x = x_ref[...].reshape(rows, d)
    h = _rms(x, g_ref[...]).astype(BF16)
    h_tb = jnp.dot(perm_ref[...], h, preferred_element_type=F32).astype(BF16)
    u_ref[...] = jnp.dot(h_tb, win_ref[...], preferred_element_type=F32)
    hs = SSM_SLAB_STATE

    def input_matmul(s):
        xs_ref[:, s * sw:(s + 1) * sw] = jnp.dot(
            u_ref[:, s * LANES:(s + 1) * LANES].astype(BF16), bblk_ref[s],
            preferred_element_type=F32)

    def output_matmul(s):
        sl = slice(s * LANES, (s + 1) * LANES)
        y = jnp.dot(xs_ref[:, s * sw:(s + 1) * sw].astype(BF16), cblk_ref[s],
                    preferred_element_type=F32) + d_ref[:, sl] * u_ref[:, sl]
        act_ref[:, sl] = _gelu_tanh(y).astype(BF16)

    ahead = 2
    for s in range(min(ahead, n_slabs)):
        input_matmul(s)
    for s in range(n_slabs):
        lo = s * sw
        if s + ahead < n_slabs:
            input_matmul(s + ahead)
        a_r = are_ref[:, lo:lo + hs]
        a_i = aim_ref[:, lo:lo + hs]
        x_r = st_ref[:, lo:lo + hs]
        x_i = st_ref[:, lo + hs:lo + sw]
        for t in range(tc):
            r0 = t * SUBLANES
            new_r = a_r * x_r - a_i * x_i + xs_ref[r0:r0 + SUBLANES, lo:lo + hs]
            new_i = a_r * x_i + a_i * x_r + xs_ref[r0:r0 + SUBLANES, lo + hs:lo + sw]
            xs_ref[r0:r0 + SUBLANES, lo:lo + hs] = new_r
            xs_ref[r0:r0 + SUBLANES, lo + hs:lo + sw] = new_i
            x_r, x_i = new_r, new_i
        st_ref[:, lo:lo + hs] = x_r
        st_ref[:, lo + hs:lo + sw] = x_i
        if s >= 1:
            output_matmul(s - 1)
    output_matmul(n_slabs - 1)
    g_bt = jnp.dot(permt_ref[...], act_ref[...], preferred_element_type=F32)
    o_ref[...] = g_bt.reshape(nb, tc, d).astype(o_ref.dtype)
    zero_ref[...] = jnp.zeros_like(zero_ref)


def _s5_mixer(x3d, gain, w_in, bblk, cblk, a_re, a_im, d_skip, *, tc, zero_shape):
    nb, seq, d = x3d.shape
    steps = seq // tc
    zero_rows = zero_shape[0] // steps
    assert zero_rows * steps == zero_shape[0] and zero_rows % SEG_ALIGN == 0
    rows = nb * tc
    n_slabs = d // LANES
    sw = 2 * SSM_SLAB_STATE
    r = np.arange(rows)
    perm = np.zeros((rows, rows), np.float32)
    perm[(r % tc) * nb + r // tc, r] = 1.0
    perm_j = jnp.asarray(perm, BF16)
    permt_j = jnp.asarray(perm.T, BF16)
    kern = _s5_kernel
    full2 = lambda a: pl.BlockSpec(a.shape, lambda i: (0, 0))
    full3 = lambda a: pl.BlockSpec(a.shape, lambda i: (0, 0, 0))
    return pl.pallas_call(
        kern,
        grid=(seq // tc,),
        in_specs=[
            pl.BlockSpec((nb, tc, d), lambda i: (0, i, 0)),
            full2(gain), full2(w_in), full2(perm_j), full2(permt_j),
            full3(bblk), full3(cblk), full2(a_re), full2(a_im), full2(d_skip),
        ],
        out_specs=[pl.BlockSpec((nb, tc, d), lambda i: (0, i, 0)),
                   pl.BlockSpec((zero_rows, zero_shape[1]), lambda i: (i, 0))],
        out_shape=[jax.ShapeDtypeStruct((nb, seq, d), BF16),
                   jax.ShapeDtypeStruct(zero_shape, BF16)],
        scratch_shapes=[
            pltpu.VMEM((rows, n_slabs * sw), F32),
            pltpu.VMEM((SUBLANES, n_slabs * sw), F32),
            pltpu.VMEM((rows, d), F32),
            pltpu.VMEM((rows, d), BF16),
        ],
        compiler_params=_params("arbitrary"),
        name="s5_mixer",
    )(x3d, gain, w_in, perm_j, permt_j, bblk, cblk, a_re, a_im, d_skip)


def _s5_coefficients(a_re, a_im, log_dt, b_re, b_im, c_re, c_im):
    dt = jnp.exp(log_dt.astype(F32))[:, None]
    ar = a_re.astype(F32)
    ai = a_im.astype(F32)
    mag = jnp.exp(ar * dt)
    abar_re = mag * jnp.cos(ai * dt)
    abar_im = mag * jnp.sin(ai * dt)
    den = ar * ar + ai * ai
    nr = abar_re - 1.0
    ni = abar_im
    coef_re = (nr * ar + ni * ai) / den
    coef_im = (ni * ar - nr * ai) / den
    br = b_re.astype(F32)
    bi = b_im.astype(F32)
    bbar_re = coef_re[..., None] * br - coef_im[..., None] * bi
    bbar_im = coef_re[..., None] * bi + coef_im[..., None] * br
    n_groups = ar.shape[0]
    n_slabs = n_groups // SSM_SLAB_GROUPS
    row_group = np.arange(LANES)[:, None] // SSM_GROUP
    col_group = np.arange(SSM_SLAB_STATE)[None, :] // SSM_STATE
    in_mask = jnp.asarray(row_group == col_group, F32)

    def in_block(bb):
        bb = bb.reshape(n_slabs, SSM_SLAB_GROUPS, SSM_STATE, SSM_GROUP)
        bb = bb.transpose(0, 1, 3, 2).reshape(n_slabs, LANES, SSM_STATE)
        return jnp.tile(bb, (1, 1, SSM_SLAB_GROUPS)) * in_mask

    def out_block(cc):
        cc = cc.reshape(n_slabs, SSM_SLAB_GROUPS, SSM_GROUP, SSM_STATE)
        cc = cc.transpose(0, 1, 3, 2).reshape(n_slabs, SSM_SLAB_STATE, SSM_GROUP)
        return jnp.tile(cc, (1, 1, SSM_SLAB_GROUPS)) * in_mask.T

    bblk = jnp.concatenate([in_block(bbar_re), in_block(bbar_im)], axis=2)
    cblk = jnp.concatenate([out_block(c_re.astype(F32)),
                            -out_block(c_im.astype(F32))], axis=1)

    def lanes(a):
        a = a.reshape(n_slabs, 1, SSM_SLAB_STATE)
        a = jnp.concatenate([a, a], axis=2).reshape(1, -1)
        return jnp.broadcast_to(a, (SUBLANES, a.shape[1]))

    return (bblk.astype(BF16), cblk.astype(BF16), lanes(abar_re), lanes(abar_im))


MOE_CHUNK = 512
MOE_FIRST = 192
MOE_REST = MOE_CHUNK - MOE_FIRST
MOE_PIECES = ((0, MOE_FIRST), (MOE_FIRST, MOE_REST))
MOE_TILE = 512


def _glu_router_kernel(x_ref, act_ref, wa_ref, wb_ref, g_ref, w_ref, b_ref,
                       x3_ref, h_ref, pos_ref, post_ref, cnt_ref):
    tm, d = x_ref.shape
    parts = 2
    half = tm // parts

    def glu(r0):
        act = act_ref[r0:r0 + half, :]
        a = jnp.dot(act, wa_ref[...], preferred_element_type=F32)
        b = jnp.dot(act, wb_ref[...], preferred_element_type=F32)
        x3 = x_ref[r0:r0 + half, :] + a * _sigmoid(b)
        x3_ref[r0:r0 + half, :] = x3
        return x3

    def route(r0, x3):
        h = _rms(x3, g_ref[...])
        h_hi = h.astype(BF16)
        h_ref[r0:r0 + half, 0:d] = h_hi
        h_lo = (h - h_hi.astype(F32)).astype(BF16)
        p_hi = jnp.dot(h_hi, w_ref[...], preferred_element_type=F32)
        p_lo = jnp.dot(h_lo, w_ref[...], preferred_element_type=F32)
        logits = p_hi + pltpu.roll(p_hi, LANES - N_EXPERTS, axis=1) + p_lo + b_ref[...]
        lane = lax.broadcasted_iota(jnp.int32, logits.shape, 1)
        logits = jnp.where(lane < N_EXPERTS, logits, -jnp.inf)
        m1 = jnp.max(logits, axis=1, keepdims=True)
        i1 = jnp.min(jnp.where(logits == m1, lane, LANES), axis=1, keepdims=True)
        rest = jnp.where(lane == i1, -jnp.inf, logits)
        m2 = jnp.max(rest, axis=1, keepdims=True)
        i2 = jnp.min(jnp.where(rest == m2, lane, LANES), axis=1, keepdims=True)
        e2 = jnp.exp(m2 - m1)
        g1 = 1.0 / (1.0 + e2)
        g2 = e2 / (1.0 + e2)
        gate = jnp.where(lane == i1, g1, 0.0) + jnp.where(lane == i2, g2, 0.0)
        g_hi = gate.astype(BF16).astype(F32)
        g_r = gate - g_hi
        g_mid = g_r.astype(BF16).astype(F32)
        packed = jnp.where(lane < N_EXPERTS, g_hi,
                           jnp.where(lane < 2 * N_EXPERTS,
                                     pltpu.roll(g_mid, N_EXPERTS, axis=1),
                                     pltpu.roll(g_r - g_mid, 2 * N_EXPERTS, axis=1)))
        h_ref[r0:r0 + half, d:d + LANES] = packed.astype(BF16)
        return jnp.where((lane == i1) | (lane == i2), 1.0, 0.0)

    members = []
    x3 = glu(0)
    for p in range(parts):
        nxt = glu((p + 1) * half) if p + 1 < parts else None
        members.append(route(p * half, x3))
        x3 = nxt
    member = jnp.concatenate(members, axis=0)
    row = lax.broadcasted_iota(jnp.int32, member.shape, 0)
    c = member
    sh = 1
    while sh < tm:
        c = c + jnp.where(row >= sh, pltpu.roll(c, sh, axis=0), 0.0)
        sh *= 2
    pos = jnp.where(member > 0.0, c - member, -1.0)
    pos_ref[...] = pos
    post_ref[0] = pos.T[0:SUBLANES, :]
    cnt_ref[0] = jnp.broadcast_to(c[tm - 1:tm, :], (SUBLANES, LANES))


def _glu_router(x2d, act2d, wa, wb, gain, w_pad, b_pad):
    t, d = x2d.shape
    tm = MOE_CHUNK
    n_chunks = t // tm
    return pl.pallas_call(
        _glu_router_kernel,
        grid=(n_chunks,),
        in_specs=[
            pl.BlockSpec((tm, d), lambda i: (i, 0)),
            pl.BlockSpec((tm, d), lambda i: (i, 0)),
            pl.BlockSpec(wa.shape, lambda i: (0, 0)),
            pl.BlockSpec(wb.shape, lambda i: (0, 0)),
            pl.BlockSpec((1, d), lambda i: (0, 0)),
            pl.BlockSpec((d, LANES), lambda i: (0, 0)),
            pl.BlockSpec((1, LANES), lambda i: (0, 0)),
        ],
        out_specs=[
            pl.BlockSpec((tm, d), lambda i: (i, 0)),
            pl.BlockSpec((tm, d + LANES), lambda i: (i, 0)),
            pl.BlockSpec((tm, LANES), lambda i: (i, 0)),
            pl.BlockSpec((1, SUBLANES, tm), lambda i: (i, 0, 0)),
            pl.BlockSpec((1, SUBLANES, LANES), lambda i: (i, 0, 0)),
        ],
        out_shape=[
            jax.ShapeDtypeStruct((t, d), F32),
            jax.ShapeDtypeStruct((t, d + LANES), BF16),
            jax.ShapeDtypeStruct((t, LANES), F32),
            jax.ShapeDtypeStruct((n_chunks, SUBLANES, tm), F32),
            jax.ShapeDtypeStruct((n_chunks, SUBLANES, LANES), F32),
        ],
        compiler_params=_params("parallel"),
        name="glu_router",
    )(x2d, act2d, wa, wb, gain, w_pad, b_pad)


MOE_SPARE = max(MOE_FIRST, MOE_REST)


def _moe_tiles(n_tokens, row_multiple):
    n_chunks = n_tokens // MOE_CHUNK
    max_rows = (TOP_K * n_tokens + (SEG_ALIGN - 1) * N_EXPERTS * n_chunks
                + N_EXPERTS * (MOE_SPARE + MOE_TILE))
    unit = math.lcm(MOE_TILE, row_multiple)
    return -(-max_rows // unit) * (unit // MOE_TILE)


def _moe_layout(cnt):
    seg = (cnt + SEG_ALIGN - 1) // SEG_ALIGN * SEG_ALIGN
    used = jnp.sum(seg, axis=0)
    padded = (used + MOE_SPARE + MOE_TILE - 1) // MOE_TILE * MOE_TILE
    ends = jnp.cumsum(padded)
    start = ends - padded
    off = start[None, :] + jnp.cumsum(seg, axis=0) - seg
    return (off.reshape(-1).astype(jnp.int32), cnt.reshape(-1).astype(jnp.int32),
            ends.astype(jnp.int32), (start + used).astype(jnp.int32))


def _tile_expert(i, ends_ref):
    first_row = i * MOE_TILE
    e = 0
    for k in range(N_EXPERTS - 1):
        e = e + (first_row >= ends_ref[k]).astype(jnp.int32)
    return e


def _segment_copies(hbm_ref, buf_ref, sem_ref, off_ref, chunk, to_hbm, slot=()):
    copies = []
    for e in range(N_EXPERTS):
        off = pl.multiple_of(off_ref[chunk * N_EXPERTS + e], SEG_ALIGN)
        for piece, (first, rows) in enumerate(MOE_PIECES):
            hbm = hbm_ref.at[pl.ds(off + first, rows)]
            buf = buf_ref.at[(*slot, e, pl.ds(first, rows))]
            src, dst = (buf, hbm) if to_hbm else (hbm, buf)
            copies.append(pltpu.make_async_copy(src, dst, sem_ref.at[(*slot, e, piece)]))
    return copies


def _onehot(index, target):
    return jnp.where(index == target, 1.0, 0.0).astype(BF16)


def _gather_kernel(off_ref, cnt_ref, h_ref, post_ref, zeros_hbm, o_hbm, stage_ref,
                   sem_ref):
    del zeros_hbm
    c = pl.program_id(0)
    h = h_ref[...]
    copies = _segment_copies(o_hbm, stage_ref, sem_ref, off_ref, c, to_hbm=True)

    def piece_onehot(e, piece):
        first, rows = MOE_PIECES[piece]
        rank = first + lax.broadcasted_iota(jnp.int32, (rows, MOE_CHUNK), 0)
        return _onehot(rank.astype(F32), post_ref[0, e:e + 1, :])

    def move_piece(e, piece, onehot=None):
        first, rows = MOE_PIECES[piece]
        onehot = piece_onehot(e, piece) if onehot is None else onehot
        stage_ref[e, first:first + rows] = jnp.dot(
            onehot, h, preferred_element_type=F32).astype(BF16)
        copies[2 * e + piece].start()

    onehot = piece_onehot(0, 0)
    for e in range(N_EXPERTS):
        nxt = piece_onehot(e + 1, 0) if e + 1 < N_EXPERTS else None
        move_piece(e, 0, onehot)
        onehot = nxt
    for e in range(N_EXPERTS):
        @pl.when(cnt_ref[c * N_EXPERTS + e] > MOE_FIRST)
        def _(e=e):
            move_piece(e, 1)

    for e in range(N_EXPERTS):
        copies[2 * e].wait()

        @pl.when(cnt_ref[c * N_EXPERTS + e] > MOE_FIRST)
        def _(e=e):
            copies[2 * e + 1].wait()


def _gather(h2d, post, off, cnt, zero_buf):
    t, d = h2d.shape
    n_chunks = t // MOE_CHUNK
    grid_spec = pltpu.PrefetchScalarGridSpec(
        num_scalar_prefetch=2,
        grid=(n_chunks,),
        in_specs=[
            pl.BlockSpec((MOE_CHUNK, d), lambda i, off, cnt: (i, 0)),
            pl.BlockSpec((1, SUBLANES, MOE_CHUNK), lambda i, off, cnt: (i, 0, 0)),
            pl.BlockSpec(memory_space=pl.ANY),
        ],
        out_specs=pl.BlockSpec(memory_space=pl.ANY),
        scratch_shapes=[
            pltpu.VMEM((N_EXPERTS, MOE_CHUNK, d), BF16),
            pltpu.SemaphoreType.DMA((N_EXPERTS, 2)),
        ],
    )
    return pl.pallas_call(
        _gather_kernel,
        grid_spec=grid_spec,
        out_shape=jax.ShapeDtypeStruct(zero_buf.shape, BF16),
        input_output_aliases={4: 0},
        compiler_params=_params("arbitrary"),
        name="moe_gather",
    )(off, cnt, h2d, post, zero_buf)


def _expert_kernel(ends_ref, used_end_ref, x_ref, w1_ref, w3_ref, w2_ref, o_ref):
    i = pl.program_id(0)
    expert = _tile_expert(i, ends_ref)
    rows = jnp.clip(used_end_ref[expert] - i * MOE_TILE, 0, MOE_TILE)
    d = o_ref.shape[1]

    def run(n):
        row = lax.broadcasted_iota(jnp.int32, (n, x_ref.shape[1]), 0)
        x = jnp.where(row < rows, x_ref[0:n, :], jnp.zeros((), x_ref.dtype))
        lane = lax.broadcasted_iota(jnp.int32, (n, LANES), 1)
        mine = (lane < 3 * N_EXPERTS) & (jnp.bitwise_and(lane, N_EXPERTS - 1) == expert)
        gate = jnp.sum(jnp.where(mine, x[:, d:d + LANES].astype(F32), 0.0),
                       axis=1, keepdims=True)
        y = _swiglu(x[:, 0:d], w1_ref, w3_ref, w2_ref)
        o_ref[0:n, :] = (gate * y).astype(o_ref.dtype)

    half = MOE_TILE // 2

    @pl.when(rows > half)
    def _():
        run(MOE_TILE)

    @pl.when((rows > 0) & (rows <= half))
    def _():
        run(half)
        o_ref[half:MOE_TILE, :] = jnp.zeros((MOE_TILE - half, d), o_ref.dtype)

    @pl.when(rows == 0)
    def _():
        o_ref[...] = jnp.zeros_like(o_ref)


def _experts(x_sorted, w1, w3, w2, ends, used_end):
    assert N_EXPERTS & (N_EXPERTS - 1) == 0
    n_rows = x_sorted.shape[0]
    f = w1.shape[1]
    d = w2.shape[1]
    grid_spec = pltpu.PrefetchScalarGridSpec(
        num_scalar_prefetch=2,
        grid=(n_rows // MOE_TILE,),
        in_specs=[
            pl.BlockSpec((MOE_TILE, d + LANES), lambda i, ends, ue: (i, 0)),
            pl.BlockSpec((d, f), lambda i, ends, ue: (_tile_expert(i, ends), 0)),
            pl.BlockSpec((d, f), lambda i, ends, ue: (_tile_expert(i, ends), 0)),
            pl.BlockSpec((f, d), lambda i, ends, ue: (_tile_expert(i, ends), 0)),
        ],
        out_specs=pl.BlockSpec((MOE_TILE, d), lambda i, ends, ue: (i, 0)),
    )
    return pl.pallas_call(
        _expert_kernel,
        grid_spec=grid_spec,
        out_shape=jax.ShapeDtypeStruct((n_rows, d), BF16),
        compiler_params=_params("arbitrary"),
        name="moe_experts",
    )(ends, used_end, x_sorted, w1, w3, w2)


def _combine_kernel(off_ref, cnt_ref, x_ref, pos_ref, fg_ref, y_hbm,
                    o_ref, ybuf_ref, acc_ref, sem_ref):
    c = pl.program_id(0)
    n_chunks = pl.num_programs(0)
    slot = c % 2

    def fetch(chunk, sl, start):
        copies = _segment_copies(y_hbm, ybuf_ref, sem_ref, off_ref, chunk,
                                 to_hbm=False, slot=(sl,))
        for e in range(N_EXPERTS):
            first, second = copies[2 * e], copies[2 * e + 1]
            if start:
                first.start()
            else:
                first.wait()

            @pl.when(cnt_ref[chunk * N_EXPERTS + e] > MOE_FIRST)
            def _(second=second):
                if start:
                    second.start()
                else:
                    second.wait()

    @pl.when(c == 0)
    def _():
        fetch(c, slot, True)

    @pl.when(c + 1 < n_chunks)
    def _():
        fetch(c + 1, 1 - slot, True)

    fetch(c, slot, False)

    def piece_onehot(e, piece):
        first, rows = MOE_PIECES[piece]
        rank = first + lax.broadcasted_iota(jnp.int32, (MOE_CHUNK, rows), 1)
        return _onehot(rank.astype(F32), pos_ref[:, e:e + 1])

    def piece_rows(e, piece, onehot=None):
        first, rows = MOE_PIECES[piece]
        onehot = piece_onehot(e, piece) if onehot is None else onehot
        return jnp.dot(onehot, ybuf_ref[slot, e, first:first + rows],
                       preferred_element_type=F32)

    acc = x_ref[...]
    onehot = piece_onehot(0, 0)
    for e in range(N_EXPERTS):
        nxt = piece_onehot(e + 1, 0) if e + 1 < N_EXPERTS else None
        acc = acc + piece_rows(e, 0, onehot)
        onehot = nxt
    acc_ref[...] = acc
    for e in range(N_EXPERTS):
        @pl.when(cnt_ref[c * N_EXPERTS + e] > MOE_FIRST)
        def _(e=e):
            acc_ref[...] += piece_rows(e, 1)

    o_ref[...] = _rms(acc_ref[...], fg_ref[...])


def _combine(x2d, pos, final_gain, y_sorted, off, cnt):
    t, d = x2d.shape
    n_chunks = t // MOE_CHUNK
    grid_spec = pltpu.PrefetchScalarGridSpec(
        num_scalar_prefetch=2,
        grid=(n_chunks,),
        in_specs=[
            pl.BlockSpec((MOE_CHUNK, d), lambda i, off, cnt: (i, 0)),
            pl.BlockSpec((MOE_CHUNK, LANES), lambda i, off, cnt: (i, 0)),
            pl.BlockSpec((1, d), lambda i, off, cnt: (0, 0)),
            pl.BlockSpec(memory_space=pl.ANY),
        ],
        out_specs=pl.BlockSpec((MOE_CHUNK, d), lambda i, off, cnt: (i, 0)),
        scratch_shapes=[
            pltpu.VMEM((2, N_EXPERTS, MOE_CHUNK, d), BF16),
            pltpu.VMEM((MOE_CHUNK, d), F32),
            pltpu.SemaphoreType.DMA((2, N_EXPERTS, 2)),
        ],
    )
    return pl.pallas_call(
        _combine_kernel,
        grid_spec=grid_spec,
        out_shape=jax.ShapeDtypeStruct((t, d), F32),
        compiler_params=_params("arbitrary"),
        name="moe_combine",
    )(off, cnt, x2d, pos, final_gain, y_sorted)


def _glu_moe(x2d, act2d, wa, wb, gain, router_w, router_b, w1, w3, w2, final_gain,
             zero_buf):
    t, d = x2d.shape
    rw_hi = router_w.astype(BF16)
    rw_lo = (router_w.astype(F32) - rw_hi.astype(F32)).astype(BF16)
    rw = jnp.pad(jnp.concatenate([rw_hi, rw_lo], axis=1),
                 ((0, 0), (0, LANES - 2 * N_EXPERTS)))
    rb = jnp.pad(router_b.astype(F32), (0, LANES - N_EXPERTS)).reshape(1, LANES)
    x3, h, pos, post, cnt = _glu_router(x2d, act2d, wa, wb, gain, rw, rb)
    cnt = cnt[:, 0, :N_EXPERTS].astype(jnp.int32)
    off, cnt_flat, ends, used_end = _moe_layout(cnt)
    h_sorted = _gather(h, post, off, cnt_flat, zero_buf)
    y_sorted = _experts(h_sorted, w1, w3, w2, ends, used_end)
    return _combine(x3, pos, final_gain, y_sorted, off, cnt_flat)


def _even_weights(w_in, b_forget):
    n = w_in.shape[1]
    w_all = jnp.pad(w_in.astype(BF16), ((0, 0), (0, -n % LANES)))
    bias = jnp.pad(b_forget.astype(F32), (0, LANES - FOX_HEADS)).reshape(1, LANES)
    qw = FOX_HEADS * LANES
    place = np.zeros((LANES, 2 * qw), np.float32)
    for hh in range(FOX_HEADS):
        for piece in range(3):
            place[piece * FOX_HEADS + hh, hh * LANES + BIAS_LANE + piece] = 1.0
            place[piece * FOX_HEADS + hh, qw + hh * LANES + ONES_LANE + piece] = -1.0
    return w_all, bias, jnp.asarray(place, BF16)


def kernel(x, even_mix_norm, even_w_in, even_b_forget, even_w_pool, even_pool_scale, even_w_out, even_ffn_norm, even_ffn_w1, even_ffn_w3, even_ffn_w2, odd_mix_norm, odd_w_in, ssm_a_re, ssm_a_im, ssm_log_dt, ssm_b_re, ssm_b_im, ssm_c_re, ssm_c_im, ssm_d, odd_w_glu_a, odd_w_glu_b, odd_moe_norm, router_w, router_b, expert_w1, expert_w3, expert_w2, final_norm):
    b, s, d = x.shape
    t = b * s
    assert b == SUBLANES, "the S5 recurrence keeps one batch row per sublane"
    x2d = x.reshape(t, d)
    row = lambda v: v.reshape(1, -1).astype(F32)

    w_all, bias, place = _even_weights(even_w_in[0], even_b_forget[0])
    (q_aug, k_aug, v, p_in), casted = _even_inproj(
        x2d, row(even_mix_norm[0]), w_all, bias, place,
        [even_ffn_w1[0], even_ffn_w3[0], even_ffn_w2[0], even_w_out[0],
         even_w_pool[0].reshape(-1, POOL_GROUP_DIM),
         odd_w_in[0], odd_w_glu_a[0], odd_w_glu_b[0]],
        seq=s, tm=512)
    ffn_w1, ffn_w3, ffn_w2, w_out, w_pool, s5_w_in, glu_a, glu_b = casted
    n_e, _, f = expert_w1[0].shape
    att, (ew1, ew3, ew2) = _attention(
        q_aug, k_aug, v,
        [expert_w1[0].reshape(n_e * d, f), expert_w3[0].reshape(n_e * d, f),
         expert_w2[0].reshape(n_e * f, d)],
        batch=b, seq=s, tq=256)
    x2 = _even_tail(x2d, att, p_in, w_pool.reshape(even_w_pool[0].shape),
                    row(even_pool_scale[0]), w_out, row(even_ffn_norm[0]),
                    ffn_w1, ffn_w3, ffn_w2, seq=s, tm=512)

    bblk, cblk, a_re, a_im = _s5_coefficients(
        ssm_a_re[0], ssm_a_im[0], ssm_log_dt[0], ssm_b_re[0], ssm_b_im[0],
        ssm_c_re[0], ssm_c_im[0])
    moe_rows = _moe_tiles(t, (s // S5_STEP) * SEG_ALIGN) * MOE_TILE
    g, zero_buf = _s5_mixer(x2.reshape(b, s, d), row(odd_mix_norm[0]),
                            s5_w_in, bblk, cblk, a_re, a_im,
                            row(ssm_d[0]), tc=S5_STEP,
                            zero_shape=(moe_rows, d + LANES))
    out = _glu_moe(x2, g.reshape(t, d), glu_a, glu_b, row(odd_moe_norm[0]),
                   router_w[0], router_b[0], ew1, ew3, ew2,
                   row(final_norm), zero_buf)
    return out.reshape(b, s, d)
```

```python
import functools
import math

import numpy as np
import jax
import jax.numpy as jnp
from jax import lax
from jax.experimental import pallas as pl
from jax.experimental.pallas import tpu as pltpu

F32 = jnp.float32
BF16 = jnp.bfloat16

EPS = 1e-6
NEG_INF = -1e30
LANES = 128
SUBLANES = 8
SEG_ALIGN = 16
VMEM_LIMIT = 56 * 1024 * 1024

FOX_HEADS = 8
FOX_HEAD_DIM = 64
FOX_WIDTH = FOX_HEADS * FOX_HEAD_DIM
POOL_WINDOWS = (2, 4, 8, 16)
POOL_GROUP_DIM = 128
POOL_WIDTH = len(POOL_WINDOWS) * POOL_GROUP_DIM
POOL_HALO = 16
SSM_GROUP = 16
SSM_STATE = 64
SSM_SLAB_GROUPS = LANES // SSM_GROUP
SSM_SLAB_STATE = SSM_SLAB_GROUPS * SSM_STATE
S5_STEP = 32
N_EXPERTS = 8
TOP_K = 2

BIAS_LANE = FOX_HEAD_DIM
ONES_LANE = FOX_HEAD_DIM + 3
DENOM_LANE = FOX_HEAD_DIM


def _params(*sem):
    return pltpu.CompilerParams(dimension_semantics=sem,
                                vmem_limit_bytes=VMEM_LIMIT)


def _rms(x, g):
    ms = jnp.mean(x * x, axis=-1, keepdims=True)
    return x * lax.rsqrt(ms + EPS) * g


def _sigmoid(x):
    return 1.0 / (1.0 + jnp.exp(-x))


def _lane_range_ones(lo, hi):
    lane = lax.broadcasted_iota(jnp.int32, (1, LANES), 1)
    return jnp.where((lane >= lo) & (lane < hi), 1.0, 0.0).astype(F32)


def _cast_specs(weights, steps, step_index):
    specs = []
    for w in weights:
        rows = -(-w.shape[0] // (steps * SEG_ALIGN)) * SEG_ALIGN
        last = -(-w.shape[0] // rows) - 1
        specs.append(pl.BlockSpec(
            (rows, w.shape[1]),
            lambda *idx, last=last: (jnp.minimum(step_index(*idx), last), 0)))
    return specs


def _cast_slabs(src_refs, dst_refs):
    for src_ref, dst_ref in zip(src_refs, dst_refs):
        dst_ref[...] = src_ref[...].astype(dst_ref.dtype)


def _even_inproj_kernel(x_ref, g_ref, w_ref, bias_ref, place_ref, *rest,
                        tiles_per_seq, n_cast):
    cast_in, rest = rest[:n_cast], rest[n_cast:]
    q_ref, k_ref, v_ref, p_ref = rest[:4]
    cast_out, carry_ref = rest[4:4 + n_cast], rest[4 + n_cast]
    _cast_slabs(cast_in, cast_out)
    i = pl.program_id(0)
    tm = x_ref.shape[0]
    parts = 2
    rows = tm // parts
    fw = FOX_WIDTH
    qw = FOX_HEADS * LANES

    @pl.when(i % tiles_per_seq == 0)
    def _():
        carry_ref[...] = jnp.zeros_like(carry_ref)

    def project(r0):
        h = _rms(x_ref[r0:r0 + rows, :], g_ref[...]).astype(BF16)
        return jnp.dot(h, w_ref[...], preferred_element_type=F32)

    def finish(r0, z, carry):
        out = slice(r0, r0 + rows)
        p_ref[out, :] = z[:, 3 * fw + FOX_HEADS:3 * fw + FOX_HEADS + POOL_WIDTH]
        fg = z[:, 3 * fw:3 * fw + LANES] + bias_ref[...]
        lf = jnp.minimum(fg, 0.0) - jnp.log1p(jnp.exp(-jnp.abs(fg)))
        row = lax.broadcasted_iota(jnp.int32, lf.shape, 0)
        c = lf
        sh = 1
        while sh < rows:
            c = c + jnp.where(row >= sh, pltpu.roll(c, sh, axis=0), 0.0)
            sh *= 2
        c = c + carry
        hi = c.astype(BF16).astype(F32)
        r1 = c - hi
        mid = r1.astype(BF16).astype(F32)
        lo = r1 - mid
        lane = lax.broadcasted_iota(jnp.int32, (rows, LANES), 1)
        packed = jnp.where(lane < FOX_HEADS, hi,
                           jnp.where(lane < 2 * FOX_HEADS,
                                     pltpu.roll(mid, FOX_HEADS, axis=1),
                                     pltpu.roll(lo, 2 * FOX_HEADS, axis=1)))
        placed = jnp.dot(packed.astype(BF16), place_ref[...],
                         preferred_element_type=F32)
        low = lane < FOX_HEAD_DIM
        ones_q = _lane_range_ones(ONES_LANE, ONES_LANE + 3)
        ones_k = _lane_range_ones(BIAS_LANE, BIAS_LANE + 3)
        ones_v = _lane_range_ones(DENOM_LANE, DENOM_LANE + 1)

        def head_lanes(base, hh):
            pair = z[:, base + (hh // 2) * LANES:base + (hh // 2 + 1) * LANES]
            return pltpu.roll(pair, FOX_HEAD_DIM, axis=1) if hh % 2 else pair

        q_scale = FOX_HEAD_DIM ** -0.5
        for hh in range(FOX_HEADS):
            sl = slice(hh * LANES, (hh + 1) * LANES)
            q_ref[out, sl] = jnp.where(low, head_lanes(0, hh) * q_scale,
                                       placed[:, sl] + ones_q).astype(BF16)
            k_ref[out, sl] = jnp.where(
                low, head_lanes(fw, hh),
                placed[:, qw + hh * LANES:qw + (hh + 1) * LANES] + ones_k).astype(BF16)
            v_ref[out, sl] = jnp.where(low, head_lanes(2 * fw, hh), ones_v).astype(BF16)
        return c[rows - 1:rows, :]

    carry = carry_ref[0:1, :]
    z = project(0)
    for part in range(parts):
        nxt = project((part + 1) * rows) if part + 1 < parts else None
        carry = finish(part * rows, z, carry)
        z = nxt
    carry_ref[...] = jnp.broadcast_to(carry, carry_ref.shape)


def _even_inproj(x2d, gain, w_all, bias, place, cast_weights, *, seq, tm):
    t, d = x2d.shape
    n = w_all.shape[1]
    qw = FOX_HEADS * LANES
    cast_specs = _cast_specs(cast_weights, t // tm, lambda i: i)
    kern = functools.partial(_even_inproj_kernel, tiles_per_seq=seq // tm,
                             n_cast=len(cast_weights))
    outs = pl.pallas_call(
        kern,
        grid=(t // tm,),
        in_specs=[
            pl.BlockSpec((tm, d), lambda i: (i, 0)),
            pl.BlockSpec((1, d), lambda i: (0, 0)),
            pl.BlockSpec((d, n), lambda i: (0, 0)),
            pl.BlockSpec((1, LANES), lambda i: (0, 0)),
            pl.BlockSpec(place.shape, lambda i: (0, 0)),
            *cast_specs,
        ],
        out_specs=[
            pl.BlockSpec((tm, qw), lambda i: (i, 0)),
            pl.BlockSpec((tm, qw), lambda i: (i, 0)),
            pl.BlockSpec((tm, qw), lambda i: (i, 0)),
            pl.BlockSpec((tm, POOL_WIDTH), lambda i: (i, 0)),
            *cast_specs,
        ],
        out_shape=[
            jax.ShapeDtypeStruct((t, qw), BF16),
            jax.ShapeDtypeStruct((t, qw), BF16),
            jax.ShapeDtypeStruct((t, qw), BF16),
            jax.ShapeDtypeStruct((t, POOL_WIDTH), F32),
        ] + [jax.ShapeDtypeStruct(w.shape, BF16) for w in cast_weights],
        scratch_shapes=[pltpu.VMEM((SUBLANES, LANES), F32)],
        compiler_params=_params("arbitrary"),
        name="even_inproj",
    )(x2d, gain, w_all, bias, place, *cast_weights)
    return outs[:4], outs[4:]


def _dot_nt(a, b):
    return lax.dot_general(a, b, (((1,), (1,)), ((), ())),
                           preferred_element_type=F32)


def _attn_kernel(q_ref, k_ref, v_ref, *rest, tq, n_cast):
    cast_in, o_ref, cast_out = rest[:n_cast], rest[n_cast], rest[n_cast + 1:]
    seq = q_ref.shape[0]
    row = lax.broadcasted_iota(jnp.int32, (tq, tq), 0)
    col = lax.broadcasted_iota(jnp.int32, (tq, tq), 1)
    causal = col <= row
    lane = lax.broadcasted_iota(jnp.int32, (tq, LANES), 1)
    work = [(qi, hh) for qi in range(seq // tq) for hh in range(2)]

    def scores(qi, hh):
        r0 = qi * tq
        sl = slice(hh * LANES, (hh + 1) * LANES)
        q = q_ref[r0:r0 + tq, sl]
        s_diag = jnp.where(causal, _dot_nt(q, k_ref[r0:r0 + tq, sl]), NEG_INF)
        s_past = _dot_nt(q, k_ref[0:r0, sl]) if qi > 0 else None
        return s_diag, s_past

    ahead = 2
    queue = [scores(*item) for item in work[:ahead]]
    first_head = None
    for n, (qi, hh) in enumerate(work):
        r0 = qi * tq
        sl = slice(hh * LANES, (hh + 1) * LANES)
        s_diag, s_past = queue.pop(0)
        if n + ahead < len(work):
            queue.append(scores(*work[n + ahead]))
        m = jnp.max(s_diag, axis=1, keepdims=True)
        if qi > 0:
            m = jnp.maximum(m, jnp.max(s_past, axis=1, keepdims=True))
        acc = jnp.dot(jnp.exp(s_diag - m).astype(BF16), v_ref[r0:r0 + tq, sl],
                      preferred_element_type=F32)
        if qi > 0:
            acc = acc + jnp.dot(jnp.exp(s_past - m).astype(BF16), v_ref[0:r0, sl],
                                preferred_element_type=F32)
        out = acc / acc[:, DENOM_LANE:DENOM_LANE + 1]
        if hh == 0:
            first_head = out
        else:
            o_ref[r0:r0 + tq, :] = jnp.where(
                lane < FOX_HEAD_DIM, first_head,
                pltpu.roll(out, FOX_HEAD_DIM, axis=1)).astype(o_ref.dtype)
    _cast_slabs(cast_in, cast_out)


def _attention(q_aug, k_aug, v_aug, cast_weights, *, batch, seq, tq):
    t = q_aug.shape[0]
    pairs = FOX_HEADS // 2
    steps = batch * pairs
    spec = pl.BlockSpec((seq, 2 * LANES), lambda b, hp: (b, hp))
    out_spec = pl.BlockSpec((seq, LANES), lambda b, hp: (b, hp))
    cast_specs = _cast_specs(cast_weights, steps, lambda b, hp: b * pairs + hp)
    outs = pl.pallas_call(
        functools.partial(_attn_kernel, tq=tq, n_cast=len(cast_weights)),
        grid=(batch, pairs),
        in_specs=[spec, spec, spec, *cast_specs],
        out_specs=[out_spec, *cast_specs],
        out_shape=[jax.ShapeDtypeStruct((t, FOX_WIDTH), BF16)]
        + [jax.ShapeDtypeStruct(w.shape, BF16) for w in cast_weights],
        compiler_params=_params("parallel", "parallel"),
        name="fox_attention",
    )(q_aug, k_aug, v_aug, *cast_weights)
    return outs[0], outs[1:]


MXU_TILE = 256


def _swiglu_partial(h, w1, w3, w2):
    a = jnp.dot(h, w1, preferred_element_type=F32)
    b = jnp.dot(h, w3, preferred_element_type=F32)
    act = a * _sigmoid(a) * b
    return jnp.dot(act.astype(BF16), w2, preferred_element_type=F32)


def _ff_splits(f):
    cut = -(-(f // MXU_TILE) // 2) * MXU_TILE
    return ((0, cut), (cut, f)) if 0 < cut < f else ((0, f),)


def _swiglu(h, w1_ref, w3_ref, w2_ref):
    y = None
    for lo, hi in _ff_splits(w1_ref.shape[1]):
        part = _swiglu_partial(h, w1_ref[:, lo:hi], w3_ref[:, lo:hi], w2_ref[lo:hi, :])
        y = part if y is None else y + part
    return y


def _even_tail_kernel(x_ref, att_ref, p_ref, halo_ref, wpool_ref, scale_ref,
                      wo_att_ref, wo_pool_ref, g_ref, w1_ref, w3_ref, w2_ref, o_ref,
                      *, tiles_per_seq):
    i = pl.program_id(0)
    tm = x_ref.shape[0]
    tile_in_seq = i % tiles_per_seq
    p = p_ref[...]
    halo = jnp.where(tile_in_seq == 0, 0.0, halo_ref[...])
    ext = jnp.concatenate([halo, p], axis=0)
    pos = tile_in_seq * tm + lax.broadcasted_iota(jnp.int32, (tm, 1), 0)
    count = (pos + 1).astype(F32)
    mixed = []
    for gi, w in enumerate(POOL_WINDOWS):
        sl = slice(gi * POOL_GROUP_DIM, (gi + 1) * POOL_GROUP_DIM)
        acc = ext[:, sl]
        sh = 1
        while sh < w:
            acc = acc + pltpu.roll(acc, sh, axis=0)
            sh *= 2
        mean = acc[POOL_HALO:, :] / jnp.minimum(count, float(w))
        pooled = (mean - p[:, sl]).astype(BF16)
        mixed.append(jnp.dot(pooled, wpool_ref[gi], preferred_element_type=F32))
    pool = (jnp.concatenate(mixed, axis=1) * scale_ref[...]).astype(BF16)
    y = jnp.dot(att_ref[...], wo_att_ref[...], preferred_element_type=F32)
    y = y + jnp.dot(pool, wo_pool_ref[...], preferred_element_type=F32)
    x1 = x_ref[...] + y
    h = _rms(x1, g_ref[...]).astype(BF16)
    o_ref[...] = x1 + _swiglu(h, w1_ref, w3_ref, w2_ref)


def _even_tail(x2d, att, p_in, w_pool, pool_scale, w_out, ffn_gain, w1, w3, w2,
               *, seq, tm):
    t, d = x2d.shape
    resident = lambda w: pl.BlockSpec(w.shape, lambda i: (0,) * w.ndim,
                                      pipeline_mode=pl.Buffered(1))
    wo_att = w_out[:FOX_WIDTH]
    wo_pool = w_out[FOX_WIDTH:]
    halo_blocks = tm // POOL_HALO
    kern = functools.partial(_even_tail_kernel, tiles_per_seq=seq // tm)
    return pl.pallas_call(
        kern,
        grid=(t // tm,),
        in_specs=[
            pl.BlockSpec((tm, d), lambda i: (i, 0)),
            pl.BlockSpec((tm, FOX_WIDTH), lambda i: (i, 0)),
            pl.BlockSpec((tm, POOL_WIDTH), lambda i: (i, 0)),
            pl.BlockSpec((POOL_HALO, POOL_WIDTH),
                         lambda i: (jnp.maximum(i * halo_blocks - 1, 0), 0)),
            resident(w_pool),
            pl.BlockSpec((1, POOL_WIDTH), lambda i: (0, 0)),
            resident(wo_att), resident(wo_pool),
            pl.BlockSpec((1, d), lambda i: (0, 0)),
            resident(w1), resident(w3), resident(w2),
        ],
        out_specs=pl.BlockSpec((tm, d), lambda i: (i, 0)),
        out_shape=jax.ShapeDtypeStruct((t, d), F32),
        compiler_params=_params("parallel"),
        name="even_tail",
    )(x2d, att, p_in, p_in, w_pool, pool_scale, wo_att, wo_pool, ffn_gain, w1, w3, w2)


def _gelu_tanh(x):
    c = math.sqrt(2.0 / math.pi)
    return 0.5 * x * (1.0 + jnp.tanh(c * (x + 0.044715 * (x * x * x))))


def _s5_kernel(x_ref, g_ref, win_ref, perm_ref, permt_ref, bblk_ref, cblk_ref,
               are_ref, aim_ref, d_ref, o_ref, zero_ref, xs_ref, st_ref, u_ref, act_ref):
    nb, tc, d = x_ref.shape
    rows = nb * tc
    n_slabs = d // LANES
    sw = 2 * SSM_SLAB_STATE

    @pl.when(pl.program_id(0) == 0)
    def _():
        st_ref[...] = jnp.zeros_like(st_ref)

    x = x_ref[...].reshape(rows, d)
    h = _rms(x, g_ref[...]).astype(BF16)
    h_tb = jnp.dot(perm_ref[...], h, preferred_element_type=F32).astype(BF16)
    u_ref[...] = jnp.dot(h_tb, win_ref[...], preferred_element_type=F32)
    hs = SSM_SLAB_STATE

    def input_matmul(s):
        xs_ref[:, s * sw:(s + 1) * sw] = jnp.dot(
            u_ref[:, s * LANES:(s + 1) * LANES].astype(BF16), bblk_ref[s],
            preferred_element_type=F32)

    def output_matmul(s):
        sl = slice(s * LANES, (s + 1) * LANES)
        y = jnp.dot(xs_ref[:, s * sw:(s + 1) * sw].astype(BF16), cblk_ref[s],
                    preferred_element_type=F32) + d_ref[:, sl] * u_ref[:, sl]
        act_ref[:, sl] = _gelu_tanh(y).astype(BF16)

    ahead = 2
    for s in range(min(ahead, n_slabs)):
        input_matmul(s)
    for s in range(n_slabs):
        lo = s * sw
        if s + ahead < n_slabs:
            input_matmul(s + ahead)
        a_r = are_ref[:, lo:lo + hs]
        a_i = aim_ref[:, lo:lo + hs]
        x_r = st_ref[:, lo:lo + hs]
        x_i = st_ref[:, lo + hs:lo + sw]
        for t in range(tc):
            r0 = t * SUBLANES
            new_r = a_r * x_r - a_i * x_i + xs_ref[r0:r0 + SUBLANES, lo:lo + hs]
            new_i = a_r * x_i + a_i * x_r + xs_ref[r0:r0 + SUBLANES, lo + hs:lo + sw]
            xs_ref[r0:r0 + SUBLANES, lo:lo + hs] = new_r
            xs_ref[r0:r0 + SUBLANES, lo + hs:lo + sw] = new_i
            x_r, x_i = new_r, new_i
        st_ref[:, lo:lo + hs] = x_r
        st_ref[:, lo + hs:lo + sw] = x_i
        if s >= 1:
            output_matmul(s - 1)
    output_matmul(n_slabs - 1)
    g_bt = jnp.dot(permt_ref[...], act_ref[...], preferred_element_type=F32)
    o_ref[...] = g_bt.reshape(nb, tc, d).astype(o_ref.dtype)
    zero_ref[...] = jnp.zeros_like(zero_ref)


def _s5_mixer(x3d, gain, w_in, bblk, cblk, a_re, a_im, d_skip, *, tc, zero_shape):
    nb, seq, d = x3d.shape
    steps = seq // tc
    zero_rows = zero_shape[0] // steps
    assert zero_rows * steps == zero_shape[0] and zero_rows % SEG_ALIGN == 0
    rows = nb * tc
    n_slabs = d // LANES
    sw = 2 * SSM_SLAB_STATE
    r = np.arange(rows)
    perm = np.zeros((rows, rows), np.float32)
    perm[(r % tc) * nb + r // tc, r] = 1.0
    perm_j = jnp.asarray(perm, BF16)
    permt_j = jnp.asarray(perm.T, BF16)
    kern = _s5_kernel
    full2 = lambda a: pl.BlockSpec(a.shape, lambda i: (0, 0))
    full3 = lambda a: pl.BlockSpec(a.shape, lambda i: (0, 0, 0))
    return pl.pallas_call(
        kern,
        grid=(seq // tc,),
        in_specs=[
            pl.BlockSpec((nb, tc, d), lambda i: (0, i, 0)),
            full2(gain), full2(w_in), full2(perm_j), full2(permt_j),
            full3(bblk), full3(cblk), full2(a_re), full2(a_im), full2(d_skip),
        ],
        out_specs=[pl.BlockSpec((nb, tc, d), lambda i: (0, i, 0)),
                   pl.BlockSpec((zero_rows, zero_shape[1]), lambda i: (i, 0))],
        out_shape=[jax.ShapeDtypeStruct((nb, seq, d), BF16),
                   jax.ShapeDtypeStruct(zero_shape, BF16)],
        scratch_shapes=[
            pltpu.VMEM((rows, n_slabs * sw), F32),
            pltpu.VMEM((SUBLANES, n_slabs * sw), F32),
            pltpu.VMEM((rows, d), F32),
            pltpu.VMEM((rows, d), BF16),
        ],
        compiler_params=_params("arbitrary"),
        name="s5_mixer",
    )(x3d, gain, w_in, perm_j, permt_j, bblk, cblk, a_re, a_im, d_skip)


def _s5_coefficients(a_re, a_im, log_dt, b_re, b_im, c_re, c_im):
    dt = jnp.exp(log_dt.astype(F32))[:, None]
    ar = a_re.astype(F32)
    ai = a_im.astype(F32)
    mag = jnp.exp(ar * dt)
    abar_re = mag * jnp.cos(ai * dt)
    abar_im = mag * jnp.sin(ai * dt)
    den = ar * ar + ai * ai
    nr = abar_re - 1.0
    ni = abar_im
    coef_re = (nr * ar + ni * ai) / den
    coef_im = (ni * ar - nr * ai) / den
    br = b_re.astype(F32)
    bi = b_im.astype(F32)
    bbar_re = coef_re[..., None] * br - coef_im[..., None] * bi
    bbar_im = coef_re[..., None] * bi + coef_im[..., None] * br
    n_groups = ar.shape[0]
    n_slabs = n_groups // SSM_SLAB_GROUPS
    row_group = np.arange(LANES)[:, None] // SSM_GROUP
    col_group = np.arange(SSM_SLAB_STATE)[None, :] // SSM_STATE
    in_mask = jnp.asarray(row_group == col_group, F32)

    def in_block(bb):
        bb = bb.reshape(n_slabs, SSM_SLAB_GROUPS, SSM_STATE, SSM_GROUP)
        bb = bb.transpose(0, 1, 3, 2).reshape(n_slabs, LANES, SSM_STATE)
        return jnp.tile(bb, (1, 1, SSM_SLAB_GROUPS)) * in_mask

    def out_block(cc):
        cc = cc.reshape(n_slabs, SSM_SLAB_GROUPS, SSM_GROUP, SSM_STATE)
        cc = cc.transpose(0, 1, 3, 2).reshape(n_slabs, SSM_SLAB_STATE, SSM_GROUP)
        return jnp.tile(cc, (1, 1, SSM_SLAB_GROUPS)) * in_mask.T

    bblk = jnp.concatenate([in_block(bbar_re), in_block(bbar_im)], axis=2)
    cblk = jnp.concatenate([out_block(c_re.astype(F32)),
                            -out_block(c_im.astype(F32))], axis=1)

    def lanes(a):
        a = a.reshape(n_slabs, 1, SSM_SLAB_STATE)
        a = jnp.concatenate([a, a], axis=2).reshape(1, -1)
        return jnp.broadcast_to(a, (SUBLANES, a.shape[1]))

    return (bblk.astype(BF16), cblk.astype(BF16), lanes(abar_re), lanes(abar_im))


MOE_CHUNK = 512
MOE_FIRST = 192
MOE_REST = MOE_CHUNK - MOE_FIRST
MOE_PIECES = ((0, MOE_FIRST), (MOE_FIRST, MOE_REST))
MOE_TILE = 512


def _glu_router_kernel(x_ref, act_ref, wa_ref, wb_ref, g_ref, w_ref, b_ref,
                       x3_ref, h_ref, pos_ref, post_ref, cnt_ref):
    tm, d = x_ref.shape
    parts = 2
    half = tm // parts

    def glu(r0):
        act = act_ref[r0:r0 + half, :]
        a = jnp.dot(act, wa_ref[...], preferred_element_type=F32)
        b = jnp.dot(act, wb_ref[...], preferred_element_type=F32)
        x3 = x_ref[r0:r0 + half, :] + a * _sigmoid(b)
        x3_ref[r0:r0 + half, :] = x3
        return x3

    def route(r0, x3):
        h = _rms(x3, g_ref[...])
        h_hi = h.astype(BF16)
        h_ref[r0:r0 + half, 0:d] = h_hi
        h_lo = (h - h_hi.astype(F32)).astype(BF16)
        p_hi = jnp.dot(h_hi, w_ref[...], preferred_element_type=F32)
        p_lo = jnp.dot(h_lo, w_ref[...], preferred_element_type=F32)
        logits = p_hi + pltpu.roll(p_hi, LANES - N_EXPERTS, axis=1) + p_lo + b_ref[...]
        lane = lax.broadcasted_iota(jnp.int32, logits.shape, 1)
        logits = jnp.where(lane < N_EXPERTS, logits, -jnp.inf)
        m1 = jnp.max(logits, axis=1, keepdims=True)
        i1 = jnp.min(jnp.where(logits == m1, lane, LANES), axis=1, keepdims=True)
        rest = jnp.where(lane == i1, -jnp.inf, logits)
        m2 = jnp.max(rest, axis=1, keepdims=True)
        i2 = jnp.min(jnp.where(rest == m2, lane, LANES), axis=1, keepdims=True)
        e2 = jnp.exp(m2 - m1)
        g1 = 1.0 / (1.0 + e2)
        g2 = e2 / (1.0 + e2)
        gate = jnp.where(lane == i1, g1, 0.0) + jnp.where(lane == i2, g2, 0.0)
        g_hi = gate.astype(BF16).astype(F32)
        g_r = gate - g_hi
        g_mid = g_r.astype(BF16).astype(F32)
        packed = jnp.where(lane < N_EXPERTS, g_hi,
                           jnp.where(lane < 2 * N_EXPERTS,
                                     pltpu.roll(g_mid, N_EXPERTS, axis=1),
                                     pltpu.roll(g_r - g_mid, 2 * N_EXPERTS, axis=1)))
        h_ref[r0:r0 + half, d:d + LANES] = packed.astype(BF16)
        return jnp.where((lane == i1) | (lane == i2), 1.0, 0.0)

    members = []
    x3 = glu(0)
    for p in range(parts):
        nxt = glu((p + 1) * half) if p + 1 < parts else None
        members.append(route(p * half, x3))
        x3 = nxt
    member = jnp.concatenate(members, axis=0)
    row = lax.broadcasted_iota(jnp.int32, member.shape, 0)
    c = member
    sh = 1
    while sh < tm:
        c = c + jnp.where(row >= sh, pltpu.roll(c, sh, axis=0), 0.0)
        sh *= 2
    pos = jnp.where(member > 0.0, c - member, -1.0)
    pos_ref[...] = pos
    post_ref[0] = pos.T[0:SUBLANES, :]
    cnt_ref[0] = jnp.broadcast_to(c[tm - 1:tm, :], (SUBLANES, LANES))


def _glu_router(x2d, act2d, wa, wb, gain, w_pad, b_pad):
    t, d = x2d.shape
    tm = MOE_CHUNK
    n_chunks = t // tm
    return pl.pallas_call(
        _glu_router_kernel,
        grid=(n_chunks,),
        in_specs=[
            pl.BlockSpec((tm, d), lambda i: (i, 0)),
            pl.BlockSpec((tm, d), lambda i: (i, 0)),
            pl.BlockSpec(wa.shape, lambda i: (0, 0)),
            pl.BlockSpec(wb.shape, lambda i: (0, 0)),
            pl.BlockSpec((1, d), lambda i: (0, 0)),
            pl.BlockSpec((d, LANES), lambda i: (0, 0)),
            pl.BlockSpec((1, LANES), lambda i: (0, 0)),
        ],
        out_specs=[
            pl.BlockSpec((tm, d), lambda i: (i, 0)),
            pl.BlockSpec((tm, d + LANES), lambda i: (i, 0)),
            pl.BlockSpec((tm, LANES), lambda i: (i, 0)),
            pl.BlockSpec((1, SUBLANES, tm), lambda i: (i, 0, 0)),
            pl.BlockSpec((1, SUBLANES, LANES), lambda i: (i, 0, 0)),
        ],
        out_shape=[
            jax.ShapeDtypeStruct((t, d), F32),
            jax.ShapeDtypeStruct((t, d + LANES), BF16),
            jax.ShapeDtypeStruct((t, LANES), F32),
            jax.ShapeDtypeStruct((n_chunks, SUBLANES, tm), F32),
            jax.ShapeDtypeStruct((n_chunks, SUBLANES, LANES), F32),
        ],
        compiler_params=_params("parallel"),
        name="glu_router",
    )(x2d, act2d, wa, wb, gain, w_pad, b_pad)


MOE_SPARE = max(MOE_FIRST, MOE_REST)


def _moe_tiles(n_tokens, row_multiple):
    n_chunks = n_tokens // MOE_CHUNK
    max_rows = (TOP_K * n_tokens + (SEG_ALIGN - 1) * N_EXPERTS * n_chunks
                + N_EXPERTS * (MOE_SPARE + MOE_TILE))
    unit = math.lcm(MOE_TILE, row_multiple)
    return -(-max_rows // unit) * (unit // MOE_TILE)


def _moe_layout(cnt):
    seg = (cnt + SEG_ALIGN - 1) // SEG_ALIGN * SEG_ALIGN
    used = jnp.sum(seg, axis=0)
    padded = (used + MOE_SPARE + MOE_TILE - 1) // MOE_TILE * MOE_TILE
    ends = jnp.cumsum(padded)
    start = ends - padded
    off = start[None, :] + jnp.cumsum(seg, axis=0) - seg
    return (off.reshape(-1).astype(jnp.int32), cnt.reshape(-1).astype(jnp.int32),
            ends.astype(jnp.int32), (start + used).astype(jnp.int32))


def _tile_expert(i, ends_ref):
    first_row = i * MOE_TILE
    e = 0
    for k in range(N_EXPERTS - 1):
        e = e + (first_row >= ends_ref[k]).astype(jnp.int32)
    return e


def _segment_copies(hbm_ref, buf_ref, sem_ref, off_ref, chunk, to_hbm, slot=()):
    copies = []
    for e in range(N_EXPERTS):
        off = pl.multiple_of(off_ref[chunk * N_EXPERTS + e], SEG_ALIGN)
        for piece, (first, rows) in enumerate(MOE_PIECES):
            hbm = hbm_ref.at[pl.ds(off + first, rows)]
            buf = buf_ref.at[(*slot, e, pl.ds(first, rows))]
            src, dst = (buf, hbm) if to_hbm else (hbm, buf)
            copies.append(pltpu.make_async_copy(src, dst, sem_ref.at[(*slot, e, piece)]))
    return copies


def _onehot(index, target):
    return jnp.where(index == target, 1.0, 0.0).astype(BF16)


def _gather_kernel(off_ref, cnt_ref, h_ref, post_ref, zeros_hbm, o_hbm, stage_ref,
                   sem_ref):
    del zeros_hbm
    c = pl.program_id(0)
    h = h_ref[...]
    copies = _segment_copies(o_hbm, stage_ref, sem_ref, off_ref, c, to_hbm=True)

    def piece_onehot(e, piece):
        first, rows = MOE_PIECES[piece]
        rank = first + lax.broadcasted_iota(jnp.int32, (rows, MOE_CHUNK), 0)
        return _onehot(rank.astype(F32), post_ref[0, e:e + 1, :])

    def move_piece(e, piece, onehot=None):
        first, rows = MOE_PIECES[piece]
        onehot = piece_onehot(e, piece) if onehot is None else onehot
        stage_ref[e, first:first + rows] = jnp.dot(
            onehot, h, preferred_element_type=F32).astype(BF16)
        copies[2 * e + piece].start()

    onehot = piece_onehot(0, 0)
    for e in range(N_EXPERTS):
        nxt = piece_onehot(e + 1, 0) if e + 1 < N_EXPERTS else None
        move_piece(e, 0, onehot)
        onehot = nxt
    for e in range(N_EXPERTS):
        @pl.when(cnt_ref[c * N_EXPERTS + e] > MOE_FIRST)
        def _(e=e):
            move_piece(e, 1)

    for e in range(N_EXPERTS):
        copies[2 * e].wait()

        @pl.when(cnt_ref[c * N_EXPERTS + e] > MOE_FIRST)
        def _(e=e):
            copies[2 * e + 1].wait()


def _gather(h2d, post, off, cnt, zero_buf):
    t, d = h2d.shape
    n_chunks = t // MOE_CHUNK
    grid_spec = pltpu.PrefetchScalarGridSpec(
        num_scalar_prefetch=2,
        grid=(n_chunks,),
        in_specs=[
            pl.BlockSpec((MOE_CHUNK, d), lambda i, off, cnt: (i, 0)),
            pl.BlockSpec((1, SUBLANES, MOE_CHUNK), lambda i, off, cnt: (i, 0, 0)),
            pl.BlockSpec(memory_space=pl.ANY),
        ],
        out_specs=pl.BlockSpec(memory_space=pl.ANY),
        scratch_shapes=[
            pltpu.VMEM((N_EXPERTS, MOE_CHUNK, d), BF16),
            pltpu.SemaphoreType.DMA((N_EXPERTS, 2)),
        ],
    )
    return pl.pallas_call(
        _gather_kernel,
        grid_spec=grid_spec,
        out_shape=jax.ShapeDtypeStruct(zero_buf.shape, BF16),
        input_output_aliases={4: 0},
        compiler_params=_params("arbitrary"),
        name="moe_gather",
    )(off, cnt, h2d, post, zero_buf)


def _expert_kernel(ends_ref, used_end_ref, x_ref, w1_ref, w3_ref, w2_ref, o_ref):
    i = pl.program_id(0)
    expert = _tile_expert(i, ends_ref)
    rows = jnp.clip(used_end_ref[expert] - i * MOE_TILE, 0, MOE_TILE)
    d = o_ref.shape[1]

    def run(n):
        row = lax.broadcasted_iota(jnp.int32, (n, x_ref.shape[1]), 0)
        x = jnp.where(row < rows, x_ref[0:n, :], jnp.zeros((), x_ref.dtype))
        lane = lax.broadcasted_iota(jnp.int32, (n, LANES), 1)
        mine = (lane < 3 * N_EXPERTS) & (jnp.bitwise_and(lane, N_EXPERTS - 1) == expert)
        gate = jnp.sum(jnp.where(mine, x[:, d:d + LANES].astype(F32), 0.0),
                       axis=1, keepdims=True)
        y = _swiglu(x[:, 0:d], w1_ref, w3_ref, w2_ref)
        o_ref[0:n, :] = (gate * y).astype(o_ref.dtype)

    half = MOE_TILE // 2

    @pl.when(rows > half)
    def _():
        run(MOE_TILE)

    @pl.when((rows > 0) & (rows <= half))
    def _():
        run(half)
        o_ref[half:MOE_TILE, :] = jnp.zeros((MOE_TILE - half, d), o_ref.dtype)

    @pl.when(rows == 0)
    def _():
        o_ref[...] = jnp.zeros_like(o_ref)


def _experts(x_sorted, w1, w3, w2, ends, used_end):
    assert N_EXPERTS & (N_EXPERTS - 1) == 0
    n_rows = x_sorted.shape[0]
    f = w1.shape[1]
    d = w2.shape[1]
    grid_spec = pltpu.PrefetchScalarGridSpec(
        num_scalar_prefetch=2,
        grid=(n_rows // MOE_TILE,),
        in_specs=[
            pl.BlockSpec((MOE_TILE, d + LANES), lambda i, ends, ue: (i, 0)),
            pl.BlockSpec((d, f), lambda i, ends, ue: (_tile_expert(i, ends), 0)),
            pl.BlockSpec((d, f), lambda i, ends, ue: (_tile_expert(i, ends), 0)),
            pl.BlockSpec((f, d), lambda i, ends, ue: (_tile_expert(i, ends), 0)),
        ],
        out_specs=pl.BlockSpec((MOE_TILE, d), lambda i, ends, ue: (i, 0)),
    )
    return pl.pallas_call(
        _expert_kernel,
        grid_spec=grid_spec,
        out_shape=jax.ShapeDtypeStruct((n_rows, d), BF16),
        compiler_params=_params("arbitrary"),
        name="moe_experts",
    )(ends, used_end, x_sorted, w1, w3, w2)


def _combine_kernel(off_ref, cnt_ref, x_ref, pos_ref, fg_ref, y_hbm,
                    o_ref, ybuf_ref, acc_ref, sem_ref):
    c = pl.program_id(0)
    n_chunks = pl.num_programs(0)
    slot = c % 2

    def fetch(chunk, sl, start):
        copies = _segment_copies(y_hbm, ybuf_ref, sem_ref, off_ref, chunk,
                                 to_hbm=False, slot=(sl,))
        for e in range(N_EXPERTS):
            first, second = copies[2 * e], copies[2 * e + 1]
            if start:
                first.start()
            else:
                first.wait()

            @pl.when(cnt_ref[chunk * N_EXPERTS + e] > MOE_FIRST)
            def _(second=second):
                if start:
                    second.start()
                else:
                    second.wait()

    @pl.when(c == 0)
    def _():
        fetch(c, slot, True)

    @pl.when(c + 1 < n_chunks)
    def _():
        fetch(c + 1, 1 - slot, True)

    fetch(c, slot, False)

    def piece_onehot(e, piece):
        first, rows = MOE_PIECES[piece]
        rank = first + lax.broadcasted_iota(jnp.int32, (MOE_CHUNK, rows), 1)
        return _onehot(rank.astype(F32), pos_ref[:, e:e + 1])

    def piece_rows(e, piece, onehot=None):
        first, rows = MOE_PIECES[piece]
        onehot = piece_onehot(e, piece) if onehot is None else onehot
        return jnp.dot(onehot, ybuf_ref[slot, e, first:first + rows],
                       preferred_element_type=F32)

    acc = x_ref[...]
    onehot = piece_onehot(0, 0)
    for e in range(N_EXPERTS):
        nxt = piece_onehot(e + 1, 0) if e + 1 < N_EXPERTS else None
        acc = acc + piece_rows(e, 0, onehot)
        onehot = nxt
    acc_ref[...] = acc
    for e in range(N_EXPERTS):
        @pl.when(cnt_ref[c * N_EXPERTS + e] > MOE_FIRST)
        def _(e=e):
            acc_ref[...] += piece_rows(e, 1)

    o_ref[...] = _rms(acc_ref[...], fg_ref[...])


def _combine(x2d, pos, final_gain, y_sorted, off, cnt):
    t, d = x2d.shape
    n_chunks = t // MOE_CHUNK
    grid_spec = pltpu.PrefetchScalarGridSpec(
        num_scalar_prefetch=2,
        grid=(n_chunks,),
        in_specs=[
            pl.BlockSpec((MOE_CHUNK, d), lambda i, off, cnt: (i, 0)),
            pl.BlockSpec((MOE_CHUNK, LANES), lambda i, off, cnt: (i, 0)),
            pl.BlockSpec((1, d), lambda i, off, cnt: (0, 0)),
            pl.BlockSpec(memory_space=pl.ANY),
        ],
        out_specs=pl.BlockSpec((MOE_CHUNK, d), lambda i, off, cnt: (i, 0)),
        scratch_shapes=[
            pltpu.VMEM((2, N_EXPERTS, MOE_CHUNK, d), BF16),
            pltpu.VMEM((MOE_CHUNK, d), F32),
            pltpu.SemaphoreType.DMA((2, N_EXPERTS, 2)),
        ],
    )
    return pl.pallas_call(
        _combine_kernel,
        grid_spec=grid_spec,
        out_shape=jax.ShapeDtypeStruct((t, d), F32),
        compiler_params=_params("arbitrary"),
        name="moe_combine",
    )(off, cnt, x2d, pos, final_gain, y_sorted)


def _glu_moe(x2d, act2d, wa, wb, gain, router_w, router_b, w1, w3, w2, final_gain,
             zero_buf):
    t, d = x2d.shape
    rw_hi = router_w.astype(BF16)
    rw_lo = (router_w.astype(F32) - rw_hi.astype(F32)).astype(BF16)
    rw = jnp.pad(jnp.concatenate([rw_hi, rw_lo], axis=1),
                 ((0, 0), (0, LANES - 2 * N_EXPERTS)))
    rb = jnp.pad(router_b.astype(F32), (0, LANES - N_EXPERTS)).reshape(1, LANES)
    x3, h, pos, post, cnt = _glu_router(x2d, act2d, wa, wb, gain, rw, rb)
    cnt = cnt[:, 0, :N_EXPERTS].astype(jnp.int32)
    off, cnt_flat, ends, used_end = _moe_layout(cnt)
    h_sorted = _gather(h, post, off, cnt_flat, zero_buf)
    y_sorted = _experts(h_sorted, w1, w3, w2, ends, used_end)
    return _combine(x3, pos, final_gain, y_sorted, off, cnt_flat)


def _even_weights(w_in, b_forget):
    n = w_in.shape[1]
    w_all = jnp.pad(w_in.astype(BF16), ((0, 0), (0, -n % LANES)))
    bias = jnp.pad(b_forget.astype(F32), (0, LANES - FOX_HEADS)).reshape(1, LANES)
    qw = FOX_HEADS * LANES
    place = np.zeros((LANES, 2 * qw), np.float32)
    for hh in range(FOX_HEADS):
        for piece in range(3):
            place[piece * FOX_HEADS + hh, hh * LANES + BIAS_LANE + piece] = 1.0
            place[piece * FOX_HEADS + hh, qw + hh * LANES + ONES_LANE + piece] = -1.0
    return w_all, bias, jnp.asarray(place, BF16)


def kernel(x, even_mix_norm, even_w_in, even_b_forget, even_w_pool, even_pool_scale, even_w_out, even_ffn_norm, even_ffn_w1, even_ffn_w3, even_ffn_w2, odd_mix_norm, odd_w_in, ssm_a_re, ssm_a_im, ssm_log_dt, ssm_b_re, ssm_b_im, ssm_c_re, ssm_c_im, ssm_d, odd_w_glu_a, odd_w_glu_b, odd_moe_norm, router_w, router_b, expert_w1, expert_w3, expert_w2, final_norm):
    b, s, d = x.shape
    t = b * s
    assert b == SUBLANES, "the S5 recurrence keeps one batch row per sublane"
    x2d = x.reshape(t, d)
    row = lambda v: v.reshape(1, -1).astype(F32)

    w_all, bias, place = _even_weights(even_w_in[0], even_b_forget[0])
    (q_aug, k_aug, v, p_in), casted = _even_inproj(
        x2d, row(even_mix_norm[0]), w_all, bias, place,
        [even_ffn_w1[0], even_ffn_w3[0], even_ffn_w2[0], even_w_out[0],
         even_w_pool[0].reshape(-1, POOL_GROUP_DIM),
         odd_w_in[0], odd_w_glu_a[0], odd_w_glu_b[0]],
        seq=s, tm=512)
    ffn_w1, ffn_w3, ffn_w2, w_out, w_pool, s5_w_in, glu_a, glu_b = casted
    n_e, _, f = expert_w1[0].shape
    att, (ew1, ew3, ew2) = _attention(
        q_aug, k_aug, v,
        [expert_w1[0].reshape(n_e * d, f), expert_w3[0].reshape(n_e * d, f),
         expert_w2[0].reshape(n_e * f, d)],
        batch=b, seq=s, tq=256)
    x2 = _even_tail(x2d, att, p_in, w_pool.reshape(even_w_pool[0].shape),
                    row(even_pool_scale[0]), w_out, row(even_ffn_norm[0]),
                    ffn_w1, ffn_w3, ffn_w2, seq=s, tm=512)

    bblk, cblk, a_re, a_im = _s5_coefficients(
        ssm_a_re[0], ssm_a_im[0], ssm_log_dt[0], ssm_b_re[0], ssm_b_im[0],
        ssm_c_re[0], ssm_c_im[0])
    moe_rows = _moe_tiles(t, (s // S5_STEP) * SEG_ALIGN) * MOE_TILE
    g, zero_buf = _s5_mixer(x2.reshape(b, s, d), row(odd_mix_norm[0]),
                            s5_w_in, bblk, cblk, a_re, a_im,
                            row(ssm_d[0]), tc=S5_STEP,
                            zero_shape=(moe_rows, d + LANES))
    out = _glu_moe(x2, g.reshape(t, d), glu_a, glu_b, row(odd_moe_norm[0]),
                   router_w[0], router_b[0], ew1, ew3, ew2,
                   row(final_norm), zero_buf)
    return out.reshape(b, s, d)
```

```python
import functools
import math

import numpy as np
import jax
import jax.numpy as jnp
from jax import lax
from jax.experimental import pallas as pl
from jax.experimental.pallas import tpu as pltpu

F32 = jnp.float32
BF16 = jnp.bfloat16

EPS = 1e-6
NEG_INF = -1e30
LANES = 128
SUBLANES = 8
SEG_ALIGN = 16
VMEM_LIMIT = 56 * 1024 * 1024

FOX_HEADS = 8
FOX_HEAD_DIM = 64
FOX_WIDTH = FOX_HEADS * FOX_HEAD_DIM
POOL_WINDOWS = (2, 4, 8, 16)
POOL_GROUP_DIM = 128
POOL_WIDTH = len(POOL_WINDOWS) * POOL_GROUP_DIM
POOL_HALO = 16
SSM_GROUP = 16
SSM_STATE = 64
SSM_SLAB_GROUPS = LANES // SSM_GROUP
SSM_SLAB_STATE = SSM_SLAB_GROUPS * SSM_STATE
S5_STEP = 64
N_EXPERTS = 8
TOP_K = 2

BIAS_LANE = FOX_HEAD_DIM
ONES_LANE = FOX_HEAD_DIM + 3
DENOM_LANE = FOX_HEAD_DIM


def _params(*sem):
    return pltpu.CompilerParams(dimension_semantics=sem,
                                vmem_limit_bytes=VMEM_LIMIT)


def _rms(x, g):
    ms = jnp.mean(x * x, axis=-1, keepdims=True)
    return x * lax.rsqrt(ms + EPS) * g


def _sigmoid(x):
    return 1.0 / (1.0 + jnp.exp(-x))


def _lane_range_ones(lo, hi):
    lane = lax.broadcasted_iota(jnp.int32, (1, LANES), 1)
    return jnp.where((lane >= lo) & (lane < hi), 1.0, 0.0).astype(F32)


def _cast_specs(weights, steps, step_index):
    specs = []
    for w in weights:
        rows = -(-w.shape[0] // (steps * SEG_ALIGN)) * SEG_ALIGN
        last = -(-w.shape[0] // rows) - 1
        specs.append(pl.BlockSpec(
            (rows, w.shape[1]),
            lambda *idx, last=last: (jnp.minimum(step_index(*idx), last), 0)))
    return specs


def _cast_slabs(src_refs, dst_refs):
    for src_ref, dst_ref in zip(src_refs, dst_refs):
        dst_ref[...] = src_ref[...].astype(dst_ref.dtype)


def _even_inproj_kernel(x_ref, g_ref, w_ref, bias_ref, place_ref, *rest,
                        tiles_per_seq, n_cast):
    cast_in, rest = rest[:n_cast], rest[n_cast:]
    q_ref, k_ref, v_ref, p_ref = rest[:4]
    cast_out, carry_ref = rest[4:4 + n_cast], rest[4 + n_cast]
    _cast_slabs(cast_in, cast_out)
    i = pl.program_id(0)
    tm = x_ref.shape[0]
    parts = 2
    rows = tm // parts
    fw = FOX_WIDTH
    qw = FOX_HEADS * LANES

    @pl.when(i % tiles_per_seq == 0)
    def _():
        carry_ref[...] = jnp.zeros_like(carry_ref)

    def project(r0):
        h = _rms(x_ref[r0:r0 + rows, :], g_ref[...]).astype(BF16)
        return jnp.dot(h, w_ref[...], preferred_element_type=F32)

    def finish(r0, z, carry):
        out = slice(r0, r0 + rows)
        p_ref[out, :] = z[:, 3 * fw + FOX_HEADS:3 * fw + FOX_HEADS + POOL_WIDTH]
        fg = z[:, 3 * fw:3 * fw + LANES] + bias_ref[...]
        lf = jnp.minimum(fg, 0.0) - jnp.log1p(jnp.exp(-jnp.abs(fg)))
        row = lax.broadcasted_iota(jnp.int32, lf.shape, 0)
        c = lf
        sh = 1
        while sh < rows:
            c = c + jnp.where(row >= sh, pltpu.roll(c, sh, axis=0), 0.0)
            sh *= 2
        c = c + carry
        hi = c.astype(BF16).astype(F32)
        r1 = c - hi
        mid = r1.astype(BF16).astype(F32)
        lo = r1 - mid
        lane = lax.broadcasted_iota(jnp.int32, (rows, LANES), 1)
        packed = jnp.where(lane < FOX_HEADS, hi,
                           jnp.where(lane < 2 * FOX_HEADS,
                                     pltpu.roll(mid, FOX_HEADS, axis=1),
                                     pltpu.roll(lo, 2 * FOX_HEADS, axis=1)))
        placed = jnp.dot(packed.astype(BF16), place_ref[...],
                         preferred_element_type=F32)
        low = lane < FOX_HEAD_DIM
        ones_q = _lane_range_ones(ONES_LANE, ONES_LANE + 3)
        ones_k = _lane_range_ones(BIAS_LANE, BIAS_LANE + 3)
        ones_v = _lane_range_ones(DENOM_LANE, DENOM_LANE + 1)

        def head_lanes(base, hh):
            pair = z[:, base + (hh // 2) * LANES:base + (hh // 2 + 1) * LANES]
            return pltpu.roll(pair, FOX_HEAD_DIM, axis=1) if hh % 2 else pair

        q_scale = FOX_HEAD_DIM ** -0.5
        for hh in range(FOX_HEADS):
            sl = slice(hh * LANES, (hh + 1) * LANES)
            q_ref[out, sl] = jnp.where(low, head_lanes(0, hh) * q_scale,
                                       placed[:, sl] + ones_q).astype(BF16)
            k_ref[out, sl] = jnp.where(
                low, head_lanes(fw, hh),
                placed[:, qw + hh * LANES:qw + (hh + 1) * LANES] + ones_k).astype(BF16)
            v_ref[out, sl] = jnp.where(low, head_lanes(2 * fw, hh), ones_v).astype(BF16)
        return c[rows - 1:rows, :]

    carry = carry_ref[0:1, :]
    z = project(0)
    for part in range(parts):
        nxt = project((part + 1) * rows) if part + 1 < parts else None
        carry = finish(part * rows, z, carry)
        z = nxt
    carry_ref[...] = jnp.broadcast_to(carry, carry_ref.shape)


def _even_inproj(x2d, gain, w_all, bias, place, cast_weights, *, seq, tm):
    t, d = x2d.shape
    n = w_all.shape[1]
    qw = FOX_HEADS * LANES
    cast_specs = _cast_specs(cast_weights, t // tm, lambda i: i)
    kern = functools.partial(_even_inproj_kernel, tiles_per_seq=seq // tm,
                             n_cast=len(cast_weights))
    outs = pl.pallas_call(
        kern,
        grid=(t // tm,),
        in_specs=[
            pl.BlockSpec((tm, d), lambda i: (i, 0)),
            pl.BlockSpec((1, d), lambda i: (0, 0)),
            pl.BlockSpec((d, n), lambda i: (0, 0)),
            pl.BlockSpec((1, LANES), lambda i: (0, 0)),
            pl.BlockSpec(place.shape, lambda i: (0, 0)),
            *cast_specs,
        ],
        out_specs=[
            pl.BlockSpec((tm, qw), lambda i: (i, 0)),
            pl.BlockSpec((tm, qw), lambda i: (i, 0)),
            pl.BlockSpec((tm, qw), lambda i: (i, 0)),
            pl.BlockSpec((tm, POOL_WIDTH), lambda i: (i, 0)),
            *cast_specs,
        ],
        out_shape=[
            jax.ShapeDtypeStruct((t, qw), BF16),
            jax.ShapeDtypeStruct((t, qw), BF16),
            jax.ShapeDtypeStruct((t, qw), BF16),
            jax.ShapeDtypeStruct((t, POOL_WIDTH), F32),
        ] + [jax.ShapeDtypeStruct(w.shape, BF16) for w in cast_weights],
        scratch_shapes=[pltpu.VMEM((SUBLANES, LANES), F32)],
        compiler_params=_params("arbitrary"),
        name="even_inproj",
    )(x2d, gain, w_all, bias, place, *cast_weights)
    return outs[:4], outs[4:]


def _dot_nt(a, b):
    return lax.dot_general(a, b, (((1,), (1,)), ((), ())),
                           preferred_element_type=F32)


def _attn_kernel(q_ref, k_ref, v_ref, *rest, tq, n_cast):
    cast_in, o_ref, cast_out = rest[:n_cast], rest[n_cast], rest[n_cast + 1:]
    seq = q_ref.shape[0]
    row = lax.broadcasted_iota(jnp.int32, (tq, tq), 0)
    col = lax.broadcasted_iota(jnp.int32, (tq, tq), 1)
    causal = col <= row
    lane = lax.broadcasted_iota(jnp.int32, (tq, LANES), 1)
    work = [(qi, hh) for qi in range(seq // tq) for hh in range(2)]

    def scores(qi, hh):
        r0 = qi * tq
        sl = slice(hh * LANES, (hh + 1) * LANES)
        q = q_ref[r0:r0 + tq, sl]
        s_diag = jnp.where(causal, _dot_nt(q, k_ref[r0:r0 + tq, sl]), NEG_INF)
        s_past = _dot_nt(q, k_ref[0:r0, sl]) if qi > 0 else None
        return s_diag, s_past

    ahead = 2
    queue = [scores(*item) for item in work[:ahead]]
    first_head = None
    for n, (qi, hh) in enumerate(work):
        r0 = qi * tq
        sl = slice(hh * LANES, (hh + 1) * LANES)
        s_diag, s_past = queue.pop(0)
        if n + ahead < len(work):
            queue.append(scores(*work[n + ahead]))
        m = jnp.max(s_diag, axis=1, keepdims=True)
        if qi > 0:
            m = jnp.maximum(m, jnp.max(s_past, axis=1, keepdims=True))
        acc = jnp.dot(jnp.exp(s_diag - m).astype(BF16), v_ref[r0:r0 + tq, sl],
                      preferred_element_type=F32)
        if qi > 0:
            acc = acc + jnp.dot(jnp.exp(s_past - m).astype(BF16), v_ref[0:r0, sl],
                                preferred_element_type=F32)
        out = acc / acc[:, DENOM_LANE:DENOM_LANE + 1]
        if hh == 0:
            first_head = out
        else:
            o_ref[r0:r0 + tq, :] = jnp.where(
                lane < FOX_HEAD_DIM, first_head,
                pltpu.roll(out, FOX_HEAD_DIM, axis=1)).astype(o_ref.dtype)
    _cast_slabs(cast_in, cast_out)


def _attention(q_aug, k_aug, v_aug, cast_weights, *, batch, seq, tq):
    t = q_aug.shape[0]
    pairs = FOX_HEADS // 2
    steps = batch * pairs
    spec = pl.BlockSpec((seq, 2 * LANES), lambda b, hp: (b, hp))
    out_spec = pl.BlockSpec((seq, LANES), lambda b, hp: (b, hp))
    cast_specs = _cast_specs(cast_weights, steps, lambda b, hp: b * pairs + hp)
    outs = pl.pallas_call(
        functools.partial(_attn_kernel, tq=tq, n_cast=len(cast_weights)),
        grid=(batch, pairs),
        in_specs=[spec, spec, spec, *cast_specs],
        out_specs=[out_spec, *cast_specs],
        out_shape=[jax.ShapeDtypeStruct((t, FOX_WIDTH), BF16)]
        + [jax.ShapeDtypeStruct(w.shape, BF16) for w in cast_weights],
        compiler_params=_params("parallel", "parallel"),
        name="fox_attention",
    )(q_aug, k_aug, v_aug, *cast_weights)
    return outs[0], outs[1:]


MXU_TILE = 256


def _swiglu_partial(h, w1, w3, w2):
    a = jnp.dot(h, w1, preferred_element_type=F32)
    b = jnp.dot(h, w3, preferred_element_type=F32)
    act = a * _sigmoid(a) * b
    return jnp.dot(act.astype(BF16), w2, preferred_element_type=F32)


def _ff_splits(f):
    cut = -(-(f // MXU_TILE) // 2) * MXU_TILE
    return ((0, cut), (cut, f)) if 0 < cut < f else ((0, f),)


def _swiglu(h, w1_ref, w3_ref, w2_ref):
    y = None
    for lo, hi in _ff_splits(w1_ref.shape[1]):
        part = _swiglu_partial(h, w1_ref[:, lo:hi], w3_ref[:, lo:hi], w2_ref[lo:hi, :])
        y = part if y is None else y + part
    return y


def _even_tail_kernel(x_ref, att_ref, p_ref, halo_ref, wpool_ref, scale_ref,
                      wo_att_ref, wo_pool_ref, g_ref, w1_ref, w3_ref, w2_ref, o_ref,
                      *, tiles_per_seq):
    i = pl.program_id(0)
    tm = x_ref.shape[0]
    tile_in_seq = i % tiles_per_seq
    p = p_ref[...]
    halo = jnp.where(tile_in_seq == 0, 0.0, halo_ref[...])
    ext = jnp.concatenate([halo, p], axis=0)
    pos = tile_in_seq * tm + lax.broadcasted_iota(jnp.int32, (tm, 1), 0)
    count = (pos + 1).astype(F32)
    mixed = []
    for gi, w in enumerate(POOL_WINDOWS):
        sl = slice(gi * POOL_GROUP_DIM, (gi + 1) * POOL_GROUP_DIM)
        acc = ext[:, sl]
        sh = 1
        while sh < w:
            acc = acc + pltpu.roll(acc, sh, axis=0)
            sh *= 2
        mean = acc[POOL_HALO:, :] / jnp.minimum(count, float(w))
        pooled = (mean - p[:, sl]).astype(BF16)
        mixed.append(jnp.dot(pooled, wpool_ref[gi], preferred_element_type=F32))
    pool = (jnp.concatenate(mixed, axis=1) * scale_ref[...]).astype(BF16)
    y = jnp.dot(att_ref[...], wo_att_ref[...], preferred_element_type=F32)
    y = y + jnp.dot(pool, wo_pool_ref[...], preferred_element_type=F32)
    x1 = x_ref[...] + y
    h = _rms(x1, g_ref[...]).astype(BF16)
    o_ref[...] = x1 + _swiglu(h, w1_ref, w3_ref, w2_ref)


def _even_tail(x2d, att, p_in, w_pool, pool_scale, w_out, ffn_gain, w1, w3, w2,
               *, seq, tm):
    t, d = x2d.shape
    resident = lambda w: pl.BlockSpec(w.shape, lambda i: (0,) * w.ndim,
                                      pipeline_mode=pl.Buffered(1))
    wo_att = w_out[:FOX_WIDTH]
    wo_pool = w_out[FOX_WIDTH:]
    halo_blocks = tm // POOL_HALO
    kern = functools.partial(_even_tail_kernel, tiles_per_seq=seq // tm)
    return pl.pallas_call(
        kern,
        grid=(t // tm,),
        in_specs=[
            pl.BlockSpec((tm, d), lambda i: (i, 0)),
            pl.BlockSpec((tm, FOX_WIDTH), lambda i: (i, 0)),
            pl.BlockSpec((tm, POOL_WIDTH), lambda i: (i, 0)),
            pl.BlockSpec((POOL_HALO, POOL_WIDTH),
                         lambda i: (jnp.maximum(i * halo_blocks - 1, 0), 0)),
            resident(w_pool),
            pl.BlockSpec((1, POOL_WIDTH), lambda i: (0, 0)),
            resident(wo_att), resident(wo_pool),
            pl.BlockSpec((1, d), lambda i: (0, 0)),
            resident(w1), resident(w3), resident(w2),
        ],
        out_specs=pl.BlockSpec((tm, d), lambda i: (i, 0)),
        out_shape=jax.ShapeDtypeStruct((t, d), F32),
        compiler_params=_params("parallel"),
        name="even_tail",
    )(x2d, att, p_in, p_in, w_pool, pool_scale, wo_att, wo_pool, ffn_gain, w1, w3, w2)


def _gelu_tanh(x):
    c = math.sqrt(2.0 / math.pi)
    return 0.5 * x * (1.0 + jnp.tanh(c * (x + 0.044715 * (x * x * x))))


def _s5_kernel(x_ref, g_ref, win_ref, perm_ref, permt_ref, bblk_ref, cblk_ref,
               are_ref, aim_ref, d_ref, o_ref, zero_ref, xs_ref, st_ref, u_ref, act_ref):
    nb, tc, d = x_ref.shape
    rows = nb * tc
    n_slabs = d // LANES
    sw = 2 * SSM_SLAB_STATE

    @pl.when(pl.program_id(0) == 0)
    def _():
        st_ref[...] = jnp.zeros_like(st_ref)

    x = x_ref[...].reshape(rows, d)
    h = _rms(x, g_ref[...]).astype(BF16)
    h_tb = jnp.dot(perm_ref[...], h, preferred_element_type=F32).astype(BF16)
    u_ref[...] = jnp.dot(h_tb, win_ref[...], preferred_element_type=F32)
    hs = SSM_SLAB_STATE

    def input_matmul(s):
        xs_ref[:, s * sw:(s + 1) * sw] = jnp.dot(
            u_ref[:, s * LANES:(s + 1) * LANES].astype(BF16), bblk_ref[s],
            preferred_element_type=F32)

    def output_matmul(s):
        sl = slice(s * LANES, (s + 1) * LANES)
        y = jnp.dot(xs_ref[:, s * sw:(s + 1) * sw].astype(BF16), cblk_ref[s],
                    preferred_element_type=F32) + d_ref[:, sl] * u_ref[:, sl]
        act_ref[:, sl] = _gelu_tanh(y).astype(BF16)

    ahead = 2
    for s in range(min(ahead, n_slabs)):
        input_matmul(s)
    for s in range(n_slabs):
        lo = s * sw
        if s + ahead < n_slabs:
            input_matmul(s + ahead)
        a_r = are_ref[:, lo:lo + hs]
        a_i = aim_ref[:, lo:lo + hs]
        x_r = st_ref[:, lo:lo + hs]
        x_i = st_ref[:, lo + hs:lo + sw]
        for t in range(tc):
            r0 = t * SUBLANES
            new_r = a_r * x_r - a_i * x_i + xs_ref[r0:r0 + SUBLANES, lo:lo + hs]
            new_i = a_r * x_i + a_i * x_r + xs_ref[r0:r0 + SUBLANES, lo + hs:lo + sw]
            xs_ref[r0:r0 + SUBLANES, lo:lo + hs] = new_r
            xs_ref[r0:r0 + SUBLANES, lo + hs:lo + sw] = new_i
            x_r, x_i = new_r, new_i
        st_ref[:, lo:lo + hs] = x_r
        st_ref[:, lo + hs:lo + sw] = x_i
        if s >= 1:
            output_matmul(s - 1)
    output_matmul(n_slabs - 1)
    g_bt = jnp.dot(permt_ref[...], act_ref[...], preferred_element_type=F32)
    o_ref[...] = g_bt.reshape(nb, tc, d).astype(o_ref.dtype)
    zero_ref[...] = jnp.zeros_like(zero_ref)


def _s5_mixer(x3d, gain, w_in, bblk, cblk, a_re, a_im, d_skip, *, tc, zero_shape):
    nb, seq, d = x3d.shape
    steps = seq // tc
    zero_rows = zero_shape[0] // steps
    assert zero_rows * steps == zero_shape[0] and zero_rows % SEG_ALIGN == 0
    rows = nb * tc
    n_slabs = d // LANES
    sw = 2 * SSM_SLAB_STATE
    r = np.arange(rows)
    perm = np.zeros((rows, rows), np.float32)
    perm[(r % tc) * nb + r // tc, r] = 1.0
    perm_j = jnp.asarray(perm, BF16)
    permt_j = jnp.asarray(perm.T, BF16)
    kern = _s5_kernel
    full2 = lambda a: pl.BlockSpec(a.shape, lambda i: (0, 0))
    full3 = lambda a: pl.BlockSpec(a.shape, lambda i: (0, 0, 0))
    return pl.pallas_call(
        kern,
        grid=(seq // tc,),
        in_specs=[
            pl.BlockSpec((nb, tc, d), lambda i: (0, i, 0)),
            full2(gain), full2(w_in), full2(perm_j), full2(permt_j),
            full3(bblk), full3(cblk), full2(a_re), full2(a_im), full2(d_skip),
        ],
        out_specs=[pl.BlockSpec((nb, tc, d), lambda i: (0, i, 0)),
                   pl.BlockSpec((zero_rows, zero_shape[1]), lambda i: (i, 0))],
        out_shape=[jax.ShapeDtypeStruct((nb, seq, d), BF16),
                   jax.ShapeDtypeStruct(zero_shape, BF16)],
        scratch_shapes=[
            pltpu.VMEM((rows, n_slabs * sw), F32),
            pltpu.VMEM((SUBLANES, n_slabs * sw), F32),
            pltpu.VMEM((rows, d), F32),
            pltpu.VMEM((rows, d), BF16),
        ],
        compiler_params=_params("arbitrary"),
        name="s5_mixer",
    )(x3d, gain, w_in, perm_j, permt_j, bblk, cblk, a_re, a_im, d_skip)


def _s5_coefficients(a_re, a_im, log_dt, b_re, b_im, c_re, c_im):
    dt = jnp.exp(log_dt.astype(F32))[:, None]
    ar = a_re.astype(F32)
    ai = a_im.astype(F32)
    mag = jnp.exp(ar * dt)
    abar_re = mag * jnp.cos(ai * dt)
    abar_im = mag * jnp.sin(ai * dt)
    den = ar * ar + ai * ai
    nr = abar_re - 1.0
    ni = abar_im
    coef_re = (nr * ar + ni * ai) / den
    coef_im = (ni * ar - nr * ai) / den
    br = b_re.astype(F32)
    bi = b_im.astype(F32)
    bbar_re = coef_re[..., None] * br - coef_im[..., None] * bi
    bbar_im = coef_re[..., None] * bi + coef_im[..., None] * br
    n_groups = ar.shape[0]
    n_slabs = n_groups // SSM_SLAB_GROUPS
    row_group = np.arange(LANES)[:, None] // SSM_GROUP
    col_group = np.arange(SSM_SLAB_STATE)[None, :] // SSM_STATE
    in_mask = jnp.asarray(row_group == col_group, F32)

    def in_block(bb):
        bb = bb.reshape(n_slabs, SSM_SLAB_GROUPS, SSM_STATE, SSM_GROUP)
        bb = bb.transpose(0, 1, 3, 2).reshape(n_slabs, LANES, SSM_STATE)
        return jnp.tile(bb, (1, 1, SSM_SLAB_GROUPS)) * in_mask

    def out_block(cc):
        cc = cc.reshape(n_slabs, SSM_SLAB_GROUPS, SSM_GROUP, SSM_STATE)
        cc = cc.transpose(0, 1, 3, 2).reshape(n_slabs, SSM_SLAB_STATE, SSM_GROUP)
        return jnp.tile(cc, (1, 1, SSM_SLAB_GROUPS)) * in_mask.T

    bblk = jnp.concatenate([in_block(bbar_re), in_block(bbar_im)], axis=2)
    cblk = jnp.concatenate([out_block(c_re.astype(F32)),
                            -out_block(c_im.astype(F32))], axis=1)

    def lanes(a):
        a = a.reshape(n_slabs, 1, SSM_SLAB_STATE)
        a = jnp.concatenate([a, a], axis=2).reshape(1, -1)
        return jnp.broadcast_to(a, (SUBLANES, a.shape[1]))

    return (bblk.astype(BF16), cblk.astype(BF16), lanes(abar_re), lanes(abar_im))


MOE_CHUNK = 512
MOE_FIRST = 192
MOE_REST = MOE_CHUNK - MOE_FIRST
MOE_PIECES = ((0, MOE_FIRST), (MOE_FIRST, MOE_REST))
MOE_TILE = 512


def _glu_router_kernel(x_ref, act_ref, wa_ref, wb_ref, g_ref, w_ref, b_ref,
                       x3_ref, h_ref, pos_ref, post_ref, cnt_ref):
    tm, d = x_ref.shape
    parts = 2
    half = tm // parts

    def glu(r0):
        act = act_ref[r0:r0 + half, :]
        a = jnp.dot(act, wa_ref[...], preferred_element_type=F32)
        b = jnp.dot(act, wb_ref[...], preferred_element_type=F32)
        x3 = x_ref[r0:r0 + half, :] + a * _sigmoid(b)
        x3_ref[r0:r0 + half, :] = x3
        return x3

    def route(r0, x3):
        h = _rms(x3, g_ref[...])
        h_hi = h.astype(BF16)
        h_ref[r0:r0 + half, 0:d] = h_hi
        h_lo = (h - h_hi.astype(F32)).astype(BF16)
        p_hi = jnp.dot(h_hi, w_ref[...], preferred_element_type=F32)
        p_lo = jnp.dot(h_lo, w_ref[...], preferred_element_type=F32)
        logits = p_hi + pltpu.roll(p_hi, LANES - N_EXPERTS, axis=1) + p_lo + b_ref[...]
        lane = lax.broadcasted_iota(jnp.int32, logits.shape, 1)
        logits = jnp.where(lane < N_EXPERTS, logits, -jnp.inf)
        m1 = jnp.max(logits, axis=1, keepdims=True)
        i1 = jnp.min(jnp.where(logits == m1, lane, LANES), axis=1, keepdims=True)
        rest = jnp.where(lane == i1, -jnp.inf, logits)
        m2 = jnp.max(rest, axis=1, keepdims=True)
        i2 = jnp.min(jnp.where(rest == m2, lane, LANES), axis=1, keepdims=True)
        e2 = jnp.exp(m2 - m1)
        g1 = 1.0 / (1.0 + e2)
        g2 = e2 / (1.0 + e2)
        gate = jnp.where(lane == i1, g1, 0.0) + jnp.where(lane == i2, g2, 0.0)
        g_hi = gate.astype(BF16).astype(F32)
        g_r = gate - g_hi
        g_mid = g_r.astype(BF16).astype(F32)
        packed = jnp.where(lane < N_EXPERTS, g_hi,
                           jnp.where(lane < 2 * N_EXPERTS,
                                     pltpu.roll(g_mid, N_EXPERTS, axis=1),
                                     pltpu.roll(g_r - g_mid, 2 * N_EXPERTS, axis=1)))
        h_ref[r0:r0 + half, d:d + LANES] = packed.astype(BF16)
        return jnp.where((lane == i1) | (lane == i2), 1.0, 0.0)

    members = []
    x3 = glu(0)
    for p in range(parts):
        nxt = glu((p + 1) * half) if p + 1 < parts else None
        members.append(route(p * half, x3))
        x3 = nxt
    member = jnp.concatenate(members, axis=0)
    row = lax.broadcasted_iota(jnp.int32, member.shape, 0)
    c = member
    sh = 1
    while sh < tm:
        c = c + jnp.where(row >= sh, pltpu.roll(c, sh, axis=0), 0.0)
        sh *= 2
    pos = jnp.where(member > 0.0, c - member, -1.0)
    pos_ref[...] = pos
    post_ref[0] = pos.T[0:SUBLANES, :]
    cnt_ref[0] = jnp.broadcast_to(c[tm - 1:tm, :], (SUBLANES, LANES))


def _glu_router(x2d, act2d, wa, wb, gain, w_pad, b_pad):
    t, d = x2d.shape
    tm = MOE_CHUNK
    n_chunks = t // tm
    return pl.pallas_call(
        _glu_router_kernel,
        grid=(n_chunks,),
        in_specs=[
            pl.BlockSpec((tm, d), lambda i: (i, 0)),
            pl.BlockSpec((tm, d), lambda i: (i, 0)),
            pl.BlockSpec(wa.shape, lambda i: (0, 0)),
            pl.BlockSpec(wb.shape, lambda i: (0, 0)),
            pl.BlockSpec((1, d), lambda i: (0, 0)),
            pl.BlockSpec((d, LANES), lambda i: (0, 0)),
            pl.BlockSpec((1, LANES), lambda i: (0, 0)),
        ],
        out_specs=[
            pl.BlockSpec((tm, d), lambda i: (i, 0)),
            pl.BlockSpec((tm, d + LANES), lambda i: (i, 0)),
            pl.BlockSpec((tm, LANES), lambda i: (i, 0)),
            pl.BlockSpec((1, SUBLANES, tm), lambda i: (i, 0, 0)),
            pl.BlockSpec((1, SUBLANES, LANES), lambda i: (i, 0, 0)),
        ],
        out_shape=[
            jax.ShapeDtypeStruct((t, d), F32),
            jax.ShapeDtypeStruct((t, d + LANES), BF16),
            jax.ShapeDtypeStruct((t, LANES), F32),
            jax.ShapeDtypeStruct((n_chunks, SUBLANES, tm), F32),
            jax.ShapeDtypeStruct((n_chunks, SUBLANES, LANES), F32),
        ],
        compiler_params=_params("parallel"),
        name="glu_router",
    )(x2d, act2d, wa, wb, gain, w_pad, b_pad)


MOE_SPARE = max(MOE_FIRST, MOE_REST)


def _moe_tiles(n_tokens, row_multiple):
    n_chunks = n_tokens // MOE_CHUNK
    max_rows = (TOP_K * n_tokens + (SEG_ALIGN - 1) * N_EXPERTS * n_chunks
                + N_EXPERTS * (MOE_SPARE + MOE_TILE))
    unit = math.lcm(MOE_TILE, row_multiple)
    return -(-max_rows // unit) * (unit // MOE_TILE)


def _moe_layout(cnt):
    seg = (cnt + SEG_ALIGN - 1) // SEG_ALIGN * SEG_ALIGN
    used = jnp.sum(seg, axis=0)
    padded = (used + MOE_SPARE + MOE_TILE - 1) // MOE_TILE * MOE_TILE
    ends = jnp.cumsum(padded)
    start = ends - padded
    off = start[None, :] + jnp.cumsum(seg, axis=0) - seg
    return (off.reshape(-1).astype(jnp.int32), cnt.reshape(-1).astype(jnp.int32),
            ends.astype(jnp.int32), (start + used).astype(jnp.int32))


def _tile_expert(i, ends_ref):
    first_row = i * MOE_TILE
    e = 0
    for k in range(N_EXPERTS - 1):
        e = e + (first_row >= ends_ref[k]).astype(jnp.int32)
    return e


def _segment_copies(hbm_ref, buf_ref, sem_ref, off_ref, chunk, to_hbm, slot=()):
    copies = []
    for e in range(N_EXPERTS):
        off = pl.multiple_of(off_ref[chunk * N_EXPERTS + e], SEG_ALIGN)
        for piece, (first, rows) in enumerate(MOE_PIECES):
            hbm = hbm_ref.at[pl.ds(off + first, rows)]
            buf = buf_ref.at[(*slot, e, pl.ds(first, rows))]
            src, dst = (buf, hbm) if to_hbm else (hbm, buf)
            copies.append(pltpu.make_async_copy(src, dst, sem_ref.at[(*slot, e, piece)]))
    return copies


def _onehot(index, target):
    return jnp.where(index == target, 1.0, 0.0).astype(BF16)


def _gather_kernel(off_ref, cnt_ref, h_ref, post_ref, zeros_hbm, o_hbm, stage_ref,
                   sem_ref):
    del zeros_hbm
    c = pl.program_id(0)
    slot = c % 2
    h = h_ref[...]

    def long_segment(chunk, e):
        return cnt_ref[chunk * N_EXPERTS + e] > MOE_FIRST

    def piece_onehot(e, piece):
        first, rows = MOE_PIECES[piece]
        rank = first + lax.broadcasted_iota(jnp.int32, (rows, MOE_CHUNK), 0)
        return _onehot(rank.astype(F32), post_ref[0, e:e + 1, :])

    def fill_piece(e, piece, onehot=None):
        first, rows = MOE_PIECES[piece]
        onehot = piece_onehot(e, piece) if onehot is None else onehot
        stage_ref[slot, e, first:first + rows] = jnp.dot(
            onehot, h, preferred_element_type=F32).astype(BF16)

    onehot = piece_onehot(0, 0)
    for e in range(N_EXPERTS):
        nxt = piece_onehot(e + 1, 0) if e + 1 < N_EXPERTS else None
        fill_piece(e, 0, onehot)
        onehot = nxt
    for e in range(N_EXPERTS):
        @pl.when(long_segment(c, e))
        def _(e=e):
            fill_piece(e, 1)

    def each_copy(chunk, sl, action):
        copies = _segment_copies(o_hbm, stage_ref, sem_ref, off_ref, chunk,
                                 to_hbm=True, slot=(sl,))
        for e in range(N_EXPERTS):
            action(copies[2 * e])

            @pl.when(long_segment(chunk, e))
            def _(e=e):
                action(copies[2 * e + 1])

    @pl.when(c > 0)
    def _():
        each_copy(c - 1, 1 - slot, lambda cp: cp.wait())

    each_copy(c, slot, lambda cp: cp.start())

    @pl.when(c == pl.num_programs(0) - 1)
    def _():
        each_copy(c, slot, lambda cp: cp.wait())


def _gather(h2d, post, off, cnt, zero_buf):
    t, d = h2d.shape
    n_chunks = t // MOE_CHUNK
    grid_spec = pltpu.PrefetchScalarGridSpec(
        num_scalar_prefetch=2,
        grid=(n_chunks,),
        in_specs=[
            pl.BlockSpec((MOE_CHUNK, d), lambda i, off, cnt: (i, 0)),
            pl.BlockSpec((1, SUBLANES, MOE_CHUNK), lambda i, off, cnt: (i, 0, 0)),
            pl.BlockSpec(memory_space=pl.ANY),
        ],
        out_specs=pl.BlockSpec(memory_space=pl.ANY),
        scratch_shapes=[
            pltpu.VMEM((2, N_EXPERTS, MOE_CHUNK, d), BF16),
            pltpu.SemaphoreType.DMA((2, N_EXPERTS, 2)),
        ],
    )
    return pl.pallas_call(
        _gather_kernel,
        grid_spec=grid_spec,
        out_shape=jax.ShapeDtypeStruct(zero_buf.shape, BF16),
        input_output_aliases={4: 0},
        compiler_params=_params("arbitrary"),
        name="moe_gather",
    )(off, cnt, h2d, post, zero_buf)


def _expert_kernel(ends_ref, used_end_ref, x_ref, w1_ref, w3_ref, w2_ref, o_ref):
    i = pl.program_id(0)
    expert = _tile_expert(i, ends_ref)
    rows = jnp.clip(used_end_ref[expert] - i * MOE_TILE, 0, MOE_TILE)
    d = o_ref.shape[1]

    def run(n):
        row = lax.broadcasted_iota(jnp.int32, (n, x_ref.shape[1]), 0)
        x = jnp.where(row < rows, x_ref[0:n, :], jnp.zeros((), x_ref.dtype))
        lane = lax.broadcasted_iota(jnp.int32, (n, LANES), 1)
        mine = (lane < 3 * N_EXPERTS) & (jnp.bitwise_and(lane, N_EXPERTS - 1) == expert)
        gate = jnp.sum(jnp.where(mine, x[:, d:d + LANES].astype(F32), 0.0),
                       axis=1, keepdims=True)
        y = _swiglu(x[:, 0:d], w1_ref, w3_ref, w2_ref)
        o_ref[0:n, :] = (gate * y).astype(o_ref.dtype)

    half = MOE_TILE // 2

    @pl.when(rows > half)
    def _():
        run(MOE_TILE)

    @pl.when((rows > 0) & (rows <= half))
    def _():
        run(half)
        o_ref[half:MOE_TILE, :] = jnp.zeros((MOE_TILE - half, d), o_ref.dtype)

    @pl.when(rows == 0)
    def _():
        o_ref[...] = jnp.zeros_like(o_ref)


def _experts(x_sorted, w1, w3, w2, ends, used_end):
    assert N_EXPERTS & (N_EXPERTS - 1) == 0
    n_rows = x_sorted.shape[0]
    f = w1.shape[1]
    d = w2.shape[1]
    grid_spec = pltpu.PrefetchScalarGridSpec(
        num_scalar_prefetch=2,
        grid=(n_rows // MOE_TILE,),
        in_specs=[
            pl.BlockSpec((MOE_TILE, d + LANES), lambda i, ends, ue: (i, 0)),
            pl.BlockSpec((d, f), lambda i, ends, ue: (_tile_expert(i, ends), 0)),
            pl.BlockSpec((d, f), lambda i, ends, ue: (_tile_expert(i, ends), 0)),
            pl.BlockSpec((f, d), lambda i, ends, ue: (_tile_expert(i, ends), 0)),
        ],
        out_specs=pl.BlockSpec((MOE_TILE, d), lambda i, ends, ue: (i, 0)),
    )
    return pl.pallas_call(
        _expert_kernel,
        grid_spec=grid_spec,
        out_shape=jax.ShapeDtypeStruct((n_rows, d), BF16),
        compiler_params=_params("arbitrary"),
        name="moe_experts",
    )(ends, used_end, x_sorted, w1, w3, w2)


def _combine_kernel(off_ref, cnt_ref, x_ref, pos_ref, fg_ref, y_hbm,
                    o_ref, ybuf_ref, acc_ref, sem_ref):
    c = pl.program_id(0)
    n_chunks = pl.num_programs(0)
    slot = c % 2

    def fetch(chunk, sl, start):
        copies = _segment_copies(y_hbm, ybuf_ref, sem_ref, off_ref, chunk,
                                 to_hbm=False, slot=(sl,))
        for e in range(N_EXPERTS):
            first, second = copies[2 * e], copies[2 * e + 1]
            if start:
                first.start()
            else:
                first.wait()

            @pl.when(cnt_ref[chunk * N_EXPERTS + e] > MOE_FIRST)
            def _(second=second):
                if start:
                    second.start()
                else:
                    second.wait()

    @pl.when(c == 0)
    def _():
        fetch(c, slot, True)

    @pl.when(c + 1 < n_chunks)
    def _():
        fetch(c + 1, 1 - slot, True)

    fetch(c, slot, False)

    def piece_onehot(e, piece):
        first, rows = MOE_PIECES[piece]
        rank = first + lax.broadcasted_iota(jnp.int32, (MOE_CHUNK, rows), 1)
        return _onehot(rank.astype(F32), pos_ref[:, e:e + 1])

    def piece_rows(e, piece, onehot=None):
        first, rows = MOE_PIECES[piece]
        onehot = piece_onehot(e, piece) if onehot is None else onehot
        return jnp.dot(onehot, ybuf_ref[slot, e, first:first + rows],
                       preferred_element_type=F32)

    acc = x_ref[...]
    onehot = piece_onehot(0, 0)
    for e in range(N_EXPERTS):
        nxt = piece_onehot(e + 1, 0) if e + 1 < N_EXPERTS else None
        acc = acc + piece_rows(e, 0, onehot)
        onehot = nxt
    acc_ref[...] = acc
    for e in range(N_EXPERTS):
        @pl.when(cnt_ref[c * N_EXPERTS + e] > MOE_FIRST)
        def _(e=e):
            acc_ref[...] += piece_rows(e, 1)

    o_ref[...] = _rms(acc_ref[...], fg_ref[...])


def _combine(x2d, pos, final_gain, y_sorted, off, cnt):
    t, d = x2d.shape
    n_chunks = t // MOE_CHUNK
    grid_spec = pltpu.PrefetchScalarGridSpec(
        num_scalar_prefetch=2,
        grid=(n_chunks,),
        in_specs=[
            pl.BlockSpec((MOE_CHUNK, d), lambda i, off, cnt: (i, 0)),
            pl.BlockSpec((MOE_CHUNK, LANES), lambda i, off, cnt: (i, 0)),
            pl.BlockSpec((1, d), lambda i, off, cnt: (0, 0)),
            pl.BlockSpec(memory_space=pl.ANY),
        ],
        out_specs=pl.BlockSpec((MOE_CHUNK, d), lambda i, off, cnt: (i, 0)),
        scratch_shapes=[
            pltpu.VMEM((2, N_EXPERTS, MOE_CHUNK, d), BF16),
            pltpu.VMEM((MOE_CHUNK, d), F32),
            pltpu.SemaphoreType.DMA((2, N_EXPERTS, 2)),
        ],
    )
    return pl.pallas_call(
        _combine_kernel,
        grid_spec=grid_spec,
        out_shape=jax.ShapeDtypeStruct((t, d), F32),
        compiler_params=_params("arbitrary"),
        name="moe_combine",
    )(off, cnt, x2d, pos, final_gain, y_sorted)


def _glu_moe(x2d, act2d, wa, wb, gain, router_w, router_b, w1, w3, w2, final_gain,
             zero_buf):
    t, d = x2d.shape
    rw_hi = router_w.astype(BF16)
    rw_lo = (router_w.astype(F32) - rw_hi.astype(F32)).astype(BF16)
    rw = jnp.pad(jnp.concatenate([rw_hi, rw_lo], axis=1),
                 ((0, 0), (0, LANES - 2 * N_EXPERTS)))
    rb = jnp.pad(router_b.astype(F32), (0, LANES - N_EXPERTS)).reshape(1, LANES)
    x3, h, pos, post, cnt = _glu_router(x2d, act2d, wa, wb, gain, rw, rb)
    cnt = cnt[:, 0, :N_EXPERTS].astype(jnp.int32)
    off, cnt_flat, ends, used_end = _moe_layout(cnt)
    h_sorted = _gather(h, post, off, cnt_flat, zero_buf)
    y_sorted = _experts(h_sorted, w1, w3, w2, ends, used_end)
    return _combine(x3, pos, final_gain, y_sorted, off, cnt_flat)


def _even_weights(w_in, b_forget):
    n = w_in.shape[1]
    w_all = jnp.pad(w_in.astype(BF16), ((0, 0), (0, -n % LANES)))
    bias = jnp.pad(b_forget.astype(F32), (0, LANES - FOX_HEADS)).reshape(1, LANES)
    qw = FOX_HEADS * LANES
    place = np.zeros((LANES, 2 * qw), np.float32)
    for hh in range(FOX_HEADS):
        for piece in range(3):
            place[piece * FOX_HEADS + hh, hh * LANES + BIAS_LANE + piece] = 1.0
            place[piece * FOX_HEADS + hh, qw + hh * LANES + ONES_LANE + piece] = -1.0
    return w_all, bias, jnp.asarray(place, BF16)


def kernel(x, even_mix_norm, even_w_in, even_b_forget, even_w_pool, even_pool_scale, even_w_out, even_ffn_norm, even_ffn_w1, even_ffn_w3, even_ffn_w2, odd_mix_norm, odd_w_in, ssm_a_re, ssm_a_im, ssm_log_dt, ssm_b_re, ssm_b_im, ssm_c_re, ssm_c_im, ssm_d, odd_w_glu_a, odd_w_glu_b, odd_moe_norm, router_w, router_b, expert_w1, expert_w3, expert_w2, final_norm):
    b, s, d = x.shape
    t = b * s
    assert b == SUBLANES, "the S5 recurrence keeps one batch row per sublane"
    x2d = x.reshape(t, d)
    row = lambda v: v.reshape(1, -1).astype(F32)

    w_all, bias, place = _even_weights(even_w_in[0], even_b_forget[0])
    (q_aug, k_aug, v, p_in), casted = _even_inproj(
        x2d, row(even_mix_norm[0]), w_all, bias, place,
        [even_ffn_w1[0], even_ffn_w3[0], even_ffn_w2[0], even_w_out[0],
         even_w_pool[0].reshape(-1, POOL_GROUP_DIM),
         odd_w_in[0], odd_w_glu_a[0], odd_w_glu_b[0]],
        seq=s, tm=512)
    ffn_w1, ffn_w3, ffn_w2, w_out, w_pool, s5_w_in, glu_a, glu_b = casted
    n_e, _, f = expert_w1[0].shape
    att, (ew1, ew3, ew2) = _attention(
        q_aug, k_aug, v,
        [expert_w1[0].reshape(n_e * d, f), expert_w3[0].reshape(n_e * d, f),
         expert_w2[0].reshape(n_e * f, d)],
        batch=b, seq=s, tq=256)
    x2 = _even_tail(x2d, att, p_in, w_pool.reshape(even_w_pool[0].shape),
                    row(even_pool_scale[0]), w_out, row(even_ffn_norm[0]),
                    ffn_w1, ffn_w3, ffn_w2, seq=s, tm=512)

    bblk, cblk, a_re, a_im = _s5_coefficients(
        ssm_a_re[0], ssm_a_im[0], ssm_log_dt[0], ssm_b_re[0], ssm_b_im[0],
        ssm_c_re[0], ssm_c_im[0])
    moe_rows = _moe_tiles(t, (s // S5_STEP) * SEG_ALIGN) * MOE_TILE
    g, zero_buf = _s5_mixer(x2.reshape(b, s, d), row(odd_mix_norm[0]),
                            s5_w_in, bblk, cblk, a_re, a_im,
                            row(ssm_d[0]), tc=S5_STEP,
                            zero_shape=(moe_rows, d + LANES))
    out = _glu_moe(x2, g.reshape(t, d), glu_a, glu_b, row(odd_moe_norm[0]),
                   router_w[0], router_b[0], ew1, ew3, ew2,
                   row(final_norm), zero_buf)
    return out.reshape(b, s, d)
```

```python
import functools
import math

import numpy as np
import jax
import jax.numpy as jnp
from jax import lax
from jax.experimental import pallas as pl
from jax.experimental.pallas import tpu as pltpu

F32 = jnp.float32
BF16 = jnp.bfloat16

EPS = 1e-6
NEG_INF = -1e30
LANES = 128
SUBLANES = 8
SEG_ALIGN = 16
VMEM_LIMIT = 56 * 1024 * 1024

FOX_HEADS = 8
FOX_HEAD_DIM = 64
FOX_WIDTH = FOX_HEADS * FOX_HEAD_DIM
POOL_WINDOWS = (2, 4, 8, 16)
POOL_GROUP_DIM = 128
POOL_WIDTH = len(POOL_WINDOWS) * POOL_GROUP_DIM
POOL_HALO = 16
SSM_GROUP = 16
SSM_STATE = 64
SSM_SLAB_GROUPS = LANES // SSM_GROUP
SSM_SLAB_STATE = SSM_SLAB_GROUPS * SSM_STATE
S5_STEP = 64
N_EXPERTS = 8
TOP_K = 2

BIAS_LANE = FOX_HEAD_DIM
ONES_LANE = FOX_HEAD_DIM + 3
DENOM_LANE = FOX_HEAD_DIM


def _params(*sem):
    return pltpu.CompilerParams(dimension_semantics=sem,
                                vmem_limit_bytes=VMEM_LIMIT)


def _rms(x, g):
    ms = jnp.mean(x * x, axis=-1, keepdims=True)
    return x * lax.rsqrt(ms + EPS) * g


def _sigmoid(x):
    return 1.0 / (1.0 + jnp.exp(-x))


def _lane_range_ones(lo, hi):
    lane = lax.broadcasted_iota(jnp.int32, (1, LANES), 1)
    return jnp.where((lane >= lo) & (lane < hi), 1.0, 0.0).astype(F32)


def _cast_specs(weights, steps, step_index):
    specs = []
    for w in weights:
        rows = -(-w.shape[0] // (steps * SEG_ALIGN)) * SEG_ALIGN
        last = -(-w.shape[0] // rows) - 1
        specs.append(pl.BlockSpec(
            (rows, w.shape[1]),
            lambda *idx, last=last: (jnp.minimum(step_index(*idx), last), 0)))
    return specs


def _cast_slabs(src_refs, dst_refs):
    for src_ref, dst_ref in zip(src_refs, dst_refs):
        dst_ref[...] = src_ref[...].astype(dst_ref.dtype)


def _even_inproj_kernel(x_ref, g_ref, w_ref, bias_ref, place_ref, *rest,
                        tiles_per_seq, n_cast):
    cast_in, rest = rest[:n_cast], rest[n_cast:]
    q_ref, k_ref, v_ref, p_ref = rest[:4]
    cast_out, carry_ref = rest[4:4 + n_cast], rest[4 + n_cast]
    _cast_slabs(cast_in, cast_out)
    i = pl.program_id(0)
    tm = x_ref.shape[0]
    parts = 2
    rows = tm // parts
    fw = FOX_WIDTH
    qw = FOX_HEADS * LANES

    @pl.when(i % tiles_per_seq == 0)
    def _():
        carry_ref[...] = jnp.zeros_like(carry_ref)

    def project(r0):
        h = _rms(x_ref[r0:r0 + rows, :], g_ref[...]).astype(BF16)
        return jnp.dot(h, w_ref[...], preferred_element_type=F32)

    def finish(r0, z, carry):
        out = slice(r0, r0 + rows)
        p_ref[out, :] = z[:, 3 * fw + FOX_HEADS:3 * fw + FOX_HEADS + POOL_WIDTH]
        fg = z[:, 3 * fw:3 * fw + LANES] + bias_ref[...]
        lf = jnp.minimum(fg, 0.0) - jnp.log1p(jnp.exp(-jnp.abs(fg)))
        row = lax.broadcasted_iota(jnp.int32, lf.shape, 0)
        c = lf
        sh = 1
        while sh < rows:
            c = c + jnp.where(row >= sh, pltpu.roll(c, sh, axis=0), 0.0)
            sh *= 2
        c = c + carry
        hi = c.astype(BF16).astype(F32)
        r1 = c - hi
        mid = r1.astype(BF16).astype(F32)
        lo = r1 - mid
        lane = lax.broadcasted_iota(jnp.int32, (rows, LANES), 1)
        packed = jnp.where(lane < FOX_HEADS, hi,
                           jnp.where(lane < 2 * FOX_HEADS,
                                     pltpu.roll(mid, FOX_HEADS, axis=1),
                                     pltpu.roll(lo, 2 * FOX_HEADS, axis=1)))
        placed = jnp.dot(packed.astype(BF16), place_ref[...],
                         preferred_element_type=F32)
        low = lane < FOX_HEAD_DIM
        ones_q = _lane_range_ones(ONES_LANE, ONES_LANE + 3)
        ones_k = _lane_range_ones(BIAS_LANE, BIAS_LANE + 3)
        ones_v = _lane_range_ones(DENOM_LANE, DENOM_LANE + 1)

        def head_lanes(base, hh):
            pair = z[:, base + (hh // 2) * LANES:base + (hh // 2 + 1) * LANES]
            return pltpu.roll(pair, FOX_HEAD_DIM, axis=1) if hh % 2 else pair

        q_scale = FOX_HEAD_DIM ** -0.5
        for hh in range(FOX_HEADS):
            sl = slice(hh * LANES, (hh + 1) * LANES)
            q_ref[out, sl] = jnp.where(low, head_lanes(0, hh) * q_scale,
                                       placed[:, sl] + ones_q).astype(BF16)
            k_ref[out, sl] = jnp.where(
                low, head_lanes(fw, hh),
                placed[:, qw + hh * LANES:qw + (hh + 1) * LANES] + ones_k).astype(BF16)
            v_ref[out, sl] = jnp.where(low, head_lanes(2 * fw, hh), ones_v).astype(BF16)
        return c[rows - 1:rows, :]

    carry = carry_ref[0:1, :]
    z = project(0)
    for part in range(parts):
        nxt = project((part + 1) * rows) if part + 1 < parts else None
        carry = finish(part * rows, z, carry)
        z = nxt
    carry_ref[...] = jnp.broadcast_to(carry, carry_ref.shape)


def _even_inproj(x2d, gain, w_all, bias, place, cast_weights, *, seq, tm):
    t, d = x2d.shape
    n = w_all.shape[1]
    qw = FOX_HEADS * LANES
    cast_specs = _cast_specs(cast_weights, t // tm, lambda i: i)
    kern = functools.partial(_even_inproj_kernel, tiles_per_seq=seq // tm,
                             n_cast=len(cast_weights))
    outs = pl.pallas_call(
        kern,
        grid=(t // tm,),
        in_specs=[
            pl.BlockSpec((tm, d), lambda i: (i, 0)),
            pl.BlockSpec((1, d), lambda i: (0, 0)),
            pl.BlockSpec((d, n), lambda i: (0, 0)),
            pl.BlockSpec((1, LANES), lambda i: (0, 0)),
            pl.BlockSpec(place.shape, lambda i: (0, 0)),
            *cast_specs,
        ],
        out_specs=[
            pl.BlockSpec((tm, qw), lambda i: (i, 0)),
            pl.BlockSpec((tm, qw), lambda i: (i, 0)),
            pl.BlockSpec((tm, qw), lambda i: (i, 0)),
            pl.BlockSpec((tm, POOL_WIDTH), lambda i: (i, 0)),
            *cast_specs,
        ],
        out_shape=[
            jax.ShapeDtypeStruct((t, qw), BF16),
            jax.ShapeDtypeStruct((t, qw), BF16),
            jax.ShapeDtypeStruct((t, qw), BF16),
            jax.ShapeDtypeStruct((t, POOL_WIDTH), F32),
        ] + [jax.ShapeDtypeStruct(w.shape, BF16) for w in cast_weights],
        scratch_shapes=[pltpu.VMEM((SUBLANES, LANES), F32)],
        compiler_params=_params("arbitrary"),
        name="even_inproj",
    )(x2d, gain, w_all, bias, place, *cast_weights)
    return outs[:4], outs[4:]


def _dot_nt(a, b):
    return lax.dot_general(a, b, (((1,), (1,)), ((), ())),
                           preferred_element_type=F32)


def _attn_kernel(q_ref, k_ref, v_ref, *rest, tq, n_cast):
    cast_in, o_ref, cast_out = rest[:n_cast], rest[n_cast], rest[n_cast + 1:]
    seq = q_ref.shape[0]
    row = lax.broadcasted_iota(jnp.int32, (tq, tq), 0)
    col = lax.broadcasted_iota(jnp.int32, (tq, tq), 1)
    causal = col <= row
    lane = lax.broadcasted_iota(jnp.int32, (tq, LANES), 1)
    work = [(qi, hh) for qi in range(seq // tq) for hh in range(2)]

    def scores(qi, hh):
        r0 = qi * tq
        sl = slice(hh * LANES, (hh + 1) * LANES)
        q = q_ref[r0:r0 + tq, sl]
        s_diag = jnp.where(causal, _dot_nt(q, k_ref[r0:r0 + tq, sl]), NEG_INF)
        s_past = _dot_nt(q, k_ref[0:r0, sl]) if qi > 0 else None
        return s_diag, s_past

    ahead = 2
    queue = [scores(*item) for item in work[:ahead]]
    first_head = None
    for n, (qi, hh) in enumerate(work):
        r0 = qi * tq
        sl = slice(hh * LANES, (hh + 1) * LANES)
        s_diag, s_past = queue.pop(0)
        if n + ahead < len(work):
            queue.append(scores(*work[n + ahead]))
        m = jnp.max(s_diag, axis=1, keepdims=True)
        if qi > 0:
            m = jnp.maximum(m, jnp.max(s_past, axis=1, keepdims=True))
        acc = jnp.dot(jnp.exp(s_diag - m).astype(BF16), v_ref[r0:r0 + tq, sl],
                      preferred_element_type=F32)
        if qi > 0:
            acc = acc + jnp.dot(jnp.exp(s_past - m).astype(BF16), v_ref[0:r0, sl],
                                preferred_element_type=F32)
        out = acc / acc[:, DENOM_LANE:DENOM_LANE + 1]
        if hh == 0:
            first_head = out
        else:
            o_ref[r0:r0 + tq, :] = jnp.where(
                lane < FOX_HEAD_DIM, first_head,
                pltpu.roll(out, FOX_HEAD_DIM, axis=1)).astype(o_ref.dtype)
    _cast_slabs(cast_in, cast_out)


def _attention(q_aug, k_aug, v_aug, cast_weights, *, batch, seq, tq):
    t = q_aug.shape[0]
    pairs = FOX_HEADS // 2
    steps = batch * pairs
    spec = pl.BlockSpec((seq, 2 * LANES), lambda b, hp: (b, hp))
    out_spec = pl.BlockSpec((seq, LANES), lambda b, hp: (b, hp))
    cast_specs = _cast_specs(cast_weights, steps, lambda b, hp: b * pairs + hp)
    outs = pl.pallas_call(
        functools.partial(_attn_kernel, tq=tq, n_cast=len(cast_weights)),
        grid=(batch, pairs),
        in_specs=[spec, spec, spec, *cast_specs],
        out_specs=[out_spec, *cast_specs],
        out_shape=[jax.ShapeDtypeStruct((t, FOX_WIDTH), BF16)]
        + [jax.ShapeDtypeStruct(w.shape, BF16) for w in cast_weights],
        compiler_params=_params("parallel", "parallel"),
        name="fox_attention",
    )(q_aug, k_aug, v_aug, *cast_weights)
    return outs[0], outs[1:]


MXU_TILE = 256


def _swiglu_partial(h, w1, w3, w2):
    a = jnp.dot(h, w1, preferred_element_type=F32)
    b = jnp.dot(h, w3, preferred_element_type=F32)
    act = a * _sigmoid(a) * b
    return jnp.dot(act.astype(BF16), w2, preferred_element_type=F32)


def _ff_splits(f):
    cut = -(-(f // MXU_TILE) // 2) * MXU_TILE
    return ((0, cut), (cut, f)) if 0 < cut < f else ((0, f),)


def _swiglu(h, w1_ref, w3_ref, w2_ref):
    y = None
    for lo, hi in _ff_splits(w1_ref.shape[1]):
        part = _swiglu_partial(h, w1_ref[:, lo:hi], w3_ref[:, lo:hi], w2_ref[lo:hi, :])
        y = part if y is None else y + part
    return y


def _even_tail_kernel(x_ref, att_ref, p_ref, halo_ref, wpool_ref, scale_ref,
                      wo_att_ref, wo_pool_ref, g_ref, w1_ref, w3_ref, w2_ref, o_ref,
                      *, tiles_per_seq):
    i = pl.program_id(0)
    tm = x_ref.shape[0]
    tile_in_seq = i % tiles_per_seq
    p = p_ref[...]
    halo = jnp.where(tile_in_seq == 0, 0.0, halo_ref[...])
    ext = jnp.concatenate([halo, p], axis=0)
    pos = tile_in_seq * tm + lax.broadcasted_iota(jnp.int32, (tm, 1), 0)
    count = (pos + 1).astype(F32)
    mixed = []
    for gi, w in enumerate(POOL_WINDOWS):
        sl = slice(gi * POOL_GROUP_DIM, (gi + 1) * POOL_GROUP_DIM)
        acc = ext[:, sl]
        sh = 1
        while sh < w:
            acc = acc + pltpu.roll(acc, sh, axis=0)
            sh *= 2
        mean = acc[POOL_HALO:, :] / jnp.minimum(count, float(w))
        pooled = (mean - p[:, sl]).astype(BF16)
        mixed.append(jnp.dot(pooled, wpool_ref[gi], preferred_element_type=F32))
    pool = (jnp.concatenate(mixed, axis=1) * scale_ref[...]).astype(BF16)
    y = jnp.dot(att_ref[...], wo_att_ref[...], preferred_element_type=F32)
    y = y + jnp.dot(pool, wo_pool_ref[...], preferred_element_type=F32)
    x1 = x_ref[...] + y
    h = _rms(x1, g_ref[...]).astype(BF16)
    o_ref[...] = x1 + _swiglu(h, w1_ref, w3_ref, w2_ref)


def _even_tail(x2d, att, p_in, w_pool, pool_scale, w_out, ffn_gain, w1, w3, w2,
               *, seq, tm):
    t, d = x2d.shape
    resident = lambda w: pl.BlockSpec(w.shape, lambda i: (0,) * w.ndim,
                                      pipeline_mode=pl.Buffered(1))
    wo_att = w_out[:FOX_WIDTH]
    wo_pool = w_out[FOX_WIDTH:]
    halo_blocks = tm // POOL_HALO
    kern = functools.partial(_even_tail_kernel, tiles_per_seq=seq // tm)
    return pl.pallas_call(
        kern,
        grid=(t // tm,),
        in_specs=[
            pl.BlockSpec((tm, d), lambda i: (i, 0)),
            pl.BlockSpec((tm, FOX_WIDTH), lambda i: (i, 0)),
            pl.BlockSpec((tm, POOL_WIDTH), lambda i: (i, 0)),
            pl.BlockSpec((POOL_HALO, POOL_WIDTH),
                         lambda i: (jnp.maximum(i * halo_blocks - 1, 0), 0)),
            resident(w_pool),
            pl.BlockSpec((1, POOL_WIDTH), lambda i: (0, 0)),
            resident(wo_att), resident(wo_pool),
            pl.BlockSpec((1, d), lambda i: (0, 0)),
            resident(w1), resident(w3), resident(w2),
        ],
        out_specs=pl.BlockSpec((tm, d), lambda i: (i, 0)),
        out_shape=jax.ShapeDtypeStruct((t, d), F32),
        compiler_params=_params("parallel"),
        name="even_tail",
    )(x2d, att, p_in, p_in, w_pool, pool_scale, wo_att, wo_pool, ffn_gain, w1, w3, w2)


def _gelu_tanh(x):
    c = math.sqrt(2.0 / math.pi)
    return 0.5 * x * (1.0 + jnp.tanh(c * (x + 0.044715 * (x * x * x))))


def _s5_kernel(x_ref, g_ref, win_ref, perm_ref, permt_ref, bblk_ref, cblk_ref,
               are_ref, aim_ref, d_ref, o_ref, zero_ref, xs_ref, st_ref, u_ref, act_ref):
    nb, tc, d = x_ref.shape
    rows = nb * tc
    n_slabs = d // LANES
    sw = 2 * SSM_SLAB_STATE

    @pl.when(pl.program_id(0) == 0)
    def _():
        st_ref[...] = jnp.zeros_like(st_ref)

    x = x_ref[...].reshape(rows, d)
    h = _rms(x, g_ref[...]).astype(BF16)
    h_tb = jnp.dot(perm_ref[...], h, preferred_element_type=F32).astype(BF16)
    u_ref[...] = jnp.dot(h_tb, win_ref[...], preferred_element_type=F32)
    hs = SSM_SLAB_STATE

    def input_matmul(s):
        xs_ref[:, s * sw:(s + 1) * sw] = jnp.dot(
            u_ref[:, s * LANES:(s + 1) * LANES].astype(BF16), bblk_ref[s],
            preferred_element_type=F32)

    def output_matmul(s):
        sl = slice(s * LANES, (s + 1) * LANES)
        y = jnp.dot(xs_ref[:, s * sw:(s + 1) * sw].astype(BF16), cblk_ref[s],
                    preferred_element_type=F32) + d_ref[:, sl] * u_ref[:, sl]
        act_ref[:, sl] = _gelu_tanh(y).astype(BF16)

    ahead = 2
    for s in range(min(ahead, n_slabs)):
        input_matmul(s)
    for s in range(n_slabs):
        lo = s * sw
        if s + ahead < n_slabs:
            input_matmul(s + ahead)
        a_r = are_ref[:, lo:lo + hs]
        a_i = aim_ref[:, lo:lo + hs]
        x_r = st_ref[:, lo:lo + hs]
        x_i = st_ref[:, lo + hs:lo + sw]
        for t in range(tc):
            r0 = t * SUBLANES
            new_r = a_r * x_r - a_i * x_i + xs_ref[r0:r0 + SUBLANES, lo:lo + hs]
            new_i = a_r * x_i + a_i * x_r + xs_ref[r0:r0 + SUBLANES, lo + hs:lo + sw]
            xs_ref[r0:r0 + SUBLANES, lo:lo + hs] = new_r
            xs_ref[r0:r0 + SUBLANES, lo + hs:lo + sw] = new_i
            x_r, x_i = new_r, new_i
        st_ref[:, lo:lo + hs] = x_r
        st_ref[:, lo + hs:lo + sw] = x_i
        if s >= 1:
            output_matmul(s - 1)
    output_matmul(n_slabs - 1)
    g_bt = jnp.dot(permt_ref[...], act_ref[...], preferred_element_type=F32)
    o_ref[...] = g_bt.reshape(nb, tc, d).astype(o_ref.dtype)
    zero_ref[...] = jnp.zeros_like(zero_ref)


def _s5_mixer(x3d, gain, w_in, bblk, cblk, a_re, a_im, d_skip, *, tc, zero_shape):
    nb, seq, d = x3d.shape
    steps = seq // tc
    zero_rows = zero_shape[0] // steps
    assert zero_rows * steps == zero_shape[0] and zero_rows % SEG_ALIGN == 0
    rows = nb * tc
    n_slabs = d // LANES
    sw = 2 * SSM_SLAB_STATE
    r = np.arange(rows)
    perm = np.zeros((rows, rows), np.float32)
    perm[(r % tc) * nb + r // tc, r] = 1.0
    perm_j = jnp.asarray(perm, BF16)
    permt_j = jnp.asarray(perm.T, BF16)
    kern = _s5_kernel
    full2 = lambda a: pl.BlockSpec(a.shape, lambda i: (0, 0))
    full3 = lambda a: pl.BlockSpec(a.shape, lambda i: (0, 0, 0))
    return pl.pallas_call(
        kern,
        grid=(seq // tc,),
        in_specs=[
            pl.BlockSpec((nb, tc, d), lambda i: (0, i, 0)),
            full2(gain), full2(w_in), full2(perm_j), full2(permt_j),
            full3(bblk), full3(cblk), full2(a_re), full2(a_im), full2(d_skip),
        ],
        out_specs=[pl.BlockSpec((nb, tc, d), lambda i: (0, i, 0)),
                   pl.BlockSpec((zero_rows, zero_shape[1]), lambda i: (i, 0))],
        out_shape=[jax.ShapeDtypeStruct((nb, seq, d), BF16),
                   jax.ShapeDtypeStruct(zero_shape, BF16)],
        scratch_shapes=[
            pltpu.VMEM((rows, n_slabs * sw), F32),
            pltpu.VMEM((SUBLANES, n_slabs * sw), F32),
            pltpu.VMEM((rows, d), F32),
            pltpu.VMEM((rows, d), BF16),
        ],
        compiler_params=_params("arbitrary"),
        name="s5_mixer",
    )(x3d, gain, w_in, perm_j, permt_j, bblk, cblk, a_re, a_im, d_skip)


def _s5_coefficients(a_re, a_im, log_dt, b_re, b_im, c_re, c_im):
    dt = jnp.exp(log_dt.astype(F32))[:, None]
    ar = a_re.astype(F32)
    ai = a_im.astype(F32)
    mag = jnp.exp(ar * dt)
    abar_re = mag * jnp.cos(ai * dt)
    abar_im = mag * jnp.sin(ai * dt)
    den = ar * ar + ai * ai
    nr = abar_re - 1.0
    ni = abar_im
    coef_re = (nr * ar + ni * ai) / den
    coef_im = (ni * ar - nr * ai) / den
    br = b_re.astype(F32)
    bi = b_im.astype(F32)
    bbar_re = coef_re[..., None] * br - coef_im[..., None] * bi
    bbar_im = coef_re[..., None] * bi + coef_im[..., None] * br
    n_groups = ar.shape[0]
    n_slabs = n_groups // SSM_SLAB_GROUPS
    row_group = np.arange(LANES)[:, None] // SSM_GROUP
    col_group = np.arange(SSM_SLAB_STATE)[None, :] // SSM_STATE
    in_mask = jnp.asarray(row_group == col_group, F32)

    def in_block(bb):
        bb = bb.reshape(n_slabs, SSM_SLAB_GROUPS, SSM_STATE, SSM_GROUP)
        bb = bb.transpose(0, 1, 3, 2).reshape(n_slabs, LANES, SSM_STATE)
        return jnp.tile(bb, (1, 1, SSM_SLAB_GROUPS)) * in_mask

    def out_block(cc):
        cc = cc.reshape(n_slabs, SSM_SLAB_GROUPS, SSM_GROUP, SSM_STATE)
        cc = cc.transpose(0, 1, 3, 2).reshape(n_slabs, SSM_SLAB_STATE, SSM_GROUP)
        return jnp.tile(cc, (1, 1, SSM_SLAB_GROUPS)) * in_mask.T

    bblk = jnp.concatenate([in_block(bbar_re), in_block(bbar_im)], axis=2)
    cblk = jnp.concatenate([out_block(c_re.astype(F32)),
                            -out_block(c_im.astype(F32))], axis=1)

    def lanes(a):
        a = a.reshape(n_slabs, 1, SSM_SLAB_STATE)
        a = jnp.concatenate([a, a], axis=2).reshape(1, -1)
        return jnp.broadcast_to(a, (SUBLANES, a.shape[1]))

    return (bblk.astype(BF16), cblk.astype(BF16), lanes(abar_re), lanes(abar_im))


MOE_CHUNK = 512
MOE_FIRST = 208
MOE_REST = MOE_CHUNK + SEG_ALIGN - MOE_FIRST
MOE_PIECES = ((0, MOE_FIRST), (MOE_FIRST, MOE_REST))
MOE_WINDOW = MOE_FIRST + MOE_REST
NOT_ROUTED = -1024.0
MOE_TILE = 512


def _glu_router_kernel(x_ref, act_ref, wa_ref, wb_ref, g_ref, w_ref, b_ref,
                       x3_ref, h_ref, pos_ref, post_ref, cnt_ref):
    tm, d = x_ref.shape
    parts = 2
    half = tm // parts

    def glu(r0):
        act = act_ref[r0:r0 + half, :]
        a = jnp.dot(act, wa_ref[...], preferred_element_type=F32)
        b = jnp.dot(act, wb_ref[...], preferred_element_type=F32)
        x3 = x_ref[r0:r0 + half, :] + a * _sigmoid(b)
        x3_ref[r0:r0 + half, :] = x3
        return x3

    def route(r0, x3):
        h = _rms(x3, g_ref[...])
        h_hi = h.astype(BF16)
        h_ref[r0:r0 + half, 0:d] = h_hi
        h_lo = (h - h_hi.astype(F32)).astype(BF16)
        p_hi = jnp.dot(h_hi, w_ref[...], preferred_element_type=F32)
        p_lo = jnp.dot(h_lo, w_ref[...], preferred_element_type=F32)
        logits = p_hi + pltpu.roll(p_hi, LANES - N_EXPERTS, axis=1) + p_lo + b_ref[...]
        lane = lax.broadcasted_iota(jnp.int32, logits.shape, 1)
        logits = jnp.where(lane < N_EXPERTS, logits, -jnp.inf)
        m1 = jnp.max(logits, axis=1, keepdims=True)
        i1 = jnp.min(jnp.where(logits == m1, lane, LANES), axis=1, keepdims=True)
        rest = jnp.where(lane == i1, -jnp.inf, logits)
        m2 = jnp.max(rest, axis=1, keepdims=True)
        i2 = jnp.min(jnp.where(rest == m2, lane, LANES), axis=1, keepdims=True)
        e2 = jnp.exp(m2 - m1)
        g1 = 1.0 / (1.0 + e2)
        g2 = e2 / (1.0 + e2)
        gate = jnp.where(lane == i1, g1, 0.0) + jnp.where(lane == i2, g2, 0.0)
        g_hi = gate.astype(BF16).astype(F32)
        g_r = gate - g_hi
        g_mid = g_r.astype(BF16).astype(F32)
        packed = jnp.where(lane < N_EXPERTS, g_hi,
                           jnp.where(lane < 2 * N_EXPERTS,
                                     pltpu.roll(g_mid, N_EXPERTS, axis=1),
                                     pltpu.roll(g_r - g_mid, 2 * N_EXPERTS, axis=1)))
        h_ref[r0:r0 + half, d:d + LANES] = packed.astype(BF16)
        return jnp.where((lane == i1) | (lane == i2), 1.0, 0.0)

    members = []
    x3 = glu(0)
    for p in range(parts):
        nxt = glu((p + 1) * half) if p + 1 < parts else None
        members.append(route(p * half, x3))
        x3 = nxt
    member = jnp.concatenate(members, axis=0)
    row = lax.broadcasted_iota(jnp.int32, member.shape, 0)
    c = member
    sh = 1
    while sh < tm:
        c = c + jnp.where(row >= sh, pltpu.roll(c, sh, axis=0), 0.0)
        sh *= 2
    pos = jnp.where(member > 0.0, c - member, NOT_ROUTED)
    pos_ref[...] = pos
    post_ref[0] = pos.T[0:SUBLANES, :]
    cnt_ref[0] = jnp.broadcast_to(c[tm - 1:tm, :], (SUBLANES, LANES))


def _glu_router(x2d, act2d, wa, wb, gain, w_pad, b_pad):
    t, d = x2d.shape
    tm = MOE_CHUNK
    n_chunks = t // tm
    return pl.pallas_call(
        _glu_router_kernel,
        grid=(n_chunks,),
        in_specs=[
            pl.BlockSpec((tm, d), lambda i: (i, 0)),
            pl.BlockSpec((tm, d), lambda i: (i, 0)),
            pl.BlockSpec(wa.shape, lambda i: (0, 0)),
            pl.BlockSpec(wb.shape, lambda i: (0, 0)),
            pl.BlockSpec((1, d), lambda i: (0, 0)),
            pl.BlockSpec((d, LANES), lambda i: (0, 0)),
            pl.BlockSpec((1, LANES), lambda i: (0, 0)),
        ],
        out_specs=[
            pl.BlockSpec((tm, d), lambda i: (i, 0)),
            pl.BlockSpec((tm, d + LANES), lambda i: (i, 0)),
            pl.BlockSpec((tm, LANES), lambda i: (i, 0)),
            pl.BlockSpec((1, SUBLANES, tm), lambda i: (i, 0, 0)),
            pl.BlockSpec((1, SUBLANES, LANES), lambda i: (i, 0, 0)),
        ],
        out_shape=[
            jax.ShapeDtypeStruct((t, d), F32),
            jax.ShapeDtypeStruct((t, d + LANES), BF16),
            jax.ShapeDtypeStruct((t, LANES), F32),
            jax.ShapeDtypeStruct((n_chunks, SUBLANES, tm), F32),
            jax.ShapeDtypeStruct((n_chunks, SUBLANES, LANES), F32),
        ],
        compiler_params=_params("parallel"),
        name="glu_router",
    )(x2d, act2d, wa, wb, gain, w_pad, b_pad)


MOE_SPARE = max(MOE_FIRST, MOE_REST)


def _moe_tiles(n_tokens, row_multiple):
    max_rows = (TOP_K * n_tokens + N_EXPERTS * (SEG_ALIGN - 1)
                + N_EXPERTS * (MOE_SPARE + MOE_TILE))
    unit = math.lcm(MOE_TILE, row_multiple)
    return -(-max_rows // unit) * (unit // MOE_TILE)


def _moe_layout(cnt):
    before = jnp.cumsum(cnt, axis=0) - cnt
    shift = before % SEG_ALIGN
    used = (jnp.sum(cnt, axis=0) + SEG_ALIGN - 1) // SEG_ALIGN * SEG_ALIGN
    padded = (used + MOE_SPARE + MOE_TILE - 1) // MOE_TILE * MOE_TILE
    ends = jnp.cumsum(padded)
    start = ends - padded
    off = start[None, :] + before - shift
    flat = lambda a: a.reshape(-1).astype(jnp.int32)
    return (flat(off), flat(shift + cnt), flat(shift),
            ends.astype(jnp.int32), (start + used).astype(jnp.int32))


def _tile_expert(i, ends_ref):
    first_row = i * MOE_TILE
    e = 0
    for k in range(N_EXPERTS - 1):
        e = e + (first_row >= ends_ref[k]).astype(jnp.int32)
    return e


def _segment_copies(hbm_ref, buf_ref, sem_ref, off_ref, chunk, to_hbm, slot=()):
    copies = []
    for e in range(N_EXPERTS):
        off = pl.multiple_of(off_ref[chunk * N_EXPERTS + e], SEG_ALIGN)
        for piece, (first, rows) in enumerate(MOE_PIECES):
            hbm = hbm_ref.at[pl.ds(off + first, rows)]
            buf = buf_ref.at[(*slot, e, pl.ds(first, rows))]
            src, dst = (buf, hbm) if to_hbm else (hbm, buf)
            copies.append(pltpu.make_async_copy(src, dst, sem_ref.at[(*slot, e, piece)]))
    return copies


def _onehot(first, shape, axis, target):
    rank = first + lax.broadcasted_iota(jnp.int32, shape, axis)
    return jnp.where(rank.astype(F32) == target, 1.0, 0.0).astype(BF16)


def _gather_kernel(off_ref, need_ref, shift_ref, h_ref, post_ref, zeros_hbm, o_hbm,
                   stage_ref, carry_ref, sem_ref):
    del zeros_hbm
    c = pl.program_id(0)
    slot = c % 2
    h = h_ref[...]

    @pl.when(c == 0)
    def _():
        carry_ref[...] = jnp.zeros_like(carry_ref)

    def long_segment(chunk, e):
        return need_ref[chunk * N_EXPERTS + e] >= MOE_FIRST

    def piece_onehot(e, piece):
        first, rows = MOE_PIECES[piece]
        return _onehot(first - shift_ref[c * N_EXPERTS + e], (rows, MOE_CHUNK), 0,
                       post_ref[0, e:e + 1, :])

    def fill_piece(e, piece, onehot=None):
        first, rows = MOE_PIECES[piece]
        onehot = piece_onehot(e, piece) if onehot is None else onehot
        stage_ref[slot, e, first:first + rows] = jnp.dot(
            onehot, h, preferred_element_type=F32).astype(BF16)
        if piece == 0:
            stage_ref[slot, e, 0:SEG_ALIGN] += carry_ref[e]

    onehot = piece_onehot(0, 0)
    for e in range(N_EXPERTS):
        nxt = piece_onehot(e + 1, 0) if e + 1 < N_EXPERTS else None
        fill_piece(e, 0, onehot)
        onehot = nxt
    for e in range(N_EXPERTS):
        @pl.when(long_segment(c, e))
        def _(e=e):
            fill_piece(e, 1)

    for e in range(N_EXPERTS):
        last_group = need_ref[c * N_EXPERTS + e] // SEG_ALIGN * SEG_ALIGN
        carry_ref[e] = stage_ref[slot, e, pl.ds(pl.multiple_of(last_group, SEG_ALIGN),
                                                SEG_ALIGN)]

    def each_copy(chunk, sl, action):
        copies = _segment_copies(o_hbm, stage_ref, sem_ref, off_ref, chunk,
                                 to_hbm=True, slot=(sl,))
        for e in range(N_EXPERTS):
            action(copies[2 * e])

            @pl.when(long_segment(chunk, e))
            def _(e=e):
                action(copies[2 * e + 1])

    @pl.when(c > 0)
    def _():
        each_copy(c - 1, 1 - slot, lambda cp: cp.wait())

    each_copy(c, slot, lambda cp: cp.start())

    @pl.when(c == pl.num_programs(0) - 1)
    def _():
        each_copy(c, slot, lambda cp: cp.wait())


def _gather(h2d, post, off, need, shift, zero_buf):
    t, d = h2d.shape
    n_chunks = t // MOE_CHUNK
    grid_spec = pltpu.PrefetchScalarGridSpec(
        num_scalar_prefetch=3,
        grid=(n_chunks,),
        in_specs=[
            pl.BlockSpec((MOE_CHUNK, d), lambda i, *_: (i, 0)),
            pl.BlockSpec((1, SUBLANES, MOE_CHUNK), lambda i, *_: (i, 0, 0)),
            pl.BlockSpec(memory_space=pl.ANY),
        ],
        out_specs=pl.BlockSpec(memory_space=pl.ANY),
        scratch_shapes=[
            pltpu.VMEM((2, N_EXPERTS, MOE_WINDOW, d), BF16),
            pltpu.VMEM((N_EXPERTS, SEG_ALIGN, d), BF16),
            pltpu.SemaphoreType.DMA((2, N_EXPERTS, 2)),
        ],
    )
    return pl.pallas_call(
        _gather_kernel,
        grid_spec=grid_spec,
        out_shape=jax.ShapeDtypeStruct(zero_buf.shape, BF16),
        input_output_aliases={5: 0},
        compiler_params=_params("arbitrary"),
        name="moe_gather",
    )(off, need, shift, h2d, post, zero_buf)


def _expert_kernel(ends_ref, used_end_ref, x_ref, w1_ref, w3_ref, w2_ref, o_ref):
    i = pl.program_id(0)
    expert = _tile_expert(i, ends_ref)
    rows = jnp.clip(used_end_ref[expert] - i * MOE_TILE, 0, MOE_TILE)
    d = o_ref.shape[1]

    def run(n):
        row = lax.broadcasted_iota(jnp.int32, (n, x_ref.shape[1]), 0)
        x = jnp.where(row < rows, x_ref[0:n, :], jnp.zeros((), x_ref.dtype))
        lane = lax.broadcasted_iota(jnp.int32, (n, LANES), 1)
        mine = (lane < 3 * N_EXPERTS) & (jnp.bitwise_and(lane, N_EXPERTS - 1) == expert)
        gate = jnp.sum(jnp.where(mine, x[:, d:d + LANES].astype(F32), 0.0),
                       axis=1, keepdims=True)
        y = _swiglu(x[:, 0:d], w1_ref, w3_ref, w2_ref)
        o_ref[0:n, :] = (gate * y).astype(o_ref.dtype)

    half = MOE_TILE // 2

    @pl.when(rows > half)
    def _():
        run(MOE_TILE)

    @pl.when((rows > 0) & (rows <= half))
    def _():
        run(half)
        o_ref[half:MOE_TILE, :] = jnp.zeros((MOE_TILE - half, d), o_ref.dtype)

    @pl.when(rows == 0)
    def _():
        o_ref[...] = jnp.zeros_like(o_ref)


def _experts(x_sorted, w1, w3, w2, ends, used_end):
    assert N_EXPERTS & (N_EXPERTS - 1) == 0
    n_rows = x_sorted.shape[0]
    f = w1.shape[1]
    d = w2.shape[1]
    grid_spec = pltpu.PrefetchScalarGridSpec(
        num_scalar_prefetch=2,
        grid=(n_rows // MOE_TILE,),
        in_specs=[
            pl.BlockSpec((MOE_TILE, d + LANES), lambda i, ends, ue: (i, 0)),
            pl.BlockSpec((d, f), lambda i, ends, ue: (_tile_expert(i, ends), 0)),
            pl.BlockSpec((d, f), lambda i, ends, ue: (_tile_expert(i, ends), 0)),
            pl.BlockSpec((f, d), lambda i, ends, ue: (_tile_expert(i, ends), 0)),
        ],
        out_specs=pl.BlockSpec((MOE_TILE, d), lambda i, ends, ue: (i, 0)),
    )
    return pl.pallas_call(
        _expert_kernel,
        grid_spec=grid_spec,
        out_shape=jax.ShapeDtypeStruct((n_rows, d), BF16),
        compiler_params=_params("arbitrary"),
        name="moe_experts",
    )(ends, used_end, x_sorted, w1, w3, w2)


def _combine_kernel(off_ref, need_ref, shift_ref, x_ref, pos_ref, fg_ref, y_hbm,
                    o_ref, ybuf_ref, acc_ref, sem_ref):
    c = pl.program_id(0)
    n_chunks = pl.num_programs(0)
    slot = c % 2

    def fetch(chunk, sl, start):
        copies = _segment_copies(y_hbm, ybuf_ref, sem_ref, off_ref, chunk,
                                 to_hbm=False, slot=(sl,))
        for e in range(N_EXPERTS):
            first, second = copies[2 * e], copies[2 * e + 1]
            if start:
                first.start()
            else:
                first.wait()

            @pl.when(need_ref[chunk * N_EXPERTS + e] >= MOE_FIRST)
            def _(second=second):
                if start:
                    second.start()
                else:
                    second.wait()

    @pl.when(c == 0)
    def _():
        fetch(c, slot, True)

    @pl.when(c + 1 < n_chunks)
    def _():
        fetch(c + 1, 1 - slot, True)

    fetch(c, slot, False)

    def piece_onehot(e, piece):
        first, rows = MOE_PIECES[piece]
        return _onehot(first - shift_ref[c * N_EXPERTS + e], (MOE_CHUNK, rows), 1,
                       pos_ref[:, e:e + 1])

    def piece_rows(e, piece, onehot=None):
        first, rows = MOE_PIECES[piece]
        onehot = piece_onehot(e, piece) if onehot is None else onehot
        return jnp.dot(onehot, ybuf_ref[slot, e, first:first + rows],
                       preferred_element_type=F32)

    acc = x_ref[...]
    onehot = piece_onehot(0, 0)
    for e in range(N_EXPERTS):
        nxt = piece_onehot(e + 1, 0) if e + 1 < N_EXPERTS else None
        acc = acc + piece_rows(e, 0, onehot)
        onehot = nxt
    acc_ref[...] = acc
    for e in range(N_EXPERTS):
        @pl.when(need_ref[c * N_EXPERTS + e] >= MOE_FIRST)
        def _(e=e):
            acc_ref[...] += piece_rows(e, 1)

    o_ref[...] = _rms(acc_ref[...], fg_ref[...])


def _combine(x2d, pos, final_gain, y_sorted, off, need, shift):
    t, d = x2d.shape
    n_chunks = t // MOE_CHUNK
    grid_spec = pltpu.PrefetchScalarGridSpec(
        num_scalar_prefetch=3,
        grid=(n_chunks,),
        in_specs=[
            pl.BlockSpec((MOE_CHUNK, d), lambda i, *_: (i, 0)),
            pl.BlockSpec((MOE_CHUNK, LANES), lambda i, *_: (i, 0)),
            pl.BlockSpec((1, d), lambda i, *_: (0, 0)),
            pl.BlockSpec(memory_space=pl.ANY),
        ],
        out_specs=pl.BlockSpec((MOE_CHUNK, d), lambda i, *_: (i, 0)),
        scratch_shapes=[
            pltpu.VMEM((2, N_EXPERTS, MOE_WINDOW, d), BF16),
            pltpu.VMEM((MOE_CHUNK, d), F32),
            pltpu.SemaphoreType.DMA((2, N_EXPERTS, 2)),
        ],
    )
    return pl.pallas_call(
        _combine_kernel,
        grid_spec=grid_spec,
        out_shape=jax.ShapeDtypeStruct((t, d), F32),
        compiler_params=_params("arbitrary"),
        name="moe_combine",
    )(off, need, shift, x2d, pos, final_gain, y_sorted)


def _glu_moe(x2d, act2d, wa, wb, gain, router_w, router_b, w1, w3, w2, final_gain,
             zero_buf):
    t, d = x2d.shape
    rw_hi = router_w.astype(BF16)
    rw_lo = (router_w.astype(F32) - rw_hi.astype(F32)).astype(BF16)
    rw = jnp.pad(jnp.concatenate([rw_hi, rw_lo], axis=1),
                 ((0, 0), (0, LANES - 2 * N_EXPERTS)))
    rb = jnp.pad(router_b.astype(F32), (0, LANES - N_EXPERTS)).reshape(1, LANES)
    x3, h, pos, post, cnt = _glu_router(x2d, act2d, wa, wb, gain, rw, rb)
    cnt = cnt[:, 0, :N_EXPERTS].astype(jnp.int32)
    off, need, shift, ends, used_end = _moe_layout(cnt)
    h_sorted = _gather(h, post, off, need, shift, zero_buf)
    y_sorted = _experts(h_sorted, w1, w3, w2, ends, used_end)
    return _combine(x3, pos, final_gain, y_sorted, off, need, shift)


def _even_weights(w_in, b_forget):
    n = w_in.shape[1]
    w_all = jnp.pad(w_in.astype(BF16), ((0, 0), (0, -n % LANES)))
    bias = jnp.pad(b_forget.astype(F32), (0, LANES - FOX_HEADS)).reshape(1, LANES)
    qw = FOX_HEADS * LANES
    place = np.zeros((LANES, 2 * qw), np.float32)
    for hh in range(FOX_HEADS):
        for piece in range(3):
            place[piece * FOX_HEADS + hh, hh * LANES + BIAS_LANE + piece] = 1.0
            place[piece * FOX_HEADS + hh, qw + hh * LANES + ONES_LANE + piece] = -1.0
    return w_all, bias, jnp.asarray(place, BF16)


def kernel(x, even_mix_norm, even_w_in, even_b_forget, even_w_pool, even_pool_scale, even_w_out, even_ffn_norm, even_ffn_w1, even_ffn_w3, even_ffn_w2, odd_mix_norm, odd_w_in, ssm_a_re, ssm_a_im, ssm_log_dt, ssm_b_re, ssm_b_im, ssm_c_re, ssm_c_im, ssm_d, odd_w_glu_a, odd_w_glu_b, odd_moe_norm, router_w, router_b, expert_w1, expert_w3, expert_w2, final_norm):
    b, s, d = x.shape
    t = b * s
    assert b == SUBLANES, "the S5 recurrence keeps one batch row per sublane"
    x2d = x.reshape(t, d)
    row = lambda v: v.reshape(1, -1).astype(F32)

    w_all, bias, place = _even_weights(even_w_in[0], even_b_forget[0])
    (q_aug, k_aug, v, p_in), casted = _even_inproj(
        x2d, row(even_mix_norm[0]), w_all, bias, place,
        [even_ffn_w1[0], even_ffn_w3[0], even_ffn_w2[0], even_w_out[0],
         even_w_pool[0].reshape(-1, POOL_GROUP_DIM),
         odd_w_in[0], odd_w_glu_a[0], odd_w_glu_b[0]],
        seq=s, tm=512)
    ffn_w1, ffn_w3, ffn_w2, w_out, w_pool, s5_w_in, glu_a, glu_b = casted
    n_e, _, f = expert_w1[0].shape
    att, (ew1, ew3, ew2) = _attention(
        q_aug, k_aug, v,
        [expert_w1[0].reshape(n_e * d, f), expert_w3[0].reshape(n_e * d, f),
         expert_w2[0].reshape(n_e * f, d)],
        batch=b, seq=s, tq=256)
    x2 = _even_tail(x2d, att, p_in, w_pool.reshape(even_w_pool[0].shape),
                    row(even_pool_scale[0]), w_out, row(even_ffn_norm[0]),
                    ffn_w1, ffn_w3, ffn_w2, seq=s, tm=512)

    bblk, cblk, a_re, a_im = _s5_coefficients(
        ssm_a_re[0], ssm_a_im[0], ssm_log_dt[0], ssm_b_re[0], ssm_b_im[0],
        ssm_c_re[0], ssm_c_im[0])
    moe_rows = _moe_tiles(t, (s // S5_STEP) * SEG_ALIGN) * MOE_TILE
    g, zero_buf = _s5_mixer(x2.reshape(b, s, d), row(odd_mix_norm[0]),
                            s5_w_in, bblk, cblk, a_re, a_im,
                            row(ssm_d[0]), tc=S5_STEP,
                            zero_shape=(moe_rows, d + LANES))
    out = _glu_moe(x2, g.reshape(t, d), glu_a, glu_b, row(odd_moe_norm[0]),
                   router_w[0], router_b[0], ew1, ew3, ew2,
                   row(final_norm), zero_buf)
    return out.reshape(b, s, d)
```

```python
import functools
import math

import numpy as np
import jax
import jax.numpy as jnp
from jax import lax
from jax.experimental import pallas as pl
from jax.experimental.pallas import tpu as pltpu

F32 = jnp.float32
BF16 = jnp.bfloat16

EPS = 1e-6
NEG_INF = -1e30
LANES = 128
SUBLANES = 8
SEG_ALIGN = 16
VMEM_LIMIT = 56 * 1024 * 1024

FOX_HEADS = 8
FOX_HEAD_DIM = 64
FOX_WIDTH = FOX_HEADS * FOX_HEAD_DIM
POOL_WINDOWS = (2, 4, 8, 16)
POOL_GROUP_DIM = 128
POOL_WIDTH = len(POOL_WINDOWS) * POOL_GROUP_DIM
POOL_HALO = 16
SSM_GROUP = 16
SSM_STATE = 64
SSM_SLAB_GROUPS = LANES // SSM_GROUP
SSM_SLAB_STATE = SSM_SLAB_GROUPS * SSM_STATE

INPROJ_ROWS = 512
ATTN_QUERY_TILE = 256
EVEN_TAIL_ROWS = 512
S5_STEP = 64
S5_PERM_STEPS = 32
N_EXPERTS = 8
TOP_K = 2

BIAS_LANE = FOX_HEAD_DIM
ONES_LANE = FOX_HEAD_DIM + 3
DENOM_LANE = FOX_HEAD_DIM


def _params(*sem):
    return pltpu.CompilerParams(dimension_semantics=sem,
                                vmem_limit_bytes=VMEM_LIMIT)


def _rms(x, g):
    ms = jnp.mean(x * x, axis=-1, keepdims=True)
    return x * lax.rsqrt(ms + EPS) * g


def _sigmoid(x):
    return 1.0 / (1.0 + jnp.exp(-x))


def _lane_range_ones(lo, hi):
    lane = lax.broadcasted_iota(jnp.int32, (1, LANES), 1)
    return jnp.where((lane >= lo) & (lane < hi), 1.0, 0.0).astype(F32)


def _cast_specs(weights, steps, step_index):
    specs = []
    for w in weights:
        rows = -(-w.shape[0] // (steps * SEG_ALIGN)) * SEG_ALIGN
        last = -(-w.shape[0] // rows) - 1
        specs.append(pl.BlockSpec(
            (rows, w.shape[1]),
            lambda *idx, last=last: (jnp.minimum(step_index(*idx), last), 0)))
    return specs


def _cast_slabs(src_refs, dst_refs):
    for src_ref, dst_ref in zip(src_refs, dst_refs):
        dst_ref[...] = src_ref[...].astype(dst_ref.dtype)


def _even_inproj_kernel(x_ref, g_ref, w_ref, bias_ref, place_ref, *rest,
                        tiles_per_seq, n_cast):
    cast_in, rest = rest[:n_cast], rest[n_cast:]
    q_ref, k_ref, v_ref, p_ref = rest[:4]
    cast_out, carry_ref = rest[4:4 + n_cast], rest[4 + n_cast]
    _cast_slabs(cast_in, cast_out)
    i = pl.program_id(0)
    tm = x_ref.shape[0]
    parts = 2
    rows = tm // parts
    fw = FOX_WIDTH
    qw = FOX_HEADS * LANES

    @pl.when(i % tiles_per_seq == 0)
    def _():
        carry_ref[...] = jnp.zeros_like(carry_ref)

    def project(r0):
        h = _rms(x_ref[r0:r0 + rows, :], g_ref[...]).astype(BF16)
        return jnp.dot(h, w_ref[...], preferred_element_type=F32)

    def finish(r0, z, carry):
        out = slice(r0, r0 + rows)
        p_ref[out, :] = z[:, 3 * fw + FOX_HEADS:3 * fw + FOX_HEADS + POOL_WIDTH]
        fg = z[:, 3 * fw:3 * fw + LANES] + bias_ref[...]
        lf = jnp.minimum(fg, 0.0) - jnp.log1p(jnp.exp(-jnp.abs(fg)))
        row = lax.broadcasted_iota(jnp.int32, lf.shape, 0)
        c = lf
        sh = 1
        while sh < rows:
            c = c + jnp.where(row >= sh, pltpu.roll(c, sh, axis=0), 0.0)
            sh *= 2
        c = c + carry
        hi = c.astype(BF16).astype(F32)
        r1 = c - hi
        mid = r1.astype(BF16).astype(F32)
        lo = r1 - mid
        lane = lax.broadcasted_iota(jnp.int32, (rows, LANES), 1)
        packed = jnp.where(lane < FOX_HEADS, hi,
                           jnp.where(lane < 2 * FOX_HEADS,
                                     pltpu.roll(mid, FOX_HEADS, axis=1),
                                     pltpu.roll(lo, 2 * FOX_HEADS, axis=1)))
        placed = jnp.dot(packed.astype(BF16), place_ref[...],
                         preferred_element_type=F32)
        low = lane < FOX_HEAD_DIM
        ones_q = _lane_range_ones(ONES_LANE, ONES_LANE + 3)
        ones_k = _lane_range_ones(BIAS_LANE, BIAS_LANE + 3)
        ones_v = _lane_range_ones(DENOM_LANE, DENOM_LANE + 1)

        def head_lanes(base, hh):
            pair = z[:, base + (hh // 2) * LANES:base + (hh // 2 + 1) * LANES]
            return pltpu.roll(pair, FOX_HEAD_DIM, axis=1) if hh % 2 else pair

        q_scale = FOX_HEAD_DIM ** -0.5
        for hh in range(FOX_HEADS):
            sl = slice(hh * LANES, (hh + 1) * LANES)
            q_ref[out, sl] = jnp.where(low, head_lanes(0, hh) * q_scale,
                                       placed[:, sl] + ones_q).astype(BF16)
            k_ref[out, sl] = jnp.where(
                low, head_lanes(fw, hh),
                placed[:, qw + hh * LANES:qw + (hh + 1) * LANES] + ones_k).astype(BF16)
            v_ref[out, sl] = jnp.where(low, head_lanes(2 * fw, hh), ones_v).astype(BF16)
        return c[rows - 1:rows, :]

    carry = carry_ref[0:1, :]
    z = project(0)
    for part in range(parts):
        nxt = project((part + 1) * rows) if part + 1 < parts else None
        carry = finish(part * rows, z, carry)
        z = nxt
    carry_ref[...] = jnp.broadcast_to(carry, carry_ref.shape)


def _even_inproj(x2d, gain, w_all, bias, place, cast_weights, *, seq, tm):
    t, d = x2d.shape
    n = w_all.shape[1]
    qw = FOX_HEADS * LANES
    cast_specs = _cast_specs(cast_weights, t // tm, lambda i: i)
    kern = functools.partial(_even_inproj_kernel, tiles_per_seq=seq // tm,
                             n_cast=len(cast_weights))
    outs = pl.pallas_call(
        kern,
        grid=(t // tm,),
        in_specs=[
            pl.BlockSpec((tm, d), lambda i: (i, 0)),
            pl.BlockSpec((1, d), lambda i: (0, 0)),
            pl.BlockSpec((d, n), lambda i: (0, 0)),
            pl.BlockSpec((1, LANES), lambda i: (0, 0)),
            pl.BlockSpec(place.shape, lambda i: (0, 0)),
            *cast_specs,
        ],
        out_specs=[
            pl.BlockSpec((tm, qw), lambda i: (i, 0)),
            pl.BlockSpec((tm, qw), lambda i: (i, 0)),
            pl.BlockSpec((tm, qw), lambda i: (i, 0)),
            pl.BlockSpec((tm, POOL_WIDTH), lambda i: (i, 0)),
            *cast_specs,
        ],
        out_shape=[
            jax.ShapeDtypeStruct((t, qw), BF16),
            jax.ShapeDtypeStruct((t, qw), BF16),
            jax.ShapeDtypeStruct((t, qw), BF16),
            jax.ShapeDtypeStruct((t, POOL_WIDTH), F32),
        ] + [jax.ShapeDtypeStruct(w.shape, BF16) for w in cast_weights],
        scratch_shapes=[pltpu.VMEM((SUBLANES, LANES), F32)],
        compiler_params=_params("arbitrary"),
        name="even_inproj",
    )(x2d, gain, w_all, bias, place, *cast_weights)
    return outs[:4], outs[4:]


def _dot_nt(a, b):
    return lax.dot_general(a, b, (((1,), (1,)), ((), ())),
                           preferred_element_type=F32)


def _attn_kernel(q_ref, k_ref, v_ref, *rest, tq, n_cast):
    cast_in, o_ref, cast_out = rest[:n_cast], rest[n_cast], rest[n_cast + 1:]
    seq = q_ref.shape[0]
    row = lax.broadcasted_iota(jnp.int32, (tq, tq), 0)
    col = lax.broadcasted_iota(jnp.int32, (tq, tq), 1)
    causal = col <= row
    lane = lax.broadcasted_iota(jnp.int32, (tq, LANES), 1)
    work = [(qi, hh) for qi in range(seq // tq) for hh in range(2)]

    def scores(qi, hh):
        r0 = qi * tq
        sl = slice(hh * LANES, (hh + 1) * LANES)
        q = q_ref[r0:r0 + tq, sl]
        s_diag = jnp.where(causal, _dot_nt(q, k_ref[r0:r0 + tq, sl]), NEG_INF)
        s_past = _dot_nt(q, k_ref[0:r0, sl]) if qi > 0 else None
        return s_diag, s_past

    ahead = 2
    queue = [scores(*item) for item in work[:ahead]]
    first_head = None
    for n, (qi, hh) in enumerate(work):
        r0 = qi * tq
        sl = slice(hh * LANES, (hh + 1) * LANES)
        s_diag, s_past = queue.pop(0)
        if n + ahead < len(work):
            queue.append(scores(*work[n + ahead]))
        m = jnp.max(s_diag, axis=1, keepdims=True)
        if qi > 0:
            m = jnp.maximum(m, jnp.max(s_past, axis=1, keepdims=True))
        acc = jnp.dot(jnp.exp(s_diag - m).astype(BF16), v_ref[r0:r0 + tq, sl],
                      preferred_element_type=F32)
        if qi > 0:
            acc = acc + jnp.dot(jnp.exp(s_past - m).astype(BF16), v_ref[0:r0, sl],
                                preferred_element_type=F32)
        out = acc / acc[:, DENOM_LANE:DENOM_LANE + 1]
        if hh == 0:
            first_head = out
        else:
            o_ref[r0:r0 + tq, :] = jnp.where(
                lane < FOX_HEAD_DIM, first_head,
                pltpu.roll(out, FOX_HEAD_DIM, axis=1)).astype(o_ref.dtype)
    _cast_slabs(cast_in, cast_out)


def _attention(q_aug, k_aug, v_aug, cast_weights, *, batch, seq, tq):
    t = q_aug.shape[0]
    pairs = FOX_HEADS // 2
    steps = batch * pairs
    spec = pl.BlockSpec((seq, 2 * LANES), lambda b, hp: (b, hp))
    out_spec = pl.BlockSpec((seq, LANES), lambda b, hp: (b, hp))
    cast_specs = _cast_specs(cast_weights, steps, lambda b, hp: b * pairs + hp)
    outs = pl.pallas_call(
        functools.partial(_attn_kernel, tq=tq, n_cast=len(cast_weights)),
        grid=(batch, pairs),
        in_specs=[spec, spec, spec, *cast_specs],
        out_specs=[out_spec, *cast_specs],
        out_shape=[jax.ShapeDtypeStruct((t, FOX_WIDTH), BF16)]
        + [jax.ShapeDtypeStruct(w.shape, BF16) for w in cast_weights],
        compiler_params=_params("parallel", "parallel"),
        name="fox_attention",
    )(q_aug, k_aug, v_aug, *cast_weights)
    return outs[0], outs[1:]


MXU_TILE = 256


def _swiglu_partial(h, w1, w3, w2):
    a = jnp.dot(h, w1, preferred_element_type=F32)
    b = jnp.dot(h, w3, preferred_element_type=F32)
    act = a * _sigmoid(a) * b
    return jnp.dot(act.astype(BF16), w2, preferred_element_type=F32)


def _ff_splits(f):
    cut = -(-(f // MXU_TILE) // 2) * MXU_TILE
    return ((0, cut), (cut, f)) if 0 < cut < f else ((0, f),)


def _swiglu(h, w1_ref, w3_ref, w2_ref):
    y = None
    for lo, hi in _ff_splits(w1_ref.shape[1]):
        part = _swiglu_partial(h, w1_ref[:, lo:hi], w3_ref[:, lo:hi], w2_ref[lo:hi, :])
        y = part if y is None else y + part
    return y


def _even_tail_kernel(x_ref, att_ref, p_ref, halo_ref, wpool_ref, scale_ref,
                      wo_att_ref, wo_pool_ref, g_ref, w1_ref, w3_ref, w2_ref, o_ref,
                      *, tiles_per_seq):
    i = pl.program_id(0)
    tm = x_ref.shape[0]
    tile_in_seq = i % tiles_per_seq
    p = p_ref[...]
    halo = jnp.where(tile_in_seq == 0, 0.0, halo_ref[...])
    ext = jnp.concatenate([halo, p], axis=0)
    pos = tile_in_seq * tm + lax.broadcasted_iota(jnp.int32, (tm, 1), 0)
    count = (pos + 1).astype(F32)
    mixed = []
    for gi, w in enumerate(POOL_WINDOWS):
        sl = slice(gi * POOL_GROUP_DIM, (gi + 1) * POOL_GROUP_DIM)
        acc = ext[:, sl]
        sh = 1
        while sh < w:
            acc = acc + pltpu.roll(acc, sh, axis=0)
            sh *= 2
        mean = acc[POOL_HALO:, :] / jnp.minimum(count, float(w))
        pooled = (mean - p[:, sl]).astype(BF16)
        mixed.append(jnp.dot(pooled, wpool_ref[gi], preferred_element_type=F32))
    pool = (jnp.concatenate(mixed, axis=1) * scale_ref[...]).astype(BF16)
    y = jnp.dot(att_ref[...], wo_att_ref[...], preferred_element_type=F32)
    y = y + jnp.dot(pool, wo_pool_ref[...], preferred_element_type=F32)
    x1 = x_ref[...] + y
    h = _rms(x1, g_ref[...]).astype(BF16)
    o_ref[...] = x1 + _swiglu(h, w1_ref, w3_ref, w2_ref)


def _even_tail(x2d, att, p_in, w_pool, pool_scale, w_out, ffn_gain, w1, w3, w2,
               *, seq, tm):
    t, d = x2d.shape
    resident = lambda w: pl.BlockSpec(w.shape, lambda i: (0,) * w.ndim,
                                      pipeline_mode=pl.Buffered(1))
    wo_att = w_out[:FOX_WIDTH]
    wo_pool = w_out[FOX_WIDTH:]
    halo_blocks = tm // POOL_HALO
    kern = functools.partial(_even_tail_kernel, tiles_per_seq=seq // tm)
    return pl.pallas_call(
        kern,
        grid=(t // tm,),
        in_specs=[
            pl.BlockSpec((tm, d), lambda i: (i, 0)),
            pl.BlockSpec((tm, FOX_WIDTH), lambda i: (i, 0)),
            pl.BlockSpec((tm, POOL_WIDTH), lambda i: (i, 0)),
            pl.BlockSpec((POOL_HALO, POOL_WIDTH),
                         lambda i: (jnp.maximum(i * halo_blocks - 1, 0), 0)),
            resident(w_pool),
            pl.BlockSpec((1, POOL_WIDTH), lambda i: (0, 0)),
            resident(wo_att), resident(wo_pool),
            pl.BlockSpec((1, d), lambda i: (0, 0)),
            resident(w1), resident(w3), resident(w2),
        ],
        out_specs=pl.BlockSpec((tm, d), lambda i: (i, 0)),
        out_shape=jax.ShapeDtypeStruct((t, d), F32),
        compiler_params=_params("parallel"),
        name="even_tail",
    )(x2d, att, p_in, p_in, w_pool, pool_scale, wo_att, wo_pool, ffn_gain, w1, w3, w2)


def _gelu_tanh(x):
    c = math.sqrt(2.0 / math.pi)
    return 0.5 * x * (1.0 + jnp.tanh(c * (x + 0.044715 * (x * x * x))))


def _s5_kernel(x_ref, g_ref, win_ref, perm_ref, permt_ref, bblk_ref, cblk_ref,
               are_ref, aim_ref, d_ref, o_ref, zero_ref, xs_ref, st_ref, u_ref, act_ref):
    nb, tc, d = x_ref.shape
    rows = nb * tc
    n_slabs = d // LANES
    sw = 2 * SSM_SLAB_STATE

    @pl.when(pl.program_id(0) == 0)
    def _():
        st_ref[...] = jnp.zeros_like(st_ref)

    pt = perm_ref.shape[0] // nb
    h_tb = []
    for t0 in range(0, tc, pt):
        x = x_ref[:, t0:t0 + pt, :].reshape(nb * pt, d)
        h = _rms(x, g_ref[...]).astype(BF16)
        h_tb.append(jnp.dot(perm_ref[...], h, preferred_element_type=F32).astype(BF16))
    u_ref[...] = jnp.dot(jnp.concatenate(h_tb, axis=0), win_ref[...],
                         preferred_element_type=F32)
    hs = SSM_SLAB_STATE

    def input_matmul(s):
        xs_ref[:, s * sw:(s + 1) * sw] = jnp.dot(
            u_ref[:, s * LANES:(s + 1) * LANES].astype(BF16), bblk_ref[s],
            preferred_element_type=F32)

    def output_matmul(s):
        sl = slice(s * LANES, (s + 1) * LANES)
        y = jnp.dot(xs_ref[:, s * sw:(s + 1) * sw].astype(BF16), cblk_ref[s],
                    preferred_element_type=F32) + d_ref[:, sl] * u_ref[:, sl]
        act_ref[:, sl] = _gelu_tanh(y).astype(BF16)

    ahead = 2
    for s in range(min(ahead, n_slabs)):
        input_matmul(s)
    for s in range(n_slabs):
        lo = s * sw
        if s + ahead < n_slabs:
            input_matmul(s + ahead)
        a_r = are_ref[:, lo:lo + hs]
        a_i = aim_ref[:, lo:lo + hs]
        x_r = st_ref[:, lo:lo + hs]
        x_i = st_ref[:, lo + hs:lo + sw]
        for t in range(tc):
            r0 = t * SUBLANES
            new_r = a_r * x_r - a_i * x_i + xs_ref[r0:r0 + SUBLANES, lo:lo + hs]
            new_i = a_r * x_i + a_i * x_r + xs_ref[r0:r0 + SUBLANES, lo + hs:lo + sw]
            xs_ref[r0:r0 + SUBLANES, lo:lo + hs] = new_r
            xs_ref[r0:r0 + SUBLANES, lo + hs:lo + sw] = new_i
            x_r, x_i = new_r, new_i
        st_ref[:, lo:lo + hs] = x_r
        st_ref[:, lo + hs:lo + sw] = x_i
        if s >= 1:
            output_matmul(s - 1)
    output_matmul(n_slabs - 1)
    for t0 in range(0, tc, pt):
        g_bt = jnp.dot(permt_ref[...], act_ref[t0 * nb:(t0 + pt) * nb, :],
                       preferred_element_type=F32)
        o_ref[:, t0:t0 + pt, :] = g_bt.reshape(nb, pt, d).astype(o_ref.dtype)
    zero_ref[...] = jnp.zeros_like(zero_ref)


def _s5_mixer(x3d, gain, w_in, bblk, cblk, a_re, a_im, d_skip, *, tc, zero_shape):
    nb, seq, d = x3d.shape
    steps = seq // tc
    zero_rows = zero_shape[0] // steps
    assert zero_rows * steps == zero_shape[0] and zero_rows % SEG_ALIGN == 0
    rows = nb * tc
    n_slabs = d // LANES
    sw = 2 * SSM_SLAB_STATE
    pt = min(tc, S5_PERM_STEPS)
    r = np.arange(nb * pt)
    perm = np.zeros((nb * pt, nb * pt), np.float32)
    perm[(r % pt) * nb + r // pt, r] = 1.0
    perm_j = jnp.asarray(perm, BF16)
    permt_j = jnp.asarray(perm.T, BF16)
    kern = _s5_kernel
    full2 = lambda a: pl.BlockSpec(a.shape, lambda i: (0, 0))
    full3 = lambda a: pl.BlockSpec(a.shape, lambda i: (0, 0, 0))
    return pl.pallas_call(
        kern,
        grid=(seq // tc,),
        in_specs=[
            pl.BlockSpec((nb, tc, d), lambda i: (0, i, 0)),
            full2(gain), full2(w_in), full2(perm_j), full2(permt_j),
            full3(bblk), full3(cblk), full2(a_re), full2(a_im), full2(d_skip),
        ],
        out_specs=[pl.BlockSpec((nb, tc, d), lambda i: (0, i, 0)),
                   pl.BlockSpec((zero_rows, zero_shape[1]), lambda i: (i, 0))],
        out_shape=[jax.ShapeDtypeStruct((nb, seq, d), BF16),
                   jax.ShapeDtypeStruct(zero_shape, BF16)],
        scratch_shapes=[
            pltpu.VMEM((rows, n_slabs * sw), F32),
            pltpu.VMEM((SUBLANES, n_slabs * sw), F32),
            pltpu.VMEM((rows, d), F32),
            pltpu.VMEM((rows, d), BF16),
        ],
        compiler_params=_params("arbitrary"),
        name="s5_mixer",
    )(x3d, gain, w_in, perm_j, permt_j, bblk, cblk, a_re, a_im, d_skip)


def _s5_coefficients(a_re, a_im, log_dt, b_re, b_im, c_re, c_im):
    dt = jnp.exp(log_dt.astype(F32))[:, None]
    ar = a_re.astype(F32)
    ai = a_im.astype(F32)
    mag = jnp.exp(ar * dt)
    abar_re = mag * jnp.cos(ai * dt)
    abar_im = mag * jnp.sin(ai * dt)
    den = ar * ar + ai * ai
    nr = abar_re - 1.0
    ni = abar_im
    coef_re = (nr * ar + ni * ai) / den
    coef_im = (ni * ar - nr * ai) / den
    br = b_re.astype(F32)
    bi = b_im.astype(F32)
    bbar_re = coef_re[..., None] * br - coef_im[..., None] * bi
    bbar_im = coef_re[..., None] * bi + coef_im[..., None] * br
    n_groups = ar.shape[0]
    n_slabs = n_groups // SSM_SLAB_GROUPS
    row_group = np.arange(LANES)[:, None] // SSM_GROUP
    col_group = np.arange(SSM_SLAB_STATE)[None, :] // SSM_STATE
    in_mask = jnp.asarray(row_group == col_group, F32)

    def in_block(bb):
        bb = bb.reshape(n_slabs, SSM_SLAB_GROUPS, SSM_STATE, SSM_GROUP)
        bb = bb.transpose(0, 1, 3, 2).reshape(n_slabs, LANES, SSM_STATE)
        return jnp.tile(bb, (1, 1, SSM_SLAB_GROUPS)) * in_mask

    def out_block(cc):
        cc = cc.reshape(n_slabs, SSM_SLAB_GROUPS, SSM_GROUP, SSM_STATE)
        cc = cc.transpose(0, 1, 3, 2).reshape(n_slabs, SSM_SLAB_STATE, SSM_GROUP)
        return jnp.tile(cc, (1, 1, SSM_SLAB_GROUPS)) * in_mask.T

    bblk = jnp.concatenate([in_block(bbar_re), in_block(bbar_im)], axis=2)
    cblk = jnp.concatenate([out_block(c_re.astype(F32)),
                            -out_block(c_im.astype(F32))], axis=1)

    def lanes(a):
        a = a.reshape(n_slabs, 1, SSM_SLAB_STATE)
        a = jnp.concatenate([a, a], axis=2).reshape(1, -1)
        return jnp.broadcast_to(a, (SUBLANES, a.shape[1]))

    return (bblk.astype(BF16), cblk.astype(BF16), lanes(abar_re), lanes(abar_im))


MOE_CHUNK = 512
MOE_FIRST = 208
MOE_REST = MOE_CHUNK + SEG_ALIGN - MOE_FIRST
MOE_PIECES = ((0, MOE_FIRST), (MOE_FIRST, MOE_REST))
MOE_WINDOW = MOE_FIRST + MOE_REST
NOT_ROUTED = -1024.0
MOE_TILE = 512


def _glu_router_kernel(x_ref, act_ref, wa_ref, wb_ref, g_ref, w_ref, b_ref,
                       x3_ref, h_ref, pos_ref, post_ref, cnt_ref):
    tm, d = x_ref.shape
    parts = 2
    half = tm // parts

    def glu(r0):
        act = act_ref[r0:r0 + half, :]
        a = jnp.dot(act, wa_ref[...], preferred_element_type=F32)
        b = jnp.dot(act, wb_ref[...], preferred_element_type=F32)
        x3 = x_ref[r0:r0 + half, :] + a * _sigmoid(b)
        x3_ref[r0:r0 + half, :] = x3
        return x3

    def route(r0, x3):
        h = _rms(x3, g_ref[...])
        h_hi = h.astype(BF16)
        h_ref[r0:r0 + half, 0:d] = h_hi
        h_lo = (h - h_hi.astype(F32)).astype(BF16)
        p_hi = jnp.dot(h_hi, w_ref[...], preferred_element_type=F32)
        p_lo = jnp.dot(h_lo, w_ref[...], preferred_element_type=F32)
        logits = p_hi + pltpu.roll(p_hi, LANES - N_EXPERTS, axis=1) + p_lo + b_ref[...]
        lane = lax.broadcasted_iota(jnp.int32, logits.shape, 1)
        logits = jnp.where(lane < N_EXPERTS, logits, -jnp.inf)
        m1 = jnp.max(logits, axis=1, keepdims=True)
        i1 = jnp.min(jnp.where(logits == m1, lane, LANES), axis=1, keepdims=True)
        rest = jnp.where(lane == i1, -jnp.inf, logits)
        m2 = jnp.max(rest, axis=1, keepdims=True)
        i2 = jnp.min(jnp.where(rest == m2, lane, LANES), axis=1, keepdims=True)
        e2 = jnp.exp(m2 - m1)
        g1 = 1.0 / (1.0 + e2)
        g2 = e2 / (1.0 + e2)
        gate = jnp.where(lane == i1, g1, 0.0) + jnp.where(lane == i2, g2, 0.0)
        g_hi = gate.astype(BF16).astype(F32)
        g_r = gate - g_hi
        g_mid = g_r.astype(BF16).astype(F32)
        packed = jnp.where(lane < N_EXPERTS, g_hi,
                           jnp.where(lane < 2 * N_EXPERTS,
                                     pltpu.roll(g_mid, N_EXPERTS, axis=1),
                                     pltpu.roll(g_r - g_mid, 2 * N_EXPERTS, axis=1)))
        h_ref[r0:r0 + half, d:d + LANES] = packed.astype(BF16)
        return jnp.where((lane == i1) | (lane == i2), 1.0, 0.0)

    members = []
    x3 = glu(0)
    for p in range(parts):
        nxt = glu((p + 1) * half) if p + 1 < parts else None
        members.append(route(p * half, x3))
        x3 = nxt
    member = jnp.concatenate(members, axis=0)
    row = lax.broadcasted_iota(jnp.int32, member.shape, 0)
    c = member
    sh = 1
    while sh < tm:
        c = c + jnp.where(row >= sh, pltpu.roll(c, sh, axis=0), 0.0)
        sh *= 2
    pos = jnp.where(member > 0.0, c - member, NOT_ROUTED)
    pos_ref[...] = pos
    post_ref[0] = pos.T[0:SUBLANES, :]
    cnt_ref[0] = jnp.broadcast_to(c[tm - 1:tm, :], (SUBLANES, LANES))


def _glu_router(x2d, act2d, wa, wb, gain, w_pad, b_pad):
    t, d = x2d.shape
    tm = MOE_CHUNK
    n_chunks = t // tm
    return pl.pallas_call(
        _glu_router_kernel,
        grid=(n_chunks,),
        in_specs=[
            pl.BlockSpec((tm, d), lambda i: (i, 0)),
            pl.BlockSpec((tm, d), lambda i: (i, 0)),
            pl.BlockSpec(wa.shape, lambda i: (0, 0)),
            pl.BlockSpec(wb.shape, lambda i: (0, 0)),
            pl.BlockSpec((1, d), lambda i: (0, 0)),
            pl.BlockSpec((d, LANES), lambda i: (0, 0)),
            pl.BlockSpec((1, LANES), lambda i: (0, 0)),
        ],
        out_specs=[
            pl.BlockSpec((tm, d), lambda i: (i, 0)),
            pl.BlockSpec((tm, d + LANES), lambda i: (i, 0)),
            pl.BlockSpec((tm, LANES), lambda i: (i, 0)),
            pl.BlockSpec((1, SUBLANES, tm), lambda i: (i, 0, 0)),
            pl.BlockSpec((1, SUBLANES, LANES), lambda i: (i, 0, 0)),
        ],
        out_shape=[
            jax.ShapeDtypeStruct((t, d), F32),
            jax.ShapeDtypeStruct((t, d + LANES), BF16),
            jax.ShapeDtypeStruct((t, LANES), F32),
            jax.ShapeDtypeStruct((n_chunks, SUBLANES, tm), F32),
            jax.ShapeDtypeStruct((n_chunks, SUBLANES, LANES), F32),
        ],
        compiler_params=_params("parallel"),
        name="glu_router",
    )(x2d, act2d, wa, wb, gain, w_pad, b_pad)


MOE_SPARE = max(MOE_FIRST, MOE_REST)


def _moe_tiles(n_tokens, row_multiple):
    max_rows = (TOP_K * n_tokens + N_EXPERTS * (SEG_ALIGN - 1)
                + N_EXPERTS * (MOE_SPARE + MOE_TILE))
    unit = math.lcm(MOE_TILE, row_multiple)
    return -(-max_rows // unit) * (unit // MOE_TILE)


def _moe_layout(cnt):
    before = jnp.cumsum(cnt, axis=0) - cnt
    shift = before % SEG_ALIGN
    used = (jnp.sum(cnt, axis=0) + SEG_ALIGN - 1) // SEG_ALIGN * SEG_ALIGN
    padded = (used + MOE_SPARE + MOE_TILE - 1) // MOE_TILE * MOE_TILE
    ends = jnp.cumsum(padded)
    start = ends - padded
    off = start[None, :] + before - shift
    flat = lambda a: a.reshape(-1).astype(jnp.int32)
    return (flat(off), flat(shift + cnt), flat(shift),
            ends.astype(jnp.int32), (start + used).astype(jnp.int32))


def _tile_expert(i, ends_ref):
    first_row = i * MOE_TILE
    e = 0
    for k in range(N_EXPERTS - 1):
        e = e + (first_row >= ends_ref[k]).astype(jnp.int32)
    return e


def _segment_copies(hbm_ref, buf_ref, sem_ref, off_ref, chunk, to_hbm, slot=()):
    copies = []
    for e in range(N_EXPERTS):
        off = pl.multiple_of(off_ref[chunk * N_EXPERTS + e], SEG_ALIGN)
        for piece, (first, rows) in enumerate(MOE_PIECES):
            hbm = hbm_ref.at[pl.ds(off + first, rows)]
            buf = buf_ref.at[(*slot, e, pl.ds(first, rows))]
            src, dst = (buf, hbm) if to_hbm else (hbm, buf)
            copies.append(pltpu.make_async_copy(src, dst, sem_ref.at[(*slot, e, piece)]))
    return copies


def _onehot(first, shape, axis, target):
    rank = first + lax.broadcasted_iota(jnp.int32, shape, axis)
    return jnp.where(rank.astype(F32) == target, 1.0, 0.0).astype(BF16)


def _gather_kernel(off_ref, need_ref, shift_ref, h_ref, post_ref, zeros_hbm, o_hbm,
                   stage_ref, carry_ref, sem_ref):
    del zeros_hbm
    c = pl.program_id(0)
    slot = c % 2
    h = h_ref[...]

    @pl.when(c == 0)
    def _():
        carry_ref[...] = jnp.zeros_like(carry_ref)

    def long_segment(chunk, e):
        return need_ref[chunk * N_EXPERTS + e] >= MOE_FIRST

    def piece_onehot(e, piece):
        first, rows = MOE_PIECES[piece]
        return _onehot(first - shift_ref[c * N_EXPERTS + e], (rows, MOE_CHUNK), 0,
                       post_ref[0, e:e + 1, :])

    def fill_piece(e, piece, onehot=None):
        first, rows = MOE_PIECES[piece]
        onehot = piece_onehot(e, piece) if onehot is None else onehot
        stage_ref[slot, e, first:first + rows] = jnp.dot(
            onehot, h, preferred_element_type=F32).astype(BF16)
        if piece == 0:
            stage_ref[slot, e, 0:SEG_ALIGN] += carry_ref[e]

    onehot = piece_onehot(0, 0)
    for e in range(N_EXPERTS):
        nxt = piece_onehot(e + 1, 0) if e + 1 < N_EXPERTS else None
        fill_piece(e, 0, onehot)
        onehot = nxt
    for e in range(N_EXPERTS):
        @pl.when(long_segment(c, e))
        def _(e=e):
            fill_piece(e, 1)

    for e in range(N_EXPERTS):
        last_group = need_ref[c * N_EXPERTS + e] // SEG_ALIGN * SEG_ALIGN
        carry_ref[e] = stage_ref[slot, e, pl.ds(pl.multiple_of(last_group, SEG_ALIGN),
                                                SEG_ALIGN)]

    def each_copy(chunk, sl, action):
        copies = _segment_copies(o_hbm, stage_ref, sem_ref, off_ref, chunk,
                                 to_hbm=True, slot=(sl,))
        for e in range(N_EXPERTS):
            action(copies[2 * e])

            @pl.when(long_segment(chunk, e))
            def _(e=e):
                action(copies[2 * e + 1])

    @pl.when(c > 0)
    def _():
        each_copy(c - 1, 1 - slot, lambda cp: cp.wait())

    each_copy(c, slot, lambda cp: cp.start())

    @pl.when(c == pl.num_programs(0) - 1)
    def _():
        each_copy(c, slot, lambda cp: cp.wait())


def _gather(h2d, post, off, need, shift, zero_buf):
    t, d = h2d.shape
    n_chunks = t // MOE_CHUNK
    grid_spec = pltpu.PrefetchScalarGridSpec(
        num_scalar_prefetch=3,
        grid=(n_chunks,),
        in_specs=[
            pl.BlockSpec((MOE_CHUNK, d), lambda i, *_: (i, 0)),
            pl.BlockSpec((1, SUBLANES, MOE_CHUNK), lambda i, *_: (i, 0, 0)),
            pl.BlockSpec(memory_space=pl.ANY),
        ],
        out_specs=pl.BlockSpec(memory_space=pl.ANY),
        scratch_shapes=[
            pltpu.VMEM((2, N_EXPERTS, MOE_WINDOW, d), BF16),
            pltpu.VMEM((N_EXPERTS, SEG_ALIGN, d), BF16),
            pltpu.SemaphoreType.DMA((2, N_EXPERTS, 2)),
        ],
    )
    return pl.pallas_call(
        _gather_kernel,
        grid_spec=grid_spec,
        out_shape=jax.ShapeDtypeStruct(zero_buf.shape, BF16),
        input_output_aliases={5: 0},
        compiler_params=_params("arbitrary"),
        name="moe_gather",
    )(off, need, shift, h2d, post, zero_buf)


def _expert_kernel(ends_ref, used_end_ref, x_ref, w1_ref, w3_ref, w2_ref, o_ref):
    i = pl.program_id(0)
    expert = _tile_expert(i, ends_ref)
    rows = jnp.clip(used_end_ref[expert] - i * MOE_TILE, 0, MOE_TILE)
    d = o_ref.shape[1]

    def run(n):
        row = lax.broadcasted_iota(jnp.int32, (n, x_ref.shape[1]), 0)
        x = jnp.where(row < rows, x_ref[0:n, :], jnp.zeros((), x_ref.dtype))
        lane = lax.broadcasted_iota(jnp.int32, (n, LANES), 1)
        mine = (lane < 3 * N_EXPERTS) & (jnp.bitwise_and(lane, N_EXPERTS - 1) == expert)
        gate = jnp.sum(jnp.where(mine, x[:, d:d + LANES].astype(F32), 0.0),
                       axis=1, keepdims=True)
        y = _swiglu(x[:, 0:d], w1_ref, w3_ref, w2_ref)
        o_ref[0:n, :] = (gate * y).astype(o_ref.dtype)

    half = MOE_TILE // 2

    @pl.when(rows > half)
    def _():
        run(MOE_TILE)

    @pl.when((rows > 0) & (rows <= half))
    def _():
        run(half)
        o_ref[half:MOE_TILE, :] = jnp.zeros((MOE_TILE - half, d), o_ref.dtype)

    @pl.when(rows == 0)
    def _():
        o_ref[...] = jnp.zeros_like(o_ref)


def _experts(x_sorted, w1, w3, w2, ends, used_end):
    assert N_EXPERTS & (N_EXPERTS - 1) == 0
    n_rows = x_sorted.shape[0]
    f = w1.shape[1]
    d = w2.shape[1]
    grid_spec = pltpu.PrefetchScalarGridSpec(
        num_scalar_prefetch=2,
        grid=(n_rows // MOE_TILE,),
        in_specs=[
            pl.BlockSpec((MOE_TILE, d + LANES), lambda i, ends, ue: (i, 0)),
            pl.BlockSpec((d, f), lambda i, ends, ue: (_tile_expert(i, ends), 0)),
            pl.BlockSpec((d, f), lambda i, ends, ue: (_tile_expert(i, ends), 0)),
            pl.BlockSpec((f, d), lambda i, ends, ue: (_tile_expert(i, ends), 0)),
        ],
        out_specs=pl.BlockSpec((MOE_TILE, d), lambda i, ends, ue: (i, 0)),
    )
    return pl.pallas_call(
        _expert_kernel,
        grid_spec=grid_spec,
        out_shape=jax.ShapeDtypeStruct((n_rows, d), BF16),
        compiler_params=_params("arbitrary"),
        name="moe_experts",
    )(ends, used_end, x_sorted, w1, w3, w2)


def _combine_kernel(off_ref, need_ref, shift_ref, x_ref, pos_ref, fg_ref, y_hbm,
                    o_ref, ybuf_ref, acc_ref, sem_ref):
    c = pl.program_id(0)
    n_chunks = pl.num_programs(0)
    slot = c % 2

    def fetch(chunk, sl, start):
        copies = _segment_copies(y_hbm, ybuf_ref, sem_ref, off_ref, chunk,
                                 to_hbm=False, slot=(sl,))
        for e in range(N_EXPERTS):
            first, second = copies[2 * e], copies[2 * e + 1]
            if start:
                first.start()
            else:
                first.wait()

            @pl.when(need_ref[chunk * N_EXPERTS + e] >= MOE_FIRST)
            def _(second=second):
                if start:
                    second.start()
                else:
                    second.wait()

    @pl.when(c == 0)
    def _():
        fetch(c, slot, True)

    @pl.when(c + 1 < n_chunks)
    def _():
        fetch(c + 1, 1 - slot, True)

    fetch(c, slot, False)

    def piece_onehot(e, piece):
        first, rows = MOE_PIECES[piece]
        return _onehot(first - shift_ref[c * N_EXPERTS + e], (MOE_CHUNK, rows), 1,
                       pos_ref[:, e:e + 1])

    def piece_rows(e, piece, onehot=None):
        first, rows = MOE_PIECES[piece]
        onehot = piece_onehot(e, piece) if onehot is None else onehot
        return jnp.dot(onehot, ybuf_ref[slot, e, first:first + rows],
                       preferred_element_type=F32)

    acc = x_ref[...]
    onehot = piece_onehot(0, 0)
    for e in range(N_EXPERTS):
        nxt = piece_onehot(e + 1, 0) if e + 1 < N_EXPERTS else None
        acc = acc + piece_rows(e, 0, onehot)
        onehot = nxt
    acc_ref[...] = acc
    for e in range(N_EXPERTS):
        @pl.when(need_ref[c * N_EXPERTS + e] >= MOE_FIRST)
        def _(e=e):
            acc_ref[...] += piece_rows(e, 1)

    o_ref[...] = _rms(acc_ref[...], fg_ref[...])


def _combine(x2d, pos, final_gain, y_sorted, off, need, shift):
    t, d = x2d.shape
    n_chunks = t // MOE_CHUNK
    grid_spec = pltpu.PrefetchScalarGridSpec(
        num_scalar_prefetch=3,
        grid=(n_chunks,),
        in_specs=[
            pl.BlockSpec((MOE_CHUNK, d), lambda i, *_: (i, 0)),
            pl.BlockSpec((MOE_CHUNK, LANES), lambda i, *_: (i, 0)),
            pl.BlockSpec((1, d), lambda i, *_: (0, 0)),
            pl.BlockSpec(memory_space=pl.ANY),
        ],
        out_specs=pl.BlockSpec((MOE_CHUNK, d), lambda i, *_: (i, 0)),
        scratch_shapes=[
            pltpu.VMEM((2, N_EXPERTS, MOE_WINDOW, d), BF16),
            pltpu.VMEM((MOE_CHUNK, d), F32),
            pltpu.SemaphoreType.DMA((2, N_EXPERTS, 2)),
        ],
    )
    return pl.pallas_call(
        _combine_kernel,
        grid_spec=grid_spec,
        out_shape=jax.ShapeDtypeStruct((t, d), F32),
        compiler_params=_params("arbitrary"),
        name="moe_combine",
    )(off, need, shift, x2d, pos, final_gain, y_sorted)


def _glu_moe(x2d, act2d, wa, wb, gain, router_w, router_b, w1, w3, w2, final_gain,
             zero_buf):
    t, d = x2d.shape
    rw_hi = router_w.astype(BF16)
    rw_lo = (router_w.astype(F32) - rw_hi.astype(F32)).astype(BF16)
    rw = jnp.pad(jnp.concatenate([rw_hi, rw_lo], axis=1),
                 ((0, 0), (0, LANES - 2 * N_EXPERTS)))
    rb = jnp.pad(router_b.astype(F32), (0, LANES - N_EXPERTS)).reshape(1, LANES)
    x3, h, pos, post, cnt = _glu_router(x2d, act2d, wa, wb, gain, rw, rb)
    cnt = cnt[:, 0, :N_EXPERTS].astype(jnp.int32)
    off, need, shift, ends, used_end = _moe_layout(cnt)
    h_sorted = _gather(h, post, off, need, shift, zero_buf)
    y_sorted = _experts(h_sorted, w1, w3, w2, ends, used_end)
    return _combine(x3, pos, final_gain, y_sorted, off, need, shift)


def _even_weights(w_in, b_forget):
    n = w_in.shape[1]
    w_all = jnp.pad(w_in.astype(BF16), ((0, 0), (0, -n % LANES)))
    bias = jnp.pad(b_forget.astype(F32), (0, LANES - FOX_HEADS)).reshape(1, LANES)
    qw = FOX_HEADS * LANES
    place = np.zeros((LANES, 2 * qw), np.float32)
    for hh in range(FOX_HEADS):
        for piece in range(3):
            place[piece * FOX_HEADS + hh, hh * LANES + BIAS_LANE + piece] = 1.0
            place[piece * FOX_HEADS + hh, qw + hh * LANES + ONES_LANE + piece] = -1.0
    return w_all, bias, jnp.asarray(place, BF16)


def kernel(x, even_mix_norm, even_w_in, even_b_forget, even_w_pool, even_pool_scale, even_w_out, even_ffn_norm, even_ffn_w1, even_ffn_w3, even_ffn_w2, odd_mix_norm, odd_w_in, ssm_a_re, ssm_a_im, ssm_log_dt, ssm_b_re, ssm_b_im, ssm_c_re, ssm_c_im, ssm_d, odd_w_glu_a, odd_w_glu_b, odd_moe_norm, router_w, router_b, expert_w1, expert_w3, expert_w2, final_norm):
    b, s, d = x.shape
    t = b * s
    assert b == SUBLANES, "the S5 recurrence keeps one batch row per sublane"
    x2d = x.reshape(t, d)
    row = lambda v: v.reshape(1, -1).astype(F32)

    w_all, bias, place = _even_weights(even_w_in[0], even_b_forget[0])
    (q_aug, k_aug, v, p_in), casted = _even_inproj(
        x2d, row(even_mix_norm[0]), w_all, bias, place,
        [even_ffn_w1[0], even_ffn_w3[0], even_ffn_w2[0], even_w_out[0],
         even_w_pool[0].reshape(-1, POOL_GROUP_DIM),
         odd_w_in[0], odd_w_glu_a[0], odd_w_glu_b[0]],
        seq=s, tm=INPROJ_ROWS)
    ffn_w1, ffn_w3, ffn_w2, w_out, w_pool, s5_w_in, glu_a, glu_b = casted
    n_e, _, f = expert_w1[0].shape
    att, (ew1, ew3, ew2) = _attention(
        q_aug, k_aug, v,
        [expert_w1[0].reshape(n_e * d, f), expert_w3[0].reshape(n_e * d, f),
         expert_w2[0].reshape(n_e * f, d)],
        batch=b, seq=s, tq=ATTN_QUERY_TILE)
    x2 = _even_tail(x2d, att, p_in, w_pool.reshape(even_w_pool[0].shape),
                    row(even_pool_scale[0]), w_out, row(even_ffn_norm[0]),
                    ffn_w1, ffn_w3, ffn_w2, seq=s, tm=EVEN_TAIL_ROWS)

    bblk, cblk, a_re, a_im = _s5_coefficients(
        ssm_a_re[0], ssm_a_im[0], ssm_log_dt[0], ssm_b_re[0], ssm_b_im[0],
        ssm_c_re[0], ssm_c_im[0])
    moe_rows = _moe_tiles(t, (s // S5_STEP) * SEG_ALIGN) * MOE_TILE
    g, zero_buf = _s5_mixer(x2.reshape(b, s, d), row(odd_mix_norm[0]),
                            s5_w_in, bblk, cblk, a_re, a_im,
                            row(ssm_d[0]), tc=S5_STEP,
                            zero_shape=(moe_rows, d + LANES))
    out = _glu_moe(x2, g.reshape(t, d), glu_a, glu_b, row(odd_moe_norm[0]),
                   router_w[0], router_b[0], ew1, ew3, ew2,
                   row(final_norm), zero_buf)
    return out.reshape(b, s, d)
```

```python
import functools
import math

import numpy as np
import jax
import jax.numpy as jnp
from jax import lax
from jax.experimental import pallas as pl
from jax.experimental.pallas import tpu as pltpu

F32 = jnp.float32
BF16 = jnp.bfloat16

EPS = 1e-6
NEG_INF = -1e30
LANES = 128
SUBLANES = 8
SEG_ALIGN = 16
VMEM_LIMIT = 56 * 1024 * 1024

FOX_HEADS = 8
FOX_HEAD_DIM = 64
FOX_WIDTH = FOX_HEADS * FOX_HEAD_DIM
POOL_WINDOWS = (2, 4, 8, 16)
POOL_GROUP_DIM = 128
POOL_WIDTH = len(POOL_WINDOWS) * POOL_GROUP_DIM
POOL_HALO = 16
SSM_GROUP = 16
SSM_STATE = 64
SSM_SLAB_GROUPS = LANES // SSM_GROUP
SSM_SLAB_STATE = SSM_SLAB_GROUPS * SSM_STATE

INPROJ_ROWS = 512
ATTN_QUERY_TILE = 256
EVEN_TAIL_ROWS = 512
S5_STEP = 64
S5_PERM_STEPS = 32
N_EXPERTS = 8
TOP_K = 2

BIAS_LANE = FOX_HEAD_DIM
ONES_LANE = FOX_HEAD_DIM + 3
DENOM_LANE = FOX_HEAD_DIM


def _params(*sem):
    return pltpu.CompilerParams(dimension_semantics=sem,
                                vmem_limit_bytes=VMEM_LIMIT)


def _rms(x, g):
    ms = jnp.mean(x * x, axis=-1, keepdims=True)
    return x * lax.rsqrt(ms + EPS) * g


def _sigmoid(x):
    return 1.0 / (1.0 + jnp.exp(-x))


def _lane_range_ones(lo, hi):
    lane = lax.broadcasted_iota(jnp.int32, (1, LANES), 1)
    return jnp.where((lane >= lo) & (lane < hi), 1.0, 0.0).astype(F32)


def _cast_specs(weights, steps, step_index):
    specs = []
    for w in weights:
        rows = -(-w.shape[0] // (steps * SEG_ALIGN)) * SEG_ALIGN
        last = -(-w.shape[0] // rows) - 1
        specs.append(pl.BlockSpec(
            (rows, w.shape[1]),
            lambda *idx, last=last: (jnp.minimum(step_index(*idx), last), 0)))
    return specs


def _cast_slabs(src_refs, dst_refs):
    for src_ref, dst_ref in zip(src_refs, dst_refs):
        dst_ref[...] = src_ref[...].astype(dst_ref.dtype)


def _even_inproj_kernel(x_ref, g_ref, w_ref, bias_ref, place_ref, *rest,
                        tiles_per_seq, n_cast):
    cast_in, rest = rest[:n_cast], rest[n_cast:]
    q_ref, k_ref, v_ref, p_ref = rest[:4]
    cast_out, carry_ref = rest[4:4 + n_cast], rest[4 + n_cast]
    _cast_slabs(cast_in, cast_out)
    i = pl.program_id(0)
    tm = x_ref.shape[0]
    parts = 2
    rows = tm // parts
    fw = FOX_WIDTH
    qw = FOX_HEADS * LANES

    @pl.when(i % tiles_per_seq == 0)
    def _():
        carry_ref[...] = jnp.zeros_like(carry_ref)

    def project(r0):
        h = _rms(x_ref[r0:r0 + rows, :], g_ref[...]).astype(BF16)
        return jnp.dot(h, w_ref[...], preferred_element_type=F32)

    def finish(r0, z, carry):
        out = slice(r0, r0 + rows)
        p_ref[out, :] = z[:, 3 * fw + FOX_HEADS:3 * fw + FOX_HEADS + POOL_WIDTH]
        fg = z[:, 3 * fw:3 * fw + LANES] + bias_ref[...]
        lf = jnp.minimum(fg, 0.0) - jnp.log1p(jnp.exp(-jnp.abs(fg)))
        row = lax.broadcasted_iota(jnp.int32, lf.shape, 0)
        c = lf
        sh = 1
        while sh < rows:
            c = c + jnp.where(row >= sh, pltpu.roll(c, sh, axis=0), 0.0)
            sh *= 2
        c = c + carry
        hi = c.astype(BF16).astype(F32)
        r1 = c - hi
        mid = r1.astype(BF16).astype(F32)
        lo = r1 - mid
        lane = lax.broadcasted_iota(jnp.int32, (rows, LANES), 1)
        packed = jnp.where(lane < FOX_HEADS, hi,
                           jnp.where(lane < 2 * FOX_HEADS,
                                     pltpu.roll(mid, FOX_HEADS, axis=1),
                                     pltpu.roll(lo, 2 * FOX_HEADS, axis=1)))
        placed = jnp.dot(packed.astype(BF16), place_ref[...],
                         preferred_element_type=F32)
        low = lane < FOX_HEAD_DIM
        ones_q = _lane_range_ones(ONES_LANE, ONES_LANE + 3)
        ones_k = _lane_range_ones(BIAS_LANE, BIAS_LANE + 3)
        ones_v = _lane_range_ones(DENOM_LANE, DENOM_LANE + 1)

        def head_lanes(base, hh):
            pair = z[:, base + (hh // 2) * LANES:base + (hh // 2 + 1) * LANES]
            return pltpu.roll(pair, FOX_HEAD_DIM, axis=1) if hh % 2 else pair

        q_scale = FOX_HEAD_DIM ** -0.5
        for hh in range(FOX_HEADS):
            sl = slice(hh * LANES, (hh + 1) * LANES)
            q_ref[out, sl] = jnp.where(low, head_lanes(0, hh) * q_scale,
                                       placed[:, sl] + ones_q).astype(BF16)
            k_ref[out, sl] = jnp.where(
                low, head_lanes(fw, hh),
                placed[:, qw + hh * LANES:qw + (hh + 1) * LANES] + ones_k).astype(BF16)
            v_ref[out, sl] = jnp.where(low, head_lanes(2 * fw, hh), ones_v).astype(BF16)
        return c[rows - 1:rows, :]

    carry = carry_ref[0:1, :]
    z = project(0)
    for part in range(parts):
        nxt = project((part + 1) * rows) if part + 1 < parts else None
        carry = finish(part * rows, z, carry)
        z = nxt
    carry_ref[...] = jnp.broadcast_to(carry, carry_ref.shape)


def _even_inproj(x2d, gain, w_all, bias, place, cast_weights, *, seq, tm):
    t, d = x2d.shape
    n = w_all.shape[1]
    qw = FOX_HEADS * LANES
    cast_specs = _cast_specs(cast_weights, t // tm, lambda i: i)
    kern = functools.partial(_even_inproj_kernel, tiles_per_seq=seq // tm,
                             n_cast=len(cast_weights))
    outs = pl.pallas_call(
        kern,
        grid=(t // tm,),
        in_specs=[
            pl.BlockSpec((tm, d), lambda i: (i, 0)),
            pl.BlockSpec((1, d), lambda i: (0, 0)),
            pl.BlockSpec((d, n), lambda i: (0, 0)),
            pl.BlockSpec((1, LANES), lambda i: (0, 0)),
            pl.BlockSpec(place.shape, lambda i: (0, 0)),
            *cast_specs,
        ],
        out_specs=[
            pl.BlockSpec((tm, qw), lambda i: (i, 0)),
            pl.BlockSpec((tm, qw), lambda i: (i, 0)),
            pl.BlockSpec((tm, qw), lambda i: (i, 0)),
            pl.BlockSpec((tm, POOL_WIDTH), lambda i: (i, 0)),
            *cast_specs,
        ],
        out_shape=[
            jax.ShapeDtypeStruct((t, qw), BF16),
            jax.ShapeDtypeStruct((t, qw), BF16),
            jax.ShapeDtypeStruct((t, qw), BF16),
            jax.ShapeDtypeStruct((t, POOL_WIDTH), F32),
        ] + [jax.ShapeDtypeStruct(w.shape, BF16) for w in cast_weights],
        scratch_shapes=[pltpu.VMEM((SUBLANES, LANES), F32)],
        compiler_params=_params("arbitrary"),
        name="even_inproj",
    )(x2d, gain, w_all, bias, place, *cast_weights)
    return outs[:4], outs[4:]


def _dot_nt(a, b):
    return lax.dot_general(a, b, (((1,), (1,)), ((), ())),
                           preferred_element_type=F32)


def _attn_kernel(q_ref, k_ref, v_ref, *rest, tq, n_cast):
    cast_in, o_ref, cast_out = rest[:n_cast], rest[n_cast], rest[n_cast + 1:]
    seq = q_ref.shape[0]
    row = lax.broadcasted_iota(jnp.int32, (tq, tq), 0)
    col = lax.broadcasted_iota(jnp.int32, (tq, tq), 1)
    causal = col <= row
    lane = lax.broadcasted_iota(jnp.int32, (tq, LANES), 1)
    work = [(qi, hh) for qi in range(seq // tq) for hh in range(2)]

    def scores(qi, hh):
        r0 = qi * tq
        sl = slice(hh * LANES, (hh + 1) * LANES)
        q = q_ref[r0:r0 + tq, sl]
        s_diag = jnp.where(causal, _dot_nt(q, k_ref[r0:r0 + tq, sl]), NEG_INF)
        s_past = _dot_nt(q, k_ref[0:r0, sl]) if qi > 0 else None
        return s_diag, s_past

    ahead = 2
    queue = [scores(*item) for item in work[:ahead]]
    first_head = None
    for n, (qi, hh) in enumerate(work):
        r0 = qi * tq
        sl = slice(hh * LANES, (hh + 1) * LANES)
        s_diag, s_past = queue.pop(0)
        if n + ahead < len(work):
            queue.append(scores(*work[n + ahead]))
        m = jnp.max(s_diag, axis=1, keepdims=True)
        if qi > 0:
            m = jnp.maximum(m, jnp.max(s_past, axis=1, keepdims=True))
        acc = jnp.dot(jnp.exp(s_diag - m).astype(BF16), v_ref[r0:r0 + tq, sl],
                      preferred_element_type=F32)
        if qi > 0:
            acc = acc + jnp.dot(jnp.exp(s_past - m).astype(BF16), v_ref[0:r0, sl],
                                preferred_element_type=F32)
        out = acc / acc[:, DENOM_LANE:DENOM_LANE + 1]
        if hh == 0:
            first_head = out
        else:
            o_ref[r0:r0 + tq, :] = jnp.where(
                lane < FOX_HEAD_DIM, first_head,
                pltpu.roll(out, FOX_HEAD_DIM, axis=1)).astype(o_ref.dtype)
    _cast_slabs(cast_in, cast_out)


def _attention(q_aug, k_aug, v_aug, cast_weights, *, batch, seq, tq):
    t = q_aug.shape[0]
    pairs = FOX_HEADS // 2
    steps = batch * pairs
    spec = pl.BlockSpec((seq, 2 * LANES), lambda b, hp: (b, hp))
    out_spec = pl.BlockSpec((seq, LANES), lambda b, hp: (b, hp))
    cast_specs = _cast_specs(cast_weights, steps, lambda b, hp: b * pairs + hp)
    outs = pl.pallas_call(
        functools.partial(_attn_kernel, tq=tq, n_cast=len(cast_weights)),
        grid=(batch, pairs),
        in_specs=[spec, spec, spec, *cast_specs],
        out_specs=[out_spec, *cast_specs],
        out_shape=[jax.ShapeDtypeStruct((t, FOX_WIDTH), BF16)]
        + [jax.ShapeDtypeStruct(w.shape, BF16) for w in cast_weights],
        compiler_params=_params("parallel", "parallel"),
        name="fox_attention",
    )(q_aug, k_aug, v_aug, *cast_weights)
    return outs[0], outs[1:]


MXU_TILE = 256


def _swiglu_partial(h, w1, w3, w2):
    a = jnp.dot(h, w1, preferred_element_type=F32)
    b = jnp.dot(h, w3, preferred_element_type=F32)
    act = a * _sigmoid(a) * b
    return jnp.dot(act.astype(BF16), w2, preferred_element_type=F32)


def _ff_splits(f):
    cut = -(-(f // MXU_TILE) // 2) * MXU_TILE
    return ((0, cut), (cut, f)) if 0 < cut < f else ((0, f),)


def _swiglu(h, w1_ref, w3_ref, w2_ref):
    y = None
    for lo, hi in _ff_splits(w1_ref.shape[1]):
        part = _swiglu_partial(h, w1_ref[:, lo:hi], w3_ref[:, lo:hi], w2_ref[lo:hi, :])
        y = part if y is None else y + part
    return y


def _even_tail_kernel(x_ref, att_ref, p_ref, halo_ref, wpool_ref, scale_ref,
                      wo_att_ref, wo_pool_ref, g_ref, w1_ref, w3_ref, w2_ref, o_ref,
                      *, tiles_per_seq):
    i = pl.program_id(0)
    tm = x_ref.shape[0]
    tile_in_seq = i % tiles_per_seq
    p = p_ref[...]
    halo = jnp.where(tile_in_seq == 0, 0.0, halo_ref[...])
    ext = jnp.concatenate([halo, p], axis=0)
    pos = tile_in_seq * tm + lax.broadcasted_iota(jnp.int32, (tm, 1), 0)
    count = (pos + 1).astype(F32)
    mixed = []
    for gi, w in enumerate(POOL_WINDOWS):
        sl = slice(gi * POOL_GROUP_DIM, (gi + 1) * POOL_GROUP_DIM)
        acc = ext[:, sl]
        sh = 1
        while sh < w:
            acc = acc + pltpu.roll(acc, sh, axis=0)
            sh *= 2
        mean = acc[POOL_HALO:, :] / jnp.minimum(count, float(w))
        pooled = (mean - p[:, sl]).astype(BF16)
        mixed.append(jnp.dot(pooled, wpool_ref[gi], preferred_element_type=F32))
    pool = (jnp.concatenate(mixed, axis=1) * scale_ref[...]).astype(BF16)
    y = jnp.dot(att_ref[...], wo_att_ref[...], preferred_element_type=F32)
    y = y + jnp.dot(pool, wo_pool_ref[...], preferred_element_type=F32)
    x1 = x_ref[...] + y
    h = _rms(x1, g_ref[...]).astype(BF16)
    o_ref[...] = x1 + _swiglu(h, w1_ref, w3_ref, w2_ref)


def _even_tail(x2d, att, p_in, w_pool, pool_scale, w_out, ffn_gain, w1, w3, w2,
               *, seq, tm):
    t, d = x2d.shape
    resident = lambda w: pl.BlockSpec(w.shape, lambda i: (0,) * w.ndim,
                                      pipeline_mode=pl.Buffered(1))
    wo_att = w_out[:FOX_WIDTH]
    wo_pool = w_out[FOX_WIDTH:]
    halo_blocks = tm // POOL_HALO
    kern = functools.partial(_even_tail_kernel, tiles_per_seq=seq // tm)
    return pl.pallas_call(
        kern,
        grid=(t // tm,),
        in_specs=[
            pl.BlockSpec((tm, d), lambda i: (i, 0)),
            pl.BlockSpec((tm, FOX_WIDTH), lambda i: (i, 0)),
            pl.BlockSpec((tm, POOL_WIDTH), lambda i: (i, 0)),
            pl.BlockSpec((POOL_HALO, POOL_WIDTH),
                         lambda i: (jnp.maximum(i * halo_blocks - 1, 0), 0)),
            resident(w_pool),
            pl.BlockSpec((1, POOL_WIDTH), lambda i: (0, 0)),
            resident(wo_att), resident(wo_pool),
            pl.BlockSpec((1, d), lambda i: (0, 0)),
            resident(w1), resident(w3), resident(w2),
        ],
        out_specs=pl.BlockSpec((tm, d), lambda i: (i, 0)),
        out_shape=jax.ShapeDtypeStruct((t, d), F32),
        compiler_params=_params("parallel"),
        name="even_tail",
    )(x2d, att, p_in, p_in, w_pool, pool_scale, wo_att, wo_pool, ffn_gain, w1, w3, w2)


def _gelu_tanh(x):
    c = math.sqrt(2.0 / math.pi)
    return 0.5 * x * (1.0 + jnp.tanh(c * (x + 0.044715 * (x * x * x))))


def _s5_kernel(x_ref, g_ref, win_ref, perm_ref, permt_ref, bblk_ref, cblk_ref,
               are_ref, aim_ref, d_ref, o_ref, zero_ref, xs_ref, st_ref, u_ref, act_ref):
    nb, tc, d = x_ref.shape
    rows = nb * tc
    n_slabs = d // LANES
    sw = 2 * SSM_SLAB_STATE

    @pl.when(pl.program_id(0) == 0)
    def _():
        st_ref[...] = jnp.zeros_like(st_ref)

    pt = perm_ref.shape[0] // nb
    h_tb = []
    for t0 in range(0, tc, pt):
        x = x_ref[:, t0:t0 + pt, :].reshape(nb * pt, d)
        h = _rms(x, g_ref[...]).astype(BF16)
        h_tb.append(jnp.dot(perm_ref[...], h, preferred_element_type=F32).astype(BF16))
    u_ref[...] = jnp.dot(jnp.concatenate(h_tb, axis=0), win_ref[...],
                         preferred_element_type=F32)
    hs = SSM_SLAB_STATE

    def input_matmul(s):
        xs_ref[:, s * sw:(s + 1) * sw] = jnp.dot(
            u_ref[:, s * LANES:(s + 1) * LANES].astype(BF16), bblk_ref[s],
            preferred_element_type=F32)

    def output_matmul(s):
        sl = slice(s * LANES, (s + 1) * LANES)
        y = jnp.dot(xs_ref[:, s * sw:(s + 1) * sw].astype(BF16), cblk_ref[s],
                    preferred_element_type=F32) + d_ref[:, sl] * u_ref[:, sl]
        act_ref[:, sl] = _gelu_tanh(y).astype(BF16)

    ahead = 2
    for s in range(min(ahead, n_slabs)):
        input_matmul(s)
    for s in range(n_slabs):
        lo = s * sw
        if s + ahead < n_slabs:
            input_matmul(s + ahead)
        a_r = are_ref[:, lo:lo + hs]
        a_i = aim_ref[:, lo:lo + hs]
        x_r = st_ref[:, lo:lo + hs]
        x_i = st_ref[:, lo + hs:lo + sw]
        for t in range(tc):
            r0 = t * SUBLANES
            new_r = a_r * x_r - a_i * x_i + xs_ref[r0:r0 + SUBLANES, lo:lo + hs]
            new_i = a_r * x_i + a_i * x_r + xs_ref[r0:r0 + SUBLANES, lo + hs:lo + sw]
            xs_ref[r0:r0 + SUBLANES, lo:lo + hs] = new_r
            xs_ref[r0:r0 + SUBLANES, lo + hs:lo + sw] = new_i
            x_r, x_i = new_r, new_i
        st_ref[:, lo:lo + hs] = x_r
        st_ref[:, lo + hs:lo + sw] = x_i
        if s >= 1:
            output_matmul(s - 1)
    output_matmul(n_slabs - 1)
    for t0 in range(0, tc, pt):
        g_bt = jnp.dot(permt_ref[...], act_ref[t0 * nb:(t0 + pt) * nb, :],
                       preferred_element_type=F32)
        o_ref[:, t0:t0 + pt, :] = g_bt.reshape(nb, pt, d).astype(o_ref.dtype)
    zero_ref[...] = jnp.zeros_like(zero_ref)


def _s5_mixer(x3d, gain, w_in, bblk, cblk, a_re, a_im, d_skip, *, tc, zero_shape):
    nb, seq, d = x3d.shape
    steps = seq // tc
    zero_rows = zero_shape[0] // steps
    assert zero_rows * steps == zero_shape[0] and zero_rows % SEG_ALIGN == 0
    rows = nb * tc
    n_slabs = d // LANES
    sw = 2 * SSM_SLAB_STATE
    pt = min(tc, S5_PERM_STEPS)
    r = np.arange(nb * pt)
    perm = np.zeros((nb * pt, nb * pt), np.float32)
    perm[(r % pt) * nb + r // pt, r] = 1.0
    perm_j = jnp.asarray(perm, BF16)
    permt_j = jnp.asarray(perm.T, BF16)
    kern = _s5_kernel
    full2 = lambda a: pl.BlockSpec(a.shape, lambda i: (0, 0))
    full3 = lambda a: pl.BlockSpec(a.shape, lambda i: (0, 0, 0))
    return pl.pallas_call(
        kern,
        grid=(seq // tc,),
        in_specs=[
            pl.BlockSpec((nb, tc, d), lambda i: (0, i, 0)),
            full2(gain), full2(w_in), full2(perm_j), full2(permt_j),
            full3(bblk), full3(cblk), full2(a_re), full2(a_im), full2(d_skip),
        ],
        out_specs=[pl.BlockSpec((nb, tc, d), lambda i: (0, i, 0)),
                   pl.BlockSpec((zero_rows, zero_shape[1]), lambda i: (i, 0))],
        out_shape=[jax.ShapeDtypeStruct((nb, seq, d), BF16),
                   jax.ShapeDtypeStruct(zero_shape, BF16)],
        scratch_shapes=[
            pltpu.VMEM((rows, n_slabs * sw), F32),
            pltpu.VMEM((SUBLANES, n_slabs * sw), F32),
            pltpu.VMEM((rows, d), F32),
            pltpu.VMEM((rows, d), BF16),
        ],
        compiler_params=_params("arbitrary"),
        name="s5_mixer",
    )(x3d, gain, w_in, perm_j, permt_j, bblk, cblk, a_re, a_im, d_skip)


def _s5_coefficients(a_re, a_im, log_dt, b_re, b_im, c_re, c_im):
    dt = jnp.exp(log_dt.astype(F32))[:, None]
    ar = a_re.astype(F32)
    ai = a_im.astype(F32)
    mag = jnp.exp(ar * dt)
    abar_re = mag * jnp.cos(ai * dt)
    abar_im = mag * jnp.sin(ai * dt)
    den = ar * ar + ai * ai
    nr = abar_re - 1.0
    ni = abar_im
    coef_re = (nr * ar + ni * ai) / den
    coef_im = (ni * ar - nr * ai) / den
    br = b_re.astype(F32)
    bi = b_im.astype(F32)
    bbar_re = coef_re[..., None] * br - coef_im[..., None] * bi
    bbar_im = coef_re[..., None] * bi + coef_im[..., None] * br
    n_groups = ar.shape[0]
    n_slabs = n_groups // SSM_SLAB_GROUPS
    row_group = np.arange(LANES)[:, None] // SSM_GROUP
    col_group = np.arange(SSM_SLAB_STATE)[None, :] // SSM_STATE
    in_mask = jnp.asarray(row_group == col_group, F32)

    def in_block(bb):
        bb = bb.reshape(n_slabs, SSM_SLAB_GROUPS, SSM_STATE, SSM_GROUP)
        bb = bb.transpose(0, 1, 3, 2).reshape(n_slabs, LANES, SSM_STATE)
        return jnp.tile(bb, (1, 1, SSM_SLAB_GROUPS)) * in_mask

    def out_block(cc):
        cc = cc.reshape(n_slabs, SSM_SLAB_GROUPS, SSM_GROUP, SSM_STATE)
        cc = cc.transpose(0, 1, 3, 2).reshape(n_slabs, SSM_SLAB_STATE, SSM_GROUP)
        return jnp.tile(cc, (1, 1, SSM_SLAB_GROUPS)) * in_mask.T

    bblk = jnp.concatenate([in_block(bbar_re), in_block(bbar_im)], axis=2)
    cblk = jnp.concatenate([out_block(c_re.astype(F32)),
                            -out_block(c_im.astype(F32))], axis=1)

    def lanes(a):
        a = a.reshape(n_slabs, 1, SSM_SLAB_STATE)
        a = jnp.concatenate([a, a], axis=2).reshape(1, -1)
        return jnp.broadcast_to(a, (SUBLANES, a.shape[1]))

    return (bblk.astype(BF16), cblk.astype(BF16), lanes(abar_re), lanes(abar_im))


MOE_CHUNK = 512
MOE_FIRST = 208
MOE_REST = MOE_CHUNK + SEG_ALIGN - MOE_FIRST
MOE_PIECES = ((0, MOE_FIRST), (MOE_FIRST, MOE_REST))
MOE_WINDOW = MOE_FIRST + MOE_REST
NOT_ROUTED = -1024.0
MOE_TILE = 512


def _glu_router_kernel(x_ref, act_ref, wa_ref, wb_ref, g_ref, w_ref, b_ref,
                       x3_ref, h_ref, pos_ref, post_ref, cnt_ref):
    tm, d = x_ref.shape
    parts = 2
    half = tm // parts

    def glu(r0):
        act = act_ref[r0:r0 + half, :]
        a = jnp.dot(act, wa_ref[...], preferred_element_type=F32)
        b = jnp.dot(act, wb_ref[...], preferred_element_type=F32)
        x3 = x_ref[r0:r0 + half, :] + a * _sigmoid(b)
        x3_ref[r0:r0 + half, :] = x3
        return x3

    def route(r0, x3):
        h = _rms(x3, g_ref[...])
        h_hi = h.astype(BF16)
        h_ref[r0:r0 + half, 0:d] = h_hi
        h_lo = (h - h_hi.astype(F32)).astype(BF16)
        p_hi = jnp.dot(h_hi, w_ref[...], preferred_element_type=F32)
        p_lo = jnp.dot(h_lo, w_ref[...], preferred_element_type=F32)
        logits = p_hi + pltpu.roll(p_hi, LANES - N_EXPERTS, axis=1) + p_lo + b_ref[...]
        lane = lax.broadcasted_iota(jnp.int32, logits.shape, 1)
        logits = jnp.where(lane < N_EXPERTS, logits, -jnp.inf)
        m1 = jnp.max(logits, axis=1, keepdims=True)
        i1 = jnp.min(jnp.where(logits == m1, lane, LANES), axis=1, keepdims=True)
        rest = jnp.where(lane == i1, -jnp.inf, logits)
        m2 = jnp.max(rest, axis=1, keepdims=True)
        i2 = jnp.min(jnp.where(rest == m2, lane, LANES), axis=1, keepdims=True)
        e2 = jnp.exp(m2 - m1)
        g1 = 1.0 / (1.0 + e2)
        g2 = e2 / (1.0 + e2)
        gate = jnp.where(lane == i1, g1, 0.0) + jnp.where(lane == i2, g2, 0.0)
        g_hi = gate.astype(BF16).astype(F32)
        g_r = gate - g_hi
        g_mid = g_r.astype(BF16).astype(F32)
        packed = jnp.where(lane < N_EXPERTS, g_hi,
                           jnp.where(lane < 2 * N_EXPERTS,
                                     pltpu.roll(g_mid, N_EXPERTS, axis=1),
                                     pltpu.roll(g_r - g_mid, 2 * N_EXPERTS, axis=1)))
        h_ref[r0:r0 + half, d:d + LANES] = packed.astype(BF16)
        return jnp.where((lane == i1) | (lane == i2), 1.0, 0.0)

    members = []
    x3 = glu(0)
    for p in range(parts):
        nxt = glu((p + 1) * half) if p + 1 < parts else None
        members.append(route(p * half, x3))
        x3 = nxt
    member = jnp.concatenate(members, axis=0)
    row = lax.broadcasted_iota(jnp.int32, member.shape, 0)
    c = member
    sh = 1
    while sh < tm:
        c = c + jnp.where(row >= sh, pltpu.roll(c, sh, axis=0), 0.0)
        sh *= 2
    pos = jnp.where(member > 0.0, c - member, NOT_ROUTED)
    pos_ref[...] = pos
    post_ref[0] = pos.T[0:SUBLANES, :]
    cnt_ref[0] = jnp.broadcast_to(c[tm - 1:tm, :], (SUBLANES, LANES))


def _glu_router(x2d, act2d, wa, wb, gain, w_pad, b_pad):
    t, d = x2d.shape
    tm = MOE_CHUNK
    n_chunks = t // tm
    return pl.pallas_call(
        _glu_router_kernel,
        grid=(n_chunks,),
        in_specs=[
            pl.BlockSpec((tm, d), lambda i: (i, 0)),
            pl.BlockSpec((tm, d), lambda i: (i, 0)),
            pl.BlockSpec(wa.shape, lambda i: (0, 0)),
            pl.BlockSpec(wb.shape, lambda i: (0, 0)),
            pl.BlockSpec((1, d), lambda i: (0, 0)),
            pl.BlockSpec((d, LANES), lambda i: (0, 0)),
            pl.BlockSpec((1, LANES), lambda i: (0, 0)),
        ],
        out_specs=[
            pl.BlockSpec((tm, d), lambda i: (i, 0)),
            pl.BlockSpec((tm, d + LANES), lambda i: (i, 0)),
            pl.BlockSpec((tm, LANES), lambda i: (i, 0)),
            pl.BlockSpec((1, SUBLANES, tm), lambda i: (i, 0, 0)),
            pl.BlockSpec((1, SUBLANES, LANES), lambda i: (i, 0, 0)),
        ],
        out_shape=[
            jax.ShapeDtypeStruct((t, d), F32),
            jax.ShapeDtypeStruct((t, d + LANES), BF16),
            jax.ShapeDtypeStruct((t, LANES), F32),
            jax.ShapeDtypeStruct((n_chunks, SUBLANES, tm), F32),
            jax.ShapeDtypeStruct((n_chunks, SUBLANES, LANES), F32),
        ],
        compiler_params=_params("parallel"),
        name="glu_router",
    )(x2d, act2d, wa, wb, gain, w_pad, b_pad)


MOE_SPARE = max(MOE_FIRST, MOE_REST)


def _moe_tiles(n_tokens, row_multiple):
    max_rows = (TOP_K * n_tokens + N_EXPERTS * (SEG_ALIGN - 1)
                + N_EXPERTS * (MOE_SPARE + MOE_TILE))
    unit = math.lcm(MOE_TILE, row_multiple)
    return -(-max_rows // unit) * (unit // MOE_TILE)


def _moe_layout(cnt):
    before = jnp.cumsum(cnt, axis=0) - cnt
    shift = before % SEG_ALIGN
    used = (jnp.sum(cnt, axis=0) + SEG_ALIGN - 1) // SEG_ALIGN * SEG_ALIGN
    padded = (used + MOE_SPARE + MOE_TILE - 1) // MOE_TILE * MOE_TILE
    ends = jnp.cumsum(padded)
    start = ends - padded
    off = start[None, :] + before - shift
    flat = lambda a: a.reshape(-1).astype(jnp.int32)
    return (flat(off), flat(shift + cnt), flat(shift),
            ends.astype(jnp.int32), (start + used).astype(jnp.int32))


def _tile_expert(i, ends_ref):
    first_row = i * MOE_TILE
    e = 0
    for k in range(N_EXPERTS - 1):
        e = e + (first_row >= ends_ref[k]).astype(jnp.int32)
    return e


def _segment_copies(hbm_ref, buf_ref, sem_ref, off_ref, chunk, to_hbm, slot=()):
    copies = []
    for e in range(N_EXPERTS):
        off = pl.multiple_of(off_ref[chunk * N_EXPERTS + e], SEG_ALIGN)
        for piece, (first, rows) in enumerate(MOE_PIECES):
            hbm = hbm_ref.at[pl.ds(off + first, rows)]
            buf = buf_ref.at[(*slot, e, pl.ds(first, rows))]
            src, dst = (buf, hbm) if to_hbm else (hbm, buf)
            copies.append(pltpu.make_async_copy(src, dst, sem_ref.at[(*slot, e, piece)]))
    return copies


def _onehot(first, shape, axis, target):
    rank = first + lax.broadcasted_iota(jnp.int32, shape, axis)
    return jnp.where(rank.astype(F32) == target, 1.0, 0.0).astype(BF16)


def _gather_kernel(off_ref, need_ref, shift_ref, h_ref, post_ref, zeros_hbm, o_hbm,
                   stage_ref, carry_ref, sem_ref):
    del zeros_hbm
    c = pl.program_id(0)
    slot = c % 2
    h = h_ref[...]

    @pl.when(c == 0)
    def _():
        carry_ref[...] = jnp.zeros_like(carry_ref)

    def long_segment(chunk, e):
        return need_ref[chunk * N_EXPERTS + e] >= MOE_FIRST

    def piece_onehot(e, piece):
        first, rows = MOE_PIECES[piece]
        return _onehot(first - shift_ref[c * N_EXPERTS + e], (rows, MOE_CHUNK), 0,
                       post_ref[0, e:e + 1, :])

    def fill_piece(e, piece, onehot=None):
        first, rows = MOE_PIECES[piece]
        onehot = piece_onehot(e, piece) if onehot is None else onehot
        stage_ref[slot, e, first:first + rows] = jnp.dot(
            onehot, h, preferred_element_type=F32).astype(BF16)
        if piece == 0:
            stage_ref[slot, e, 0:SEG_ALIGN] += carry_ref[e]

    onehot = piece_onehot(0, 0)
    for e in range(N_EXPERTS):
        nxt = piece_onehot(e + 1, 0) if e + 1 < N_EXPERTS else None
        fill_piece(e, 0, onehot)
        onehot = nxt
    for e in range(N_EXPERTS):
        @pl.when(long_segment(c, e))
        def _(e=e):
            fill_piece(e, 1)

    for e in range(N_EXPERTS):
        last_group = need_ref[c * N_EXPERTS + e] // SEG_ALIGN * SEG_ALIGN
        carry_ref[e] = stage_ref[slot, e, pl.ds(pl.multiple_of(last_group, SEG_ALIGN),
                                                SEG_ALIGN)]

    def each_copy(chunk, sl, action):
        copies = _segment_copies(o_hbm, stage_ref, sem_ref, off_ref, chunk,
                                 to_hbm=True, slot=(sl,))
        for e in range(N_EXPERTS):
            action(copies[2 * e])

            @pl.when(long_segment(chunk, e))
            def _(e=e):
                action(copies[2 * e + 1])

    @pl.when(c > 0)
    def _():
        each_copy(c - 1, 1 - slot, lambda cp: cp.wait())

    each_copy(c, slot, lambda cp: cp.start())

    @pl.when(c == pl.num_programs(0) - 1)
    def _():
        each_copy(c, slot, lambda cp: cp.wait())


def _gather(h2d, post, off, need, shift, zero_buf):
    t, d = h2d.shape
    n_chunks = t // MOE_CHUNK
    grid_spec = pltpu.PrefetchScalarGridSpec(
        num_scalar_prefetch=3,
        grid=(n_chunks,),
        in_specs=[
            pl.BlockSpec((MOE_CHUNK, d), lambda i, *_: (i, 0)),
            pl.BlockSpec((1, SUBLANES, MOE_CHUNK), lambda i, *_: (i, 0, 0)),
            pl.BlockSpec(memory_space=pl.ANY),
        ],
        out_specs=pl.BlockSpec(memory_space=pl.ANY),
        scratch_shapes=[
            pltpu.VMEM((2, N_EXPERTS, MOE_WINDOW, d), BF16),
            pltpu.VMEM((N_EXPERTS, SEG_ALIGN, d), BF16),
            pltpu.SemaphoreType.DMA((2, N_EXPERTS, 2)),
        ],
    )
    return pl.pallas_call(
        _gather_kernel,
        grid_spec=grid_spec,
        out_shape=jax.ShapeDtypeStruct(zero_buf.shape, BF16),
        input_output_aliases={5: 0},
        compiler_params=_params("arbitrary"),
        name="moe_gather",
    )(off, need, shift, h2d, post, zero_buf)


def _expert_kernel(ends_ref, used_end_ref, x_ref, w1_ref, w3_ref, w2_ref, o_ref):
    i = pl.program_id(0)
    expert = _tile_expert(i, ends_ref)
    rows = jnp.clip(used_end_ref[expert] - i * MOE_TILE, 0, MOE_TILE)
    d = o_ref.shape[1]

    def run(n):
        row = lax.broadcasted_iota(jnp.int32, (n, x_ref.shape[1]), 0)
        x = jnp.where(row < rows, x_ref[0:n, :], jnp.zeros((), x_ref.dtype))
        lane = lax.broadcasted_iota(jnp.int32, (n, LANES), 1)
        mine = (lane < 3 * N_EXPERTS) & (jnp.bitwise_and(lane, N_EXPERTS - 1) == expert)
        gate = jnp.sum(jnp.where(mine, x[:, d:d + LANES].astype(F32), 0.0),
                       axis=1, keepdims=True)
        y = _swiglu(x[:, 0:d], w1_ref, w3_ref, w2_ref)
        o_ref[0:n, :] = (gate * y).astype(o_ref.dtype)

    quarter = MOE_TILE // 4
    for n in range(quarter, MOE_TILE + 1, quarter):
        @pl.when((rows > n - quarter) & (rows <= n))
        def _(n=n):
            run(n)
            if n < MOE_TILE:
                o_ref[n:MOE_TILE, :] = jnp.zeros((MOE_TILE - n, d), o_ref.dtype)

    @pl.when(rows == 0)
    def _():
        o_ref[...] = jnp.zeros_like(o_ref)


def _experts(x_sorted, w1, w3, w2, ends, used_end):
    assert N_EXPERTS & (N_EXPERTS - 1) == 0
    n_rows = x_sorted.shape[0]
    f = w1.shape[1]
    d = w2.shape[1]
    grid_spec = pltpu.PrefetchScalarGridSpec(
        num_scalar_prefetch=2,
        grid=(n_rows // MOE_TILE,),
        in_specs=[
            pl.BlockSpec((MOE_TILE, d + LANES), lambda i, ends, ue: (i, 0)),
            pl.BlockSpec((d, f), lambda i, ends, ue: (_tile_expert(i, ends), 0)),
            pl.BlockSpec((d, f), lambda i, ends, ue: (_tile_expert(i, ends), 0)),
            pl.BlockSpec((f, d), lambda i, ends, ue: (_tile_expert(i, ends), 0)),
        ],
        out_specs=pl.BlockSpec((MOE_TILE, d), lambda i, ends, ue: (i, 0)),
    )
    return pl.pallas_call(
        _expert_kernel,
        grid_spec=grid_spec,
        out_shape=jax.ShapeDtypeStruct((n_rows, d), BF16),
        compiler_params=_params("arbitrary"),
        name="moe_experts",
    )(ends, used_end, x_sorted, w1, w3, w2)


def _combine_kernel(off_ref, need_ref, shift_ref, x_ref, pos_ref, fg_ref, y_hbm,
                    o_ref, ybuf_ref, acc_ref, sem_ref):
    c = pl.program_id(0)
    n_chunks = pl.num_programs(0)
    slot = c % 2

    def fetch(chunk, sl, start):
        copies = _segment_copies(y_hbm, ybuf_ref, sem_ref, off_ref, chunk,
                                 to_hbm=False, slot=(sl,))
        for e in range(N_EXPERTS):
            first, second = copies[2 * e], copies[2 * e + 1]
            if start:
                first.start()
            else:
                first.wait()

            @pl.when(need_ref[chunk * N_EXPERTS + e] >= MOE_FIRST)
            def _(second=second):
                if start:
                    second.start()
                else:
                    second.wait()

    @pl.when(c == 0)
    def _():
        fetch(c, slot, True)

    @pl.when(c + 1 < n_chunks)
    def _():
        fetch(c + 1, 1 - slot, True)

    fetch(c, slot, False)

    def piece_onehot(e, piece):
        first, rows = MOE_PIECES[piece]
        return _onehot(first - shift_ref[c * N_EXPERTS + e], (MOE_CHUNK, rows), 1,
                       pos_ref[:, e:e + 1])

    def piece_rows(e, piece, onehot=None):
        first, rows = MOE_PIECES[piece]
        onehot = piece_onehot(e, piece) if onehot is None else onehot
        return jnp.dot(onehot, ybuf_ref[slot, e, first:first + rows],
                       preferred_element_type=F32)

    acc = x_ref[...]
    onehot = piece_onehot(0, 0)
    for e in range(N_EXPERTS):
        nxt = piece_onehot(e + 1, 0) if e + 1 < N_EXPERTS else None
        acc = acc + piece_rows(e, 0, onehot)
        onehot = nxt
    acc_ref[...] = acc
    for e in range(N_EXPERTS):
        @pl.when(need_ref[c * N_EXPERTS + e] >= MOE_FIRST)
        def _(e=e):
            acc_ref[...] += piece_rows(e, 1)

    o_ref[...] = _rms(acc_ref[...], fg_ref[...])


def _combine(x2d, pos, final_gain, y_sorted, off, need, shift):
    t, d = x2d.shape
    n_chunks = t // MOE_CHUNK
    grid_spec = pltpu.PrefetchScalarGridSpec(
        num_scalar_prefetch=3,
        grid=(n_chunks,),
        in_specs=[
            pl.BlockSpec((MOE_CHUNK, d), lambda i, *_: (i, 0)),
            pl.BlockSpec((MOE_CHUNK, LANES), lambda i, *_: (i, 0)),
            pl.BlockSpec((1, d), lambda i, *_: (0, 0)),
            pl.BlockSpec(memory_space=pl.ANY),
        ],
        out_specs=pl.BlockSpec((MOE_CHUNK, d), lambda i, *_: (i, 0)),
        scratch_shapes=[
            pltpu.VMEM((2, N_EXPERTS, MOE_WINDOW, d), BF16),
            pltpu.VMEM((MOE_CHUNK, d), F32),
            pltpu.SemaphoreType.DMA((2, N_EXPERTS, 2)),
        ],
    )
    return pl.pallas_call(
        _combine_kernel,
        grid_spec=grid_spec,
        out_shape=jax.ShapeDtypeStruct((t, d), F32),
        compiler_params=_params("arbitrary"),
        name="moe_combine",
    )(off, need, shift, x2d, pos, final_gain, y_sorted)


def _glu_moe(x2d, act2d, wa, wb, gain, router_w, router_b, w1, w3, w2, final_gain,
             zero_buf):
    t, d = x2d.shape
    rw_hi = router_w.astype(BF16)
    rw_lo = (router_w.astype(F32) - rw_hi.astype(F32)).astype(BF16)
    rw = jnp.pad(jnp.concatenate([rw_hi, rw_lo], axis=1),
                 ((0, 0), (0, LANES - 2 * N_EXPERTS)))
    rb = jnp.pad(router_b.astype(F32), (0, LANES - N_EXPERTS)).reshape(1, LANES)
    x3, h, pos, post, cnt = _glu_router(x2d, act2d, wa, wb, gain, rw, rb)
    cnt = cnt[:, 0, :N_EXPERTS].astype(jnp.int32)
    off, need, shift, ends, used_end = _moe_layout(cnt)
    h_sorted = _gather(h, post, off, need, shift, zero_buf)
    y_sorted = _experts(h_sorted, w1, w3, w2, ends, used_end)
    return _combine(x3, pos, final_gain, y_sorted, off, need, shift)


def _even_weights(w_in, b_forget):
    n = w_in.shape[1]
    w_all = jnp.pad(w_in.astype(BF16), ((0, 0), (0, -n % LANES)))
    bias = jnp.pad(b_forget.astype(F32), (0, LANES - FOX_HEADS)).reshape(1, LANES)
    qw = FOX_HEADS * LANES
    place = np.zeros((LANES, 2 * qw), np.float32)
    for hh in range(FOX_HEADS):
        for piece in range(3):
            place[piece * FOX_HEADS + hh, hh * LANES + BIAS_LANE + piece] = 1.0
            place[piece * FOX_HEADS + hh, qw + hh * LANES + ONES_LANE + piece] = -1.0
    return w_all, bias, jnp.asarray(place, BF16)


def kernel(x, even_mix_norm, even_w_in, even_b_forget, even_w_pool, even_pool_scale, even_w_out, even_ffn_norm, even_ffn_w1, even_ffn_w3, even_ffn_w2, odd_mix_norm, odd_w_in, ssm_a_re, ssm_a_im, ssm_log_dt, ssm_b_re, ssm_b_im, ssm_c_re, ssm_c_im, ssm_d, odd_w_glu_a, odd_w_glu_b, odd_moe_norm, router_w, router_b, expert_w1, expert_w3, expert_w2, final_norm):
    b, s, d = x.shape
    t = b * s
    assert b == SUBLANES, "the S5 recurrence keeps one batch row per sublane"
    x2d = x.reshape(t, d)
    row = lambda v: v.reshape(1, -1).astype(F32)

    w_all, bias, place = _even_weights(even_w_in[0], even_b_forget[0])
    (q_aug, k_aug, v, p_in), casted = _even_inproj(
        x2d, row(even_mix_norm[0]), w_all, bias, place,
        [even_ffn_w1[0], even_ffn_w3[0], even_ffn_w2[0], even_w_out[0],
         even_w_pool[0].reshape(-1, POOL_GROUP_DIM),
         odd_w_in[0], odd_w_glu_a[0], odd_w_glu_b[0]],
        seq=s, tm=INPROJ_ROWS)
    ffn_w1, ffn_w3, ffn_w2, w_out, w_pool, s5_w_in, glu_a, glu_b = casted
    n_e, _, f = expert_w1[0].shape
    att, (ew1, ew3, ew2) = _attention(
        q_aug, k_aug, v,
        [expert_w1[0].reshape(n_e * d, f), expert_w3[0].reshape(n_e * d, f),
         expert_w2[0].reshape(n_e * f, d)],
        batch=b, seq=s, tq=ATTN_QUERY_TILE)
    x2 = _even_tail(x2d, att, p_in, w_pool.reshape(even_w_pool[0].shape),
                    row(even_pool_scale[0]), w_out, row(even_ffn_norm[0]),
                    ffn_w1, ffn_w3, ffn_w2, seq=s, tm=EVEN_TAIL_ROWS)

    bblk, cblk, a_re, a_im = _s5_coefficients(
        ssm_a_re[0], ssm_a_im[0], ssm_log_dt[0], ssm_b_re[0], ssm_b_im[0],
        ssm_c_re[0], ssm_c_im[0])
    moe_rows = _moe_tiles(t, (s // S5_STEP) * SEG_ALIGN) * MOE_TILE
    g, zero_buf = _s5_mixer(x2.reshape(b, s, d), row(odd_mix_norm[0]),
                            s5_w_in, bblk, cblk, a_re, a_im,
                            row(ssm_d[0]), tc=S5_STEP,
                            zero_shape=(moe_rows, d + LANES))
    out = _glu_moe(x2, g.reshape(t, d), glu_a, glu_b, row(odd_moe_norm[0]),
                   router_w[0], router_b[0], ew1, ew3, ew2,
                   row(final_norm), zero_buf)
    return out.reshape(b, s, d)
```

```python
import functools
import math

import numpy as np
import jax
import jax.numpy as jnp
from jax import lax
from jax.experimental import pallas as pl
from jax.experimental.pallas import tpu as pltpu

F32 = jnp.float32
BF16 = jnp.bfloat16

EPS = 1e-6
NEG_INF = -1e30
LANES = 128
SUBLANES = 8
SEG_ALIGN = 16
VMEM_LIMIT = 56 * 1024 * 1024

FOX_HEADS = 8
FOX_HEAD_DIM = 64
FOX_WIDTH = FOX_HEADS * FOX_HEAD_DIM
POOL_WINDOWS = (2, 4, 8, 16)
POOL_GROUP_DIM = 128
POOL_WIDTH = len(POOL_WINDOWS) * POOL_GROUP_DIM
POOL_HALO = 16
SSM_GROUP = 16
SSM_STATE = 64
SSM_SLAB_GROUPS = LANES // SSM_GROUP
SSM_SLAB_STATE = SSM_SLAB_GROUPS * SSM_STATE

INPROJ_ROWS = 512
ATTN_QUERY_TILE = 256
EVEN_TAIL_ROWS = 512
S5_STEP = 64
S5_PERM_STEPS = 32
N_EXPERTS = 8
TOP_K = 2

BIAS_LANE = FOX_HEAD_DIM
ONES_LANE = FOX_HEAD_DIM + 3
DENOM_LANE = FOX_HEAD_DIM


def _params(*sem):
    return pltpu.CompilerParams(dimension_semantics=sem,
                                vmem_limit_bytes=VMEM_LIMIT)


def _rms(x, g):
    ms = jnp.mean(x * x, axis=-1, keepdims=True)
    return x * lax.rsqrt(ms + EPS) * g


def _sigmoid(x):
    return 1.0 / (1.0 + jnp.exp(-x))


def _lane_range_ones(lo, hi):
    lane = lax.broadcasted_iota(jnp.int32, (1, LANES), 1)
    return jnp.where((lane >= lo) & (lane < hi), 1.0, 0.0).astype(F32)


def _cast_specs(weights, steps, step_index):
    specs = []
    for w in weights:
        rows = -(-w.shape[0] // (steps * SEG_ALIGN)) * SEG_ALIGN
        last = -(-w.shape[0] // rows) - 1
        specs.append(pl.BlockSpec(
            (rows, w.shape[1]),
            lambda *idx, last=last: (jnp.minimum(step_index(*idx), last), 0)))
    return specs


def _cast_slabs(src_refs, dst_refs):
    for src_ref, dst_ref in zip(src_refs, dst_refs):
        dst_ref[...] = src_ref[...].astype(dst_ref.dtype)


def _even_inproj_kernel(x_ref, g_ref, w_ref, bias_ref, place_ref, *rest,
                        tiles_per_seq, n_cast):
    cast_in, rest = rest[:n_cast], rest[n_cast:]
    q_ref, k_ref, v_ref, p_ref = rest[:4]
    cast_out, carry_ref = rest[4:4 + n_cast], rest[4 + n_cast]
    _cast_slabs(cast_in, cast_out)
    i = pl.program_id(0)
    tm = x_ref.shape[0]
    parts = 2
    rows = tm // parts
    fw = FOX_WIDTH
    qw = FOX_HEADS * LANES

    @pl.when(i % tiles_per_seq == 0)
    def _():
        carry_ref[...] = jnp.zeros_like(carry_ref)

    def project(r0):
        h = _rms(x_ref[r0:r0 + rows, :], g_ref[...]).astype(BF16)
        return jnp.dot(h, w_ref[...], preferred_element_type=F32)

    def finish(r0, z, carry):
        out = slice(r0, r0 + rows)
        p_ref[out, :] = z[:, 3 * fw + FOX_HEADS:3 * fw + FOX_HEADS + POOL_WIDTH]
        fg = z[:, 3 * fw:3 * fw + LANES] + bias_ref[...]
        lf = jnp.minimum(fg, 0.0) - jnp.log1p(jnp.exp(-jnp.abs(fg)))
        row = lax.broadcasted_iota(jnp.int32, lf.shape, 0)
        c = lf
        sh = 1
        while sh < rows:
            c = c + jnp.where(row >= sh, pltpu.roll(c, sh, axis=0), 0.0)
            sh *= 2
        c = c + carry
        hi = c.astype(BF16).astype(F32)
        r1 = c - hi
        mid = r1.astype(BF16).astype(F32)
        lo = r1 - mid
        lane = lax.broadcasted_iota(jnp.int32, (rows, LANES), 1)
        packed = jnp.where(lane < FOX_HEADS, hi,
                           jnp.where(lane < 2 * FOX_HEADS,
                                     pltpu.roll(mid, FOX_HEADS, axis=1),
                                     pltpu.roll(lo, 2 * FOX_HEADS, axis=1)))
        placed = jnp.dot(packed.astype(BF16), place_ref[...],
                         preferred_element_type=F32)
        low = lane < FOX_HEAD_DIM
        ones_q = _lane_range_ones(ONES_LANE, ONES_LANE + 3)
        ones_k = _lane_range_ones(BIAS_LANE, BIAS_LANE + 3)
        ones_v = _lane_range_ones(DENOM_LANE, DENOM_LANE + 1)

        def head_lanes(base, hh):
            pair = z[:, base + (hh // 2) * LANES:base + (hh // 2 + 1) * LANES]
            return pltpu.roll(pair, FOX_HEAD_DIM, axis=1) if hh % 2 else pair

        q_scale = FOX_HEAD_DIM ** -0.5
        for hh in range(FOX_HEADS):
            sl = slice(hh * LANES, (hh + 1) * LANES)
            q_ref[out, sl] = jnp.where(low, head_lanes(0, hh) * q_scale,
                                       placed[:, sl] + ones_q).astype(BF16)
            k_ref[out, sl] = jnp.where(
                low, head_lanes(fw, hh),
                placed[:, qw + hh * LANES:qw + (hh + 1) * LANES] + ones_k).astype(BF16)
            v_ref[out, sl] = jnp.where(low, head_lanes(2 * fw, hh), ones_v).astype(BF16)
        return c[rows - 1:rows, :]

    carry = carry_ref[0:1, :]
    z = project(0)
    for part in range(parts):
        nxt = project((part + 1) * rows) if part + 1 < parts else None
        carry = finish(part * rows, z, carry)
        z = nxt
    carry_ref[...] = jnp.broadcast_to(carry, carry_ref.shape)


def _even_inproj(x2d, gain, w_all, bias, place, cast_weights, *, seq, tm):
    t, d = x2d.shape
    n = w_all.shape[1]
    qw = FOX_HEADS * LANES
    cast_specs = _cast_specs(cast_weights, t // tm, lambda i: i)
    kern = functools.partial(_even_inproj_kernel, tiles_per_seq=seq // tm,
                             n_cast=len(cast_weights))
    outs = pl.pallas_call(
        kern,
        grid=(t // tm,),
        in_specs=[
            pl.BlockSpec((tm, d), lambda i: (i, 0)),
            pl.BlockSpec((1, d), lambda i: (0, 0)),
            pl.BlockSpec((d, n), lambda i: (0, 0)),
            pl.BlockSpec((1, LANES), lambda i: (0, 0)),
            pl.BlockSpec(place.shape, lambda i: (0, 0)),
            *cast_specs,
        ],
        out_specs=[
            pl.BlockSpec((tm, qw), lambda i: (i, 0)),
            pl.BlockSpec((tm, qw), lambda i: (i, 0)),
            pl.BlockSpec((tm, qw), lambda i: (i, 0)),
            pl.BlockSpec((tm, POOL_WIDTH), lambda i: (i, 0)),
            *cast_specs,
        ],
        out_shape=[
            jax.ShapeDtypeStruct((t, qw), BF16),
            jax.ShapeDtypeStruct((t, qw), BF16),
            jax.ShapeDtypeStruct((t, qw), BF16),
            jax.ShapeDtypeStruct((t, POOL_WIDTH), F32),
        ] + [jax.ShapeDtypeStruct(w.shape, BF16) for w in cast_weights],
        scratch_shapes=[pltpu.VMEM((SUBLANES, LANES), F32)],
        compiler_params=_params("arbitrary"),
        name="even_inproj",
    )(x2d, gain, w_all, bias, place, *cast_weights)
    return outs[:4], outs[4:]


def _dot_nt(a, b):
    return lax.dot_general(a, b, (((1,), (1,)), ((), ())),
                           preferred_element_type=F32)


def _attn_kernel(q_ref, k_ref, v_ref, *rest, tq, n_cast):
    cast_in, o_ref, cast_out = rest[:n_cast], rest[n_cast], rest[n_cast + 1:]
    seq = q_ref.shape[0]
    row = lax.broadcasted_iota(jnp.int32, (tq, tq), 0)
    col = lax.broadcasted_iota(jnp.int32, (tq, tq), 1)
    causal = col <= row
    lane = lax.broadcasted_iota(jnp.int32, (tq, LANES), 1)
    work = [(qi, hh) for qi in range(seq // tq) for hh in range(2)]

    def scores(qi, hh):
        r0 = qi * tq
        sl = slice(hh * LANES, (hh + 1) * LANES)
        q = q_ref[r0:r0 + tq, sl]
        s = _dot_nt(q, k_ref[0:r0 + tq, sl])
        s_diag = jnp.where(causal, s[:, r0:r0 + tq], NEG_INF)
        return s_diag if qi == 0 else jnp.concatenate([s[:, 0:r0], s_diag], axis=1)

    ahead = 2
    queue = [scores(*item) for item in work[:ahead]]
    first_head = None
    for n, (qi, hh) in enumerate(work):
        r0 = qi * tq
        sl = slice(hh * LANES, (hh + 1) * LANES)
        s = queue.pop(0)
        if n + ahead < len(work):
            queue.append(scores(*work[n + ahead]))
        m = jnp.max(s, axis=1, keepdims=True)
        acc = jnp.dot(jnp.exp(s - m).astype(BF16), v_ref[0:r0 + tq, sl],
                      preferred_element_type=F32)
        out = acc / acc[:, DENOM_LANE:DENOM_LANE + 1]
        if hh == 0:
            first_head = out
        else:
            o_ref[r0:r0 + tq, :] = jnp.where(
                lane < FOX_HEAD_DIM, first_head,
                pltpu.roll(out, FOX_HEAD_DIM, axis=1)).astype(o_ref.dtype)
    _cast_slabs(cast_in, cast_out)


def _attention(q_aug, k_aug, v_aug, cast_weights, *, batch, seq, tq):
    t = q_aug.shape[0]
    pairs = FOX_HEADS // 2
    steps = batch * pairs
    spec = pl.BlockSpec((seq, 2 * LANES), lambda b, hp: (b, hp))
    out_spec = pl.BlockSpec((seq, LANES), lambda b, hp: (b, hp))
    cast_specs = _cast_specs(cast_weights, steps, lambda b, hp: b * pairs + hp)
    outs = pl.pallas_call(
        functools.partial(_attn_kernel, tq=tq, n_cast=len(cast_weights)),
        grid=(batch, pairs),
        in_specs=[spec, spec, spec, *cast_specs],
        out_specs=[out_spec, *cast_specs],
        out_shape=[jax.ShapeDtypeStruct((t, FOX_WIDTH), BF16)]
        + [jax.ShapeDtypeStruct(w.shape, BF16) for w in cast_weights],
        compiler_params=_params("parallel", "parallel"),
        name="fox_attention",
    )(q_aug, k_aug, v_aug, *cast_weights)
    return outs[0], outs[1:]


MXU_TILE = 256


def _swiglu_partial(h, w1, w3, w2):
    a = jnp.dot(h, w1, preferred_element_type=F32)
    b = jnp.dot(h, w3, preferred_element_type=F32)
    act = a * _sigmoid(a) * b
    return jnp.dot(act.astype(BF16), w2, preferred_element_type=F32)


def _ff_splits(f):
    cut = -(-(f // MXU_TILE) // 2) * MXU_TILE
    return ((0, cut), (cut, f)) if 0 < cut < f else ((0, f),)


def _swiglu(h, w1_ref, w3_ref, w2_ref):
    y = None
    for lo, hi in _ff_splits(w1_ref.shape[1]):
        part = _swiglu_partial(h, w1_ref[:, lo:hi], w3_ref[:, lo:hi], w2_ref[lo:hi, :])
        y = part if y is None else y + part
    return y


def _even_tail_kernel(x_ref, att_ref, p_ref, halo_ref, wpool_ref, scale_ref,
                      wo_att_ref, wo_pool_ref, g_ref, w1_ref, w3_ref, w2_ref, o_ref,
                      *, tiles_per_seq):
    i = pl.program_id(0)
    tm = x_ref.shape[0]
    tile_in_seq = i % tiles_per_seq
    p = p_ref[...]
    halo = jnp.where(tile_in_seq == 0, 0.0, halo_ref[...])
    ext = jnp.concatenate([halo, p], axis=0)
    pos = tile_in_seq * tm + lax.broadcasted_iota(jnp.int32, (tm, 1), 0)
    count = (pos + 1).astype(F32)
    mixed = []
    for gi, w in enumerate(POOL_WINDOWS):
        sl = slice(gi * POOL_GROUP_DIM, (gi + 1) * POOL_GROUP_DIM)
        acc = ext[:, sl]
        sh = 1
        while sh < w:
            acc = acc + pltpu.roll(acc, sh, axis=0)
            sh *= 2
        mean = acc[POOL_HALO:, :] / jnp.minimum(count, float(w))
        pooled = (mean - p[:, sl]).astype(BF16)
        mixed.append(jnp.dot(pooled, wpool_ref[gi], preferred_element_type=F32))
    pool = (jnp.concatenate(mixed, axis=1) * scale_ref[...]).astype(BF16)
    y = jnp.dot(att_ref[...], wo_att_ref[...], preferred_element_type=F32)
    y = y + jnp.dot(pool, wo_pool_ref[...], preferred_element_type=F32)
    x1 = x_ref[...] + y
    h = _rms(x1, g_ref[...]).astype(BF16)
    o_ref[...] = x1 + _swiglu(h, w1_ref, w3_ref, w2_ref)


def _even_tail(x2d, att, p_in, w_pool, pool_scale, w_out, ffn_gain, w1, w3, w2,
               *, seq, tm):
    t, d = x2d.shape
    resident = lambda w: pl.BlockSpec(w.shape, lambda i: (0,) * w.ndim,
                                      pipeline_mode=pl.Buffered(1))
    wo_att = w_out[:FOX_WIDTH]
    wo_pool = w_out[FOX_WIDTH:]
    halo_blocks = tm // POOL_HALO
    kern = functools.partial(_even_tail_kernel, tiles_per_seq=seq // tm)
    return pl.pallas_call(
        kern,
        grid=(t // tm,),
        in_specs=[
            pl.BlockSpec((tm, d), lambda i: (i, 0)),
            pl.BlockSpec((tm, FOX_WIDTH), lambda i: (i, 0)),
            pl.BlockSpec((tm, POOL_WIDTH), lambda i: (i, 0)),
            pl.BlockSpec((POOL_HALO, POOL_WIDTH),
                         lambda i: (jnp.maximum(i * halo_blocks - 1, 0), 0)),
            resident(w_pool),
            pl.BlockSpec((1, POOL_WIDTH), lambda i: (0, 0)),
            resident(wo_att), resident(wo_pool),
            pl.BlockSpec((1, d), lambda i: (0, 0)),
            resident(w1), resident(w3), resident(w2),
        ],
        out_specs=pl.BlockSpec((tm, d), lambda i: (i, 0)),
        out_shape=jax.ShapeDtypeStruct((t, d), F32),
        compiler_params=_params("parallel"),
        name="even_tail",
    )(x2d, att, p_in, p_in, w_pool, pool_scale, wo_att, wo_pool, ffn_gain, w1, w3, w2)


def _gelu_tanh(x):
    c = math.sqrt(2.0 / math.pi)
    return 0.5 * x * (1.0 + jnp.tanh(c * (x + 0.044715 * (x * x * x))))


def _s5_kernel(x_ref, g_ref, win_ref, perm_ref, permt_ref, bblk_ref, cblk_ref,
               are_ref, aim_ref, d_ref, o_ref, zero_ref, xs_ref, st_ref, u_ref, act_ref):
    nb, tc, d = x_ref.shape
    rows = nb * tc
    n_slabs = d // LANES
    sw = 2 * SSM_SLAB_STATE

    @pl.when(pl.program_id(0) == 0)
    def _():
        st_ref[...] = jnp.zeros_like(st_ref)

    pt = perm_ref.shape[0] // nb
    h_tb = []
    for t0 in range(0, tc, pt):
        x = x_ref[:, t0:t0 + pt, :].reshape(nb * pt, d)
        h = _rms(x, g_ref[...]).astype(BF16)
        h_tb.append(jnp.dot(perm_ref[...], h, preferred_element_type=F32).astype(BF16))
    u_ref[...] = jnp.dot(jnp.concatenate(h_tb, axis=0), win_ref[...],
                         preferred_element_type=F32)
    hs = SSM_SLAB_STATE

    def input_matmul(s):
        xs_ref[:, s * sw:(s + 1) * sw] = jnp.dot(
            u_ref[:, s * LANES:(s + 1) * LANES].astype(BF16), bblk_ref[s],
            preferred_element_type=F32)

    def output_matmul(s):
        sl = slice(s * LANES, (s + 1) * LANES)
        y = jnp.dot(xs_ref[:, s * sw:(s + 1) * sw].astype(BF16), cblk_ref[s],
                    preferred_element_type=F32) + d_ref[:, sl] * u_ref[:, sl]
        act_ref[:, sl] = _gelu_tanh(y).astype(BF16)

    ahead = 2
    for s in range(min(ahead, n_slabs)):
        input_matmul(s)
    for s in range(n_slabs):
        lo = s * sw
        if s + ahead < n_slabs:
            input_matmul(s + ahead)
        a_r = are_ref[:, lo:lo + hs]
        a_i = aim_ref[:, lo:lo + hs]
        x_r = st_ref[:, lo:lo + hs]
        x_i = st_ref[:, lo + hs:lo + sw]
        for t in range(tc):
            r0 = t * SUBLANES
            new_r = a_r * x_r - a_i * x_i + xs_ref[r0:r0 + SUBLANES, lo:lo + hs]
            new_i = a_r * x_i + a_i * x_r + xs_ref[r0:r0 + SUBLANES, lo + hs:lo + sw]
            xs_ref[r0:r0 + SUBLANES, lo:lo + hs] = new_r
            xs_ref[r0:r0 + SUBLANES, lo + hs:lo + sw] = new_i
            x_r, x_i = new_r, new_i
        st_ref[:, lo:lo + hs] = x_r
        st_ref[:, lo + hs:lo + sw] = x_i
        if s >= 1:
            output_matmul(s - 1)
    output_matmul(n_slabs - 1)
    for t0 in range(0, tc, pt):
        g_bt = jnp.dot(permt_ref[...], act_ref[t0 * nb:(t0 + pt) * nb, :],
                       preferred_element_type=F32)
        o_ref[:, t0:t0 + pt, :] = g_bt.reshape(nb, pt, d).astype(o_ref.dtype)
    zero_ref[...] = jnp.zeros_like(zero_ref)


def _s5_mixer(x3d, gain, w_in, bblk, cblk, a_re, a_im, d_skip, *, tc, zero_shape):
    nb, seq, d = x3d.shape
    steps = seq // tc
    zero_rows = zero_shape[0] // steps
    assert zero_rows * steps == zero_shape[0] and zero_rows % SEG_ALIGN == 0
    rows = nb * tc
    n_slabs = d // LANES
    sw = 2 * SSM_SLAB_STATE
    pt = min(tc, S5_PERM_STEPS)
    r = np.arange(nb * pt)
    perm = np.zeros((nb * pt, nb * pt), np.float32)
    perm[(r % pt) * nb + r // pt, r] = 1.0
    perm_j = jnp.asarray(perm, BF16)
    permt_j = jnp.asarray(perm.T, BF16)
    kern = _s5_kernel
    full2 = lambda a: pl.BlockSpec(a.shape, lambda i: (0, 0))
    full3 = lambda a: pl.BlockSpec(a.shape, lambda i: (0, 0, 0))
    return pl.pallas_call(
        kern,
        grid=(seq // tc,),
        in_specs=[
            pl.BlockSpec((nb, tc, d), lambda i: (0, i, 0)),
            full2(gain), full2(w_in), full2(perm_j), full2(permt_j),
            full3(bblk), full3(cblk), full2(a_re), full2(a_im), full2(d_skip),
        ],
        out_specs=[pl.BlockSpec((nb, tc, d), lambda i: (0, i, 0)),
                   pl.BlockSpec((zero_rows, zero_shape[1]), lambda i: (i, 0))],
        out_shape=[jax.ShapeDtypeStruct((nb, seq, d), BF16),
                   jax.ShapeDtypeStruct(zero_shape, BF16)],
        scratch_shapes=[
            pltpu.VMEM((rows, n_slabs * sw), F32),
            pltpu.VMEM((SUBLANES, n_slabs * sw), F32),
            pltpu.VMEM((rows, d), F32),
            pltpu.VMEM((rows, d), BF16),
        ],
        compiler_params=_params("arbitrary"),
        name="s5_mixer",
    )(x3d, gain, w_in, perm_j, permt_j, bblk, cblk, a_re, a_im, d_skip)


def _s5_coefficients(a_re, a_im, log_dt, b_re, b_im, c_re, c_im):
    dt = jnp.exp(log_dt.astype(F32))[:, None]
    ar = a_re.astype(F32)
    ai = a_im.astype(F32)
    mag = jnp.exp(ar * dt)
    abar_re = mag * jnp.cos(ai * dt)
    abar_im = mag * jnp.sin(ai * dt)
    den = ar * ar + ai * ai
    nr = abar_re - 1.0
    ni = abar_im
    coef_re = (nr * ar + ni * ai) / den
    coef_im = (ni * ar - nr * ai) / den
    br = b_re.astype(F32)
    bi = b_im.astype(F32)
    bbar_re = coef_re[..., None] * br - coef_im[..., None] * bi
    bbar_im = coef_re[..., None] * bi + coef_im[..., None] * br
    n_groups = ar.shape[0]
    n_slabs = n_groups // SSM_SLAB_GROUPS
    row_group = np.arange(LANES)[:, None] // SSM_GROUP
    col_group = np.arange(SSM_SLAB_STATE)[None, :] // SSM_STATE
    in_mask = jnp.asarray(row_group == col_group, F32)

    def in_block(bb):
        bb = bb.reshape(n_slabs, SSM_SLAB_GROUPS, SSM_STATE, SSM_GROUP)
        bb = bb.transpose(0, 1, 3, 2).reshape(n_slabs, LANES, SSM_STATE)
        return jnp.tile(bb, (1, 1, SSM_SLAB_GROUPS)) * in_mask

    def out_block(cc):
        cc = cc.reshape(n_slabs, SSM_SLAB_GROUPS, SSM_GROUP, SSM_STATE)
        cc = cc.transpose(0, 1, 3, 2).reshape(n_slabs, SSM_SLAB_STATE, SSM_GROUP)
        return jnp.tile(cc, (1, 1, SSM_SLAB_GROUPS)) * in_mask.T

    bblk = jnp.concatenate([in_block(bbar_re), in_block(bbar_im)], axis=2)
    cblk = jnp.concatenate([out_block(c_re.astype(F32)),
                            -out_block(c_im.astype(F32))], axis=1)

    def lanes(a):
        a = a.reshape(n_slabs, 1, SSM_SLAB_STATE)
        a = jnp.concatenate([a, a], axis=2).reshape(1, -1)
        return jnp.broadcast_to(a, (SUBLANES, a.shape[1]))

    return (bblk.astype(BF16), cblk.astype(BF16), lanes(abar_re), lanes(abar_im))


MOE_CHUNK = 512
MOE_FIRST = 208
MOE_REST = MOE_CHUNK + SEG_ALIGN - MOE_FIRST
MOE_PIECES = ((0, MOE_FIRST), (MOE_FIRST, MOE_REST))
MOE_WINDOW = MOE_FIRST + MOE_REST
NOT_ROUTED = -1024.0
MOE_TILE = 512


def _glu_router_kernel(x_ref, act_ref, wa_ref, wb_ref, g_ref, w_ref, b_ref,
                       x3_ref, h_ref, pos_ref, post_ref, cnt_ref):
    tm, d = x_ref.shape
    parts = 2
    half = tm // parts

    def glu(r0):
        act = act_ref[r0:r0 + half, :]
        a = jnp.dot(act, wa_ref[...], preferred_element_type=F32)
        b = jnp.dot(act, wb_ref[...], preferred_element_type=F32)
        x3 = x_ref[r0:r0 + half, :] + a * _sigmoid(b)
        x3_ref[r0:r0 + half, :] = x3
        return x3

    def route(r0, x3):
        h = _rms(x3, g_ref[...])
        h_hi = h.astype(BF16)
        h_ref[r0:r0 + half, 0:d] = h_hi
        h_lo = (h - h_hi.astype(F32)).astype(BF16)
        p_hi = jnp.dot(h_hi, w_ref[...], preferred_element_type=F32)
        p_lo = jnp.dot(h_lo, w_ref[...], preferred_element_type=F32)
        logits = p_hi + pltpu.roll(p_hi, LANES - N_EXPERTS, axis=1) + p_lo + b_ref[...]
        lane = lax.broadcasted_iota(jnp.int32, logits.shape, 1)
        logits = jnp.where(lane < N_EXPERTS, logits, -jnp.inf)
        m1 = jnp.max(logits, axis=1, keepdims=True)
        i1 = jnp.min(jnp.where(logits == m1, lane, LANES), axis=1, keepdims=True)
        rest = jnp.where(lane == i1, -jnp.inf, logits)
        m2 = jnp.max(rest, axis=1, keepdims=True)
        i2 = jnp.min(jnp.where(rest == m2, lane, LANES), axis=1, keepdims=True)
        e2 = jnp.exp(m2 - m1)
        g1 = 1.0 / (1.0 + e2)
        g2 = e2 / (1.0 + e2)
        gate = jnp.where(lane == i1, g1, 0.0) + jnp.where(lane == i2, g2, 0.0)
        g_hi = gate.astype(BF16).astype(F32)
        g_r = gate - g_hi
        g_mid = g_r.astype(BF16).astype(F32)
        packed = jnp.where(lane < N_EXPERTS, g_hi,
                           jnp.where(lane < 2 * N_EXPERTS,
                                     pltpu.roll(g_mid, N_EXPERTS, axis=1),
                                     pltpu.roll(g_r - g_mid, 2 * N_EXPERTS, axis=1)))
        h_ref[r0:r0 + half, d:d + LANES] = packed.astype(BF16)
        return jnp.where((lane == i1) | (lane == i2), 1.0, 0.0)

    members = []
    x3 = glu(0)
    for p in range(parts):
        nxt = glu((p + 1) * half) if p + 1 < parts else None
        members.append(route(p * half, x3))
        x3 = nxt
    member = jnp.concatenate(members, axis=0)
    row = lax.broadcasted_iota(jnp.int32, member.shape, 0)
    c = member
    sh = 1
    while sh < tm:
        c = c + jnp.where(row >= sh, pltpu.roll(c, sh, axis=0), 0.0)
        sh *= 2
    pos = jnp.where(member > 0.0, c - member, NOT_ROUTED)
    pos_ref[...] = pos
    post_ref[0] = pos.T[0:SUBLANES, :]
    cnt_ref[0] = jnp.broadcast_to(c[tm - 1:tm, :], (SUBLANES, LANES))


def _glu_router(x2d, act2d, wa, wb, gain, w_pad, b_pad):
    t, d = x2d.shape
    tm = MOE_CHUNK
    n_chunks = t // tm
    return pl.pallas_call(
        _glu_router_kernel,
        grid=(n_chunks,),
        in_specs=[
            pl.BlockSpec((tm, d), lambda i: (i, 0)),
            pl.BlockSpec((tm, d), lambda i: (i, 0)),
            pl.BlockSpec(wa.shape, lambda i: (0, 0)),
            pl.BlockSpec(wb.shape, lambda i: (0, 0)),
            pl.BlockSpec((1, d), lambda i: (0, 0)),
            pl.BlockSpec((d, LANES), lambda i: (0, 0)),
            pl.BlockSpec((1, LANES), lambda i: (0, 0)),
        ],
        out_specs=[
            pl.BlockSpec((tm, d), lambda i: (i, 0)),
            pl.BlockSpec((tm, d + LANES), lambda i: (i, 0)),
            pl.BlockSpec((tm, LANES), lambda i: (i, 0)),
            pl.BlockSpec((1, SUBLANES, tm), lambda i: (i, 0, 0)),
            pl.BlockSpec((1, SUBLANES, LANES), lambda i: (i, 0, 0)),
        ],
        out_shape=[
            jax.ShapeDtypeStruct((t, d), F32),
            jax.ShapeDtypeStruct((t, d + LANES), BF16),
            jax.ShapeDtypeStruct((t, LANES), F32),
            jax.ShapeDtypeStruct((n_chunks, SUBLANES, tm), F32),
            jax.ShapeDtypeStruct((n_chunks, SUBLANES, LANES), F32),
        ],
        compiler_params=_params("parallel"),
        name="glu_router",
    )(x2d, act2d, wa, wb, gain, w_pad, b_pad)


MOE_SPARE = max(MOE_FIRST, MOE_REST)


def _moe_tiles(n_tokens, row_multiple):
    max_rows = (TOP_K * n_tokens + N_EXPERTS * (SEG_ALIGN - 1)
                + N_EXPERTS * (MOE_SPARE + MOE_TILE))
    unit = math.lcm(MOE_TILE, row_multiple)
    return -(-max_rows // unit) * (unit // MOE_TILE)


def _moe_layout(cnt):
    before = jnp.cumsum(cnt, axis=0) - cnt
    shift = before % SEG_ALIGN
    used = (jnp.sum(cnt, axis=0) + SEG_ALIGN - 1) // SEG_ALIGN * SEG_ALIGN
    padded = (used + MOE_SPARE + MOE_TILE - 1) // MOE_TILE * MOE_TILE
    ends = jnp.cumsum(padded)
    start = ends - padded
    off = start[None, :] + before - shift
    flat = lambda a: a.reshape(-1).astype(jnp.int32)
    return (flat(off), flat(shift + cnt), flat(shift),
            ends.astype(jnp.int32), (start + used).astype(jnp.int32))


def _tile_expert(i, ends_ref):
    first_row = i * MOE_TILE
    e = 0
    for k in range(N_EXPERTS - 1):
        e = e + (first_row >= ends_ref[k]).astype(jnp.int32)
    return e


def _segment_copies(hbm_ref, buf_ref, sem_ref, off_ref, chunk, to_hbm, slot=()):
    copies = []
    for e in range(N_EXPERTS):
        off = pl.multiple_of(off_ref[chunk * N_EXPERTS + e], SEG_ALIGN)
        for piece, (first, rows) in enumerate(MOE_PIECES):
            hbm = hbm_ref.at[pl.ds(off + first, rows)]
            buf = buf_ref.at[(*slot, e, pl.ds(first, rows))]
            src, dst = (buf, hbm) if to_hbm else (hbm, buf)
            copies.append(pltpu.make_async_copy(src, dst, sem_ref.at[(*slot, e, piece)]))
    return copies


def _onehot(first, shape, axis, target):
    rank = first + lax.broadcasted_iota(jnp.int32, shape, axis)
    return jnp.where(rank.astype(F32) == target, 1.0, 0.0).astype(BF16)


def _gather_kernel(off_ref, need_ref, shift_ref, h_ref, post_ref, zeros_hbm, o_hbm,
                   stage_ref, carry_ref, sem_ref):
    del zeros_hbm
    c = pl.program_id(0)
    slot = c % 2
    h = h_ref[...]

    @pl.when(c == 0)
    def _():
        carry_ref[...] = jnp.zeros_like(carry_ref)

    def long_segment(chunk, e):
        return need_ref[chunk * N_EXPERTS + e] >= MOE_FIRST

    def piece_onehot(e, piece):
        first, rows = MOE_PIECES[piece]
        return _onehot(first - shift_ref[c * N_EXPERTS + e], (rows, MOE_CHUNK), 0,
                       post_ref[0, e:e + 1, :])

    def fill_piece(e, piece, onehot=None):
        first, rows = MOE_PIECES[piece]
        onehot = piece_onehot(e, piece) if onehot is None else onehot
        stage_ref[slot, e, first:first + rows] = jnp.dot(
            onehot, h, preferred_element_type=F32).astype(BF16)
        if piece == 0:
            stage_ref[slot, e, 0:SEG_ALIGN] += carry_ref[e]

    onehot = piece_onehot(0, 0)
    for e in range(N_EXPERTS):
        nxt = piece_onehot(e + 1, 0) if e + 1 < N_EXPERTS else None
        fill_piece(e, 0, onehot)
        onehot = nxt
    for e in range(N_EXPERTS):
        @pl.when(long_segment(c, e))
        def _(e=e):
            fill_piece(e, 1)

    for e in range(N_EXPERTS):
        last_group = need_ref[c * N_EXPERTS + e] // SEG_ALIGN * SEG_ALIGN
        carry_ref[e] = stage_ref[slot, e, pl.ds(pl.multiple_of(last_group, SEG_ALIGN),
                                                SEG_ALIGN)]

    def each_copy(chunk, sl, action):
        copies = _segment_copies(o_hbm, stage_ref, sem_ref, off_ref, chunk,
                                 to_hbm=True, slot=(sl,))
        for e in range(N_EXPERTS):
            action(copies[2 * e])

            @pl.when(long_segment(chunk, e))
            def _(e=e):
                action(copies[2 * e + 1])

    @pl.when(c > 0)
    def _():
        each_copy(c - 1, 1 - slot, lambda cp: cp.wait())

    each_copy(c, slot, lambda cp: cp.start())

    @pl.when(c == pl.num_programs(0) - 1)
    def _():
        each_copy(c, slot, lambda cp: cp.wait())


def _gather(h2d, post, off, need, shift, zero_buf):
    t, d = h2d.shape
    n_chunks = t // MOE_CHUNK
    grid_spec = pltpu.PrefetchScalarGridSpec(
        num_scalar_prefetch=3,
        grid=(n_chunks,),
        in_specs=[
            pl.BlockSpec((MOE_CHUNK, d), lambda i, *_: (i, 0)),
            pl.BlockSpec((1, SUBLANES, MOE_CHUNK), lambda i, *_: (i, 0, 0)),
            pl.BlockSpec(memory_space=pl.ANY),
        ],
        out_specs=pl.BlockSpec(memory_space=pl.ANY),
        scratch_shapes=[
            pltpu.VMEM((2, N_EXPERTS, MOE_WINDOW, d), BF16),
            pltpu.VMEM((N_EXPERTS, SEG_ALIGN, d), BF16),
            pltpu.SemaphoreType.DMA((2, N_EXPERTS, 2)),
        ],
    )
    return pl.pallas_call(
        _gather_kernel,
        grid_spec=grid_spec,
        out_shape=jax.ShapeDtypeStruct(zero_buf.shape, BF16),
        input_output_aliases={5: 0},
        compiler_params=_params("arbitrary"),
        name="moe_gather",
    )(off, need, shift, h2d, post, zero_buf)


def _expert_kernel(ends_ref, used_end_ref, x_ref, w1_ref, w3_ref, w2_ref, o_ref):
    i = pl.program_id(0)
    expert = _tile_expert(i, ends_ref)
    rows = jnp.clip(used_end_ref[expert] - i * MOE_TILE, 0, MOE_TILE)
    d = o_ref.shape[1]

    def run(n):
        row = lax.broadcasted_iota(jnp.int32, (n, x_ref.shape[1]), 0)
        x = jnp.where(row < rows, x_ref[0:n, :], jnp.zeros((), x_ref.dtype))
        lane = lax.broadcasted_iota(jnp.int32, (n, LANES), 1)
        mine = (lane < 3 * N_EXPERTS) & (jnp.bitwise_and(lane, N_EXPERTS - 1) == expert)
        gate = jnp.sum(jnp.where(mine, x[:, d:d + LANES].astype(F32), 0.0),
                       axis=1, keepdims=True)
        y = _swiglu(x[:, 0:d], w1_ref, w3_ref, w2_ref)
        o_ref[0:n, :] = (gate * y).astype(o_ref.dtype)

    quarter = MOE_TILE // 4
    for n in range(quarter, MOE_TILE + 1, quarter):
        @pl.when((rows > n - quarter) & (rows <= n))
        def _(n=n):
            run(n)
            if n < MOE_TILE:
                o_ref[n:MOE_TILE, :] = jnp.zeros((MOE_TILE - n, d), o_ref.dtype)

    @pl.when(rows == 0)
    def _():
        o_ref[...] = jnp.zeros_like(o_ref)


def _experts(x_sorted, w1, w3, w2, ends, used_end):
    assert N_EXPERTS & (N_EXPERTS - 1) == 0
    n_rows = x_sorted.shape[0]
    f = w1.shape[1]
    d = w2.shape[1]
    grid_spec = pltpu.PrefetchScalarGridSpec(
        num_scalar_prefetch=2,
        grid=(n_rows // MOE_TILE,),
        in_specs=[
            pl.BlockSpec((MOE_TILE, d + LANES), lambda i, ends, ue: (i, 0)),
            pl.BlockSpec((d, f), lambda i, ends, ue: (_tile_expert(i, ends), 0)),
            pl.BlockSpec((d, f), lambda i, ends, ue: (_tile_expert(i, ends), 0)),
            pl.BlockSpec((f, d), lambda i, ends, ue: (_tile_expert(i, ends), 0)),
        ],
        out_specs=pl.BlockSpec((MOE_TILE, d), lambda i, ends, ue: (i, 0)),
    )
    return pl.pallas_call(
        _expert_kernel,
        grid_spec=grid_spec,
        out_shape=jax.ShapeDtypeStruct((n_rows, d), BF16),
        compiler_params=_params("arbitrary"),
        name="moe_experts",
    )(ends, used_end, x_sorted, w1, w3, w2)


def _combine_kernel(off_ref, need_ref, shift_ref, x_ref, pos_ref, fg_ref, y_hbm,
                    o_ref, ybuf_ref, acc_ref, sem_ref):
    c = pl.program_id(0)
    n_chunks = pl.num_programs(0)
    slot = c % 2

    def fetch(chunk, sl, start):
        copies = _segment_copies(y_hbm, ybuf_ref, sem_ref, off_ref, chunk,
                                 to_hbm=False, slot=(sl,))
        for e in range(N_EXPERTS):
            first, second = copies[2 * e], copies[2 * e + 1]
            if start:
                first.start()
            else:
                first.wait()

            @pl.when(need_ref[chunk * N_EXPERTS + e] >= MOE_FIRST)
            def _(second=second):
                if start:
                    second.start()
                else:
                    second.wait()

    @pl.when(c == 0)
    def _():
        fetch(c, slot, True)

    @pl.when(c + 1 < n_chunks)
    def _():
        fetch(c + 1, 1 - slot, True)

    fetch(c, slot, False)

    def piece_onehot(e, piece):
        first, rows = MOE_PIECES[piece]
        return _onehot(first - shift_ref[c * N_EXPERTS + e], (MOE_CHUNK, rows), 1,
                       pos_ref[:, e:e + 1])

    def piece_rows(e, piece, onehot=None):
        first, rows = MOE_PIECES[piece]
        onehot = piece_onehot(e, piece) if onehot is None else onehot
        return jnp.dot(onehot, ybuf_ref[slot, e, first:first + rows],
                       preferred_element_type=F32)

    acc = x_ref[...]
    onehot = piece_onehot(0, 0)
    for e in range(N_EXPERTS):
        nxt = piece_onehot(e + 1, 0) if e + 1 < N_EXPERTS else None
        acc = acc + piece_rows(e, 0, onehot)
        onehot = nxt
    acc_ref[...] = acc
    for e in range(N_EXPERTS):
        @pl.when(need_ref[c * N_EXPERTS + e] >= MOE_FIRST)
        def _(e=e):
            acc_ref[...] += piece_rows(e, 1)

    o_ref[...] = _rms(acc_ref[...], fg_ref[...])


def _combine(x2d, pos, final_gain, y_sorted, off, need, shift):
    t, d = x2d.shape
    n_chunks = t // MOE_CHUNK
    grid_spec = pltpu.PrefetchScalarGridSpec(
        num_scalar_prefetch=3,
        grid=(n_chunks,),
        in_specs=[
            pl.BlockSpec((MOE_CHUNK, d), lambda i, *_: (i, 0)),
            pl.BlockSpec((MOE_CHUNK, LANES), lambda i, *_: (i, 0)),
            pl.BlockSpec((1, d), lambda i, *_: (0, 0)),
            pl.BlockSpec(memory_space=pl.ANY),
        ],
        out_specs=pl.BlockSpec((MOE_CHUNK, d), lambda i, *_: (i, 0)),
        scratch_shapes=[
            pltpu.VMEM((2, N_EXPERTS, MOE_WINDOW, d), BF16),
            pltpu.VMEM((MOE_CHUNK, d), F32),
            pltpu.SemaphoreType.DMA((2, N_EXPERTS, 2)),
        ],
    )
    return pl.pallas_call(
        _combine_kernel,
        grid_spec=grid_spec,
        out_shape=jax.ShapeDtypeStruct((t, d), F32),
        compiler_params=_params("arbitrary"),
        name="moe_combine",
    )(off, need, shift, x2d, pos, final_gain, y_sorted)


def _glu_moe(x2d, act2d, wa, wb, gain, router_w, router_b, w1, w3, w2, final_gain,
             zero_buf):
    t, d = x2d.shape
    rw_hi = router_w.astype(BF16)
    rw_lo = (router_w.astype(F32) - rw_hi.astype(F32)).astype(BF16)
    rw = jnp.pad(jnp.concatenate([rw_hi, rw_lo], axis=1),
                 ((0, 0), (0, LANES - 2 * N_EXPERTS)))
    rb = jnp.pad(router_b.astype(F32), (0, LANES - N_EXPERTS)).reshape(1, LANES)
    x3, h, pos, post, cnt = _glu_router(x2d, act2d, wa, wb, gain, rw, rb)
    cnt = cnt[:, 0, :N_EXPERTS].astype(jnp.int32)
    off, need, shift, ends, used_end = _moe_layout(cnt)
    h_sorted = _gather(h, post, off, need, shift, zero_buf)
    y_sorted = _experts(h_sorted, w1, w3, w2, ends, used_end)
    return _combine(x3, pos, final_gain, y_sorted, off, need, shift)


def _even_weights(w_in, b_forget):
    n = w_in.shape[1]
    w_all = jnp.pad(w_in.astype(BF16), ((0, 0), (0, -n % LANES)))
    bias = jnp.pad(b_forget.astype(F32), (0, LANES - FOX_HEADS)).reshape(1, LANES)
    qw = FOX_HEADS * LANES
    place = np.zeros((LANES, 2 * qw), np.float32)
    for hh in range(FOX_HEADS):
        for piece in range(3):
            place[piece * FOX_HEADS + hh, hh * LANES + BIAS_LANE + piece] = 1.0
            place[piece * FOX_HEADS + hh, qw + hh * LANES + ONES_LANE + piece] = -1.0
    return w_all, bias, jnp.asarray(place, BF16)


def kernel(x, even_mix_norm, even_w_in, even_b_forget, even_w_pool, even_pool_scale, even_w_out, even_ffn_norm, even_ffn_w1, even_ffn_w3, even_ffn_w2, odd_mix_norm, odd_w_in, ssm_a_re, ssm_a_im, ssm_log_dt, ssm_b_re, ssm_b_im, ssm_c_re, ssm_c_im, ssm_d, odd_w_glu_a, odd_w_glu_b, odd_moe_norm, router_w, router_b, expert_w1, expert_w3, expert_w2, final_norm):
    b, s, d = x.shape
    t = b * s
    assert b == SUBLANES, "the S5 recurrence keeps one batch row per sublane"
    x2d = x.reshape(t, d)
    row = lambda v: v.reshape(1, -1).astype(F32)

    w_all, bias, place = _even_weights(even_w_in[0], even_b_forget[0])
    (q_aug, k_aug, v, p_in), casted = _even_inproj(
        x2d, row(even_mix_norm[0]), w_all, bias, place,
        [even_ffn_w1[0], even_ffn_w3[0], even_ffn_w2[0], even_w_out[0],
         even_w_pool[0].reshape(-1, POOL_GROUP_DIM),
         odd_w_in[0], odd_w_glu_a[0], odd_w_glu_b[0]],
        seq=s, tm=INPROJ_ROWS)
    ffn_w1, ffn_w3, ffn_w2, w_out, w_pool, s5_w_in, glu_a, glu_b = casted
    n_e, _, f = expert_w1[0].shape
    att, (ew1, ew3, ew2) = _attention(
        q_aug, k_aug, v,
        [expert_w1[0].reshape(n_e * d, f), expert_w3[0].reshape(n_e * d, f),
         expert_w2[0].reshape(n_e * f, d)],
        batch=b, seq=s, tq=ATTN_QUERY_TILE)
    x2 = _even_tail(x2d, att, p_in, w_pool.reshape(even_w_pool[0].shape),
                    row(even_pool_scale[0]), w_out, row(even_ffn_norm[0]),
                    ffn_w1, ffn_w3, ffn_w2, seq=s, tm=EVEN_TAIL_ROWS)

    bblk, cblk, a_re, a_im = _s5_coefficients(
        ssm_a_re[0], ssm_a_im[0], ssm_log_dt[0], ssm_b_re[0], ssm_b_im[0],
        ssm_c_re[0], ssm_c_im[0])
    moe_rows = _moe_tiles(t, (s // S5_STEP) * SEG_ALIGN) * MOE_TILE
    g, zero_buf = _s5_mixer(x2.reshape(b, s, d), row(odd_mix_norm[0]),
                            s5_w_in, bblk, cblk, a_re, a_im,
                            row(ssm_d[0]), tc=S5_STEP,
                            zero_shape=(moe_rows, d + LANES))
    out = _glu_moe(x2, g.reshape(t, d), glu_a, glu_b, row(odd_moe_norm[0]),
                   router_w[0], router_b[0], ew1, ew3, ew2,
                   row(final_norm), zero_buf)
    return out.reshape(b, s, d)
```

```python
import functools
import math

import numpy as np
import jax
import jax.numpy as jnp
from jax import lax
from jax.experimental import pallas as pl
from jax.experimental.pallas import tpu as pltpu

F32 = jnp.float32
BF16 = jnp.bfloat16

EPS = 1e-6
NEG_INF = -1e30
LANES = 128
SUBLANES = 8
SEG_ALIGN = 16
VMEM_LIMIT = 56 * 1024 * 1024

FOX_HEADS = 8
FOX_HEAD_DIM = 64
FOX_WIDTH = FOX_HEADS * FOX_HEAD_DIM
POOL_WINDOWS = (2, 4, 8, 16)
POOL_GROUP_DIM = 128
POOL_WIDTH = len(POOL_WINDOWS) * POOL_GROUP_DIM
POOL_HALO = 16
SSM_GROUP = 16
SSM_STATE = 64
SSM_SLAB_GROUPS = LANES // SSM_GROUP
SSM_SLAB_STATE = SSM_SLAB_GROUPS * SSM_STATE

INPROJ_ROWS = 1024
INPROJ_PART_ROWS = 256
ATTN_QUERY_TILE = 256
EVEN_TAIL_ROWS = 512
S5_STEP = 64
S5_PERM_STEPS = 32
N_EXPERTS = 8
TOP_K = 2

BIAS_LANE = FOX_HEAD_DIM
ONES_LANE = FOX_HEAD_DIM + 3
DENOM_LANE = FOX_HEAD_DIM


def _params(*sem):
    return pltpu.CompilerParams(dimension_semantics=sem,
                                vmem_limit_bytes=VMEM_LIMIT)


def _rms(x, g):
    ms = jnp.mean(x * x, axis=-1, keepdims=True)
    return x * lax.rsqrt(ms + EPS) * g


def _sigmoid(x):
    return 1.0 / (1.0 + jnp.exp(-x))


def _lane_range_ones(lo, hi):
    lane = lax.broadcasted_iota(jnp.int32, (1, LANES), 1)
    return jnp.where((lane >= lo) & (lane < hi), 1.0, 0.0).astype(F32)


def _cast_specs(weights, steps, step_index):
    specs = []
    for w in weights:
        rows = -(-w.shape[0] // (steps * SEG_ALIGN)) * SEG_ALIGN
        last = -(-w.shape[0] // rows) - 1
        specs.append(pl.BlockSpec(
            (rows, w.shape[1]),
            lambda *idx, last=last: (jnp.minimum(step_index(*idx), last), 0)))
    return specs


def _cast_slabs(src_refs, dst_refs):
    for src_ref, dst_ref in zip(src_refs, dst_refs):
        dst_ref[...] = src_ref[...].astype(dst_ref.dtype)


def _even_inproj_kernel(x_ref, g_ref, w_ref, bias_ref, place_ref, *rest,
                        tiles_per_seq, n_cast):
    cast_in, rest = rest[:n_cast], rest[n_cast:]
    q_ref, k_ref, v_ref, p_ref = rest[:4]
    cast_out, carry_ref = rest[4:4 + n_cast], rest[4 + n_cast]
    _cast_slabs(cast_in, cast_out)
    i = pl.program_id(0)
    tm = x_ref.shape[0]
    rows = INPROJ_PART_ROWS
    parts = tm // rows
    fw = FOX_WIDTH
    qw = FOX_HEADS * LANES

    @pl.when(i % tiles_per_seq == 0)
    def _():
        carry_ref[...] = jnp.zeros_like(carry_ref)

    def project(r0):
        h = _rms(x_ref[r0:r0 + rows, :], g_ref[...]).astype(BF16)
        return jnp.dot(h, w_ref[...], preferred_element_type=F32)

    def finish(r0, z, carry):
        out = slice(r0, r0 + rows)
        p_ref[out, :] = z[:, 3 * fw + FOX_HEADS:3 * fw + FOX_HEADS + POOL_WIDTH]
        fg = z[:, 3 * fw:3 * fw + LANES] + bias_ref[...]
        lf = jnp.minimum(fg, 0.0) - jnp.log1p(jnp.exp(-jnp.abs(fg)))
        row = lax.broadcasted_iota(jnp.int32, lf.shape, 0)
        c = lf
        sh = 1
        while sh < rows:
            c = c + jnp.where(row >= sh, pltpu.roll(c, sh, axis=0), 0.0)
            sh *= 2
        c = c + carry
        hi = c.astype(BF16).astype(F32)
        r1 = c - hi
        mid = r1.astype(BF16).astype(F32)
        lo = r1 - mid
        lane = lax.broadcasted_iota(jnp.int32, (rows, LANES), 1)
        packed = jnp.where(lane < FOX_HEADS, hi,
                           jnp.where(lane < 2 * FOX_HEADS,
                                     pltpu.roll(mid, FOX_HEADS, axis=1),
                                     pltpu.roll(lo, 2 * FOX_HEADS, axis=1)))
        placed = jnp.dot(packed.astype(BF16), place_ref[...],
                         preferred_element_type=F32)
        low = lane < FOX_HEAD_DIM
        ones_q = _lane_range_ones(ONES_LANE, ONES_LANE + 3)
        ones_k = _lane_range_ones(BIAS_LANE, BIAS_LANE + 3)
        ones_v = _lane_range_ones(DENOM_LANE, DENOM_LANE + 1)

        def head_lanes(base, hh):
            pair = z[:, base + (hh // 2) * LANES:base + (hh // 2 + 1) * LANES]
            return pltpu.roll(pair, FOX_HEAD_DIM, axis=1) if hh % 2 else pair

        q_scale = FOX_HEAD_DIM ** -0.5
        for hh in range(FOX_HEADS):
            sl = slice(hh * LANES, (hh + 1) * LANES)
            q_ref[out, sl] = jnp.where(low, head_lanes(0, hh) * q_scale,
                                       placed[:, sl] + ones_q).astype(BF16)
            k_ref[out, sl] = jnp.where(
                low, head_lanes(fw, hh),
                placed[:, qw + hh * LANES:qw + (hh + 1) * LANES] + ones_k).astype(BF16)
            v_ref[out, sl] = jnp.where(low, head_lanes(2 * fw, hh), ones_v).astype(BF16)
        return c[rows - 1:rows, :]

    carry = carry_ref[0:1, :]
    z = project(0)
    for part in range(parts):
        nxt = project((part + 1) * rows) if part + 1 < parts else None
        carry = finish(part * rows, z, carry)
        z = nxt
    carry_ref[...] = jnp.broadcast_to(carry, carry_ref.shape)


def _even_inproj(x2d, gain, w_all, bias, place, cast_weights, *, seq, tm):
    t, d = x2d.shape
    n = w_all.shape[1]
    qw = FOX_HEADS * LANES
    cast_specs = _cast_specs(cast_weights, t // tm, lambda i: i)
    kern = functools.partial(_even_inproj_kernel, tiles_per_seq=seq // tm,
                             n_cast=len(cast_weights))
    outs = pl.pallas_call(
        kern,
        grid=(t // tm,),
        in_specs=[
            pl.BlockSpec((tm, d), lambda i: (i, 0)),
            pl.BlockSpec((1, d), lambda i: (0, 0)),
            pl.BlockSpec((d, n), lambda i: (0, 0)),
            pl.BlockSpec((1, LANES), lambda i: (0, 0)),
            pl.BlockSpec(place.shape, lambda i: (0, 0)),
            *cast_specs,
        ],
        out_specs=[
            pl.BlockSpec((tm, qw), lambda i: (i, 0)),
            pl.BlockSpec((tm, qw), lambda i: (i, 0)),
            pl.BlockSpec((tm, qw), lambda i: (i, 0)),
            pl.BlockSpec((tm, POOL_WIDTH), lambda i: (i, 0)),
            *cast_specs,
        ],
        out_shape=[
            jax.ShapeDtypeStruct((t, qw), BF16),
            jax.ShapeDtypeStruct((t, qw), BF16),
            jax.ShapeDtypeStruct((t, qw), BF16),
            jax.ShapeDtypeStruct((t, POOL_WIDTH), F32),
        ] + [jax.ShapeDtypeStruct(w.shape, BF16) for w in cast_weights],
        scratch_shapes=[pltpu.VMEM((SUBLANES, LANES), F32)],
        compiler_params=_params("arbitrary"),
        name="even_inproj",
    )(x2d, gain, w_all, bias, place, *cast_weights)
    return outs[:4], outs[4:]


def _dot_nt(a, b):
    return lax.dot_general(a, b, (((1,), (1,)), ((), ())),
                           preferred_element_type=F32)


def _attn_kernel(q_ref, k_ref, v_ref, *rest, tq, n_cast):
    cast_in, o_ref, cast_out = rest[:n_cast], rest[n_cast], rest[n_cast + 1:]
    seq = q_ref.shape[0]
    row = lax.broadcasted_iota(jnp.int32, (tq, tq), 0)
    col = lax.broadcasted_iota(jnp.int32, (tq, tq), 1)
    causal = col <= row
    lane = lax.broadcasted_iota(jnp.int32, (tq, LANES), 1)
    work = [(qi, hh) for qi in range(seq // tq) for hh in range(2)]

    def scores(qi, hh):
        r0 = qi * tq
        sl = slice(hh * LANES, (hh + 1) * LANES)
        q = q_ref[r0:r0 + tq, sl]
        s = _dot_nt(q, k_ref[0:r0 + tq, sl])
        s_diag = jnp.where(causal, s[:, r0:r0 + tq], NEG_INF)
        return s_diag if qi == 0 else jnp.concatenate([s[:, 0:r0], s_diag], axis=1)

    ahead = 2
    queue = [scores(*item) for item in work[:ahead]]
    first_head = None
    for n, (qi, hh) in enumerate(work):
        r0 = qi * tq
        sl = slice(hh * LANES, (hh + 1) * LANES)
        s = queue.pop(0)
        if n + ahead < len(work):
            queue.append(scores(*work[n + ahead]))
        m = jnp.max(s, axis=1, keepdims=True)
        acc = jnp.dot(jnp.exp(s - m).astype(BF16), v_ref[0:r0 + tq, sl],
                      preferred_element_type=F32)
        out = acc / acc[:, DENOM_LANE:DENOM_LANE + 1]
        if hh == 0:
            first_head = out
        else:
            o_ref[r0:r0 + tq, :] = jnp.where(
                lane < FOX_HEAD_DIM, first_head,
                pltpu.roll(out, FOX_HEAD_DIM, axis=1)).astype(o_ref.dtype)
    _cast_slabs(cast_in, cast_out)


def _attention(q_aug, k_aug, v_aug, cast_weights, *, batch, seq, tq):
    t = q_aug.shape[0]
    pairs = FOX_HEADS // 2
    steps = batch * pairs
    spec = pl.BlockSpec((seq, 2 * LANES), lambda b, hp: (b, hp))
    out_spec = pl.BlockSpec((seq, LANES), lambda b, hp: (b, hp))
    cast_specs = _cast_specs(cast_weights, steps, lambda b, hp: b * pairs + hp)
    outs = pl.pallas_call(
        functools.partial(_attn_kernel, tq=tq, n_cast=len(cast_weights)),
        grid=(batch, pairs),
        in_specs=[spec, spec, spec, *cast_specs],
        out_specs=[out_spec, *cast_specs],
        out_shape=[jax.ShapeDtypeStruct((t, FOX_WIDTH), BF16)]
        + [jax.ShapeDtypeStruct(w.shape, BF16) for w in cast_weights],
        compiler_params=_params("parallel", "parallel"),
        name="fox_attention",
    )(q_aug, k_aug, v_aug, *cast_weights)
    return outs[0], outs[1:]


MXU_TILE = 256


def _swiglu_partial(h, w1, w3, w2):
    a = jnp.dot(h, w1, preferred_element_type=F32)
    b = jnp.dot(h, w3, preferred_element_type=F32)
    act = a * _sigmoid(a) * b
    return jnp.dot(act.astype(BF16), w2, preferred_element_type=F32)


def _ff_splits(f):
    cut = -(-(f // MXU_TILE) // 2) * MXU_TILE
    return ((0, cut), (cut, f)) if 0 < cut < f else ((0, f),)


def _swiglu(h, w1_ref, w3_ref, w2_ref):
    y = None
    for lo, hi in _ff_splits(w1_ref.shape[1]):
        part = _swiglu_partial(h, w1_ref[:, lo:hi], w3_ref[:, lo:hi], w2_ref[lo:hi, :])
        y = part if y is None else y + part
    return y


def _even_tail_kernel(x_ref, att_ref, p_ref, halo_ref, wpool_ref, scale_ref,
                      wo_att_ref, wo_pool_ref, g_ref, w1_ref, w3_ref, w2_ref, o_ref,
                      *, tiles_per_seq):
    i = pl.program_id(0)
    tm = x_ref.shape[0]
    tile_in_seq = i % tiles_per_seq
    p = p_ref[...]
    halo = jnp.where(tile_in_seq == 0, 0.0, halo_ref[...])
    ext = jnp.concatenate([halo, p], axis=0)
    pos = tile_in_seq * tm + lax.broadcasted_iota(jnp.int32, (tm, 1), 0)
    count = (pos + 1).astype(F32)
    mixed = []
    for gi, w in enumerate(POOL_WINDOWS):
        sl = slice(gi * POOL_GROUP_DIM, (gi + 1) * POOL_GROUP_DIM)
        acc = ext[:, sl]
        sh = 1
        while sh < w:
            acc = acc + pltpu.roll(acc, sh, axis=0)
            sh *= 2
        mean = acc[POOL_HALO:, :] / jnp.minimum(count, float(w))
        pooled = (mean - p[:, sl]).astype(BF16)
        mixed.append(jnp.dot(pooled, wpool_ref[gi], preferred_element_type=F32))
    pool = (jnp.concatenate(mixed, axis=1) * scale_ref[...]).astype(BF16)
    y = jnp.dot(att_ref[...], wo_att_ref[...], preferred_element_type=F32)
    y = y + jnp.dot(pool, wo_pool_ref[...], preferred_element_type=F32)
    x1 = x_ref[...] + y
    h = _rms(x1, g_ref[...]).astype(BF16)
    o_ref[...] = x1 + _swiglu(h, w1_ref, w3_ref, w2_ref)


def _even_tail(x2d, att, p_in, w_pool, pool_scale, w_out, ffn_gain, w1, w3, w2,
               *, seq, tm):
    t, d = x2d.shape
    resident = lambda w: pl.BlockSpec(w.shape, lambda i: (0,) * w.ndim,
                                      pipeline_mode=pl.Buffered(1))
    wo_att = w_out[:FOX_WIDTH]
    wo_pool = w_out[FOX_WIDTH:]
    halo_blocks = tm // POOL_HALO
    kern = functools.partial(_even_tail_kernel, tiles_per_seq=seq // tm)
    return pl.pallas_call(
        kern,
        grid=(t // tm,),
        in_specs=[
            pl.BlockSpec((tm, d), lambda i: (i, 0)),
            pl.BlockSpec((tm, FOX_WIDTH), lambda i: (i, 0)),
            pl.BlockSpec((tm, POOL_WIDTH), lambda i: (i, 0)),
            pl.BlockSpec((POOL_HALO, POOL_WIDTH),
                         lambda i: (jnp.maximum(i * halo_blocks - 1, 0), 0)),
            resident(w_pool),
            pl.BlockSpec((1, POOL_WIDTH), lambda i: (0, 0)),
            resident(wo_att), resident(wo_pool),
            pl.BlockSpec((1, d), lambda i: (0, 0)),
            resident(w1), resident(w3), resident(w2),
        ],
        out_specs=pl.BlockSpec((tm, d), lambda i: (i, 0)),
        out_shape=jax.ShapeDtypeStruct((t, d), F32),
        compiler_params=_params("parallel"),
        name="even_tail",
    )(x2d, att, p_in, p_in, w_pool, pool_scale, wo_att, wo_pool, ffn_gain, w1, w3, w2)


def _gelu_tanh(x):
    c = math.sqrt(2.0 / math.pi)
    return 0.5 * x * (1.0 + jnp.tanh(c * (x + 0.044715 * (x * x * x))))


def _s5_kernel(x_ref, g_ref, win_ref, perm_ref, permt_ref, bblk_ref, cblk_ref,
               are_ref, aim_ref, d_ref, o_ref, zero_ref, xs_ref, st_ref, u_ref, act_ref):
    nb, tc, d = x_ref.shape
    rows = nb * tc
    n_slabs = d // LANES
    sw = 2 * SSM_SLAB_STATE

    @pl.when(pl.program_id(0) == 0)
    def _():
        st_ref[...] = jnp.zeros_like(st_ref)

    pt = perm_ref.shape[0] // nb
    h_tb = []
    for t0 in range(0, tc, pt):
        x = x_ref[:, t0:t0 + pt, :].reshape(nb * pt, d)
        h = _rms(x, g_ref[...]).astype(BF16)
        h_tb.append(jnp.dot(perm_ref[...], h, preferred_element_type=F32).astype(BF16))
    u_ref[...] = jnp.dot(jnp.concatenate(h_tb, axis=0), win_ref[...],
                         preferred_element_type=F32)
    hs = SSM_SLAB_STATE

    def input_matmul(s):
        xs_ref[:, s * sw:(s + 1) * sw] = jnp.dot(
            u_ref[:, s * LANES:(s + 1) * LANES].astype(BF16), bblk_ref[s],
            preferred_element_type=F32)

    def output_matmul(s):
        sl = slice(s * LANES, (s + 1) * LANES)
        y = jnp.dot(xs_ref[:, s * sw:(s + 1) * sw].astype(BF16), cblk_ref[s],
                    preferred_element_type=F32) + d_ref[:, sl] * u_ref[:, sl]
        act_ref[:, sl] = _gelu_tanh(y).astype(BF16)

    ahead = 2
    for s in range(min(ahead, n_slabs)):
        input_matmul(s)
    for s in range(n_slabs):
        lo = s * sw
        if s + ahead < n_slabs:
            input_matmul(s + ahead)
        a_r = are_ref[:, lo:lo + hs]
        a_i = aim_ref[:, lo:lo + hs]
        x_r = st_ref[:, lo:lo + hs]
        x_i = st_ref[:, lo + hs:lo + sw]
        for t in range(tc):
            r0 = t * SUBLANES
            new_r = a_r * x_r - a_i * x_i + xs_ref[r0:r0 + SUBLANES, lo:lo + hs]
            new_i = a_r * x_i + a_i * x_r + xs_ref[r0:r0 + SUBLANES, lo + hs:lo + sw]
            xs_ref[r0:r0 + SUBLANES, lo:lo + hs] = new_r
            xs_ref[r0:r0 + SUBLANES, lo + hs:lo + sw] = new_i
            x_r, x_i = new_r, new_i
        st_ref[:, lo:lo + hs] = x_r
        st_ref[:, lo + hs:lo + sw] = x_i
        if s >= 1:
            output_matmul(s - 1)
    output_matmul(n_slabs - 1)
    for t0 in range(0, tc, pt):
        g_bt = jnp.dot(permt_ref[...], act_ref[t0 * nb:(t0 + pt) * nb, :],
                       preferred_element_type=F32)
        o_ref[:, t0:t0 + pt, :] = g_bt.reshape(nb, pt, d).astype(o_ref.dtype)
    zero_ref[...] = jnp.zeros_like(zero_ref)


def _s5_mixer(x3d, gain, w_in, bblk, cblk, a_re, a_im, d_skip, *, tc, zero_shape):
    nb, seq, d = x3d.shape
    steps = seq // tc
    zero_rows = zero_shape[0] // steps
    assert zero_rows * steps == zero_shape[0] and zero_rows % SEG_ALIGN == 0
    rows = nb * tc
    n_slabs = d // LANES
    sw = 2 * SSM_SLAB_STATE
    pt = min(tc, S5_PERM_STEPS)
    r = np.arange(nb * pt)
    perm = np.zeros((nb * pt, nb * pt), np.float32)
    perm[(r % pt) * nb + r // pt, r] = 1.0
    perm_j = jnp.asarray(perm, BF16)
    permt_j = jnp.asarray(perm.T, BF16)
    kern = _s5_kernel
    full2 = lambda a: pl.BlockSpec(a.shape, lambda i: (0, 0))
    full3 = lambda a: pl.BlockSpec(a.shape, lambda i: (0, 0, 0))
    return pl.pallas_call(
        kern,
        grid=(seq // tc,),
        in_specs=[
            pl.BlockSpec((nb, tc, d), lambda i: (0, i, 0)),
            full2(gain), full2(w_in), full2(perm_j), full2(permt_j),
            full3(bblk), full3(cblk), full2(a_re), full2(a_im), full2(d_skip),
        ],
        out_specs=[pl.BlockSpec((nb, tc, d), lambda i: (0, i, 0)),
                   pl.BlockSpec((zero_rows, zero_shape[1]), lambda i: (i, 0))],
        out_shape=[jax.ShapeDtypeStruct((nb, seq, d), BF16),
                   jax.ShapeDtypeStruct(zero_shape, BF16)],
        scratch_shapes=[
            pltpu.VMEM((rows, n_slabs * sw), F32),
            pltpu.VMEM((SUBLANES, n_slabs * sw), F32),
            pltpu.VMEM((rows, d), F32),
            pltpu.VMEM((rows, d), BF16),
        ],
        compiler_params=_params("arbitrary"),
        name="s5_mixer",
    )(x3d, gain, w_in, perm_j, permt_j, bblk, cblk, a_re, a_im, d_skip)


def _s5_coefficients(a_re, a_im, log_dt, b_re, b_im, c_re, c_im):
    dt = jnp.exp(log_dt.astype(F32))[:, None]
    ar = a_re.astype(F32)
    ai = a_im.astype(F32)
    mag = jnp.exp(ar * dt)
    abar_re = mag * jnp.cos(ai * dt)
    abar_im = mag * jnp.sin(ai * dt)
    den = ar * ar + ai * ai
    nr = abar_re - 1.0
    ni = abar_im
    coef_re = (nr * ar + ni * ai) / den
    coef_im = (ni * ar - nr * ai) / den
    br = b_re.astype(F32)
    bi = b_im.astype(F32)
    bbar_re = coef_re[..., None] * br - coef_im[..., None] * bi
    bbar_im = coef_re[..., None] * bi + coef_im[..., None] * br
    n_groups = ar.shape[0]
    n_slabs = n_groups // SSM_SLAB_GROUPS
    row_group = np.arange(LANES)[:, None] // SSM_GROUP
    col_group = np.arange(SSM_SLAB_STATE)[None, :] // SSM_STATE
    in_mask = jnp.asarray(row_group == col_group, F32)

    def in_block(bb):
        bb = bb.reshape(n_slabs, SSM_SLAB_GROUPS, SSM_STATE, SSM_GROUP)
        bb = bb.transpose(0, 1, 3, 2).reshape(n_slabs, LANES, SSM_STATE)
        return jnp.tile(bb, (1, 1, SSM_SLAB_GROUPS)) * in_mask

    def out_block(cc):
        cc = cc.reshape(n_slabs, SSM_SLAB_GROUPS, SSM_GROUP, SSM_STATE)
        cc = cc.transpose(0, 1, 3, 2).reshape(n_slabs, SSM_SLAB_STATE, SSM_GROUP)
        return jnp.tile(cc, (1, 1, SSM_SLAB_GROUPS)) * in_mask.T

    bblk = jnp.concatenate([in_block(bbar_re), in_block(bbar_im)], axis=2)
    cblk = jnp.concatenate([out_block(c_re.astype(F32)),
                            -out_block(c_im.astype(F32))], axis=1)

    def lanes(a):
        a = a.reshape(n_slabs, 1, SSM_SLAB_STATE)
        a = jnp.concatenate([a, a], axis=2).reshape(1, -1)
        return jnp.broadcast_to(a, (SUBLANES, a.shape[1]))

    return (bblk.astype(BF16), cblk.astype(BF16), lanes(abar_re), lanes(abar_im))


MOE_CHUNK = 512
MOE_FIRST = 208
MOE_REST = MOE_CHUNK + SEG_ALIGN - MOE_FIRST
MOE_PIECES = ((0, MOE_FIRST), (MOE_FIRST, MOE_REST))
MOE_WINDOW = MOE_FIRST + MOE_REST
NOT_ROUTED = -1024.0
MOE_TILE = 512


def _glu_router_kernel(x_ref, act_ref, wa_ref, wb_ref, g_ref, w_ref, b_ref,
                       x3_ref, h_ref, pos_ref, post_ref, cnt_ref):
    tm, d = x_ref.shape
    parts = 2
    half = tm // parts

    def glu(r0):
        act = act_ref[r0:r0 + half, :]
        a = jnp.dot(act, wa_ref[...], preferred_element_type=F32)
        b = jnp.dot(act, wb_ref[...], preferred_element_type=F32)
        x3 = x_ref[r0:r0 + half, :] + a * _sigmoid(b)
        x3_ref[r0:r0 + half, :] = x3
        return x3

    def route(r0, x3):
        h = _rms(x3, g_ref[...])
        h_hi = h.astype(BF16)
        h_ref[r0:r0 + half, 0:d] = h_hi
        h_lo = (h - h_hi.astype(F32)).astype(BF16)
        p_hi = jnp.dot(h_hi, w_ref[...], preferred_element_type=F32)
        p_lo = jnp.dot(h_lo, w_ref[...], preferred_element_type=F32)
        logits = p_hi + pltpu.roll(p_hi, LANES - N_EXPERTS, axis=1) + p_lo + b_ref[...]
        lane = lax.broadcasted_iota(jnp.int32, logits.shape, 1)
        logits = jnp.where(lane < N_EXPERTS, logits, -jnp.inf)
        m1 = jnp.max(logits, axis=1, keepdims=True)
        i1 = jnp.min(jnp.where(logits == m1, lane, LANES), axis=1, keepdims=True)
        rest = jnp.where(lane == i1, -jnp.inf, logits)
        m2 = jnp.max(rest, axis=1, keepdims=True)
        i2 = jnp.min(jnp.where(rest == m2, lane, LANES), axis=1, keepdims=True)
        e2 = jnp.exp(m2 - m1)
        g1 = 1.0 / (1.0 + e2)
        g2 = e2 / (1.0 + e2)
        gate = jnp.where(lane == i1, g1, 0.0) + jnp.where(lane == i2, g2, 0.0)
        g_hi = gate.astype(BF16).astype(F32)
        g_r = gate - g_hi
        g_mid = g_r.astype(BF16).astype(F32)
        packed = jnp.where(lane < N_EXPERTS, g_hi,
                           jnp.where(lane < 2 * N_EXPERTS,
                                     pltpu.roll(g_mid, N_EXPERTS, axis=1),
                                     pltpu.roll(g_r - g_mid, 2 * N_EXPERTS, axis=1)))
        h_ref[r0:r0 + half, d:d + LANES] = packed.astype(BF16)
        return jnp.where((lane == i1) | (lane == i2), 1.0, 0.0)

    members = []
    x3 = glu(0)
    for p in range(parts):
        nxt = glu((p + 1) * half) if p + 1 < parts else None
        members.append(route(p * half, x3))
        x3 = nxt
    member = jnp.concatenate(members, axis=0)
    row = lax.broadcasted_iota(jnp.int32, member.shape, 0)
    c = member
    sh = 1
    while sh < tm:
        c = c + jnp.where(row >= sh, pltpu.roll(c, sh, axis=0), 0.0)
        sh *= 2
    pos = jnp.where(member > 0.0, c - member, NOT_ROUTED)
    pos_ref[...] = pos
    post_ref[0] = pos.T[0:SUBLANES, :]
    cnt_ref[0] = jnp.broadcast_to(c[tm - 1:tm, :], (SUBLANES, LANES))


def _glu_router(x2d, act2d, wa, wb, gain, w_pad, b_pad):
    t, d = x2d.shape
    tm = MOE_CHUNK
    n_chunks = t // tm
    return pl.pallas_call(
        _glu_router_kernel,
        grid=(n_chunks,),
        in_specs=[
            pl.BlockSpec((tm, d), lambda i: (i, 0)),
            pl.BlockSpec((tm, d), lambda i: (i, 0)),
            pl.BlockSpec(wa.shape, lambda i: (0, 0)),
            pl.BlockSpec(wb.shape, lambda i: (0, 0)),
            pl.BlockSpec((1, d), lambda i: (0, 0)),
            pl.BlockSpec((d, LANES), lambda i: (0, 0)),
            pl.BlockSpec((1, LANES), lambda i: (0, 0)),
        ],
        out_specs=[
            pl.BlockSpec((tm, d), lambda i: (i, 0)),
            pl.BlockSpec((tm, d + LANES), lambda i: (i, 0)),
            pl.BlockSpec((tm, LANES), lambda i: (i, 0)),
            pl.BlockSpec((1, SUBLANES, tm), lambda i: (i, 0, 0)),
            pl.BlockSpec((1, SUBLANES, LANES), lambda i: (i, 0, 0)),
        ],
        out_shape=[
            jax.ShapeDtypeStruct((t, d), F32),
            jax.ShapeDtypeStruct((t, d + LANES), BF16),
            jax.ShapeDtypeStruct((t, LANES), F32),
            jax.ShapeDtypeStruct((n_chunks, SUBLANES, tm), F32),
            jax.ShapeDtypeStruct((n_chunks, SUBLANES, LANES), F32),
        ],
        compiler_params=_params("parallel"),
        name="glu_router",
    )(x2d, act2d, wa, wb, gain, w_pad, b_pad)


MOE_SPARE = max(MOE_FIRST, MOE_REST)


def _moe_tiles(n_tokens, row_multiple):
    max_rows = (TOP_K * n_tokens + N_EXPERTS * (SEG_ALIGN - 1)
                + N_EXPERTS * (MOE_SPARE + MOE_TILE))
    unit = math.lcm(MOE_TILE, row_multiple)
    return -(-max_rows // unit) * (unit // MOE_TILE)


def _moe_layout(cnt):
    before = jnp.cumsum(cnt, axis=0) - cnt
    shift = before % SEG_ALIGN
    used = (jnp.sum(cnt, axis=0) + SEG_ALIGN - 1) // SEG_ALIGN * SEG_ALIGN
    padded = (used + MOE_SPARE + MOE_TILE - 1) // MOE_TILE * MOE_TILE
    ends = jnp.cumsum(padded)
    start = ends - padded
    off = start[None, :] + before - shift
    flat = lambda a: a.reshape(-1).astype(jnp.int32)
    return (flat(off), flat(shift + cnt), flat(shift),
            ends.astype(jnp.int32), (start + used).astype(jnp.int32))


def _tile_expert(i, ends_ref):
    first_row = i * MOE_TILE
    e = 0
    for k in range(N_EXPERTS - 1):
        e = e + (first_row >= ends_ref[k]).astype(jnp.int32)
    return e


def _segment_copies(hbm_ref, buf_ref, sem_ref, off_ref, chunk, to_hbm, slot=()):
    copies = []
    for e in range(N_EXPERTS):
        off = pl.multiple_of(off_ref[chunk * N_EXPERTS + e], SEG_ALIGN)
        for piece, (first, rows) in enumerate(MOE_PIECES):
            hbm = hbm_ref.at[pl.ds(off + first, rows)]
            buf = buf_ref.at[(*slot, e, pl.ds(first, rows))]
            src, dst = (buf, hbm) if to_hbm else (hbm, buf)
            copies.append(pltpu.make_async_copy(src, dst, sem_ref.at[(*slot, e, piece)]))
    return copies


def _onehot(first, shape, axis, target):
    rank = first + lax.broadcasted_iota(jnp.int32, shape, axis)
    return jnp.where(rank.astype(F32) == target, 1.0, 0.0).astype(BF16)


def _gather_kernel(off_ref, need_ref, shift_ref, h_ref, post_ref, zeros_hbm, o_hbm,
                   stage_ref, carry_ref, sem_ref):
    del zeros_hbm
    c = pl.program_id(0)
    slot = c % 2
    h = h_ref[...]

    @pl.when(c == 0)
    def _():
        carry_ref[...] = jnp.zeros_like(carry_ref)

    def long_segment(chunk, e):
        return need_ref[chunk * N_EXPERTS + e] >= MOE_FIRST

    def piece_onehot(e, piece):
        first, rows = MOE_PIECES[piece]
        return _onehot(first - shift_ref[c * N_EXPERTS + e], (rows, MOE_CHUNK), 0,
                       post_ref[0, e:e + 1, :])

    def fill_piece(e, piece, onehot=None):
        first, rows = MOE_PIECES[piece]
        onehot = piece_onehot(e, piece) if onehot is None else onehot
        stage_ref[slot, e, first:first + rows] = jnp.dot(
            onehot, h, preferred_element_type=F32).astype(BF16)
        if piece == 0:
            stage_ref[slot, e, 0:SEG_ALIGN] += carry_ref[e]

    onehot = piece_onehot(0, 0)
    for e in range(N_EXPERTS):
        nxt = piece_onehot(e + 1, 0) if e + 1 < N_EXPERTS else None
        fill_piece(e, 0, onehot)
        onehot = nxt
    for e in range(N_EXPERTS):
        @pl.when(long_segment(c, e))
        def _(e=e):
            fill_piece(e, 1)

    for e in range(N_EXPERTS):
        last_group = need_ref[c * N_EXPERTS + e] // SEG_ALIGN * SEG_ALIGN
        carry_ref[e] = stage_ref[slot, e, pl.ds(pl.multiple_of(last_group, SEG_ALIGN),
                                                SEG_ALIGN)]

    def each_copy(chunk, sl, action):
        copies = _segment_copies(o_hbm, stage_ref, sem_ref, off_ref, chunk,
                                 to_hbm=True, slot=(sl,))
        for e in range(N_EXPERTS):
            action(copies[2 * e])

            @pl.when(long_segment(chunk, e))
            def _(e=e):
                action(copies[2 * e + 1])

    @pl.when(c > 0)
    def _():
        each_copy(c - 1, 1 - slot, lambda cp: cp.wait())

    each_copy(c, slot, lambda cp: cp.start())

    @pl.when(c == pl.num_programs(0) - 1)
    def _():
        each_copy(c, slot, lambda cp: cp.wait())


def _gather(h2d, post, off, need, shift, zero_buf):
    t, d = h2d.shape
    n_chunks = t // MOE_CHUNK
    grid_spec = pltpu.PrefetchScalarGridSpec(
        num_scalar_prefetch=3,
        grid=(n_chunks,),
        in_specs=[
            pl.BlockSpec((MOE_CHUNK, d), lambda i, *_: (i, 0)),
            pl.BlockSpec((1, SUBLANES, MOE_CHUNK), lambda i, *_: (i, 0, 0)),
            pl.BlockSpec(memory_space=pl.ANY),
        ],
        out_specs=pl.BlockSpec(memory_space=pl.ANY),
        scratch_shapes=[
            pltpu.VMEM((2, N_EXPERTS, MOE_WINDOW, d), BF16),
            pltpu.VMEM((N_EXPERTS, SEG_ALIGN, d), BF16),
            pltpu.SemaphoreType.DMA((2, N_EXPERTS, 2)),
        ],
    )
    return pl.pallas_call(
        _gather_kernel,
        grid_spec=grid_spec,
        out_shape=jax.ShapeDtypeStruct(zero_buf.shape, BF16),
        input_output_aliases={5: 0},
        compiler_params=_params("arbitrary"),
        name="moe_gather",
    )(off, need, shift, h2d, post, zero_buf)


def _expert_kernel(ends_ref, used_end_ref, x_ref, w1_ref, w3_ref, w2_ref, o_ref):
    i = pl.program_id(0)
    expert = _tile_expert(i, ends_ref)
    rows = jnp.clip(used_end_ref[expert] - i * MOE_TILE, 0, MOE_TILE)
    d = o_ref.shape[1]

    def run(n):
        row = lax.broadcasted_iota(jnp.int32, (n, x_ref.shape[1]), 0)
        x = jnp.where(row < rows, x_ref[0:n, :], jnp.zeros((), x_ref.dtype))
        lane = lax.broadcasted_iota(jnp.int32, (n, LANES), 1)
        mine = (lane < 3 * N_EXPERTS) & (jnp.bitwise_and(lane, N_EXPERTS - 1) == expert)
        gate = jnp.sum(jnp.where(mine, x[:, d:d + LANES].astype(F32), 0.0),
                       axis=1, keepdims=True)
        y = _swiglu(x[:, 0:d], w1_ref, w3_ref, w2_ref)
        o_ref[0:n, :] = (gate * y).astype(o_ref.dtype)

    quarter = MOE_TILE // 4
    for n in range(quarter, MOE_TILE + 1, quarter):
        @pl.when((rows > n - quarter) & (rows <= n))
        def _(n=n):
            run(n)
            if n < MOE_TILE:
                o_ref[n:MOE_TILE, :] = jnp.zeros((MOE_TILE - n, d), o_ref.dtype)

    @pl.when(rows == 0)
    def _():
        o_ref[...] = jnp.zeros_like(o_ref)


def _experts(x_sorted, w1, w3, w2, ends, used_end):
    assert N_EXPERTS & (N_EXPERTS - 1) == 0
    n_rows = x_sorted.shape[0]
    f = w1.shape[1]
    d = w2.shape[1]
    grid_spec = pltpu.PrefetchScalarGridSpec(
        num_scalar_prefetch=2,
        grid=(n_rows // MOE_TILE,),
        in_specs=[
            pl.BlockSpec((MOE_TILE, d + LANES), lambda i, ends, ue: (i, 0)),
            pl.BlockSpec((d, f), lambda i, ends, ue: (_tile_expert(i, ends), 0)),
            pl.BlockSpec((d, f), lambda i, ends, ue: (_tile_expert(i, ends), 0)),
            pl.BlockSpec((f, d), lambda i, ends, ue: (_tile_expert(i, ends), 0)),
        ],
        out_specs=pl.BlockSpec((MOE_TILE, d), lambda i, ends, ue: (i, 0)),
    )
    return pl.pallas_call(
        _expert_kernel,
        grid_spec=grid_spec,
        out_shape=jax.ShapeDtypeStruct((n_rows, d), BF16),
        compiler_params=_params("arbitrary"),
        name="moe_experts",
    )(ends, used_end, x_sorted, w1, w3, w2)


def _combine_kernel(off_ref, need_ref, shift_ref, x_ref, pos_ref, fg_ref, y_hbm,
                    o_ref, ybuf_ref, acc_ref, sem_ref):
    c = pl.program_id(0)
    n_chunks = pl.num_programs(0)
    slot = c % 2

    def fetch(chunk, sl, start):
        copies = _segment_copies(y_hbm, ybuf_ref, sem_ref, off_ref, chunk,
                                 to_hbm=False, slot=(sl,))
        for e in range(N_EXPERTS):
            first, second = copies[2 * e], copies[2 * e + 1]
            if start:
                first.start()
            else:
                first.wait()

            @pl.when(need_ref[chunk * N_EXPERTS + e] >= MOE_FIRST)
            def _(second=second):
                if start:
                    second.start()
                else:
                    second.wait()

    @pl.when(c == 0)
    def _():
        fetch(c, slot, True)

    @pl.when(c + 1 < n_chunks)
    def _():
        fetch(c + 1, 1 - slot, True)

    fetch(c, slot, False)

    def piece_onehot(e, piece):
        first, rows = MOE_PIECES[piece]
        return _onehot(first - shift_ref[c * N_EXPERTS + e], (MOE_CHUNK, rows), 1,
                       pos_ref[:, e:e + 1])

    def piece_rows(e, piece, onehot=None):
        first, rows = MOE_PIECES[piece]
        onehot = piece_onehot(e, piece) if onehot is None else onehot
        return jnp.dot(onehot, ybuf_ref[slot, e, first:first + rows],
                       preferred_element_type=F32)

    acc = x_ref[...]
    onehot = piece_onehot(0, 0)
    for e in range(N_EXPERTS):
        nxt = piece_onehot(e + 1, 0) if e + 1 < N_EXPERTS else None
        acc = acc + piece_rows(e, 0, onehot)
        onehot = nxt
    acc_ref[...] = acc
    for e in range(N_EXPERTS):
        @pl.when(need_ref[c * N_EXPERTS + e] >= MOE_FIRST)
        def _(e=e):
            acc_ref[...] += piece_rows(e, 1)

    o_ref[...] = _rms(acc_ref[...], fg_ref[...])


def _combine(x2d, pos, final_gain, y_sorted, off, need, shift):
    t, d = x2d.shape
    n_chunks = t // MOE_CHUNK
    grid_spec = pltpu.PrefetchScalarGridSpec(
        num_scalar_prefetch=3,
        grid=(n_chunks,),
        in_specs=[
            pl.BlockSpec((MOE_CHUNK, d), lambda i, *_: (i, 0)),
            pl.BlockSpec((MOE_CHUNK, LANES), lambda i, *_: (i, 0)),
            pl.BlockSpec((1, d), lambda i, *_: (0, 0)),
            pl.BlockSpec(memory_space=pl.ANY),
        ],
        out_specs=pl.BlockSpec((MOE_CHUNK, d), lambda i, *_: (i, 0)),
        scratch_shapes=[
            pltpu.VMEM((2, N_EXPERTS, MOE_WINDOW, d), BF16),
            pltpu.VMEM((MOE_CHUNK, d), F32),
            pltpu.SemaphoreType.DMA((2, N_EXPERTS, 2)),
        ],
    )
    return pl.pallas_call(
        _combine_kernel,
        grid_spec=grid_spec,
        out_shape=jax.ShapeDtypeStruct((t, d), F32),
        compiler_params=_params("arbitrary"),
        name="moe_combine",
    )(off, need, shift, x2d, pos, final_gain, y_sorted)


def _glu_moe(x2d, act2d, wa, wb, gain, router_w, router_b, w1, w3, w2, final_gain,
             zero_buf):
    t, d = x2d.shape
    rw_hi = router_w.astype(BF16)
    rw_lo = (router_w.astype(F32) - rw_hi.astype(F32)).astype(BF16)
    rw = jnp.pad(jnp.concatenate([rw_hi, rw_lo], axis=1),
                 ((0, 0), (0, LANES - 2 * N_EXPERTS)))
    rb = jnp.pad(router_b.astype(F32), (0, LANES - N_EXPERTS)).reshape(1, LANES)
    x3, h, pos, post, cnt = _glu_router(x2d, act2d, wa, wb, gain, rw, rb)
    cnt = cnt[:, 0, :N_EXPERTS].astype(jnp.int32)
    off, need, shift, ends, used_end = _moe_layout(cnt)
    h_sorted = _gather(h, post, off, need, shift, zero_buf)
    y_sorted = _experts(h_sorted, w1, w3, w2, ends, used_end)
    return _combine(x3, pos, final_gain, y_sorted, off, need, shift)


def _even_weights(w_in, b_forget):
    n = w_in.shape[1]
    w_all = jnp.pad(w_in.astype(BF16), ((0, 0), (0, -n % LANES)))
    bias = jnp.pad(b_forget.astype(F32), (0, LANES - FOX_HEADS)).reshape(1, LANES)
    qw = FOX_HEADS * LANES
    place = np.zeros((LANES, 2 * qw), np.float32)
    for hh in range(FOX_HEADS):
        for piece in range(3):
            place[piece * FOX_HEADS + hh, hh * LANES + BIAS_LANE + piece] = 1.0
            place[piece * FOX_HEADS + hh, qw + hh * LANES + ONES_LANE + piece] = -1.0
    return w_all, bias, jnp.asarray(place, BF16)


def kernel(x, even_mix_norm, even_w_in, even_b_forget, even_w_pool, even_pool_scale, even_w_out, even_ffn_norm, even_ffn_w1, even_ffn_w3, even_ffn_w2, odd_mix_norm, odd_w_in, ssm_a_re, ssm_a_im, ssm_log_dt, ssm_b_re, ssm_b_im, ssm_c_re, ssm_c_im, ssm_d, odd_w_glu_a, odd_w_glu_b, odd_moe_norm, router_w, router_b, expert_w1, expert_w3, expert_w2, final_norm):
    b, s, d = x.shape
    t = b * s
    assert b == SUBLANES, "the S5 recurrence keeps one batch row per sublane"
    x2d = x.reshape(t, d)
    row = lambda v: v.reshape(1, -1).astype(F32)

    w_all, bias, place = _even_weights(even_w_in[0], even_b_forget[0])
    (q_aug, k_aug, v, p_in), casted = _even_inproj(
        x2d, row(even_mix_norm[0]), w_all, bias, place,
        [even_ffn_w1[0], even_ffn_w3[0], even_ffn_w2[0], even_w_out[0],
         even_w_pool[0].reshape(-1, POOL_GROUP_DIM),
         odd_w_in[0], odd_w_glu_a[0], odd_w_glu_b[0]],
        seq=s, tm=INPROJ_ROWS)
    ffn_w1, ffn_w3, ffn_w2, w_out, w_pool, s5_w_in, glu_a, glu_b = casted
    n_e, _, f = expert_w1[0].shape
    att, (ew1, ew3, ew2) = _attention(
        q_aug, k_aug, v,
        [expert_w1[0].reshape(n_e * d, f), expert_w3[0].reshape(n_e * d, f),
         expert_w2[0].reshape(n_e * f, d)],
        batch=b, seq=s, tq=ATTN_QUERY_TILE)
    x2 = _even_tail(x2d, att, p_in, w_pool.reshape(even_w_pool[0].shape),
                    row(even_pool_scale[0]), w_out, row(even_ffn_norm[0]),
                    ffn_w1, ffn_w3, ffn_w2, seq=s, tm=EVEN_TAIL_ROWS)

    bblk, cblk, a_re, a_im = _s5_coefficients(
        ssm_a_re[0], ssm_a_im[0], ssm_log_dt[0], ssm_b_re[0], ssm_b_im[0],
        ssm_c_re[0], ssm_c_im[0])
    moe_rows = _moe_tiles(t, (s // S5_STEP) * SEG_ALIGN) * MOE_TILE
    g, zero_buf = _s5_mixer(x2.reshape(b, s, d), row(odd_mix_norm[0]),
                            s5_w_in, bblk, cblk, a_re, a_im,
                            row(ssm_d[0]), tc=S5_STEP,
                            zero_shape=(moe_rows, d + LANES))
    out = _glu_moe(x2, g.reshape(t, d), glu_a, glu_b, row(odd_moe_norm[0]),
                   router_w[0], router_b[0], ew1, ew3, ew2,
                   row(final_norm), zero_buf)
    return out.reshape(b, s, d)
```

```python
import functools
import math

import numpy as np
import jax
import jax.numpy as jnp
from jax import lax
from jax.experimental import pallas as pl
from jax.experimental.pallas import tpu as pltpu

F32 = jnp.float32
BF16 = jnp.bfloat16

EPS = 1e-6
NEG_INF = -1e30
LANES = 128
SUBLANES = 8
SEG_ALIGN = 16
VMEM_LIMIT = 56 * 1024 * 1024

FOX_HEADS = 8
FOX_HEAD_DIM = 64
FOX_WIDTH = FOX_HEADS * FOX_HEAD_DIM
POOL_WINDOWS = (2, 4, 8, 16)
POOL_GROUP_DIM = 128
POOL_WIDTH = len(POOL_WINDOWS) * POOL_GROUP_DIM
POOL_HALO = 16
SSM_GROUP = 16
SSM_STATE = 64
SSM_SLAB_GROUPS = LANES // SSM_GROUP
SSM_SLAB_STATE = SSM_SLAB_GROUPS * SSM_STATE

INPROJ_ROWS = 1024
INPROJ_PART_ROWS = 256
ATTN_QUERY_TILE = 256
EVEN_TAIL_ROWS = 512
S5_STEP = 64
S5_PERM_STEPS = 32
N_EXPERTS = 8
TOP_K = 2

BIAS_LANE = FOX_HEAD_DIM
ONES_LANE = FOX_HEAD_DIM + 3
DENOM_LANE = FOX_HEAD_DIM


def _params(*sem):
    return pltpu.CompilerParams(dimension_semantics=sem,
                                vmem_limit_bytes=VMEM_LIMIT)


def _rms(x, g):
    ms = jnp.mean(x * x, axis=-1, keepdims=True)
    return x * lax.rsqrt(ms + EPS) * g


def _sigmoid(x):
    return 1.0 / (1.0 + jnp.exp(-x))


def _lane_range_ones(lo, hi):
    lane = lax.broadcasted_iota(jnp.int32, (1, LANES), 1)
    return jnp.where((lane >= lo) & (lane < hi), 1.0, 0.0).astype(F32)


def _cast_specs(weights, steps, step_index):
    specs = []
    for w in weights:
        rows = -(-w.shape[0] // (steps * SEG_ALIGN)) * SEG_ALIGN
        last = -(-w.shape[0] // rows) - 1
        specs.append(pl.BlockSpec(
            (rows, w.shape[1]),
            lambda *idx, last=last: (jnp.minimum(step_index(*idx), last), 0)))
    return specs


def _cast_slabs(src_refs, dst_refs):
    for src_ref, dst_ref in zip(src_refs, dst_refs):
        dst_ref[...] = src_ref[...].astype(dst_ref.dtype)


def _even_inproj_kernel(x_ref, g_ref, w_ref, bias_ref, place_ref, *rest,
                        tiles_per_seq, n_cast):
    cast_in, rest = rest[:n_cast], rest[n_cast:]
    q_ref, k_ref, v_ref, p_ref = rest[:4]
    cast_out, carry_ref = rest[4:4 + n_cast], rest[4 + n_cast]
    _cast_slabs(cast_in, cast_out)
    i = pl.program_id(0)
    tm = x_ref.shape[0]
    rows = INPROJ_PART_ROWS
    parts = tm // rows
    fw = FOX_WIDTH
    qw = FOX_HEADS * LANES

    @pl.when(i % tiles_per_seq == 0)
    def _():
        carry_ref[...] = jnp.zeros_like(carry_ref)

    def project(r0):
        h = _rms(x_ref[r0:r0 + rows, :], g_ref[...]).astype(BF16)
        return jnp.dot(h, w_ref[...], preferred_element_type=F32)

    def finish(r0, z, carry):
        out = slice(r0, r0 + rows)
        p_ref[out, :] = z[:, 3 * fw + FOX_HEADS:3 * fw + FOX_HEADS + POOL_WIDTH]
        fg = z[:, 3 * fw:3 * fw + LANES] + bias_ref[...]
        lf = jnp.minimum(fg, 0.0) - jnp.log1p(jnp.exp(-jnp.abs(fg)))
        row = lax.broadcasted_iota(jnp.int32, lf.shape, 0)
        c = lf
        sh = 1
        while sh < rows:
            c = c + jnp.where(row >= sh, pltpu.roll(c, sh, axis=0), 0.0)
            sh *= 2
        c = c + carry
        hi = c.astype(BF16).astype(F32)
        r1 = c - hi
        mid = r1.astype(BF16).astype(F32)
        lo = r1 - mid
        lane = lax.broadcasted_iota(jnp.int32, (rows, LANES), 1)
        packed = jnp.where(lane < FOX_HEADS, hi,
                           jnp.where(lane < 2 * FOX_HEADS,
                                     pltpu.roll(mid, FOX_HEADS, axis=1),
                                     pltpu.roll(lo, 2 * FOX_HEADS, axis=1)))
        placed = jnp.dot(packed.astype(BF16), place_ref[...],
                         preferred_element_type=F32)
        low = lane < FOX_HEAD_DIM
        ones_q = _lane_range_ones(ONES_LANE, ONES_LANE + 3)
        ones_k = _lane_range_ones(BIAS_LANE, BIAS_LANE + 3)
        ones_v = _lane_range_ones(DENOM_LANE, DENOM_LANE + 1)

        def head_lanes(base, hh):
            pair = z[:, base + (hh // 2) * LANES:base + (hh // 2 + 1) * LANES]
            return pltpu.roll(pair, FOX_HEAD_DIM, axis=1) if hh % 2 else pair

        q_scale = FOX_HEAD_DIM ** -0.5
        for hh in range(FOX_HEADS):
            sl = slice(hh * LANES, (hh + 1) * LANES)
            q_ref[out, sl] = jnp.where(low, head_lanes(0, hh) * q_scale,
                                       placed[:, sl] + ones_q).astype(BF16)
            k_ref[out, sl] = jnp.where(
                low, head_lanes(fw, hh),
                placed[:, qw + hh * LANES:qw + (hh + 1) * LANES] + ones_k).astype(BF16)
            v_ref[out, sl] = jnp.where(low, head_lanes(2 * fw, hh), ones_v).astype(BF16)
        return c[rows - 1:rows, :]

    carry = carry_ref[0:1, :]
    z = project(0)
    for part in range(parts):
        nxt = project((part + 1) * rows) if part + 1 < parts else None
        carry = finish(part * rows, z, carry)
        z = nxt
    carry_ref[...] = jnp.broadcast_to(carry, carry_ref.shape)


def _even_inproj(x2d, gain, w_all, bias, place, cast_weights, *, seq, tm):
    t, d = x2d.shape
    n = w_all.shape[1]
    qw = FOX_HEADS * LANES
    cast_specs = _cast_specs(cast_weights, t // tm, lambda i: i)
    kern = functools.partial(_even_inproj_kernel, tiles_per_seq=seq // tm,
                             n_cast=len(cast_weights))
    outs = pl.pallas_call(
        kern,
        grid=(t // tm,),
        in_specs=[
            pl.BlockSpec((tm, d), lambda i: (i, 0)),
            pl.BlockSpec((1, d), lambda i: (0, 0)),
            pl.BlockSpec((d, n), lambda i: (0, 0)),
            pl.BlockSpec((1, LANES), lambda i: (0, 0)),
            pl.BlockSpec(place.shape, lambda i: (0, 0)),
            *cast_specs,
        ],
        out_specs=[
            pl.BlockSpec((tm, qw), lambda i: (i, 0)),
            pl.BlockSpec((tm, qw), lambda i: (i, 0)),
            pl.BlockSpec((tm, qw), lambda i: (i, 0)),
            pl.BlockSpec((tm, POOL_WIDTH), lambda i: (i, 0)),
            *cast_specs,
        ],
        out_shape=[
            jax.ShapeDtypeStruct((t, qw), BF16),
            jax.ShapeDtypeStruct((t, qw), BF16),
            jax.ShapeDtypeStruct((t, qw), BF16),
            jax.ShapeDtypeStruct((t, POOL_WIDTH), F32),
        ] + [jax.ShapeDtypeStruct(w.shape, BF16) for w in cast_weights],
        scratch_shapes=[pltpu.VMEM((SUBLANES, LANES), F32)],
        compiler_params=_params("arbitrary"),
        name="even_inproj",
    )(x2d, gain, w_all, bias, place, *cast_weights)
    return outs[:4], outs[4:]


def _dot_nt(a, b):
    return lax.dot_general(a, b, (((1,), (1,)), ((), ())),
                           preferred_element_type=F32)


def _attn_kernel(q_ref, k_ref, v_ref, *rest, tq, n_cast):
    cast_in, o_ref, cast_out = rest[:n_cast], rest[n_cast], rest[n_cast + 1:]
    seq = q_ref.shape[0]
    row = lax.broadcasted_iota(jnp.int32, (tq, tq), 0)
    col = lax.broadcasted_iota(jnp.int32, (tq, tq), 1)
    causal = col <= row
    lane = lax.broadcasted_iota(jnp.int32, (tq, LANES), 1)
    work = [(qi, hh) for qi in range(seq // tq) for hh in range(2)]

    def scores(qi, hh):
        r0 = qi * tq
        sl = slice(hh * LANES, (hh + 1) * LANES)
        q = q_ref[r0:r0 + tq, sl]
        s = _dot_nt(q, k_ref[0:r0 + tq, sl])
        s_diag = jnp.where(causal, s[:, r0:r0 + tq], NEG_INF)
        return s_diag if qi == 0 else jnp.concatenate([s[:, 0:r0], s_diag], axis=1)

    ahead = 2
    queue = [scores(*item) for item in work[:ahead]]
    first_head = None
    for n, (qi, hh) in enumerate(work):
        r0 = qi * tq
        sl = slice(hh * LANES, (hh + 1) * LANES)
        s = queue.pop(0)
        if n + ahead < len(work):
            queue.append(scores(*work[n + ahead]))
        m = jnp.max(s, axis=1, keepdims=True)
        acc = jnp.dot(jnp.exp(s - m).astype(BF16), v_ref[0:r0 + tq, sl],
                      preferred_element_type=F32)
        out = acc / acc[:, DENOM_LANE:DENOM_LANE + 1]
        if hh == 0:
            first_head = out
        else:
            o_ref[r0:r0 + tq, :] = jnp.where(
                lane < FOX_HEAD_DIM, first_head,
                pltpu.roll(out, FOX_HEAD_DIM, axis=1)).astype(o_ref.dtype)
    _cast_slabs(cast_in, cast_out)


def _attention(q_aug, k_aug, v_aug, cast_weights, *, batch, seq, tq):
    t = q_aug.shape[0]
    pairs = FOX_HEADS // 2
    steps = batch * pairs
    spec = pl.BlockSpec((seq, 2 * LANES), lambda b, hp: (b, hp))
    out_spec = pl.BlockSpec((seq, LANES), lambda b, hp: (b, hp))
    cast_specs = _cast_specs(cast_weights, steps, lambda b, hp: b * pairs + hp)
    outs = pl.pallas_call(
        functools.partial(_attn_kernel, tq=tq, n_cast=len(cast_weights)),
        grid=(batch, pairs),
        in_specs=[spec, spec, spec, *cast_specs],
        out_specs=[out_spec, *cast_specs],
        out_shape=[jax.ShapeDtypeStruct((t, FOX_WIDTH), BF16)]
        + [jax.ShapeDtypeStruct(w.shape, BF16) for w in cast_weights],
        compiler_params=_params("parallel", "parallel"),
        name="fox_attention",
    )(q_aug, k_aug, v_aug, *cast_weights)
    return outs[0], outs[1:]


MXU_TILE = 256


def _swiglu_partial(h, w1, w3, w2):
    a = jnp.dot(h, w1, preferred_element_type=F32)
    b = jnp.dot(h, w3, preferred_element_type=F32)
    act = a * _sigmoid(a) * b
    return jnp.dot(act.astype(BF16), w2, preferred_element_type=F32)


def _ff_splits(f):
    cut = -(-(f // MXU_TILE) // 2) * MXU_TILE
    return ((0, cut), (cut, f)) if 0 < cut < f else ((0, f),)


def _swiglu(h, w1_ref, w3_ref, w2_ref):
    y = None
    for lo, hi in _ff_splits(w1_ref.shape[1]):
        part = _swiglu_partial(h, w1_ref[:, lo:hi], w3_ref[:, lo:hi], w2_ref[lo:hi, :])
        y = part if y is None else y + part
    return y


def _even_tail_kernel(x_ref, att_ref, p_ref, halo_ref, wpool_ref, scale_ref,
                      wo_att_ref, wo_pool_ref, g_ref, w1_ref, w3_ref, w2_ref, o_ref,
                      *, tiles_per_seq):
    i = pl.program_id(0)
    tm = x_ref.shape[0]
    tile_in_seq = i % tiles_per_seq
    p = p_ref[...]
    halo = jnp.where(tile_in_seq == 0, 0.0, halo_ref[...])
    ext = jnp.concatenate([halo, p], axis=0)
    pos = tile_in_seq * tm + lax.broadcasted_iota(jnp.int32, (tm, 1), 0)
    count = (pos + 1).astype(F32)
    mixed = []
    for gi, w in enumerate(POOL_WINDOWS):
        sl = slice(gi * POOL_GROUP_DIM, (gi + 1) * POOL_GROUP_DIM)
        acc = ext[:, sl]
        sh = 1
        while sh < w:
            acc = acc + pltpu.roll(acc, sh, axis=0)
            sh *= 2
        mean = acc[POOL_HALO:, :] / jnp.minimum(count, float(w))
        pooled = (mean - p[:, sl]).astype(BF16)
        mixed.append(jnp.dot(pooled, wpool_ref[gi], preferred_element_type=F32))
    pool = (jnp.concatenate(mixed, axis=1) * scale_ref[...]).astype(BF16)
    y = jnp.dot(att_ref[...], wo_att_ref[...], preferred_element_type=F32)
    y = y + jnp.dot(pool, wo_pool_ref[...], preferred_element_type=F32)
    x1 = x_ref[...] + y
    h = _rms(x1, g_ref[...]).astype(BF16)
    o_ref[...] = x1 + _swiglu(h, w1_ref, w3_ref, w2_ref)


def _even_tail(x2d, att, p_in, w_pool, pool_scale, w_out, ffn_gain, w1, w3, w2,
               *, seq, tm):
    t, d = x2d.shape
    resident = lambda w: pl.BlockSpec(w.shape, lambda i: (0,) * w.ndim,
                                      pipeline_mode=pl.Buffered(1))
    wo_att = w_out[:FOX_WIDTH]
    wo_pool = w_out[FOX_WIDTH:]
    halo_blocks = tm // POOL_HALO
    kern = functools.partial(_even_tail_kernel, tiles_per_seq=seq // tm)
    return pl.pallas_call(
        kern,
        grid=(t // tm,),
        in_specs=[
            pl.BlockSpec((tm, d), lambda i: (i, 0)),
            pl.BlockSpec((tm, FOX_WIDTH), lambda i: (i, 0)),
            pl.BlockSpec((tm, POOL_WIDTH), lambda i: (i, 0)),
            pl.BlockSpec((POOL_HALO, POOL_WIDTH),
                         lambda i: (jnp.maximum(i * halo_blocks - 1, 0), 0)),
            resident(w_pool),
            pl.BlockSpec((1, POOL_WIDTH), lambda i: (0, 0)),
            resident(wo_att), resident(wo_pool),
            pl.BlockSpec((1, d), lambda i: (0, 0)),
            resident(w1), resident(w3), resident(w2),
        ],
        out_specs=pl.BlockSpec((tm, d), lambda i: (i, 0)),
        out_shape=jax.ShapeDtypeStruct((t, d), F32),
        compiler_params=_params("parallel"),
        name="even_tail",
    )(x2d, att, p_in, p_in, w_pool, pool_scale, wo_att, wo_pool, ffn_gain, w1, w3, w2)


def _gelu_tanh(x):
    c = math.sqrt(2.0 / math.pi)
    return 0.5 * x * (1.0 + jnp.tanh(c * (x + 0.044715 * (x * x * x))))


def _s5_kernel(x_ref, g_ref, win_ref, perm_ref, permt_ref, bblk_ref, cblk_ref,
               are_ref, aim_ref, d_ref, o_ref, zero_ref, xs_ref, st_ref, u_ref, act_ref):
    nb, tc, d = x_ref.shape
    rows = nb * tc
    n_slabs = d // LANES
    sw = 2 * SSM_SLAB_STATE

    @pl.when(pl.program_id(0) == 0)
    def _():
        st_ref[...] = jnp.zeros_like(st_ref)

    pt = perm_ref.shape[0] // nb
    h_tb = []
    for t0 in range(0, tc, pt):
        x = x_ref[:, t0:t0 + pt, :].reshape(nb * pt, d)
        h = _rms(x, g_ref[...]).astype(BF16)
        h_tb.append(jnp.dot(perm_ref[...], h, preferred_element_type=F32).astype(BF16))
    u_ref[...] = jnp.dot(jnp.concatenate(h_tb, axis=0), win_ref[...],
                         preferred_element_type=F32)
    hs = SSM_SLAB_STATE

    def input_matmul(s):
        xs_ref[:, s * sw:(s + 1) * sw] = jnp.dot(
            u_ref[:, s * LANES:(s + 1) * LANES].astype(BF16), bblk_ref[s],
            preferred_element_type=F32)

    def output_matmul(s):
        sl = slice(s * LANES, (s + 1) * LANES)
        y = jnp.dot(xs_ref[:, s * sw:(s + 1) * sw].astype(BF16), cblk_ref[s],
                    preferred_element_type=F32) + d_ref[:, sl] * u_ref[:, sl]
        act_ref[:, sl] = _gelu_tanh(y).astype(BF16)

    ahead = 2
    for s in range(min(ahead, n_slabs)):
        input_matmul(s)
    for s in range(n_slabs):
        lo = s * sw
        if s + ahead < n_slabs:
            input_matmul(s + ahead)
        a_r = are_ref[:, lo:lo + hs]
        a_i = aim_ref[:, lo:lo + hs]
        x_r = st_ref[:, lo:lo + hs]
        x_i = st_ref[:, lo + hs:lo + sw]
        for t in range(tc):
            r0 = t * SUBLANES
            new_r = a_r * x_r - a_i * x_i + xs_ref[r0:r0 + SUBLANES, lo:lo + hs]
            new_i = a_r * x_i + a_i * x_r + xs_ref[r0:r0 + SUBLANES, lo + hs:lo + sw]
            xs_ref[r0:r0 + SUBLANES, lo:lo + hs] = new_r
            xs_ref[r0:r0 + SUBLANES, lo + hs:lo + sw] = new_i
            x_r, x_i = new_r, new_i
        st_ref[:, lo:lo + hs] = x_r
        st_ref[:, lo + hs:lo + sw] = x_i
        if s >= 1:
            output_matmul(s - 1)
    output_matmul(n_slabs - 1)
    for t0 in range(0, tc, pt):
        g_bt = jnp.dot(permt_ref[...], act_ref[t0 * nb:(t0 + pt) * nb, :],
                       preferred_element_type=F32)
        o_ref[:, t0:t0 + pt, :] = g_bt.reshape(nb, pt, d).astype(o_ref.dtype)
    zero_ref[...] = jnp.zeros_like(zero_ref)


def _s5_mixer(x3d, gain, w_in, bblk, cblk, a_re, a_im, d_skip, *, tc, zero_shape):
    nb, seq, d = x3d.shape
    steps = seq // tc
    zero_rows = zero_shape[0] // steps
    assert zero_rows * steps == zero_shape[0] and zero_rows % SEG_ALIGN == 0
    rows = nb * tc
    n_slabs = d // LANES
    sw = 2 * SSM_SLAB_STATE
    pt = min(tc, S5_PERM_STEPS)
    r = np.arange(nb * pt)
    perm = np.zeros((nb * pt, nb * pt), np.float32)
    perm[(r % pt) * nb + r // pt, r] = 1.0
    perm_j = jnp.asarray(perm, BF16)
    permt_j = jnp.asarray(perm.T, BF16)
    kern = _s5_kernel
    full2 = lambda a: pl.BlockSpec(a.shape, lambda i: (0, 0))
    full3 = lambda a: pl.BlockSpec(a.shape, lambda i: (0, 0, 0))
    return pl.pallas_call(
        kern,
        grid=(seq // tc,),
        in_specs=[
            pl.BlockSpec((nb, tc, d), lambda i: (0, i, 0)),
            full2(gain), full2(w_in), full2(perm_j), full2(permt_j),
            full3(bblk), full3(cblk), full2(a_re), full2(a_im), full2(d_skip),
        ],
        out_specs=[pl.BlockSpec((nb, tc, d), lambda i: (0, i, 0)),
                   pl.BlockSpec((zero_rows, zero_shape[1]), lambda i: (i, 0))],
        out_shape=[jax.ShapeDtypeStruct((nb, seq, d), BF16),
                   jax.ShapeDtypeStruct(zero_shape, BF16)],
        scratch_shapes=[
            pltpu.VMEM((rows, n_slabs * sw), F32),
            pltpu.VMEM((SUBLANES, n_slabs * sw), F32),
            pltpu.VMEM((rows, d), F32),
            pltpu.VMEM((rows, d), BF16),
        ],
        compiler_params=_params("arbitrary"),
        name="s5_mixer",
    )(x3d, gain, w_in, perm_j, permt_j, bblk, cblk, a_re, a_im, d_skip)


def _s5_coefficients(a_re, a_im, log_dt, b_re, b_im, c_re, c_im):
    dt = jnp.exp(log_dt.astype(F32))[:, None]
    ar = a_re.astype(F32)
    ai = a_im.astype(F32)
    mag = jnp.exp(ar * dt)
    abar_re = mag * jnp.cos(ai * dt)
    abar_im = mag * jnp.sin(ai * dt)
    den = ar * ar + ai * ai
    nr = abar_re - 1.0
    ni = abar_im
    coef_re = (nr * ar + ni * ai) / den
    coef_im = (ni * ar - nr * ai) / den
    br = b_re.astype(F32)
    bi = b_im.astype(F32)
    bbar_re = coef_re[..., None] * br - coef_im[..., None] * bi
    bbar_im = coef_re[..., None] * bi + coef_im[..., None] * br
    n_groups = ar.shape[0]
    n_slabs = n_groups // SSM_SLAB_GROUPS
    row_group = np.arange(LANES)[:, None] // SSM_GROUP
    col_group = np.arange(SSM_SLAB_STATE)[None, :] // SSM_STATE
    in_mask = jnp.asarray(row_group == col_group, F32)

    def in_block(bb):
        bb = bb.reshape(n_slabs, SSM_SLAB_GROUPS, SSM_STATE, SSM_GROUP)
        bb = bb.transpose(0, 1, 3, 2).reshape(n_slabs, LANES, SSM_STATE)
        return jnp.tile(bb, (1, 1, SSM_SLAB_GROUPS)) * in_mask

    def out_block(cc):
        cc = cc.reshape(n_slabs, SSM_SLAB_GROUPS, SSM_GROUP, SSM_STATE)
        cc = cc.transpose(0, 1, 3, 2).reshape(n_slabs, SSM_SLAB_STATE, SSM_GROUP)
        return jnp.tile(cc, (1, 1, SSM_SLAB_GROUPS)) * in_mask.T

    bblk = jnp.concatenate([in_block(bbar_re), in_block(bbar_im)], axis=2)
    cblk = jnp.concatenate([out_block(c_re.astype(F32)),
                            -out_block(c_im.astype(F32))], axis=1)

    def lanes(a):
        a = a.reshape(n_slabs, 1, SSM_SLAB_STATE)
        a = jnp.concatenate([a, a], axis=2).reshape(1, -1)
        return jnp.broadcast_to(a, (SUBLANES, a.shape[1]))

    return (bblk.astype(BF16), cblk.astype(BF16), lanes(abar_re), lanes(abar_im))


MOE_CHUNK = 512
MOE_FIRST = 208
MOE_REST = MOE_CHUNK + SEG_ALIGN - MOE_FIRST
MOE_PIECES = ((0, MOE_FIRST), (MOE_FIRST, MOE_REST))
MOE_WINDOW = MOE_FIRST + MOE_REST
NOT_ROUTED = -1024.0
MOE_TILE = 512


def _glu_router_kernel(x_ref, act_ref, wa_ref, wb_ref, g_ref, w_ref, b_ref,
                       x3_ref, h_ref, pos_ref, post_ref, cnt_ref):
    tm, d = x_ref.shape
    parts = 2
    half = tm // parts

    def glu(r0):
        act = act_ref[r0:r0 + half, :]
        a = jnp.dot(act, wa_ref[...], preferred_element_type=F32)
        b = jnp.dot(act, wb_ref[...], preferred_element_type=F32)
        x3 = x_ref[r0:r0 + half, :] + a * _sigmoid(b)
        x3_ref[r0:r0 + half, :] = x3
        return x3

    def route(r0, x3):
        h = _rms(x3, g_ref[...])
        h_hi = h.astype(BF16)
        h_ref[r0:r0 + half, 0:d] = h_hi
        h_lo = (h - h_hi.astype(F32)).astype(BF16)
        p_hi = jnp.dot(h_hi, w_ref[...], preferred_element_type=F32)
        p_lo = jnp.dot(h_lo, w_ref[...], preferred_element_type=F32)
        logits = p_hi + pltpu.roll(p_hi, LANES - N_EXPERTS, axis=1) + p_lo + b_ref[...]
        lane = lax.broadcasted_iota(jnp.int32, logits.shape, 1)
        logits = jnp.where(lane < N_EXPERTS, logits, -jnp.inf)
        m1 = jnp.max(logits, axis=1, keepdims=True)
        i1 = jnp.min(jnp.where(logits == m1, lane, LANES), axis=1, keepdims=True)
        rest = jnp.where(lane == i1, -jnp.inf, logits)
        m2 = jnp.max(rest, axis=1, keepdims=True)
        i2 = jnp.min(jnp.where(rest == m2, lane, LANES), axis=1, keepdims=True)
        e2 = jnp.exp(m2 - m1)
        g1 = 1.0 / (1.0 + e2)
        g2 = e2 / (1.0 + e2)
        gate = jnp.where(lane == i1, g1, 0.0) + jnp.where(lane == i2, g2, 0.0)
        g_hi = gate.astype(BF16).astype(F32)
        g_r = gate - g_hi
        g_mid = g_r.astype(BF16).astype(F32)
        packed = jnp.where(lane < N_EXPERTS, g_hi,
                           jnp.where(lane < 2 * N_EXPERTS,
                                     pltpu.roll(g_mid, N_EXPERTS, axis=1),
                                     pltpu.roll(g_r - g_mid, 2 * N_EXPERTS, axis=1)))
        h_ref[r0:r0 + half, d:d + LANES] = packed.astype(BF16)
        return jnp.where((lane == i1) | (lane == i2), 1.0, 0.0)

    members = []
    x3 = glu(0)
    for p in range(parts):
        nxt = glu((p + 1) * half) if p + 1 < parts else None
        members.append(route(p * half, x3))
        x3 = nxt
    member = jnp.concatenate(members, axis=0)
    row = lax.broadcasted_iota(jnp.int32, member.shape, 0)
    c = member
    sh = 1
    while sh < tm:
        c = c + jnp.where(row >= sh, pltpu.roll(c, sh, axis=0), 0.0)
        sh *= 2
    pos = jnp.where(member > 0.0, c - member, NOT_ROUTED)
    pos_ref[...] = pos
    post_ref[0] = pos.T[0:SUBLANES, :]
    cnt_ref[0] = jnp.broadcast_to(c[tm - 1:tm, :], (SUBLANES, LANES))


def _glu_router(x2d, act2d, wa, wb, gain, w_pad, b_pad):
    t, d = x2d.shape
    tm = MOE_CHUNK
    n_chunks = t // tm
    return pl.pallas_call(
        _glu_router_kernel,
        grid=(n_chunks,),
        in_specs=[
            pl.BlockSpec((tm, d), lambda i: (i, 0)),
            pl.BlockSpec((tm, d), lambda i: (i, 0)),
            pl.BlockSpec(wa.shape, lambda i: (0, 0)),
            pl.BlockSpec(wb.shape, lambda i: (0, 0)),
            pl.BlockSpec((1, d), lambda i: (0, 0)),
            pl.BlockSpec((d, LANES), lambda i: (0, 0)),
            pl.BlockSpec((1, LANES), lambda i: (0, 0)),
        ],
        out_specs=[
            pl.BlockSpec((tm, d), lambda i: (i, 0)),
            pl.BlockSpec((tm, d + LANES), lambda i: (i, 0)),
            pl.BlockSpec((tm, LANES), lambda i: (i, 0)),
            pl.BlockSpec((1, SUBLANES, tm), lambda i: (i, 0, 0)),
            pl.BlockSpec((1, SUBLANES, LANES), lambda i: (i, 0, 0)),
        ],
        out_shape=[
            jax.ShapeDtypeStruct((t, d), F32),
            jax.ShapeDtypeStruct((t, d + LANES), BF16),
            jax.ShapeDtypeStruct((t, LANES), F32),
            jax.ShapeDtypeStruct((n_chunks, SUBLANES, tm), F32),
            jax.ShapeDtypeStruct((n_chunks, SUBLANES, LANES), F32),
        ],
        compiler_params=_params("parallel"),
        name="glu_router",
    )(x2d, act2d, wa, wb, gain, w_pad, b_pad)


MOE_SPARE = max(MOE_FIRST, MOE_REST)


def _moe_tiles(n_tokens, row_multiple):
    max_rows = (TOP_K * n_tokens + N_EXPERTS * (SEG_ALIGN - 1)
                + N_EXPERTS * (MOE_SPARE + MOE_TILE))
    unit = math.lcm(MOE_TILE, row_multiple)
    return -(-max_rows // unit) * (unit // MOE_TILE)


def _moe_layout(cnt):
    before = jnp.cumsum(cnt, axis=0) - cnt
    shift = before % SEG_ALIGN
    used = (jnp.sum(cnt, axis=0) + SEG_ALIGN - 1) // SEG_ALIGN * SEG_ALIGN
    padded = (used + MOE_SPARE + MOE_TILE - 1) // MOE_TILE * MOE_TILE
    ends = jnp.cumsum(padded)
    start = ends - padded
    off = start[None, :] + before - shift
    flat = lambda a: a.reshape(-1).astype(jnp.int32)
    return (flat(off), flat(shift + cnt), flat(shift),
            ends.astype(jnp.int32), (start + used).astype(jnp.int32))


def _tile_expert(i, ends_ref):
    first_row = i * MOE_TILE
    e = 0
    for k in range(N_EXPERTS - 1):
        e = e + (first_row >= ends_ref[k]).astype(jnp.int32)
    return e


def _segment_copies(hbm_ref, buf_ref, sem_ref, off_ref, chunk, to_hbm, slot=()):
    copies = []
    for e in range(N_EXPERTS):
        off = pl.multiple_of(off_ref[chunk * N_EXPERTS + e], SEG_ALIGN)
        for piece, (first, rows) in enumerate(MOE_PIECES):
            hbm = hbm_ref.at[pl.ds(off + first, rows)]
            buf = buf_ref.at[(*slot, e, pl.ds(first, rows))]
            src, dst = (buf, hbm) if to_hbm else (hbm, buf)
            copies.append(pltpu.make_async_copy(src, dst, sem_ref.at[(*slot, e, piece)]))
    return copies


def _onehot(first, shape, axis, target):
    rank = first + lax.broadcasted_iota(jnp.int32, shape, axis)
    return jnp.where(rank.astype(F32) == target, 1.0, 0.0).astype(BF16)


def _gather_kernel(off_ref, need_ref, shift_ref, h_ref, post_ref, zeros_hbm, o_hbm,
                   stage_ref, carry_ref, sem_ref):
    del zeros_hbm
    c = pl.program_id(0)
    slot = c % 2
    h = h_ref[...]

    @pl.when(c == 0)
    def _():
        carry_ref[...] = jnp.zeros_like(carry_ref)

    def long_segment(chunk, e):
        return need_ref[chunk * N_EXPERTS + e] >= MOE_FIRST

    def piece_onehot(e, piece):
        first, rows = MOE_PIECES[piece]
        return _onehot(first - shift_ref[c * N_EXPERTS + e], (rows, MOE_CHUNK), 0,
                       post_ref[0, e:e + 1, :])

    def fill_piece(e, piece, onehot=None):
        first, rows = MOE_PIECES[piece]
        onehot = piece_onehot(e, piece) if onehot is None else onehot
        stage_ref[slot, e, first:first + rows] = jnp.dot(
            onehot, h, preferred_element_type=F32).astype(BF16)
        if piece == 0:
            stage_ref[slot, e, 0:SEG_ALIGN] += carry_ref[e]

    onehot = piece_onehot(0, 0)
    for e in range(N_EXPERTS):
        nxt = piece_onehot(e + 1, 0) if e + 1 < N_EXPERTS else None
        fill_piece(e, 0, onehot)
        onehot = nxt
    for e in range(N_EXPERTS):
        @pl.when(long_segment(c, e))
        def _(e=e):
            fill_piece(e, 1)

    for e in range(N_EXPERTS):
        last_group = need_ref[c * N_EXPERTS + e] // SEG_ALIGN * SEG_ALIGN
        carry_ref[e] = stage_ref[slot, e, pl.ds(pl.multiple_of(last_group, SEG_ALIGN),
                                                SEG_ALIGN)]

    def each_copy(chunk, sl, action):
        copies = _segment_copies(o_hbm, stage_ref, sem_ref, off_ref, chunk,
                                 to_hbm=True, slot=(sl,))
        for e in range(N_EXPERTS):
            action(copies[2 * e], e)

            @pl.when(long_segment(chunk, e))
            def _(e=e):
                action(copies[2 * e + 1], e)

    @pl.when(c > 0)
    def _():
        each_copy(c - 1, 1 - slot, lambda cp, e: cp.wait())

    each_copy(c, slot, lambda cp, e: cp.start(priority=e % 2))

    @pl.when(c == pl.num_programs(0) - 1)
    def _():
        each_copy(c, slot, lambda cp, e: cp.wait())


def _gather(h2d, post, off, need, shift, zero_buf):
    t, d = h2d.shape
    n_chunks = t // MOE_CHUNK
    grid_spec = pltpu.PrefetchScalarGridSpec(
        num_scalar_prefetch=3,
        grid=(n_chunks,),
        in_specs=[
            pl.BlockSpec((MOE_CHUNK, d), lambda i, *_: (i, 0)),
            pl.BlockSpec((1, SUBLANES, MOE_CHUNK), lambda i, *_: (i, 0, 0)),
            pl.BlockSpec(memory_space=pl.ANY),
        ],
        out_specs=pl.BlockSpec(memory_space=pl.ANY),
        scratch_shapes=[
            pltpu.VMEM((2, N_EXPERTS, MOE_WINDOW, d), BF16),
            pltpu.VMEM((N_EXPERTS, SEG_ALIGN, d), BF16),
            pltpu.SemaphoreType.DMA((2, N_EXPERTS, 2)),
        ],
    )
    return pl.pallas_call(
        _gather_kernel,
        grid_spec=grid_spec,
        out_shape=jax.ShapeDtypeStruct(zero_buf.shape, BF16),
        input_output_aliases={5: 0},
        compiler_params=_params("arbitrary"),
        name="moe_gather",
    )(off, need, shift, h2d, post, zero_buf)


def _expert_kernel(ends_ref, used_end_ref, x_ref, w1_ref, w3_ref, w2_ref, o_ref):
    i = pl.program_id(0)
    expert = _tile_expert(i, ends_ref)
    rows = jnp.clip(used_end_ref[expert] - i * MOE_TILE, 0, MOE_TILE)
    d = o_ref.shape[1]

    def run(n):
        row = lax.broadcasted_iota(jnp.int32, (n, x_ref.shape[1]), 0)
        x = jnp.where(row < rows, x_ref[0:n, :], jnp.zeros((), x_ref.dtype))
        lane = lax.broadcasted_iota(jnp.int32, (n, LANES), 1)
        mine = (lane < 3 * N_EXPERTS) & (jnp.bitwise_and(lane, N_EXPERTS - 1) == expert)
        gate = jnp.sum(jnp.where(mine, x[:, d:d + LANES].astype(F32), 0.0),
                       axis=1, keepdims=True)
        y = _swiglu(x[:, 0:d], w1_ref, w3_ref, w2_ref)
        o_ref[0:n, :] = (gate * y).astype(o_ref.dtype)

    quarter = MOE_TILE // 4
    for n in range(quarter, MOE_TILE + 1, quarter):
        @pl.when((rows > n - quarter) & (rows <= n))
        def _(n=n):
            run(n)
            if n < MOE_TILE:
                o_ref[n:MOE_TILE, :] = jnp.zeros((MOE_TILE - n, d), o_ref.dtype)

    @pl.when(rows == 0)
    def _():
        o_ref[...] = jnp.zeros_like(o_ref)


def _experts(x_sorted, w1, w3, w2, ends, used_end):
    assert N_EXPERTS & (N_EXPERTS - 1) == 0
    n_rows = x_sorted.shape[0]
    f = w1.shape[1]
    d = w2.shape[1]
    grid_spec = pltpu.PrefetchScalarGridSpec(
        num_scalar_prefetch=2,
        grid=(n_rows // MOE_TILE,),
        in_specs=[
            pl.BlockSpec((MOE_TILE, d + LANES), lambda i, ends, ue: (i, 0)),
            pl.BlockSpec((d, f), lambda i, ends, ue: (_tile_expert(i, ends), 0)),
            pl.BlockSpec((d, f), lambda i, ends, ue: (_tile_expert(i, ends), 0)),
            pl.BlockSpec((f, d), lambda i, ends, ue: (_tile_expert(i, ends), 0)),
        ],
        out_specs=pl.BlockSpec((MOE_TILE, d), lambda i, ends, ue: (i, 0)),
    )
    return pl.pallas_call(
        _expert_kernel,
        grid_spec=grid_spec,
        out_shape=jax.ShapeDtypeStruct((n_rows, d), BF16),
        compiler_params=_params("arbitrary"),
        name="moe_experts",
    )(ends, used_end, x_sorted, w1, w3, w2)


def _combine_kernel(off_ref, need_ref, shift_ref, x_ref, pos_ref, fg_ref, y_hbm,
                    o_ref, ybuf_ref, acc_ref, sem_ref):
    c = pl.program_id(0)
    n_chunks = pl.num_programs(0)
    slot = c % 2

    def fetch(chunk, sl, start):
        copies = _segment_copies(y_hbm, ybuf_ref, sem_ref, off_ref, chunk,
                                 to_hbm=False, slot=(sl,))
        for e in range(N_EXPERTS):
            first, second = copies[2 * e], copies[2 * e + 1]
            if start:
                first.start(priority=e % 2)
            else:
                first.wait()

            @pl.when(need_ref[chunk * N_EXPERTS + e] >= MOE_FIRST)
            def _(second=second, e=e):
                if start:
                    second.start(priority=e % 2)
                else:
                    second.wait()

    @pl.when(c == 0)
    def _():
        fetch(c, slot, True)

    @pl.when(c + 1 < n_chunks)
    def _():
        fetch(c + 1, 1 - slot, True)

    fetch(c, slot, False)

    def piece_onehot(e, piece):
        first, rows = MOE_PIECES[piece]
        return _onehot(first - shift_ref[c * N_EXPERTS + e], (MOE_CHUNK, rows), 1,
                       pos_ref[:, e:e + 1])

    def piece_rows(e, piece, onehot=None):
        first, rows = MOE_PIECES[piece]
        onehot = piece_onehot(e, piece) if onehot is None else onehot
        return jnp.dot(onehot, ybuf_ref[slot, e, first:first + rows],
                       preferred_element_type=F32)

    acc = x_ref[...]
    onehot = piece_onehot(0, 0)
    for e in range(N_EXPERTS):
        nxt = piece_onehot(e + 1, 0) if e + 1 < N_EXPERTS else None
        acc = acc + piece_rows(e, 0, onehot)
        onehot = nxt
    acc_ref[...] = acc
    for e in range(N_EXPERTS):
        @pl.when(need_ref[c * N_EXPERTS + e] >= MOE_FIRST)
        def _(e=e):
            acc_ref[...] += piece_rows(e, 1)

    o_ref[...] = _rms(acc_ref[...], fg_ref[...])


def _combine(x2d, pos, final_gain, y_sorted, off, need, shift):
    t, d = x2d.shape
    n_chunks = t // MOE_CHUNK
    grid_spec = pltpu.PrefetchScalarGridSpec(
        num_scalar_prefetch=3,
        grid=(n_chunks,),
        in_specs=[
            pl.BlockSpec((MOE_CHUNK, d), lambda i, *_: (i, 0)),
            pl.BlockSpec((MOE_CHUNK, LANES), lambda i, *_: (i, 0)),
            pl.BlockSpec((1, d), lambda i, *_: (0, 0)),
            pl.BlockSpec(memory_space=pl.ANY),
        ],
        out_specs=pl.BlockSpec((MOE_CHUNK, d), lambda i, *_: (i, 0)),
        scratch_shapes=[
            pltpu.VMEM((2, N_EXPERTS, MOE_WINDOW, d), BF16),
            pltpu.VMEM((MOE_CHUNK, d), F32),
            pltpu.SemaphoreType.DMA((2, N_EXPERTS, 2)),
        ],
    )
    return pl.pallas_call(
        _combine_kernel,
        grid_spec=grid_spec,
        out_shape=jax.ShapeDtypeStruct((t, d), F32),
        compiler_params=_params("arbitrary"),
        name="moe_combine",
    )(off, need, shift, x2d, pos, final_gain, y_sorted)


def _glu_moe(x2d, act2d, wa, wb, gain, router_w, router_b, w1, w3, w2, final_gain,
             zero_buf):
    t, d = x2d.shape
    rw_hi = router_w.astype(BF16)
    rw_lo = (router_w.astype(F32) - rw_hi.astype(F32)).astype(BF16)
    rw = jnp.pad(jnp.concatenate([rw_hi, rw_lo], axis=1),
                 ((0, 0), (0, LANES - 2 * N_EXPERTS)))
    rb = jnp.pad(router_b.astype(F32), (0, LANES - N_EXPERTS)).reshape(1, LANES)
    x3, h, pos, post, cnt = _glu_router(x2d, act2d, wa, wb, gain, rw, rb)
    cnt = cnt[:, 0, :N_EXPERTS].astype(jnp.int32)
    off, need, shift, ends, used_end = _moe_layout(cnt)
    h_sorted = _gather(h, post, off, need, shift, zero_buf)
    y_sorted = _experts(h_sorted, w1, w3, w2, ends, used_end)
    return _combine(x3, pos, final_gain, y_sorted, off, need, shift)


def _even_weights(w_in, b_forget):
    n = w_in.shape[1]
    w_all = jnp.pad(w_in.astype(BF16), ((0, 0), (0, -n % LANES)))
    bias = jnp.pad(b_forget.astype(F32), (0, LANES - FOX_HEADS)).reshape(1, LANES)
    qw = FOX_HEADS * LANES
    place = np.zeros((LANES, 2 * qw), np.float32)
    for hh in range(FOX_HEADS):
        for piece in range(3):
            place[piece * FOX_HEADS + hh, hh * LANES + BIAS_LANE + piece] = 1.0
            place[piece * FOX_HEADS + hh, qw + hh * LANES + ONES_LANE + piece] = -1.0
    return w_all, bias, jnp.asarray(place, BF16)


def kernel(x, even_mix_norm, even_w_in, even_b_forget, even_w_pool, even_pool_scale, even_w_out, even_ffn_norm, even_ffn_w1, even_ffn_w3, even_ffn_w2, odd_mix_norm, odd_w_in, ssm_a_re, ssm_a_im, ssm_log_dt, ssm_b_re, ssm_b_im, ssm_c_re, ssm_c_im, ssm_d, odd_w_glu_a, odd_w_glu_b, odd_moe_norm, router_w, router_b, expert_w1, expert_w3, expert_w2, final_norm):
    b, s, d = x.shape
    t = b * s
    assert b == SUBLANES, "the S5 recurrence keeps one batch row per sublane"
    x2d = x.reshape(t, d)
    row = lambda v: v.reshape(1, -1).astype(F32)

    w_all, bias, place = _even_weights(even_w_in[0], even_b_forget[0])
    (q_aug, k_aug, v, p_in), casted = _even_inproj(
        x2d, row(even_mix_norm[0]), w_all, bias, place,
        [even_ffn_w1[0], even_ffn_w3[0], even_ffn_w2[0], even_w_out[0],
         even_w_pool[0].reshape(-1, POOL_GROUP_DIM),
         odd_w_in[0], odd_w_glu_a[0], odd_w_glu_b[0]],
        seq=s, tm=INPROJ_ROWS)
    ffn_w1, ffn_w3, ffn_w2, w_out, w_pool, s5_w_in, glu_a, glu_b = casted
    n_e, _, f = expert_w1[0].shape
    att, (ew1, ew3, ew2) = _attention(
        q_aug, k_aug, v,
        [expert_w1[0].reshape(n_e * d, f), expert_w3[0].reshape(n_e * d, f),
         expert_w2[0].reshape(n_e * f, d)],
        batch=b, seq=s, tq=ATTN_QUERY_TILE)
    x2 = _even_tail(x2d, att, p_in, w_pool.reshape(even_w_pool[0].shape),
                    row(even_pool_scale[0]), w_out, row(even_ffn_norm[0]),
                    ffn_w1, ffn_w3, ffn_w2, seq=s, tm=EVEN_TAIL_ROWS)

    bblk, cblk, a_re, a_im = _s5_coefficients(
        ssm_a_re[0], ssm_a_im[0], ssm_log_dt[0], ssm_b_re[0], ssm_b_im[0],
        ssm_c_re[0], ssm_c_im[0])
    moe_rows = _moe_tiles(t, (s // S5_STEP) * SEG_ALIGN) * MOE_TILE
    g, zero_buf = _s5_mixer(x2.reshape(b, s, d), row(odd_mix_norm[0]),
                            s5_w_in, bblk, cblk, a_re, a_im,
                            row(ssm_d[0]), tc=S5_STEP,
                            zero_shape=(moe_rows, d + LANES))
    out = _glu_moe(x2, g.reshape(t, d), glu_a, glu_b, row(odd_moe_norm[0]),
                   router_w[0], router_b[0], ew1, ew3, ew2,
                   row(final_norm), zero_buf)
    return out.reshape(b, s, d)
```
